```python
import math
import jax
import jax.numpy as jnp
from jax import lax
import numpy as np

D_MODEL = 2048
BATCH = 4
SEQ = 4096
DEPTH = 4
DEC_BATCH = 16
DEC_SEQ = 16
PAST_LEN = 2048

CHUNK = 64
N_MIXERS = 3
N_A = (DEPTH + 2) // 3
N_B = (DEPTH + 1) // 3
N_C = DEPTH // 3
EPS = 1e-6
ROPE_THETA = 10000.0
NEG_INF = -1e30
HEAD_DIM = 128

A_HEADS = 16
A_KV_HEADS = 4
IDX_HEADS = 8
IDX_DIM = 64
TOPK_MAX = 256
A_QBLK = 128
A_Q = A_HEADS * HEAD_DIM
A_KV = A_KV_HEADS * HEAD_DIM
A_IN = A_Q + 2 * A_KV + IDX_HEADS * IDX_DIM + IDX_DIM + IDX_HEADS
IDX_W_SCALE = (IDX_HEADS * IDX_DIM) ** -0.5

S5_GROUP = 16
S5_GROUPS = D_MODEL // S5_GROUP
S5_STATE = 64
DT_MIN = 1e-3
DT_MAX = 1e-1

C_HEADS = 16
C_BAND_CHUNKS = 8
C_REACH = C_BAND_CHUNKS * CHUNK
REL_CLIP = 128
C_QKV = 3 * C_HEADS * HEAD_DIM

FFN_HALF = 2 * D_MODEL
CONV_W = 3

kernel_name = 'hybrid_stream_dsa_s5_chunkband_step'

F32 = jnp.float32


def rmsnorm(x, g):
    xf = x.astype(F32)
    y = xf * lax.rsqrt(jnp.mean(xf * xf, axis=-1, keepdims=True) + EPS)
    return (y * g.astype(F32)).astype(x.dtype)


def rope(x, pos):
    half = x.shape[-1] // 2
    inv = jnp.power(ROPE_THETA, -jnp.arange(half, dtype=F32) / half)
    ang = pos.astype(F32)[:, None] * inv[None, :]
    cos = jnp.cos(ang)[:, None, :]
    sin = jnp.sin(ang)[:, None, :]
    xf = x.astype(F32)
    x1, x2 = xf[..., :half], xf[..., half:]
    return jnp.concatenate([x1 * cos - x2 * sin, x2 * cos + x1 * sin], axis=-1).astype(x.dtype)


def a_project(xn, w_in, q_gain, k_gain, pos):
    B, T, _ = xn.shape
    h = xn @ w_in
    o1 = A_Q
    o2 = o1 + A_KV
    o3 = o2 + A_KV
    o4 = o3 + IDX_HEADS * IDX_DIM
    o5 = o4 + IDX_DIM
    q = rope(rmsnorm(h[..., :o1].reshape(B, T, A_HEADS, HEAD_DIM), q_gain), pos)
    k = rope(rmsnorm(h[..., o1:o2].reshape(B, T, A_KV_HEADS, HEAD_DIM), k_gain), pos)
    v = h[..., o2:o3].reshape(B, T, A_KV_HEADS, HEAD_DIM)
    qi = rope(h[..., o3:o4].reshape(B, T, IDX_HEADS, IDX_DIM), pos)
    ki = rope(h[..., o4:o5][:, :, None, :], pos)[:, :, 0]
    wi = h[..., o5:] * IDX_W_SCALE
    return q, k, v, qi, ki, wi


def index_scores(qi, wi, ki):
    s = jnp.einsum('bthd,bld->bthl', qi, ki, preferred_element_type=F32)
    return jnp.einsum('bth,bthl->btl', wi.astype(F32), jax.nn.relu(s))


def sparse_attend(q, k, v, sel, valid):
    B, T = q.shape[:2]
    kg = jax.vmap(lambda kb, ib: kb[ib])(k, sel)
    vg = jax.vmap(lambda vb, ib: vb[ib])(v, sel)
    qg = q.reshape(B, T, A_KV_HEADS, A_HEADS // A_KV_HEADS, HEAD_DIM)
    s = jnp.einsum('btjgd,btnjd->btjgn', qg, kg, preferred_element_type=F32) * (HEAD_DIM ** -0.5)
    if valid is not None:
        s = jnp.where(valid[:, :, None, None, :], s, NEG_INF)
    p = jax.nn.softmax(s, axis=-1)
    o = jnp.einsum('btjgn,btnjd->btjgd', p, vg.astype(F32))
    return o.reshape(B, T, A_Q).astype(q.dtype)


def mixer_a_prompt(xn, w_in, w_out, q_gain, k_gain):
    B, S, _ = xn.shape
    pos = jnp.arange(S)
    q, k, v, qi, ki, wi = a_project(xn, w_in, q_gain, k_gain, pos)
    ksel = min(TOPK_MAX, S // 4)
    kchunk = jnp.arange(S) // CHUNK

    def block(j):
        t0 = j * A_QBLK
        qb = lax.dynamic_slice_in_dim(q, t0, A_QBLK, axis=1)
        qib = lax.dynamic_slice_in_dim(qi, t0, A_QBLK, axis=1)
        wib = lax.dynamic_slice_in_dim(wi, t0, A_QBLK, axis=1)
        qchunk = (t0 + jnp.arange(A_QBLK)) // CHUNK
        sc = index_scores(qib, wib, ki)
        adm = kchunk[None, :] <= qchunk[:, None]
        sc = jnp.where(adm[None], sc, NEG_INF)
        _, sel = lax.top_k(sc, ksel)
        valid = (sel // CHUNK) <= qchunk[None, :, None]
        return sparse_attend(qb, k, v, sel, valid)

    o = lax.map(block, jnp.arange(S // A_QBLK))
    o = jnp.transpose(o, (1, 0, 2, 3)).reshape(B, S, A_Q)
    return o @ w_out, k, v, ki


def mixer_a_sample(xn, ck, cv, cki, w_in, w_out, q_gain, k_gain):
    B, T, _ = xn.shape
    pos = PAST_LEN + jnp.arange(T)
    q, k, v, qi, ki, wi = a_project(xn, w_in, q_gain, k_gain, pos)
    k_all = jnp.concatenate([ck.astype(k.dtype), k], axis=1)
    v_all = jnp.concatenate([cv.astype(v.dtype), v], axis=1)
    ki_all = jnp.concatenate([cki.astype(ki.dtype), ki], axis=1)
    L = k_all.shape[1]
    ksel = min(TOPK_MAX, L // 4)
    sc = index_scores(qi, wi, ki_all)
    _, sel = lax.top_k(sc, ksel)
    o = sparse_attend(q, k_all, v_all, sel, None)
    return o @ w_out, k, v, ki


def s5_discretize(a_re, a_im, log_dt, b_re, b_im):
    dt = jnp.exp(log_dt.astype(F32))[:, None]
    ar, ai = a_re.astype(F32), a_im.astype(F32)
    mag = jnp.exp(ar * dt)
    lr, li = mag * jnp.cos(ai * dt), mag * jnp.sin(ai * dt)
    den = ar * ar + ai * ai
    nr = lr - 1.0
    fr = (nr * ar + li * ai) / den
    fi = (li * ar - nr * ai) / den
    br, bi = b_re.astype(F32), b_im.astype(F32)
    bbr = fr[..., None] * br - fi[..., None] * bi
    bbi = fr[..., None] * bi + fi[..., None] * br
    return lr, li, bbr, bbi


def s5_combine(e1, e2):
    a1r, a1i, b1r, b1i = e1
    a2r, a2i, b2r, b2i = e2
    return (a2r * a1r - a2i * a1i, a2r * a1i + a2i * a1r,
            a2r * b1r - a2i * b1i + b2r, a2r * b1i + a2i * b1r + b2i)


def s5_stream(u, h0r, h0i, lr, li, bbr, bbi, cr, ci, df):
    T = u.shape[0]
    uf = u.astype(F32)
    ug = uf.reshape(T, S5_GROUPS, S5_GROUP)
    xr = jnp.einsum('tgc,gpc->tgp', ug, bbr)
    xi = jnp.einsum('tgc,gpc->tgp', ug, bbi)
    xr = xr.at[0].add(lr * h0r - li * h0i)
    xi = xi.at[0].add(lr * h0i + li * h0r)
    ar = jnp.broadcast_to(lr, xr.shape)
    ai = jnp.broadcast_to(li, xi.shape)
    _, _, hr, hi = lax.associative_scan(s5_combine, (ar, ai, xr, xi), axis=0)
    y = jnp.einsum('tgp,gcp->tgc', hr, cr) - jnp.einsum('tgp,gcp->tgc', hi, ci)
    y = y.reshape(T, D_MODEL) + df * uf
    return y, hr[-1], hi[-1]


def mixer_b(xn, h0r, h0i, a_re, a_im, log_dt, b_re, b_im, c_re, c_im, d, w_glu):
    lr, li, bbr, bbi = s5_discretize(a_re, a_im, log_dt, b_re, b_im)
    cr, ci, df = c_re.astype(F32), c_im.astype(F32), d.astype(F32)

    def stream(args):
        u, hr0, hi0 = args
        return s5_stream(u, hr0.astype(F32), hi0.astype(F32), lr, li, bbr, bbi, cr, ci, df)

    y, hr, hi = lax.map(stream, (xn, h0r, h0i))
    g = jax.nn.gelu(y).astype(xn.dtype)
    ga, gb = jnp.split(g @ w_glu, 2, axis=-1)
    return ga * jax.nn.sigmoid(gb), hr, hi


def c_project(xn, w_in, q_gain, k_gain):
    B, T, _ = xn.shape
    q, k, v = jnp.split(xn @ w_in, 3, axis=-1)
    q = rmsnorm(q.reshape(B, T, C_HEADS, HEAD_DIM), q_gain)
    k = rmsnorm(k.reshape(B, T, C_HEADS, HEAD_DIM), k_gain)
    return q, k, v.reshape(B, T, C_HEADS, HEAD_DIM)


def rel_bias_lookup(rel, dist):
    return rel.astype(F32)[:, jnp.clip(dist, -REL_CLIP, REL_CLIP) + REL_CLIP]


def band_attend(q, k, v, bias, valid):
    s = jnp.einsum('bqhd,bkhd->bhqk', q, k, preferred_element_type=F32) * (HEAD_DIM ** -0.5) + bias[None]
    if valid is not None:
        s = jnp.where(valid, s, NEG_INF)
    p = jax.nn.softmax(s, axis=-1)
    return jnp.einsum('bhqk,bkhd->bqhd', p, v.astype(F32)).astype(q.dtype)


def mixer_c_prompt(xn, w_in, w_out, q_gain, k_gain, rel):
    B, S, _ = xn.shape
    q, k, v = c_project(xn, w_in, q_gain, k_gain)
    pad = ((0, 0), (C_REACH, 0), (0, 0), (0, 0))
    kp, vp = jnp.pad(k, pad), jnp.pad(v, pad)
    band = C_REACH + CHUNK
    dist = C_REACH + jnp.arange(CHUNK)[:, None] - jnp.arange(band)[None, :]
    bias = rel_bias_lookup(rel, dist)

    def chunk(c):
        qc = lax.dynamic_slice_in_dim(q, c * CHUNK, CHUNK, axis=1)
        kc = lax.dynamic_slice_in_dim(kp, c * CHUNK, band, axis=1)
        vc = lax.dynamic_slice_in_dim(vp, c * CHUNK, band, axis=1)
        valid = (c * CHUNK - C_REACH + jnp.arange(band)) >= 0
        return band_attend(qc, kc, vc, bias, valid[None, :])

    o = lax.map(chunk, jnp.arange(S // CHUNK))
    o = jnp.transpose(o, (1, 0, 2, 3, 4)).reshape(B, S, C_HEADS * HEAD_DIM)
    keep = min(C_REACH, S)
    return o @ w_out, k[:, S - keep:], v[:, S - keep:]


def mixer_c_sample(xn, ck, cv, w_in, w_out, q_gain, k_gain, rel):
    B, T, _ = xn.shape
    q, k, v = c_project(xn, w_in, q_gain, k_gain)
    cp = ck.shape[1]
    k_all = jnp.concatenate([ck.astype(k.dtype), k], axis=1)
    v_all = jnp.concatenate([cv.astype(v.dtype), v], axis=1)
    q_pos = PAST_LEN + jnp.arange(T)
    k_pos = PAST_LEN - cp + jnp.arange(cp + T)
    bias = rel_bias_lookup(rel, q_pos[:, None] - k_pos[None, :])
    o = band_attend(q, k_all, v_all, bias, None).reshape(B, T, C_HEADS * HEAD_DIM)
    return o @ w_out, k, v


def conv_ffn(xn, hist, w_up, conv_w, conv_b, w_down):
    T = xn.shape[1]
    h = xn @ w_up
    hp = jnp.concatenate([hist.astype(h.dtype), h], axis=1)
    c = conv_b
    for j in range(CONV_W):
        c = c + conv_w[j] * hp[:, j:j + T]
    ga, gv = jnp.split(c, 2, axis=-1)
    return (jax.nn.silu(ga) * gv) @ w_down, hp[:, -(CONV_W - 1):]


def setup_inputs(seed: int = 0):
    key = jax.random.key(seed)
    ks = iter(jax.random.split(key, 48))
    D = D_MODEL
    F2 = 2 * FFN_HALF
    c_past = min(C_REACH, PAST_LEN)

    def nrm(shape, scale):
        return jax.random.normal(next(ks), shape, F32) * scale

    def gain(shape):
        return 1.0 + 0.05 * jax.random.normal(next(ks), shape, F32)

    n_idx = jnp.arange(S5_STATE, dtype=F32)
    return {
        'x_prompt': nrm((BATCH, SEQ, D), 1.0),
        'x_sample': nrm((DEC_BATCH, DEC_SEQ, D), 1.0),
        'cache_a_k': nrm((N_A, DEC_BATCH, PAST_LEN, A_KV_HEADS, HEAD_DIM), 1.0),
        'cache_a_v': nrm((N_A, DEC_BATCH, PAST_LEN, A_KV_HEADS, HEAD_DIM), 1.0),
        'cache_a_kidx': nrm((N_A, DEC_BATCH, PAST_LEN, IDX_DIM), 1.0),
        'cache_c_k': nrm((N_C, DEC_BATCH, c_past, C_HEADS, HEAD_DIM), 1.0),
        'cache_c_v': nrm((N_C, DEC_BATCH, c_past, C_HEADS, HEAD_DIM), 1.0),
        'state_b_re': nrm((N_B, DEC_BATCH, S5_GROUPS, S5_STATE), 0.1),
        'state_b_im': nrm((N_B, DEC_BATCH, S5_GROUPS, S5_STATE), 0.1),
        'state_ffn_conv': nrm((DEPTH, DEC_BATCH, CONV_W - 1, F2), 1.0),
        'norm_mix': gain((DEPTH, D)),
        'norm_ffn': gain((DEPTH, D)),
        'a_w_in': nrm((N_A, D, A_IN), D ** -0.5),
        'a_w_out': nrm((N_A, A_Q, D), A_Q ** -0.5),
        'a_q_norm': gain((N_A, HEAD_DIM)),
        'a_k_norm': gain((N_A, HEAD_DIM)),
        'b_a_re': -0.5 + nrm((N_B, S5_GROUPS, S5_STATE), 0.01),
        'b_a_im': jnp.pi * n_idx + nrm((N_B, S5_GROUPS, S5_STATE), 0.01),
        'b_log_dt': jax.random.uniform(next(ks), (N_B, S5_GROUPS), F32, math.log(DT_MIN), math.log(DT_MAX)),
        'b_b_re': nrm((N_B, S5_GROUPS, S5_STATE, S5_GROUP), (2 * S5_GROUP) ** -0.5),
        'b_b_im': nrm((N_B, S5_GROUPS, S5_STATE, S5_GROUP), (2 * S5_GROUP) ** -0.5),
        'b_c_re': nrm((N_B, S5_GROUPS, S5_GROUP, S5_STATE), (2 * S5_STATE) ** -0.5),
        'b_c_im': nrm((N_B, S5_GROUPS, S5_GROUP, S5_STATE), (2 * S5_STATE) ** -0.5),
        'b_d': nrm((N_B, D), 1.0),
        'b_w_glu': nrm((N_B, D, 2 * D), D ** -0.5),
        'c_w_in': nrm((N_C, D, C_QKV), D ** -0.5),
        'c_w_out': nrm((N_C, C_HEADS * HEAD_DIM, D), (C_HEADS * HEAD_DIM) ** -0.5),
        'c_q_norm': gain((N_C, HEAD_DIM)),
        'c_k_norm': gain((N_C, HEAD_DIM)),
        'c_rel_bias': nrm((N_C, C_HEADS, 2 * REL_CLIP + 1), 0.1),
        'ffn_w_up': nrm((DEPTH, D, F2), D ** -0.5),
        'ffn_conv_w': nrm((DEPTH, CONV_W, F2), CONV_W ** -0.5),
        'ffn_conv_b': nrm((DEPTH, F2), 0.01),
        'ffn_w_down': nrm((DEPTH, FFN_HALF, D), FFN_HALF ** -0.5),
    }


def reference(x_prompt, x_sample, cache_a_k, cache_a_v, cache_a_kidx, cache_c_k, cache_c_v,
              state_b_re, state_b_im, state_ffn_conv, norm_mix, norm_ffn,
              a_w_in, a_w_out, a_q_norm, a_k_norm,
              b_a_re, b_a_im, b_log_dt, b_b_re, b_b_im, b_c_re, b_c_im, b_d, b_w_glu,
              c_w_in, c_w_out, c_q_norm, c_k_norm, c_rel_bias,
              ffn_w_up, ffn_conv_w, ffn_conv_b, ffn_w_down):
    xp, xs = x_prompt, x_sample
    B = xp.shape[0]
    a_k_p, a_v_p, a_ki_p, a_k_s, a_v_s, a_ki_s = [], [], [], [], [], []
    c_k_p, c_v_p, c_k_s, c_v_s = [], [], [], []
    b_re_p, b_im_p, b_re_s, b_im_s = [], [], [], []
    f_p, f_s = [], []
    for i in range(DEPTH):
        li = i // N_MIXERS
        kind = i % N_MIXERS
        hp = rmsnorm(xp, norm_mix[i])
        hs = rmsnorm(xs, norm_mix[i])
        if kind == 0:
            op, kp, vp, kip = mixer_a_prompt(hp, a_w_in[li], a_w_out[li], a_q_norm[li], a_k_norm[li])
            os_, ks_, vs_, kis = mixer_a_sample(hs, cache_a_k[li], cache_a_v[li], cache_a_kidx[li],
                                                a_w_in[li], a_w_out[li], a_q_norm[li], a_k_norm[li])
            a_k_p.append(kp); a_v_p.append(vp); a_ki_p.append(kip)
            a_k_s.append(ks_); a_v_s.append(vs_); a_ki_s.append(kis)
        elif kind == 1:
            bargs = (b_a_re[li], b_a_im[li], b_log_dt[li], b_b_re[li], b_b_im[li],
                     b_c_re[li], b_c_im[li], b_d[li], b_w_glu[li])
            h0 = jnp.zeros((B, S5_GROUPS, S5_STATE), F32)
            op, hrp, hip = mixer_b(hp, h0, h0, *bargs)
            os_, hrs, his = mixer_b(hs, state_b_re[li], state_b_im[li], *bargs)
            b_re_p.append(hrp); b_im_p.append(hip); b_re_s.append(hrs); b_im_s.append(his)
        else:
            op, kp, vp = mixer_c_prompt(hp, c_w_in[li], c_w_out[li], c_q_norm[li], c_k_norm[li], c_rel_bias[li])
            os_, ks_, vs_ = mixer_c_sample(hs, cache_c_k[li], cache_c_v[li], c_w_in[li], c_w_out[li],
                                           c_q_norm[li], c_k_norm[li], c_rel_bias[li])
            c_k_p.append(kp); c_v_p.append(vp); c_k_s.append(ks_); c_v_s.append(vs_)
        xp = xp + op.astype(xp.dtype)
        xs = xs + os_.astype(xs.dtype)
        hp = rmsnorm(xp, norm_ffn[i])
        hs = rmsnorm(xs, norm_ffn[i])
        hist0 = jnp.zeros((B, CONV_W - 1, 2 * FFN_HALF), hp.dtype)
        fp, histp = conv_ffn(hp, hist0, ffn_w_up[i], ffn_conv_w[i], ffn_conv_b[i], ffn_w_down[i])
        fs, hists = conv_ffn(hs, state_ffn_conv[i], ffn_w_up[i], ffn_conv_w[i], ffn_conv_b[i], ffn_w_down[i])
        xp = xp + fp.astype(xp.dtype)
        xs = xs + fs.astype(xs.dtype)
        f_p.append(histp); f_s.append(hists)
    return (xp, xs,
            jnp.stack(a_k_p), jnp.stack(a_v_p), jnp.stack(a_ki_p),
            jnp.stack(a_k_s), jnp.stack(a_v_s), jnp.stack(a_ki_s),
            jnp.stack(c_k_p), jnp.stack(c_v_p), jnp.stack(c_k_s), jnp.stack(c_v_s),
            jnp.stack(b_re_p), jnp.stack(b_im_p), jnp.stack(b_re_s), jnp.stack(b_im_s),
            jnp.stack(f_p), jnp.stack(f_s))
```

```python
import functools
import math

import jax
import jax.numpy as jnp
from jax import lax
from jax.experimental import pallas as pl
from jax.experimental.pallas import tpu as pltpu

F32 = jnp.float32
BF16 = jnp.bfloat16
I32 = jnp.int32

EPS = 1e-6
NEG_INF = -1e30
ROW_MAX_INIT = -1e29
ROPE_THETA = 10000.0
CHUNK = 64
HEAD_DIM = 128
LANES = 128

A_HEADS = 16
A_KV_HEADS = 4
A_GROUP = A_HEADS // A_KV_HEADS
IDX_HEADS = 8
IDX_DIM = 64
TOPK_MAX = 256
A_Q = A_HEADS * HEAD_DIM
A_KV = A_KV_HEADS * HEAD_DIM
A_QI = IDX_HEADS * IDX_DIM
IDX_W_SCALE = (IDX_HEADS * IDX_DIM) ** -0.5
A_OFF_K = A_Q
A_OFF_V = A_OFF_K + A_KV
A_OFF_QI = A_OFF_V + A_KV
A_OFF_KI = A_OFF_QI + A_QI
A_OFF_WI = A_OFF_KI + LANES
A_IN_PAD = A_OFF_WI + LANES

S5_GROUP = 16
S5_STATE = 64
S5_CHUNK = 16
S5_LB_GROUPS = LANES // S5_GROUP

C_HEADS = 16
C_BAND_CHUNKS = 8
C_REACH = C_BAND_CHUNKS * CHUNK
REL_CLIP = 128
CONV_W = 3

VMEM_LIMIT = 56 * 1024 * 1024


def _cparams(*sem):
    return pltpu.CompilerParams(dimension_semantics=sem, vmem_limit_bytes=VMEM_LIMIT)


def _pick(n, target, mult):
    best = None
    for t in range(mult, min(n, target) + 1, mult):
        if n % t == 0:
            best = t
    return best if best is not None else n


def _dot(a, b):
    return jnp.dot(a, b, preferred_element_type=F32)


def _dot_nt(a, b):
    return lax.dot_general(a, b, (((1,), (1,)), ((), ())), preferred_element_type=F32)


def _rms(x, g):
    return x * lax.rsqrt(jnp.mean(x * x, axis=-1, keepdims=True) + EPS) * g


def _rms_matmul_kernel(x_ref, g_ref, w_ref, o_ref, xn_ref):
    @pl.when(pl.program_id(1) == 0)
    def _():
        xn_ref[...] = _rms(x_ref[...], g_ref[...]).astype(BF16)

    o_ref[...] = _dot(xn_ref[...], w_ref[...])


def _rms_matmul(x, g, w, *, tm=512, tn=1024):
    M, D = x.shape
    N = w.shape[1]
    tm = _pick(M, tm, 16)
    tn = _pick(N, tn, LANES)
    return pl.pallas_call(
        _rms_matmul_kernel,
        grid=(M // tm, N // tn),
        in_specs=[pl.BlockSpec((tm, D), lambda i, j: (i, 0)),
                  pl.BlockSpec((1, D), lambda i, j: (0, 0)),
                  pl.BlockSpec((D, tn), lambda i, j: (0, j))],
        out_specs=pl.BlockSpec((tm, tn), lambda i, j: (i, j)),
        out_shape=jax.ShapeDtypeStruct((M, N), F32),
        scratch_shapes=[pltpu.VMEM((tm, D), BF16)],
        compiler_params=_cparams("parallel", "arbitrary"),
        name="rms_matmul",
    )(x, g, w)


def _rmsnorm_kernel(x_ref, g_ref, o_ref):
    o_ref[...] = _rms(x_ref[...], g_ref[...])


def _rmsnorm(x, g, *, tm=512):
    M, D = x.shape
    tm = _pick(M, tm, 8)
    return pl.pallas_call(
        _rmsnorm_kernel,
        grid=(M // tm,),
        in_specs=[pl.BlockSpec((tm, D), lambda i: (i, 0)),
                  pl.BlockSpec((1, D), lambda i: (0, 0))],
        out_specs=pl.BlockSpec((tm, D), lambda i: (i, 0)),
        out_shape=jax.ShapeDtypeStruct((M, D), F32),
        compiler_params=_cparams("parallel"),
        name="rmsnorm",
    )(x, g)


def _matmul_res_kernel(x_ref, w_ref, r_ref, o_ref):
    o_ref[...] = r_ref[...] + _dot(x_ref[...], w_ref[...])


def _matmul_res(x, w, res, *, tm=512, tn=1024):
    M, K = x.shape
    N = w.shape[1]
    tm = _pick(M, tm, 16)
    tn = _pick(N, tn, LANES)
    return pl.pallas_call(
        _matmul_res_kernel,
        grid=(M // tm, N // tn),
        in_specs=[pl.BlockSpec((tm, K), lambda i, j: (i, 0)),
                  pl.BlockSpec((K, tn), lambda i, j: (0, j)),
                  pl.BlockSpec((tm, tn), lambda i, j: (i, j))],
        out_specs=pl.BlockSpec((tm, tn), lambda i, j: (i, j)),
        out_shape=jax.ShapeDtypeStruct((M, N), F32),
        compiler_params=_cparams("parallel", "arbitrary"),
        name="matmul_res",
    )(x, w, res)


def _rope_tables(pos, dim):
    half = dim // 2
    inv = jnp.power(ROPE_THETA, -jnp.arange(half, dtype=F32) / half)
    ang = pos.astype(F32)[:, None] * inv[None, :]
    cos, sin = jnp.cos(ang), jnp.sin(ang)
    reps = LANES // dim
    return (jnp.tile(jnp.concatenate([cos, cos], axis=-1), (1, reps)),
            jnp.tile(jnp.concatenate([-sin, sin], axis=-1), (1, reps)))


def _a_post_kernel(h_ref, c128_ref, s128_ref, c64_ref, s64_ref, qg_ref, kg_ref,
                   q_ref, kf_ref, kb_ref, vb_ref, qi_ref, kif_ref, kib_ref, wi_ref):
    c128, s128 = c128_ref[...], s128_ref[...]
    c64, s64 = c64_ref[...], s64_ref[...]
    tm = h_ref.shape[0]
    lane = lax.broadcasted_iota(I32, (tm, LANES), 1)

    def rope128(x):
        return x * c128 + pltpu.roll(x, HEAD_DIM // 2, axis=1) * s128

    def rope64(x):
        lo = (lane % IDX_DIM) < (IDX_DIM // 2)
        partner = jnp.where(lo, pltpu.roll(x, LANES - IDX_DIM // 2, axis=1),
                            pltpu.roll(x, IDX_DIM // 2, axis=1))
        return x * c64 + partner * s64

    for h in range(A_HEADS):
        x = h_ref[:, h * LANES:(h + 1) * LANES]
        q_ref[:, h * LANES:(h + 1) * LANES] = (
            rope128(_rms(x, qg_ref[...])) * (HEAD_DIM ** -0.5)).astype(BF16)
    for h in range(A_KV_HEADS):
        x = h_ref[:, A_OFF_K + h * LANES:A_OFF_K + (h + 1) * LANES]
        k = rope128(_rms(x, kg_ref[...]))
        kf_ref[:, h * LANES:(h + 1) * LANES] = k
        kb_ref[:, h * LANES:(h + 1) * LANES] = k.astype(BF16)
    vb_ref[...] = h_ref[:, A_OFF_V:A_OFF_V + A_KV].astype(BF16)
    for p in range(A_QI // LANES):
        x = rope64(h_ref[:, A_OFF_QI + p * LANES:A_OFF_QI + (p + 1) * LANES])
        qi_ref[:, (2 * p) * LANES:(2 * p + 1) * LANES] = (
            jnp.where(lane < IDX_DIM, x, 0.0).astype(BF16))
        qi_ref[:, (2 * p + 1) * LANES:(2 * p + 2) * LANES] = (
            jnp.where(lane >= IDX_DIM, x, 0.0).astype(BF16))
    ki = rope64(h_ref[:, A_OFF_KI:A_OFF_KI + LANES])
    kif_ref[...] = ki
    kib_ref[...] = (ki + pltpu.roll(ki, IDX_DIM, axis=1)).astype(BF16)
    wi_ref[...] = h_ref[:, A_OFF_WI:A_OFF_WI + LANES] * IDX_W_SCALE


def _a_post(h, tabs, qg, kg, *, tm=256):
    M = h.shape[0]
    P = tabs[0].shape[0]
    tm = _pick(math.gcd(M, P), tm, 16)
    nt = P // tm
    row = lambda i: (i, 0)
    tab = lambda i: (i % nt, 0)
    one = lambda i: (0, 0)
    widths = [(A_Q, BF16), (A_KV, F32), (A_KV, BF16), (A_KV, BF16),
              (IDX_HEADS * LANES, BF16), (LANES, F32), (LANES, BF16), (LANES, F32)]
    return pl.pallas_call(
        _a_post_kernel,
        grid=(M // tm,),
        in_specs=[pl.BlockSpec((tm, A_IN_PAD), row)]
        + [pl.BlockSpec((tm, LANES), tab)] * 4
        + [pl.BlockSpec((1, LANES), one)] * 2,
        out_specs=[pl.BlockSpec((tm, w), row) for w, _ in widths],
        out_shape=[jax.ShapeDtypeStruct((M, w), dt) for w, dt in widths],
        compiler_params=_cparams("parallel"),
        name="a_post",
    )(h, *tabs, qg, kg)


def _dsa_kernel(q_ref, qi_ref, wi_ref, k_ref, v_ref, ki_ref, o_ref,
                key_scr, bias_scr, qst_scr, m_scr, l_scr, acc_scr,
                *, tq, tk, ksel, causal, l_valid, n_chunks, idx_bits):
    t0 = pl.program_id(1) * tq
    nch = (t0 + tq + tk - 1) // tk if causal else n_chunks
    nslab = tk // LANES
    row = lax.broadcasted_iota(I32, (tq, tk), 0)
    col = lax.broadcasted_iota(I32, (tq, tk), 1)
    lane = lax.broadcasted_iota(I32, (tq, LANES), 1)
    limit = ((t0 + row) // CHUNK + 1) * CHUNK if causal else l_valid

    def chunk_off(c):
        return pl.multiple_of(c * tk, tk)

    wi = wi_ref[0]

    def score_body(c, carry):
        off = chunk_off(c)
        kic = ki_ref[0, pl.ds(off, tk), :]
        s = jnp.zeros((tq, tk), F32)
        for h in range(IDX_HEADS):
            sh = _dot_nt(qi_ref[0, :, h * LANES:(h + 1) * LANES], kic)
            s = s + wi[:, h:h + 1] * jnp.maximum(sh, 0.0)
        s = jnp.where(s == 0.0, 0.0, s)
        s = jnp.where(off + col < limit, s, NEG_INF)
        bits = pltpu.bitcast(s, I32)
        key_scr[:, pl.ds(off, tk)] = bits ^ ((bits >> 31) & 0x7FFFFFFF)
        return carry

    lax.fori_loop(0, nch, score_body, 0)

    def count(pred):
        def body(c, acc):
            off = chunk_off(c)
            blk = key_scr[:, pl.ds(off, tk)]
            for s in range(nslab):
                sl = slice(s * LANES, (s + 1) * LANES)
                acc = acc + pred(blk[:, sl], lane + (off + s * LANES))
            return acc

        acc = lax.fori_loop(0, nch, body, jnp.zeros((tq, LANES), F32))
        return jnp.broadcast_to(jnp.sum(acc, axis=-1, keepdims=True), (tq, LANES))

    def thr_body(i, thr):
        cand = thr + lax.shift_left(jnp.int32(1), 31 - i)
        cnt = count(lambda key, idx: jnp.where(key >= cand, 1.0, 0.0))
        return jnp.where(cnt >= ksel, cand, thr)

    thr = lax.fori_loop(0, 32, thr_body, jnp.full((tq, LANES), -2 ** 31, I32))

    need = ksel - count(lambda key, idx: jnp.where(key > thr, 1.0, 0.0))
    n_eq = count(lambda key, idx: jnp.where(key == thr, 1.0, 0.0))

    def cut_search():
        def body(i, cut):
            cand = cut + lax.shift_left(jnp.int32(1), idx_bits - 1 - i)
            below = count(lambda key, idx: jnp.where(key == thr, jnp.where(idx < cand, 1.0, 0.0), 0.0))
            return jnp.where(below <= need, cand, cut)

        return lax.fori_loop(0, idx_bits, body, jnp.zeros((tq, LANES), I32))

    cut = lax.cond(jnp.max(n_eq - need) > 0.0, cut_search,
                   lambda: jnp.full((tq, LANES), 2 ** idx_bits, I32))

    thr_t = jnp.tile(thr, (1, nslab))
    cut_t = jnp.tile(cut, (1, nslab))

    def bias_body(c, carry):
        off = chunk_off(c)
        key = key_scr[:, pl.ds(off, tk)]
        idx = off + col
        tie = jnp.where(key == thr_t, jnp.where(idx < cut_t, 0.0, NEG_INF), NEG_INF)
        bias = jnp.where(key > thr_t, 0.0, tie)
        bias_scr[:, pl.ds(off, tk)] = jnp.where(idx < limit, bias, NEG_INF)
        return carry

    lax.fori_loop(0, nch, bias_body, 0)

    for j in range(A_KV_HEADS):
        for g in range(A_GROUP):
            h = j * A_GROUP + g
            qst_scr[j, g * tq:(g + 1) * tq, :] = q_ref[0, :, h * LANES:(h + 1) * LANES]
    m_scr[...] = jnp.full(m_scr.shape, ROW_MAX_INIT, F32)
    l_scr[...] = jnp.zeros(l_scr.shape, F32)
    acc_scr[...] = jnp.zeros(acc_scr.shape, F32)
    gr = A_GROUP * tq

    def attn_body(c, carry):
        off = chunk_off(c)
        bias = bias_scr[:, pl.ds(off, tk)]
        bias4 = jnp.concatenate([bias] * A_GROUP, axis=0)
        for j in range(A_KV_HEADS):
            kc = k_ref[0, pl.ds(off, tk), j * LANES:(j + 1) * LANES]
            vc = v_ref[0, pl.ds(off, tk), j * LANES:(j + 1) * LANES]
            s = _dot_nt(qst_scr[j], kc) + bias4
            rows = slice(j * gr, (j + 1) * gr)
            m_old = m_scr[rows, :]
            m_new = jnp.maximum(m_old, jnp.max(s, axis=-1, keepdims=True))
            alpha = jnp.exp(m_old - m_new)
            p = jnp.exp(s - jnp.tile(m_new, (1, nslab)))
            l_scr[rows, :] = alpha * l_scr[rows, :] + jnp.sum(p, axis=-1, keepdims=True)
            acc_scr[rows, :] = alpha * acc_scr[rows, :] + _dot(p.astype(BF16), vc)
            m_scr[rows, :] = m_new
        return carry

    lax.fori_loop(0, nch, attn_body, 0)
    for h in range(A_HEADS):
        rows = slice(h * tq, (h + 1) * tq)
        o_ref[0, :, h * LANES:(h + 1) * LANES] = (acc_scr[rows, :] / l_scr[rows, :]).astype(BF16)


def _dsa(q, qi, wi, k, v, ki, *, tq, tk, ksel, causal, l_valid):
    B, T, _ = q.shape
    L = k.shape[1]
    blk_q = lambda w: pl.BlockSpec((1, tq, w), lambda b, i: (b, i, 0))
    blk_k = lambda w: pl.BlockSpec((1, L, w), lambda b, i: (b, 0, 0))
    kern = functools.partial(
        _dsa_kernel, tq=tq, tk=tk, ksel=ksel, causal=causal, l_valid=l_valid,
        n_chunks=-(-l_valid // tk), idx_bits=max(1, (L - 1).bit_length()) + 1)
    return pl.pallas_call(
        kern,
        grid=(B, T // tq),
        in_specs=[blk_q(A_Q), blk_q(IDX_HEADS * LANES), blk_q(LANES),
                  blk_k(A_KV), blk_k(A_KV), blk_k(LANES)],
        out_specs=blk_q(A_Q),
        out_shape=jax.ShapeDtypeStruct((B, T, A_Q), BF16),
        scratch_shapes=[pltpu.VMEM((tq, L), I32), pltpu.VMEM((tq, L), F32),
                        pltpu.VMEM((A_KV_HEADS, A_GROUP * tq, LANES), BF16),
                        pltpu.VMEM((A_HEADS * tq, LANES), F32),
                        pltpu.VMEM((A_HEADS * tq, LANES), F32),
                        pltpu.VMEM((A_HEADS * tq, LANES), F32)],
        compiler_params=_cparams("parallel", "arbitrary"),
        name="dsa_causal" if causal else "dsa_cached",
    )(q, qi, wi, k, v, ki)


def _pack_a_w_in(w):
    D = w.shape[0]
    o_ki = A_Q + 2 * A_KV + A_QI
    pad = lambda n: jnp.zeros((D, n), w.dtype)
    return jnp.concatenate(
        [w[:, :o_ki], w[:, o_ki:o_ki + IDX_DIM], pad(LANES - IDX_DIM),
         w[:, o_ki + IDX_DIM:], pad(LANES - IDX_HEADS)], axis=1).astype(BF16)


def _mixer_a(xp, xs, g, w_in, w_out, qg, kg, ck, cv, cki):
    B, S, D = xp.shape
    Bs, T, _ = xs.shape
    past = ck.shape[1]
    w_in = _pack_a_w_in(w_in)
    w_out = w_out.astype(BF16)
    g, qg, kg = g[None], qg[None], kg[None]

    def project(x, pos):
        b, t, _ = x.shape
        h = _rms_matmul(x.reshape(b * t, D), g, w_in, tn=A_IN_PAD // 5)
        tabs = _rope_tables(pos, HEAD_DIM) + _rope_tables(pos, IDX_DIM)
        outs = _a_post(h, tabs, qg, kg)
        v = h[:, A_OFF_V:A_OFF_V + A_KV]
        return [o.reshape(b, t, -1) for o in outs] + [v.reshape(b, t, A_KV_HEADS, HEAD_DIM)]

    q, kf, kb, vb, qi, kif, kib, wi, v_p = project(xp, jnp.arange(S))
    tk = _pick(S, 512, LANES)
    o = _dsa(q, qi, wi, kb, vb, kib, tq=min(128, S), tk=tk, ksel=min(TOPK_MAX, S // 4),
             causal=True, l_valid=S)
    yp = _matmul_res(o.reshape(B * S, A_Q), w_out, xp.reshape(B * S, D)).reshape(B, S, D)
    k_p = kf.reshape(B, S, A_KV_HEADS, HEAD_DIM)
    ki_p = kif[..., :IDX_DIM]

    pos_s = jnp.tile(past + jnp.arange(T), Bs)
    q, kf, kb, vb, qi, kif, kib, wi, v_s = project(xs, pos_s)
    L = past + T
    tk = 512
    pad = -(-L // tk) * tk - L
    cat = lambda c, n, w: jnp.concatenate(
        [c.reshape(Bs, past, w).astype(BF16), n, jnp.zeros((Bs, pad, w), BF16)], axis=1)
    o = _dsa(q, qi, wi, cat(ck, kb, A_KV), cat(cv, vb, A_KV),
             cat(jnp.concatenate([cki, cki], axis=-1), kib, LANES),
             tq=T, tk=tk, ksel=min(TOPK_MAX, L // 4), causal=False, l_valid=L)
    ys = _matmul_res(o.reshape(Bs * T, A_Q), w_out, xs.reshape(Bs * T, D)).reshape(Bs, T, D)
    k_s = kf.reshape(Bs, T, A_KV_HEADS, HEAD_DIM)
    return yp, ys, (k_p, v_p, ki_p, k_s, v_s, kif[..., :IDX_DIM])


def _c_post_kernel(h_ref, qg_ref, kg_ref, q_ref, kf_ref, kb_ref, vb_ref):
    W = C_HEADS * HEAD_DIM
    for h in range(C_HEADS):
        sl = slice(h * LANES, (h + 1) * LANES)
        q_ref[:, sl] = (_rms(h_ref[:, sl], qg_ref[...]) * (HEAD_DIM ** -0.5)).astype(BF16)
        k = _rms(h_ref[:, W + h * LANES:W + (h + 1) * LANES], kg_ref[...])
        kf_ref[:, sl] = k
        kb_ref[:, sl] = k.astype(BF16)
    vb_ref[...] = h_ref[:, 2 * W:3 * W].astype(BF16)


def _c_post(h, qg, kg, *, tm=256):
    M = h.shape[0]
    W = C_HEADS * HEAD_DIM
    tm = _pick(M, tm, 16)
    row = lambda i: (i, 0)
    dts = [BF16, F32, BF16, BF16]
    return pl.pallas_call(
        _c_post_kernel,
        grid=(M // tm,),
        in_specs=[pl.BlockSpec((tm, 3 * W), row),
                  pl.BlockSpec((1, LANES), lambda i: (0, 0)),
                  pl.BlockSpec((1, LANES), lambda i: (0, 0))],
        out_specs=[pl.BlockSpec((tm, W), row)] * 4,
        out_shape=[jax.ShapeDtypeStruct((M, W), dt) for dt in dts],
        compiler_params=_cparams("parallel"),
        name="c_post",
    )(h, qg, kg)


def _band_kernel(q_ref, k_ref, v_ref, b_ref, o_ref, *, sub, nsub, win, first_key):
    t0 = pl.program_id(2) * (sub * nsub)
    for s in range(nsub):
        start = pl.multiple_of(t0 + s * sub, sub)
        kw = k_ref[0, pl.ds(start, win), :]
        vw = v_ref[0, pl.ds(start, win), :]
        lg = _dot_nt(q_ref[0, s * sub:(s + 1) * sub, :], kw) + b_ref[0]
        if first_key is not None:
            kk = lax.broadcasted_iota(I32, (sub, win), 1)
            lg = jnp.where(kk >= first_key - start, lg, NEG_INF)
        p = jnp.exp(lg - jnp.max(lg, axis=-1, keepdims=True))
        o = _dot(p.astype(BF16), vw) / jnp.sum(p, axis=-1, keepdims=True)
        o_ref[0, s * sub:(s + 1) * sub, :] = o.astype(BF16)


def _band(q, k, v, bias, *, sub, nsub, first_key):
    B, T, W = q.shape
    Lk = k.shape[1]
    win = bias.shape[2]
    tq = sub * nsub
    kern = functools.partial(_band_kernel, sub=sub, nsub=nsub, win=win, first_key=first_key)
    return pl.pallas_call(
        kern,
        grid=(B, C_HEADS, T // tq),
        in_specs=[pl.BlockSpec((1, tq, LANES), lambda b, h, i: (b, i, h)),
                  pl.BlockSpec((1, Lk, LANES), lambda b, h, i: (b, 0, h)),
                  pl.BlockSpec((1, Lk, LANES), lambda b, h, i: (b, 0, h)),
                  pl.BlockSpec((1, sub, win), lambda b, h, i: (h, 0, 0))],
        out_specs=pl.BlockSpec((1, tq, LANES), lambda b, h, i: (b, i, h)),
        out_shape=jax.ShapeDtypeStruct((B, T, W), BF16),
        compiler_params=_cparams("parallel", "parallel", "arbitrary"),
        name="band_attn",
    )(q, k, v, bias)


def _mixer_c(xp, xs, g, w_in, w_out, qg, kg, rel, ck, cv):
    B, S, D = xp.shape
    Bs, T, _ = xs.shape
    W = C_HEADS * HEAD_DIM
    cp = ck.shape[1]
    w_in = w_in.astype(BF16)
    w_out = w_out.astype(BF16)
    g, qg, kg = g[None], qg[None], kg[None]
    relf = rel.astype(F32)

    def project(x):
        b, t, _ = x.shape
        h = _rms_matmul(x.reshape(b * t, D), g, w_in)
        q, kf, kb, vb = _c_post(h, qg, kg)
        vf = h[:, 2 * W:].reshape(b, t, C_HEADS, HEAD_DIM)
        return (q.reshape(b, t, W), kf.reshape(b, t, C_HEADS, HEAD_DIM),
                kb.reshape(b, t, W), vb.reshape(b, t, W), vf)

    sub = min(2 * CHUNK, S)
    win = C_REACH + sub
    r = jnp.arange(sub)[:, None]
    kk = jnp.arange(win)[None, :]
    bias = relf[:, jnp.clip(r + C_REACH - kk, -REL_CLIP, REL_CLIP) + REL_CLIP]
    lo = (r // CHUNK) * CHUNK
    bias = jnp.where((kk >= lo) & (kk < lo + C_REACH + CHUNK), bias, NEG_INF)
    q, kf, kb, vb, vf = project(xp)
    padk = lambda a: jnp.pad(a, ((0, 0), (C_REACH, 0), (0, 0)))
    o = _band(q, padk(kb), padk(vb), bias, sub=sub, nsub=_pick(S // sub, 4, 1),
              first_key=C_REACH)
    yp = _matmul_res(o.reshape(B * S, W), w_out, xp.reshape(B * S, D)).reshape(B, S, D)
    keep = min(C_REACH, S)
    k_p, v_p = kf[:, S - keep:], vf[:, S - keep:]

    L = cp + T
    win = -(-L // LANES) * LANES
    r = jnp.arange(T)[:, None]
    kk = jnp.arange(win)[None, :]
    bias = relf[:, jnp.clip(r + cp - kk, -REL_CLIP, REL_CLIP) + REL_CLIP]
    bias = jnp.where(kk < L, bias, NEG_INF)
    q, kf, kb, vb, vf = project(xs)
    cat = lambda c, n: jnp.concatenate(
        [c.reshape(Bs, cp, W).astype(BF16), n, jnp.zeros((Bs, win - L, W), BF16)], axis=1)
    o = _band(q, cat(ck, kb), cat(cv, vb), bias, sub=T, nsub=1, first_key=None)
    ys = _matmul_res(o.reshape(Bs * T, W), w_out, xs.reshape(Bs * T, D)).reshape(Bs, T, D)
    return yp, ys, (k_p, v_p, kf, vf)


def _s5_operators(a_re, a_im, log_dt, b_re, b_im, c_re, c_im):
    G, P = a_re.shape
    n = S5_CHUNK
    nlb = G // S5_LB_GROUPS
    dt = jnp.exp(log_dt.astype(F32))[:, None]
    ar, ai = a_re.astype(F32), a_im.astype(F32)
    mag = jnp.exp(ar * dt)
    lr, li = mag * jnp.cos(ai * dt), mag * jnp.sin(ai * dt)
    den = ar * ar + ai * ai
    nr = lr - 1.0
    fr = (nr * ar + li * ai) / den
    fi = (li * ar - nr * ai) / den
    br, bi = b_re.astype(F32), b_im.astype(F32)
    bbr = fr[..., None] * br - fi[..., None] * bi
    bbi = fr[..., None] * bi + fi[..., None] * br
    pr, pi = [jnp.ones_like(lr)], [jnp.zeros_like(li)]
    for _ in range(n):
        pr, pi = pr + [pr[-1] * lr - pi[-1] * li], pi + [pr[-1] * li + pi[-1] * lr]
    pr, pi = jnp.stack(pr), jnp.stack(pi)
    xr = pr[:n, :, :, None] * bbr[None] - pi[:n, :, :, None] * bbi[None]
    xi = pr[:n, :, :, None] * bbi[None] + pi[:n, :, :, None] * bbr[None]
    cr, ci = c_re.astype(F32), c_im.astype(F32)
    lag = jnp.einsum('tgpc,gdp->gtcd', xr, cr) - jnp.einsum('tgpc,gdp->gtcd', xi, ci)
    eye = jnp.eye(S5_LB_GROUPS, dtype=F32)
    t_in = jnp.arange(n)[:, None]
    t_out = jnp.arange(n)[None, :]
    toep = lag[:, jnp.clip(t_out - t_in, 0, n - 1)] * (t_out >= t_in)[None, :, :, None, None]
    toep = toep.reshape(nlb, S5_LB_GROUPS, n, n, S5_GROUP, S5_GROUP)
    toep = jnp.einsum('lgtucd,gh->ltgcuhd', toep, eye).reshape(nlb, n * LANES, n * LANES)

    def inc(x):
        w = x[::-1].reshape(n, nlb, S5_LB_GROUPS, P, S5_GROUP)
        return jnp.einsum('tlgpc,gh->ltgchp', w, eye).reshape(nlb, n * LANES, S5_LB_GROUPS * P)

    def out(x):
        w = x.reshape(n, nlb, S5_LB_GROUPS, S5_GROUP, P)
        return jnp.einsum('tlgcp,gh->lgpthc', w, eye).reshape(nlb, S5_LB_GROUPS * P, n * LANES)

    pr1, pi1 = pr[1:, :, None, :], pi[1:, :, None, :]
    w_out_re = out(cr[None] * pr1 - ci[None] * pi1)
    w_out_im = out(-(cr[None] * pi1 + ci[None] * pr1))
    lam_n = (pr[n].reshape(1, 1, G * P), pi[n].reshape(1, 1, G * P))
    bf = lambda a: a.astype(BF16)
    return bf(toep), bf(inc(xr)), bf(inc(xi)), bf(w_out_re), bf(w_out_im), lam_n


def _u_specs(tc, nlb):
    return [pl.BlockSpec((tc, LANES), functools.partial(lambda lb, i, t: (i, t * nlb + lb), t=t))
            for t in range(S5_CHUNK)]


def _s5_inc_kernel(*refs):
    u_refs, (wr_ref, wi_ref, sr_ref, si_ref) = refs[:S5_CHUNK], refs[S5_CHUNK:]
    u = jnp.concatenate([r[...].astype(BF16) for r in u_refs], axis=1)
    sr_ref[...] = _dot(u, wr_ref[0])
    si_ref[...] = _dot(u, wi_ref[0])


def _s5_inc(u2, w_re, w_im, *, tc):
    Nc = u2.shape[0]
    nlb, K, Wn = w_re.shape
    wspec = pl.BlockSpec((1, K, Wn), lambda lb, i: (lb, 0, 0))
    ospec = pl.BlockSpec((tc, Wn), lambda lb, i: (i, lb))
    return pl.pallas_call(
        _s5_inc_kernel,
        grid=(nlb, Nc // tc),
        in_specs=_u_specs(tc, nlb) + [wspec, wspec],
        out_specs=[ospec, ospec],
        out_shape=[jax.ShapeDtypeStruct((Nc, nlb * Wn), F32)] * 2,
        compiler_params=_cparams("parallel", "arbitrary"),
        name="s5_inc",
    )(*([u2] * S5_CHUNK), w_re, w_im)


def _s5_scan_kernel(sr_ref, si_ref, lr_ref, li_ref, h0r_ref, h0i_ref,
                    pr_ref, pi_ref, fr_ref, fi_ref):
    lr, li = lr_ref[...], li_ref[...]
    nck = sr_ref.shape[1]

    def body(k, carry):
        hr, hi = carry
        pr_ref[:, pl.ds(k, 1), :] = hr
        pi_ref[:, pl.ds(k, 1), :] = hi
        sr = sr_ref[:, pl.ds(k, 1), :]
        si = si_ref[:, pl.ds(k, 1), :]
        return lr * hr - li * hi + sr, lr * hi + li * hr + si

    hr, hi = lax.fori_loop(0, nck, body, (h0r_ref[...], h0i_ref[...]))
    fr_ref[...] = hr
    fi_ref[...] = hi


def _s5_scan(sr, si, lam, h0r, h0i, *, cb=1024):
    B, nck, GP = sr.shape
    cb = _pick(GP, cb, LANES)
    full = pl.BlockSpec((B, nck, cb), lambda j: (0, 0, j))
    vec = pl.BlockSpec((1, 1, cb), lambda j: (0, 0, j))
    st = pl.BlockSpec((B, 1, cb), lambda j: (0, 0, j))
    return pl.pallas_call(
        _s5_scan_kernel,
        grid=(GP // cb,),
        in_specs=[full, full, vec, vec, st, st],
        out_specs=[full, full, st, st],
        out_shape=[jax.ShapeDtypeStruct((B, nck, GP), F32)] * 2
        + [jax.ShapeDtypeStruct((B, 1, GP), F32)] * 2,
        compiler_params=_cparams("parallel"),
        name="s5_scan",
    )(sr, si, lam[0], lam[1], h0r, h0i)


def _s5_out_kernel(*refs):
    u_refs = refs[:S5_CHUNK]
    toep_ref, hr_ref, hi_ref, wr_ref, wi_ref, d_ref, o_ref = refs[S5_CHUNK:]
    u = jnp.concatenate([r[...] for r in u_refs], axis=1)
    y = _dot(u.astype(BF16), toep_ref[0])
    y = y + _dot(hr_ref[...].astype(BF16), wr_ref[0]) + _dot(hi_ref[...].astype(BF16), wi_ref[0])
    y = y + jnp.tile(d_ref[...], (1, S5_CHUNK)) * u
    o_ref[0] = jax.nn.gelu(y).astype(BF16)


def _s5_out(u2, toep, hr, hi, w_re, w_im, d, *, tc):
    Nc = u2.shape[0]
    nlb, K, _ = toep.shape
    Wn = w_re.shape[1]
    hspec = pl.BlockSpec((tc, Wn), lambda lb, i: (i, lb))
    return pl.pallas_call(
        _s5_out_kernel,
        grid=(nlb, Nc // tc),
        in_specs=_u_specs(tc, nlb)
        + [pl.BlockSpec((1, K, K), lambda lb, i: (lb, 0, 0)), hspec, hspec,
           pl.BlockSpec((1, Wn, K), lambda lb, i: (lb, 0, 0)),
           pl.BlockSpec((1, Wn, K), lambda lb, i: (lb, 0, 0)),
           pl.BlockSpec((1, LANES), lambda lb, i: (0, lb))],
        out_specs=pl.BlockSpec((1, tc, K), lambda lb, i: (lb, i, 0)),
        out_shape=jax.ShapeDtypeStruct((nlb, Nc, K), BF16),
        compiler_params=_cparams("parallel", "arbitrary"),
        name="s5_out",
    )(*([u2] * S5_CHUNK), toep, hr, hi, w_re, w_im, d)


def _glu_res_kernel(x_ref, wa_ref, wb_ref, r_ref, o_ref, xc_ref):
    @pl.when(pl.program_id(1) == 0)
    def _():
        for lb in range(x_ref.shape[0]):
            xc_ref[:, lb * LANES:(lb + 1) * LANES] = x_ref[lb]

    x = xc_ref[...]
    a = _dot(x, wa_ref[...])
    b = _dot(x, wb_ref[...])
    o_ref[...] = r_ref[...] + a * jax.nn.sigmoid(b)


def _glu_res(xl, w, res, *, tm=512, tn=512):
    nlb, M, _ = xl.shape
    D = nlb * LANES
    N = w.shape[1] // 2
    tm = _pick(M, tm, 16)
    tn = _pick(N, tn, LANES)
    nj = N // tn
    return pl.pallas_call(
        _glu_res_kernel,
        grid=(M // tm, nj),
        in_specs=[pl.BlockSpec((nlb, tm, LANES), lambda i, j: (0, i, 0)),
                  pl.BlockSpec((D, tn), lambda i, j: (0, j)),
                  pl.BlockSpec((D, tn), lambda i, j: (0, nj + j)),
                  pl.BlockSpec((tm, tn), lambda i, j: (i, j))],
        out_specs=pl.BlockSpec((tm, tn), lambda i, j: (i, j)),
        out_shape=jax.ShapeDtypeStruct((M, N), F32),
        scratch_shapes=[pltpu.VMEM((tm, D), BF16)],
        compiler_params=_cparams("parallel", "arbitrary"),
        name="glu_res",
    )(xl, w, w, res)


def _mixer_b(xp, xs, g, ops, d, w_glu, h0r, h0i):
    toep, inc_re, inc_im, out_re, out_im, lam = ops
    n = S5_CHUNK
    GP = lam[0].shape[-1]
    g, d = g[None], d.astype(F32)[None]
    w_glu = w_glu.astype(BF16)

    def stream(x, hr0, hi0):
        B, T, D = x.shape
        nck = T // n
        Nc = B * nck
        u2 = _rmsnorm(x.reshape(B * T, D), g).reshape(Nc, n * D)
        tc = _pick(Nc, 256, 8)
        sr, si = _s5_inc(u2, inc_re, inc_im, tc=tc)
        pr, pi, fr, fi = _s5_scan(sr.reshape(B, nck, GP), si.reshape(B, nck, GP), lam, hr0, hi0)
        gl = _s5_out(u2, toep, pr.reshape(Nc, GP), pi.reshape(Nc, GP), out_re, out_im, d, tc=tc)
        y = _glu_res(gl.reshape(gl.shape[0], B * T, LANES), w_glu, x.reshape(B * T, D))
        shp = (B, GP // S5_STATE, S5_STATE)
        return y.reshape(B, T, D), fr.reshape(shp), fi.reshape(shp)

    B = xp.shape[0]
    zero = jnp.zeros((B, 1, GP), F32)
    yp, rp, ip = stream(xp, zero, zero)
    Bs = xs.shape[0]
    ys, rs, is_ = stream(xs, h0r.astype(F32).reshape(Bs, 1, GP), h0i.astype(F32).reshape(Bs, 1, GP))
    return yp, ys, (rp, ip, rs, is_)


def _ffn_kernel(*refs, seq_tiles, seq_len, halo):
    if halo:
        (x_ref, xh_ref, g_ref, wa_ref, wv_ref, cwa_ref, cwv_ref, cba_ref, cbv_ref, wd_ref,
         o_ref, xn_ref, xhn_ref) = refs
    else:
        (x_ref, p1a_ref, p1v_ref, p2a_ref, p2v_ref, g_ref, wa_ref, wv_ref, cwa_ref, cwv_ref,
         cba_ref, cbv_ref, wd_ref, o_ref, xn_ref) = refs
    i, f = pl.program_id(0), pl.program_id(1)
    tm = x_ref.shape[0]

    @pl.when(f == 0)
    def _():
        x = x_ref[...]
        xn_ref[...] = _rms(x, g_ref[...]).astype(BF16)
        o_ref[...] = x
        if halo:
            xhn_ref[...] = _rms(xh_ref[...], g_ref[...]).astype(BF16)

    xn = xn_ref[...]
    tf = wa_ref.shape[1]
    r = lax.broadcasted_iota(I32, (tm, tf), 0)

    def conv(w_ref, cw_ref, cb_ref, p1_ref, p2_ref):
        h = _dot(xn, w_ref[...])
        h1 = pltpu.roll(h, 1, axis=0)
        h2 = pltpu.roll(h, 2, axis=0)
        if halo:
            hh = _dot(xhn_ref[...], w_ref[...])
            hh = hh * jnp.where(i % seq_tiles == 0, 0.0, 1.0)
            last, prev = hh[-1:, :], hh[-2:-1, :]
            h1 = jnp.where(r == 0, last, h1)
            h2 = jnp.where(r == 0, prev, jnp.where(r == 1, last, h2))
        else:
            t = r % seq_len
            h1 = jnp.where(t >= 1, h1, p1_ref[...])
            h2 = jnp.where(t >= 2, h2, p2_ref[...])
        cw = cw_ref[...]
        return cb_ref[...] + cw[0:1] * h2 + cw[1:2] * h1 + cw[2:3] * h

    if halo:
        ca = conv(wa_ref, cwa_ref, cba_ref, None, None)
        cv = conv(wv_ref, cwv_ref, cbv_ref, None, None)
    else:
        ca = conv(wa_ref, cwa_ref, cba_ref, p1a_ref, p2a_ref)
        cv = conv(wv_ref, cwv_ref, cbv_ref, p1v_ref, p2v_ref)
    act = (jax.nn.silu(ca) * cv).astype(BF16)
    o_ref[...] += _dot(act, wd_ref[...])


FFN_HALO = 16


def _ffn(x, g, w_up, conv_w, conv_b, w_down, *, seq_len, hist=None, tm=512, tf=1024):
    M, D = x.shape
    F = w_down.shape[0]
    halo = hist is None
    tm = _pick(seq_len, tm, FFN_HALO) if halo else M
    tf = _pick(F, tf, LANES)
    nf = F // tf
    xs = pl.BlockSpec((tm, D), lambda i, f: (i, 0))
    a_col = lambda i, f: (0, f)
    v_col = lambda i, f: (0, nf + f)
    wspecs = [pl.BlockSpec((1, D), lambda i, f: (0, 0)),
              pl.BlockSpec((D, tf), a_col), pl.BlockSpec((D, tf), v_col),
              pl.BlockSpec((CONV_W, tf), a_col), pl.BlockSpec((CONV_W, tf), v_col),
              pl.BlockSpec((1, tf), a_col), pl.BlockSpec((1, tf), v_col),
              pl.BlockSpec((tf, D), lambda i, f: (f, 0))]
    wargs = [g, w_up, w_up, conv_w, conv_w, conv_b, conv_b, w_down]
    scratch = [pltpu.VMEM((tm, D), BF16)]
    if halo:
        per = tm // FFN_HALO
        extra = [pl.BlockSpec((FFN_HALO, D), lambda i, f: (jnp.maximum(i * per - 1, 0), 0))]
        eargs = [x]
        scratch.append(pltpu.VMEM((FFN_HALO, D), BF16))
    else:
        ha = pl.BlockSpec((tm, tf), lambda i, f: (i, f))
        hv = pl.BlockSpec((tm, tf), lambda i, f: (i, nf + f))
        extra = [ha, hv, ha, hv]
        eargs = [hist[0], hist[0], hist[1], hist[1]]
    kern = functools.partial(_ffn_kernel, seq_tiles=max(seq_len // tm, 1), seq_len=seq_len,
                             halo=halo)
    return pl.pallas_call(
        kern,
        grid=(M // tm, nf),
        in_specs=[xs] + extra + wspecs,
        out_specs=pl.BlockSpec((tm, D), lambda i, f: (i, 0)),
        out_shape=jax.ShapeDtypeStruct((M, D), F32),
        scratch_shapes=scratch,
        compiler_params=_cparams("parallel", "arbitrary"),
        name="conv_ffn" if halo else "conv_ffn_hist",
    )(x, *eargs, *wargs)


def _conv_ffn(xp, xs, g, w_up, conv_w, conv_b, w_down, hist_s):
    B, S, D = xp.shape
    Bs, T, _ = xs.shape
    F2 = w_up.shape[1]
    g = g[None]
    w_up = w_up.astype(BF16)
    w_down = w_down.astype(BF16)
    conv_b = conv_b[None]
    yp = _ffn(xp.reshape(B * S, D), g, w_up, conv_w, conv_b, w_down, seq_len=S)
    hs = hist_s.astype(F32)
    z = jnp.zeros((Bs, T, F2), F32)
    p1 = z.at[:, 0].set(hs[:, -1]).reshape(Bs * T, F2)
    p2 = z.at[:, 0].set(hs[:, -2]).at[:, 1].set(hs[:, -1]).reshape(Bs * T, F2)
    ys = _ffn(xs.reshape(Bs * T, D), g, w_up, conv_w, conv_b, w_down, seq_len=T, hist=(p1, p2))
    tail = jnp.concatenate([xp[:, S - (CONV_W - 1):].reshape(-1, D),
                            xs[:, T - (CONV_W - 1):].reshape(-1, D)], axis=0)
    n_tail = tail.shape[0]
    tail = jnp.pad(tail, ((0, -n_tail % 16), (0, 0)))
    ht = _rms_matmul(tail, g, w_up)
    nb = B * (CONV_W - 1)
    return (yp.reshape(B, S, D), ys.reshape(Bs, T, D),
            ht[:nb].reshape(B, CONV_W - 1, F2), ht[nb:n_tail].reshape(Bs, CONV_W - 1, F2))


def kernel(x_prompt, x_sample, cache_a_k, cache_a_v, cache_a_kidx, cache_c_k, cache_c_v, state_b_re, state_b_im, state_ffn_conv, norm_mix, norm_ffn, a_w_in, a_w_out, a_q_norm, a_k_norm, b_a_re, b_a_im, b_log_dt, b_b_re, b_b_im, b_c_re, b_c_im, b_d, b_w_glu, c_w_in, c_w_out, c_q_norm, c_k_norm, c_rel_bias, ffn_w_up, ffn_conv_w, ffn_conv_b, ffn_w_down):
    xp, xs = x_prompt, x_sample
    depth = norm_mix.shape[0]
    outs_a, outs_b, outs_c, f_p, f_s = [], [], [], [], []
    for i in range(depth):
        li, kind = i // 3, i % 3
        if kind == 0:
            xp, xs, o = _mixer_a(xp, xs, norm_mix[i], a_w_in[li], a_w_out[li], a_q_norm[li],
                                 a_k_norm[li], cache_a_k[li], cache_a_v[li], cache_a_kidx[li])
            outs_a.append(o)
        elif kind == 1:
            ops = _s5_operators(b_a_re[li], b_a_im[li], b_log_dt[li], b_b_re[li], b_b_im[li],
                                b_c_re[li], b_c_im[li])
            xp, xs, o = _mixer_b(xp, xs, norm_mix[i], ops, b_d[li], b_w_glu[li],
                                 state_b_re[li], state_b_im[li])
            outs_b.append(o)
        else:
            xp, xs, o = _mixer_c(xp, xs, norm_mix[i], c_w_in[li], c_w_out[li], c_q_norm[li],
                                 c_k_norm[li], c_rel_bias[li], cache_c_k[li], cache_c_v[li])
            outs_c.append(o)
        xp, xs, hp, hs = _conv_ffn(xp, xs, norm_ffn[i], ffn_w_up[i], ffn_conv_w[i],
                                   ffn_conv_b[i], ffn_w_down[i], state_ffn_conv[i])
        f_p.append(hp)
        f_s.append(hs)
    stack = lambda outs, j: jnp.stack([o[j] for o in outs])
    return (xp, xs,
            *[stack(outs_a, j) for j in range(6)],
            *[stack(outs_c, j) for j in range(4)],
            *[stack(outs_b, j) for j in range(4)],
            jnp.stack(f_p), jnp.stack(f_s))
```

```python
import functools
import math

import jax
import jax.numpy as jnp
from jax import lax
from jax.experimental import pallas as pl
from jax.experimental.pallas import tpu as pltpu

F32 = jnp.float32
BF16 = jnp.bfloat16
I32 = jnp.int32

EPS = 1e-6
NEG_INF = -1e30
ROW_MAX_INIT = -1e29
ROPE_THETA = 10000.0
CHUNK = 64
HEAD_DIM = 128
LANES = 128

A_HEADS = 16
A_KV_HEADS = 4
A_GROUP = A_HEADS // A_KV_HEADS
IDX_HEADS = 8
IDX_DIM = 64
TOPK_MAX = 256
A_Q = A_HEADS * HEAD_DIM
A_KV = A_KV_HEADS * HEAD_DIM
A_QI = IDX_HEADS * IDX_DIM
IDX_W_SCALE = (IDX_HEADS * IDX_DIM) ** -0.5
A_OFF_K = A_Q
A_OFF_V = A_OFF_K + A_KV
A_OFF_QI = A_OFF_V + A_KV
A_OFF_KI = A_OFF_QI + A_QI
A_OFF_WI = A_OFF_KI + LANES
A_IN_PAD = A_OFF_WI + LANES

S5_GROUP = 16
S5_STATE = 64
S5_CHUNK = 16
S5_LB_GROUPS = LANES // S5_GROUP

C_HEADS = 16
C_BAND_CHUNKS = 8
C_REACH = C_BAND_CHUNKS * CHUNK
REL_CLIP = 128
CONV_W = 3

VMEM_LIMIT = 56 * 1024 * 1024


def _cparams(*sem):
    return pltpu.CompilerParams(dimension_semantics=sem, vmem_limit_bytes=VMEM_LIMIT)


def _pick(n, target, mult):
    best = None
    for t in range(mult, min(n, target) + 1, mult):
        if n % t == 0:
            best = t
    return best if best is not None else n


def _dot(a, b):
    return jnp.dot(a, b, preferred_element_type=F32)


def _dot_nt(a, b):
    return lax.dot_general(a, b, (((1,), (1,)), ((), ())), preferred_element_type=F32)


def _rms(x, g):
    return x * lax.rsqrt(jnp.mean(x * x, axis=-1, keepdims=True) + EPS) * g


def _rms_matmul_kernel(x_ref, g_ref, w_ref, o_ref, xn_ref):
    @pl.when(pl.program_id(1) == 0)
    def _():
        xn_ref[...] = _rms(x_ref[...], g_ref[...]).astype(BF16)

    o_ref[...] = _dot(xn_ref[...], w_ref[...])


def _rms_matmul(x, g, w, *, tm=512, tn=1024):
    M, D = x.shape
    N = w.shape[1]
    tm = _pick(M, tm, 16)
    tn = _pick(N, tn, LANES)
    return pl.pallas_call(
        _rms_matmul_kernel,
        grid=(M // tm, N // tn),
        in_specs=[pl.BlockSpec((tm, D), lambda i, j: (i, 0)),
                  pl.BlockSpec((1, D), lambda i, j: (0, 0)),
                  pl.BlockSpec((D, tn), lambda i, j: (0, j))],
        out_specs=pl.BlockSpec((tm, tn), lambda i, j: (i, j)),
        out_shape=jax.ShapeDtypeStruct((M, N), F32),
        scratch_shapes=[pltpu.VMEM((tm, D), BF16)],
        compiler_params=_cparams("parallel", "arbitrary"),
        name="rms_matmul",
    )(x, g, w)


def _rmsnorm_kernel(x_ref, g_ref, o_ref):
    o_ref[...] = _rms(x_ref[...], g_ref[...])


def _rmsnorm(x, g, *, tm=512):
    M, D = x.shape
    tm = _pick(M, tm, 8)
    return pl.pallas_call(
        _rmsnorm_kernel,
        grid=(M // tm,),
        in_specs=[pl.BlockSpec((tm, D), lambda i: (i, 0)),
                  pl.BlockSpec((1, D), lambda i: (0, 0))],
        out_specs=pl.BlockSpec((tm, D), lambda i: (i, 0)),
        out_shape=jax.ShapeDtypeStruct((M, D), F32),
        compiler_params=_cparams("parallel"),
        name="rmsnorm",
    )(x, g)


def _matmul_res_kernel(x_ref, w_ref, r_ref, o_ref):
    o_ref[...] = r_ref[...] + _dot(x_ref[...], w_ref[...])


def _matmul_res(x, w, res, *, tm=512, tn=1024):
    M, K = x.shape
    N = w.shape[1]
    tm = _pick(M, tm, 16)
    tn = _pick(N, tn, LANES)
    return pl.pallas_call(
        _matmul_res_kernel,
        grid=(M // tm, N // tn),
        in_specs=[pl.BlockSpec((tm, K), lambda i, j: (i, 0)),
                  pl.BlockSpec((K, tn), lambda i, j: (0, j)),
                  pl.BlockSpec((tm, tn), lambda i, j: (i, j))],
        out_specs=pl.BlockSpec((tm, tn), lambda i, j: (i, j)),
        out_shape=jax.ShapeDtypeStruct((M, N), F32),
        compiler_params=_cparams("parallel", "arbitrary"),
        name="matmul_res",
    )(x, w, res)


def _rope_tables(pos, dim):
    half = dim // 2
    inv = jnp.power(ROPE_THETA, -jnp.arange(half, dtype=F32) / half)
    ang = pos.astype(F32)[:, None] * inv[None, :]
    cos, sin = jnp.cos(ang), jnp.sin(ang)
    reps = LANES // dim
    return (jnp.tile(jnp.concatenate([cos, cos], axis=-1), (1, reps)),
            jnp.tile(jnp.concatenate([-sin, sin], axis=-1), (1, reps)))


def _a_post_kernel(h_ref, c128_ref, s128_ref, c64_ref, s64_ref, qg_ref, kg_ref,
                   q_ref, kf_ref, kb_ref, vb_ref, qi_ref, kif_ref, kib_ref, wi_ref):
    c128, s128 = c128_ref[...], s128_ref[...]
    c64, s64 = c64_ref[...], s64_ref[...]
    tm = h_ref.shape[0]
    lane = lax.broadcasted_iota(I32, (tm, LANES), 1)

    def rope128(x):
        return x * c128 + pltpu.roll(x, HEAD_DIM // 2, axis=1) * s128

    def rope64(x):
        lo = (lane % IDX_DIM) < (IDX_DIM // 2)
        partner = jnp.where(lo, pltpu.roll(x, LANES - IDX_DIM // 2, axis=1),
                            pltpu.roll(x, IDX_DIM // 2, axis=1))
        return x * c64 + partner * s64

    for h in range(A_HEADS):
        x = h_ref[:, h * LANES:(h + 1) * LANES]
        q_ref[:, h * LANES:(h + 1) * LANES] = (
            rope128(_rms(x, qg_ref[...])) * (HEAD_DIM ** -0.5)).astype(BF16)
    for h in range(A_KV_HEADS):
        x = h_ref[:, A_OFF_K + h * LANES:A_OFF_K + (h + 1) * LANES]
        k = rope128(_rms(x, kg_ref[...]))
        kf_ref[:, h * LANES:(h + 1) * LANES] = k
        kb_ref[:, h * LANES:(h + 1) * LANES] = k.astype(BF16)
    ones = jnp.ones((tm, LANES), BF16)
    for h in range(A_KV_HEADS):
        v = h_ref[:, A_OFF_V + h * LANES:A_OFF_V + (h + 1) * LANES]
        vb_ref[:, 2 * h * LANES:(2 * h + 1) * LANES] = v.astype(BF16)
        vb_ref[:, (2 * h + 1) * LANES:(2 * h + 2) * LANES] = ones
    for p in range(A_QI // LANES):
        x = rope64(h_ref[:, A_OFF_QI + p * LANES:A_OFF_QI + (p + 1) * LANES])
        qi_ref[:, (2 * p) * LANES:(2 * p + 1) * LANES] = (
            jnp.where(lane < IDX_DIM, x, 0.0).astype(BF16))
        qi_ref[:, (2 * p + 1) * LANES:(2 * p + 2) * LANES] = (
            jnp.where(lane >= IDX_DIM, x, 0.0).astype(BF16))
    ki = rope64(h_ref[:, A_OFF_KI:A_OFF_KI + LANES])
    kif_ref[...] = ki
    kib_ref[...] = (ki + pltpu.roll(ki, IDX_DIM, axis=1)).astype(BF16)
    wi_ref[...] = h_ref[:, A_OFF_WI:A_OFF_WI + LANES] * IDX_W_SCALE


def _a_post(h, tabs, qg, kg, *, tm=256):
    M = h.shape[0]
    P = tabs[0].shape[0]
    tm = _pick(math.gcd(M, P), tm, 16)
    nt = P // tm
    row = lambda i: (i, 0)
    tab = lambda i: (i % nt, 0)
    one = lambda i: (0, 0)
    widths = [(A_Q, BF16), (A_KV, F32), (A_KV, BF16), (2 * A_KV, BF16),
              (IDX_HEADS * LANES, BF16), (LANES, F32), (LANES, BF16), (LANES, F32)]
    return pl.pallas_call(
        _a_post_kernel,
        grid=(M // tm,),
        in_specs=[pl.BlockSpec((tm, A_IN_PAD), row)]
        + [pl.BlockSpec((tm, LANES), tab)] * 4
        + [pl.BlockSpec((1, LANES), one)] * 2,
        out_specs=[pl.BlockSpec((tm, w), row) for w, _ in widths],
        out_shape=[jax.ShapeDtypeStruct((M, w), dt) for w, dt in widths],
        compiler_params=_cparams("parallel"),
        name="a_post",
    )(h, *tabs, qg, kg)


def _dsa_kernel(q_ref, qi_ref, wi_ref, k_ref, v_ref, ki_ref, o_ref,
                key_scr, bias_scr, qst_scr, p_scr, m_scr, acc_scr,
                *, tq, tk, ksel, causal, l_valid, n_chunks, idx_bits):
    t0 = pl.program_id(1) * tq
    nch = (t0 + tq + tk - 1) // tk if causal else n_chunks
    nslab = tk // LANES
    row = lax.broadcasted_iota(I32, (tq, tk), 0)
    col = lax.broadcasted_iota(I32, (tq, tk), 1)
    lane = lax.broadcasted_iota(I32, (tq, LANES), 1)
    limit = ((t0 + row) // CHUNK + 1) * CHUNK if causal else l_valid

    def chunk_off(c):
        return pl.multiple_of(c * tk, tk)

    wi = wi_ref[0]

    def score_body(c, carry):
        off = chunk_off(c)
        kic = ki_ref[0, pl.ds(off, tk), :]
        s = jnp.zeros((tq, tk), F32)
        for h in range(IDX_HEADS):
            sh = _dot_nt(qi_ref[0, :, h * LANES:(h + 1) * LANES], kic)
            s = s + wi[:, h:h + 1] * jnp.maximum(sh, 0.0)
        s = jnp.where(s == 0.0, 0.0, s)
        s = jnp.where(off + col < limit, s, NEG_INF)
        bits = pltpu.bitcast(s, I32)
        key_scr[:, pl.ds(off, tk)] = bits ^ ((bits >> 31) & 0x7FFFFFFF)
        return carry

    lax.fori_loop(0, nch, score_body, 0)

    def count(pred):
        def body(c, acc):
            off = chunk_off(c)
            blk = key_scr[:, pl.ds(off, tk)]
            for s in range(nslab):
                sl = slice(s * LANES, (s + 1) * LANES)
                acc = acc + pred(blk[:, sl], lane + (off + s * LANES))
            return acc

        acc = lax.fori_loop(0, nch, body, jnp.zeros((tq, LANES), F32))
        return jnp.broadcast_to(jnp.sum(acc, axis=-1, keepdims=True), (tq, LANES))

    def thr_cond(st):
        i, _, n_ge = st
        return jnp.logical_and(i < 32, jnp.max(n_ge) > ksel)

    def thr_body(st):
        i, thr, n_ge = st
        cand = thr + lax.shift_left(jnp.int32(1), 31 - i)
        cnt = count(lambda key, idx: jnp.where(key >= cand, 1.0, 0.0))
        take = cnt >= ksel
        return i + 1, jnp.where(take, cand, thr), jnp.where(take, cnt, n_ge)

    n_keys = (nch * tk).astype(F32) if causal else float(nch * tk)
    _, thr, _ = lax.while_loop(
        thr_cond, thr_body,
        (jnp.int32(0), jnp.full((tq, LANES), -2 ** 31, I32), jnp.full((tq, LANES), n_keys, F32)))

    need = ksel - count(lambda key, idx: jnp.where(key > thr, 1.0, 0.0))
    n_eq = count(lambda key, idx: jnp.where(key == thr, 1.0, 0.0))

    def cut_search():
        def body(i, cut):
            cand = cut + lax.shift_left(jnp.int32(1), idx_bits - 1 - i)
            below = count(lambda key, idx: jnp.where(key == thr, jnp.where(idx < cand, 1.0, 0.0), 0.0))
            return jnp.where(below <= need, cand, cut)

        return lax.fori_loop(0, idx_bits, body, jnp.zeros((tq, LANES), I32))

    cut = lax.cond(jnp.max(n_eq - need) > 0.0, cut_search,
                   lambda: jnp.full((tq, LANES), 2 ** idx_bits, I32))

    thr_t = jnp.tile(thr, (1, nslab))
    cut_t = jnp.tile(cut, (1, nslab))

    def bias_body(c, carry):
        off = chunk_off(c)
        key = key_scr[:, pl.ds(off, tk)]
        idx = off + col
        tie = jnp.where(key == thr_t, jnp.where(idx < cut_t, 0.0, NEG_INF), NEG_INF)
        bias = jnp.where(key > thr_t, 0.0, tie)
        bias_scr[:, pl.ds(off, tk)] = jnp.where(idx < limit, bias, NEG_INF)
        return carry

    lax.fori_loop(0, nch, bias_body, 0)

    for j in range(A_KV_HEADS):
        for g in range(A_GROUP):
            h = j * A_GROUP + g
            qst_scr[j, g * tq:(g + 1) * tq, :] = q_ref[0, :, h * LANES:(h + 1) * LANES]
    m_scr[...] = jnp.full(m_scr.shape, ROW_MAX_INIT, F32)
    acc_scr[...] = jnp.zeros(acc_scr.shape, F32)
    gr = A_GROUP * tq

    def attn_body(c, carry):
        off = chunk_off(c)
        bias = bias_scr[:, pl.ds(off, tk)]
        for j in range(A_KV_HEADS):
            kc = k_ref[0, pl.ds(off, tk), j * LANES:(j + 1) * LANES]
            vc = v_ref[0, pl.ds(off, tk), 2 * j * LANES:2 * (j + 1) * LANES]
            s = _dot_nt(qst_scr[j], kc)
            rows = slice(j * gr, (j + 1) * gr)
            m_old = m_scr[rows, :]
            alphas = []
            for g in range(A_GROUP):
                sl = slice(g * tq, (g + 1) * tq)
                sg = s[sl, :] + bias
                m_new = jnp.maximum(m_old[sl, :], jnp.max(sg, axis=-1, keepdims=True))
                alphas.append(jnp.exp(m_old[sl, :] - m_new))
                p_scr[sl, :] = jnp.exp((sg - jnp.tile(m_new, (1, nslab))).astype(BF16))
                m_scr[j * gr + g * tq:j * gr + (g + 1) * tq, :] = m_new
            alpha = jnp.concatenate(alphas, axis=0)
            acc_scr[rows, :] = jnp.tile(alpha, (1, 2)) * acc_scr[rows, :] + _dot(p_scr[...], vc)
        return carry

    lax.fori_loop(0, nch, attn_body, 0)
    for h in range(A_HEADS):
        rows = slice(h * tq, (h + 1) * tq)
        o_ref[0, :, h * LANES:(h + 1) * LANES] = (
            acc_scr[rows, :LANES] / acc_scr[rows, LANES:]).astype(BF16)


def _dsa(q, qi, wi, k, v, ki, *, tq, tk, ksel, causal, l_valid):
    B, T, _ = q.shape
    L = k.shape[1]
    blk_q = lambda w: pl.BlockSpec((1, tq, w), lambda b, i: (b, i, 0))
    blk_k = lambda w: pl.BlockSpec((1, L, w), lambda b, i: (b, 0, 0))
    kern = functools.partial(
        _dsa_kernel, tq=tq, tk=tk, ksel=ksel, causal=causal, l_valid=l_valid,
        n_chunks=-(-l_valid // tk), idx_bits=max(1, (L - 1).bit_length()) + 1)
    return pl.pallas_call(
        kern,
        grid=(B, T // tq),
        in_specs=[blk_q(A_Q), blk_q(IDX_HEADS * LANES), blk_q(LANES),
                  blk_k(A_KV), blk_k(2 * A_KV), blk_k(LANES)],
        out_specs=blk_q(A_Q),
        out_shape=jax.ShapeDtypeStruct((B, T, A_Q), BF16),
        scratch_shapes=[pltpu.VMEM((tq, L), I32), pltpu.VMEM((tq, L), F32),
                        pltpu.VMEM((A_KV_HEADS, A_GROUP * tq, LANES), BF16),
                        pltpu.VMEM((A_GROUP * tq, tk), BF16),
                        pltpu.VMEM((A_HEADS * tq, LANES), F32),
                        pltpu.VMEM((A_HEADS * tq, 2 * LANES), F32)],
        compiler_params=_cparams("parallel", "arbitrary"),
        name="dsa_causal" if causal else "dsa_cached",
    )(q, qi, wi, k, v, ki)


def _pack_a_w_in(w):
    D = w.shape[0]
    o_ki = A_Q + 2 * A_KV + A_QI
    pad = lambda n: jnp.zeros((D, n), w.dtype)
    return jnp.concatenate(
        [w[:, :o_ki], w[:, o_ki:o_ki + IDX_DIM], pad(LANES - IDX_DIM),
         w[:, o_ki + IDX_DIM:], pad(LANES - IDX_HEADS)], axis=1).astype(BF16)


def _mixer_a(xp, xs, g, w_in, w_out, qg, kg, ck, cv, cki):
    B, S, D = xp.shape
    Bs, T, _ = xs.shape
    past = ck.shape[1]
    w_in = _pack_a_w_in(w_in)
    w_out = w_out.astype(BF16)
    g, qg, kg = g[None], qg[None], kg[None]

    def project(x, pos):
        b, t, _ = x.shape
        h = _rms_matmul(x.reshape(b * t, D), g, w_in, tn=A_IN_PAD // 5)
        tabs = _rope_tables(pos, HEAD_DIM) + _rope_tables(pos, IDX_DIM)
        outs = _a_post(h, tabs, qg, kg)
        v = h[:, A_OFF_V:A_OFF_V + A_KV]
        return [o.reshape(b, t, -1) for o in outs] + [v.reshape(b, t, A_KV_HEADS, HEAD_DIM)]

    q, kf, kb, vb, qi, kif, kib, wi, v_p = project(xp, jnp.arange(S))
    tk = _pick(S, 512, LANES)
    o = _dsa(q, qi, wi, kb, vb, kib, tq=min(128, S), tk=tk, ksel=min(TOPK_MAX, S // 4),
             causal=True, l_valid=S)
    yp = _matmul_res(o.reshape(B * S, A_Q), w_out, xp.reshape(B * S, D)).reshape(B, S, D)
    k_p = kf.reshape(B, S, A_KV_HEADS, HEAD_DIM)
    ki_p = kif[..., :IDX_DIM]

    pos_s = jnp.tile(past + jnp.arange(T), Bs)
    q, kf, kb, vb, qi, kif, kib, wi, v_s = project(xs, pos_s)
    L = past + T
    tk = 512
    pad = -(-L // tk) * tk - L
    cat = lambda c, n, w: jnp.concatenate(
        [c.reshape(Bs, past, w).astype(BF16), n, jnp.zeros((Bs, pad, w), BF16)], axis=1)
    cv1 = jnp.concatenate([cv, jnp.ones_like(cv)], axis=-1)
    o = _dsa(q, qi, wi, cat(ck, kb, A_KV), cat(cv1, vb, 2 * A_KV),
             cat(jnp.concatenate([cki, cki], axis=-1), kib, LANES),
             tq=T, tk=tk, ksel=min(TOPK_MAX, L // 4), causal=False, l_valid=L)
    ys = _matmul_res(o.reshape(Bs * T, A_Q), w_out, xs.reshape(Bs * T, D)).reshape(Bs, T, D)
    k_s = kf.reshape(Bs, T, A_KV_HEADS, HEAD_DIM)
    return yp, ys, (k_p, v_p, ki_p, k_s, v_s, kif[..., :IDX_DIM])


def _c_post_kernel(h_ref, qg_ref, kg_ref, q_ref, kf_ref, kb_ref, vb_ref):
    W = C_HEADS * HEAD_DIM
    for h in range(C_HEADS):
        sl = slice(h * LANES, (h + 1) * LANES)
        q_ref[:, sl] = (_rms(h_ref[:, sl], qg_ref[...]) * (HEAD_DIM ** -0.5)).astype(BF16)
        k = _rms(h_ref[:, W + h * LANES:W + (h + 1) * LANES], kg_ref[...])
        kf_ref[:, sl] = k
        kb_ref[:, sl] = k.astype(BF16)
    vb_ref[...] = h_ref[:, 2 * W:3 * W].astype(BF16)


def _c_post(h, qg, kg, *, tm=256):
    M = h.shape[0]
    W = C_HEADS * HEAD_DIM
    tm = _pick(M, tm, 16)
    row = lambda i: (i, 0)
    dts = [BF16, F32, BF16, BF16]
    return pl.pallas_call(
        _c_post_kernel,
        grid=(M // tm,),
        in_specs=[pl.BlockSpec((tm, 3 * W), row),
                  pl.BlockSpec((1, LANES), lambda i: (0, 0)),
                  pl.BlockSpec((1, LANES), lambda i: (0, 0))],
        out_specs=[pl.BlockSpec((tm, W), row)] * 4,
        out_shape=[jax.ShapeDtypeStruct((M, W), dt) for dt in dts],
        compiler_params=_cparams("parallel"),
        name="c_post",
    )(h, qg, kg)


def _band_kernel(q_ref, k_ref, v_ref, b_ref, o_ref, *, sub, nsub, win, first_key):
    t0 = pl.program_id(2) * (sub * nsub)
    for s in range(nsub):
        start = pl.multiple_of(t0 + s * sub, sub)
        kw = k_ref[0, pl.ds(start, win), :]
        vw = v_ref[0, pl.ds(start, win), :]
        lg = _dot_nt(q_ref[0, s * sub:(s + 1) * sub, :], kw) + b_ref[0]
        if first_key is not None:
            kk = lax.broadcasted_iota(I32, (sub, win), 1)
            lg = jnp.where(kk >= first_key - start, lg, NEG_INF)
        p = jnp.exp(lg - jnp.max(lg, axis=-1, keepdims=True))
        o = _dot(p.astype(BF16), vw) / jnp.sum(p, axis=-1, keepdims=True)
        o_ref[0, s * sub:(s + 1) * sub, :] = o.astype(BF16)


def _band(q, k, v, bias, *, sub, nsub, first_key):
    B, T, W = q.shape
    Lk = k.shape[1]
    win = bias.shape[2]
    tq = sub * nsub
    kern = functools.partial(_band_kernel, sub=sub, nsub=nsub, win=win, first_key=first_key)
    return pl.pallas_call(
        kern,
        grid=(B, C_HEADS, T // tq),
        in_specs=[pl.BlockSpec((1, tq, LANES), lambda b, h, i: (b, i, h)),
                  pl.BlockSpec((1, Lk, LANES), lambda b, h, i: (b, 0, h)),
                  pl.BlockSpec((1, Lk, LANES), lambda b, h, i: (b, 0, h)),
                  pl.BlockSpec((1, sub, win), lambda b, h, i: (h, 0, 0))],
        out_specs=pl.BlockSpec((1, tq, LANES), lambda b, h, i: (b, i, h)),
        out_shape=jax.ShapeDtypeStruct((B, T, W), BF16),
        compiler_params=_cparams("parallel", "parallel", "arbitrary"),
        name="band_attn",
    )(q, k, v, bias)


def _mixer_c(xp, xs, g, w_in, w_out, qg, kg, rel, ck, cv):
    B, S, D = xp.shape
    Bs, T, _ = xs.shape
    W = C_HEADS * HEAD_DIM
    cp = ck.shape[1]
    w_in = w_in.astype(BF16)
    w_out = w_out.astype(BF16)
    g, qg, kg = g[None], qg[None], kg[None]
    relf = rel.astype(F32)

    def project(x):
        b, t, _ = x.shape
        h = _rms_matmul(x.reshape(b * t, D), g, w_in)
        q, kf, kb, vb = _c_post(h, qg, kg)
        vf = h[:, 2 * W:].reshape(b, t, C_HEADS, HEAD_DIM)
        return (q.reshape(b, t, W), kf.reshape(b, t, C_HEADS, HEAD_DIM),
                kb.reshape(b, t, W), vb.reshape(b, t, W), vf)

    sub = min(2 * CHUNK, S)
    win = C_REACH + sub
    r = jnp.arange(sub)[:, None]
    kk = jnp.arange(win)[None, :]
    bias = relf[:, jnp.clip(r + C_REACH - kk, -REL_CLIP, REL_CLIP) + REL_CLIP]
    lo = (r // CHUNK) * CHUNK
    bias = jnp.where((kk >= lo) & (kk < lo + C_REACH + CHUNK), bias, NEG_INF)
    q, kf, kb, vb, vf = project(xp)
    padk = lambda a: jnp.pad(a, ((0, 0), (C_REACH, 0), (0, 0)))
    o = _band(q, padk(kb), padk(vb), bias, sub=sub, nsub=_pick(S // sub, 4, 1),
              first_key=C_REACH)
    yp = _matmul_res(o.reshape(B * S, W), w_out, xp.reshape(B * S, D)).reshape(B, S, D)
    keep = min(C_REACH, S)
    k_p, v_p = kf[:, S - keep:], vf[:, S - keep:]

    L = cp + T
    win = -(-L // LANES) * LANES
    r = jnp.arange(T)[:, None]
    kk = jnp.arange(win)[None, :]
    bias = relf[:, jnp.clip(r + cp - kk, -REL_CLIP, REL_CLIP) + REL_CLIP]
    bias = jnp.where(kk < L, bias, NEG_INF)
    q, kf, kb, vb, vf = project(xs)
    cat = lambda c, n: jnp.concatenate(
        [c.reshape(Bs, cp, W).astype(BF16), n, jnp.zeros((Bs, win - L, W), BF16)], axis=1)
    o = _band(q, cat(ck, kb), cat(cv, vb), bias, sub=T, nsub=1, first_key=None)
    ys = _matmul_res(o.reshape(Bs * T, W), w_out, xs.reshape(Bs * T, D)).reshape(Bs, T, D)
    return yp, ys, (k_p, v_p, kf, vf)


def _s5_operators(a_re, a_im, log_dt, b_re, b_im, c_re, c_im):
    G, P = a_re.shape
    n = S5_CHUNK
    nlb = G // S5_LB_GROUPS
    dt = jnp.exp(log_dt.astype(F32))[:, None]
    ar, ai = a_re.astype(F32), a_im.astype(F32)
    mag = jnp.exp(ar * dt)
    lr, li = mag * jnp.cos(ai * dt), mag * jnp.sin(ai * dt)
    den = ar * ar + ai * ai
    nr = lr - 1.0
    fr = (nr * ar + li * ai) / den
    fi = (li * ar - nr * ai) / den
    br, bi = b_re.astype(F32), b_im.astype(F32)
    bbr = fr[..., None] * br - fi[..., None] * bi
    bbi = fr[..., None] * bi + fi[..., None] * br
    pr, pi = [jnp.ones_like(lr)], [jnp.zeros_like(li)]
    for _ in range(n):
        pr, pi = pr + [pr[-1] * lr - pi[-1] * li], pi + [pr[-1] * li + pi[-1] * lr]
    pr, pi = jnp.stack(pr), jnp.stack(pi)
    xr = pr[:n, :, :, None] * bbr[None] - pi[:n, :, :, None] * bbi[None]
    xi = pr[:n, :, :, None] * bbi[None] + pi[:n, :, :, None] * bbr[None]
    cr, ci = c_re.astype(F32), c_im.astype(F32)
    lag = jnp.einsum('tgpc,gdp->gtcd', xr, cr) - jnp.einsum('tgpc,gdp->gtcd', xi, ci)
    eye = jnp.eye(S5_LB_GROUPS, dtype=F32)
    taps = jnp.einsum('lgtcd,gh->ltgchd',
                      lag.reshape(nlb, S5_LB_GROUPS, n, S5_GROUP, S5_GROUP), eye)
    taps = taps.reshape(nlb, n, LANES, LANES)

    def inc(x):
        w = x[::-1].reshape(n, nlb, S5_LB_GROUPS, P, S5_GROUP)
        return jnp.einsum('tlgpc,gh->ltgchp', w, eye).reshape(nlb, n * LANES, S5_LB_GROUPS * P)

    def out(x):
        w = x.reshape(n, nlb, S5_LB_GROUPS, S5_GROUP, P)
        return jnp.einsum('tlgcp,gh->lgpthc', w, eye).reshape(nlb, S5_LB_GROUPS * P, n * LANES)

    pr1, pi1 = pr[1:, :, None, :], pi[1:, :, None, :]
    w_out_re = out(cr[None] * pr1 - ci[None] * pi1)
    w_out_im = out(-(cr[None] * pi1 + ci[None] * pr1))
    lam_n = (pr[n].reshape(1, 1, G * P), pi[n].reshape(1, 1, G * P))
    bf = lambda a: a.astype(BF16)
    return bf(taps), bf(inc(xr)), bf(inc(xi)), bf(w_out_re), bf(w_out_im), lam_n


def _chunk_rows(u_ref, tc):
    return jnp.concatenate(
        [u_ref[pl.ds(t, tc, stride=S5_CHUNK), :] for t in range(S5_CHUNK)], axis=1)


def _s5_inc_kernel(u_ref, wr_ref, wi_ref, sr_ref, si_ref):
    u = _chunk_rows(u_ref, sr_ref.shape[0]).astype(BF16)
    sr_ref[...] = _dot(u, wr_ref[0])
    si_ref[...] = _dot(u, wi_ref[0])


def _s5_inc(u, w_re, w_im, *, tc):
    Nc = u.shape[0] // S5_CHUNK
    nlb, K, Wn = w_re.shape
    wspec = pl.BlockSpec((1, K, Wn), lambda lb, i: (lb, 0, 0))
    ospec = pl.BlockSpec((tc, Wn), lambda lb, i: (i, lb))
    return pl.pallas_call(
        _s5_inc_kernel,
        grid=(nlb, Nc // tc),
        in_specs=[pl.BlockSpec((tc * S5_CHUNK, LANES), lambda lb, i: (i, lb)), wspec, wspec],
        out_specs=[ospec, ospec],
        out_shape=[jax.ShapeDtypeStruct((Nc, nlb * Wn), F32)] * 2,
        compiler_params=_cparams("parallel", "arbitrary"),
        name="s5_inc",
    )(u, w_re, w_im)


def _s5_scan_kernel(sr_ref, si_ref, lr_ref, li_ref, h0r_ref, h0i_ref,
                    pr_ref, pi_ref, fr_ref, fi_ref):
    lr, li = lr_ref[...], li_ref[...]
    nck = sr_ref.shape[1]

    def body(k, carry):
        hr, hi = carry
        pr_ref[:, pl.ds(k, 1), :] = hr
        pi_ref[:, pl.ds(k, 1), :] = hi
        sr = sr_ref[:, pl.ds(k, 1), :]
        si = si_ref[:, pl.ds(k, 1), :]
        return lr * hr - li * hi + sr, lr * hi + li * hr + si

    hr, hi = lax.fori_loop(0, nck, body, (h0r_ref[...], h0i_ref[...]))
    fr_ref[...] = hr
    fi_ref[...] = hi


def _s5_scan(sr, si, lam, h0r, h0i, *, cb=1024):
    B, nck, GP = sr.shape
    cb = _pick(GP, cb, LANES)
    full = pl.BlockSpec((B, nck, cb), lambda j: (0, 0, j))
    vec = pl.BlockSpec((1, 1, cb), lambda j: (0, 0, j))
    st = pl.BlockSpec((B, 1, cb), lambda j: (0, 0, j))
    return pl.pallas_call(
        _s5_scan_kernel,
        grid=(GP // cb,),
        in_specs=[full, full, vec, vec, st, st],
        out_specs=[full, full, st, st],
        out_shape=[jax.ShapeDtypeStruct((B, nck, GP), F32)] * 2
        + [jax.ShapeDtypeStruct((B, 1, GP), F32)] * 2,
        compiler_params=_cparams("parallel"),
        name="s5_scan",
    )(sr, si, lam[0], lam[1], h0r, h0i)


def _s5_out_kernel(u_ref, taps_ref, hr_ref, hi_ref, wr_ref, wi_ref, d_ref, o_ref, toep_ref):
    n = S5_CHUNK
    tc = hr_ref.shape[0]

    @pl.when(pl.program_id(1) == 0)
    def _():
        zero = jnp.zeros((LANES, LANES), BF16)
        for t in range(n):
            for v in range(n):
                blk = taps_ref[0, v - t] if v >= t else zero
                toep_ref[t * LANES:(t + 1) * LANES, v * LANES:(v + 1) * LANES] = blk

    u = _chunk_rows(u_ref, tc)
    y = _dot(u.astype(BF16), toep_ref[...])
    y = y + _dot(hr_ref[...].astype(BF16), wr_ref[0]) + _dot(hi_ref[...].astype(BF16), wi_ref[0])
    y = jax.nn.gelu(y + jnp.tile(d_ref[...], (1, n)) * u)
    for t in range(n):
        o_ref[pl.ds(t, tc, stride=n), :] = y[:, t * LANES:(t + 1) * LANES]


def _s5_out(u, taps, hr, hi, w_re, w_im, d, *, tc):
    M, D = u.shape
    n = S5_CHUNK
    nlb = taps.shape[0]
    Wn, K = w_re.shape[1:]
    hspec = pl.BlockSpec((tc, Wn), lambda lb, i: (i, lb))
    uspec = pl.BlockSpec((tc * n, LANES), lambda lb, i: (i, lb))
    return pl.pallas_call(
        _s5_out_kernel,
        grid=(nlb, M // (tc * n)),
        in_specs=[uspec,
                  pl.BlockSpec((1, n, LANES, LANES), lambda lb, i: (lb, 0, 0, 0)), hspec, hspec,
                  pl.BlockSpec((1, Wn, K), lambda lb, i: (lb, 0, 0)),
                  pl.BlockSpec((1, Wn, K), lambda lb, i: (lb, 0, 0)),
                  pl.BlockSpec((1, LANES), lambda lb, i: (0, lb))],
        out_specs=uspec,
        out_shape=jax.ShapeDtypeStruct((M, D), F32),
        scratch_shapes=[pltpu.VMEM((K, K), BF16)],
        compiler_params=_cparams("parallel", "arbitrary"),
        name="s5_out",
    )(u, taps, hr, hi, w_re, w_im, d)


def _glu_res_kernel(x_ref, wa_ref, wb_ref, r_ref, o_ref, xc_ref):
    @pl.when(pl.program_id(1) == 0)
    def _():
        xc_ref[...] = x_ref[...].astype(BF16)

    x = xc_ref[...]
    a = _dot(x, wa_ref[...])
    b = _dot(x, wb_ref[...])
    o_ref[...] = r_ref[...] + a * jax.nn.sigmoid(b)


def _glu_res(x, w, res, *, tm=512, tn=512):
    M, D = x.shape
    N = w.shape[1] // 2
    tm = _pick(M, tm, 16)
    tn = _pick(N, tn, LANES)
    nj = N // tn
    return pl.pallas_call(
        _glu_res_kernel,
        grid=(M // tm, nj),
        in_specs=[pl.BlockSpec((tm, D), lambda i, j: (i, 0)),
                  pl.BlockSpec((D, tn), lambda i, j: (0, j)),
                  pl.BlockSpec((D, tn), lambda i, j: (0, nj + j)),
                  pl.BlockSpec((tm, tn), lambda i, j: (i, j))],
        out_specs=pl.BlockSpec((tm, tn), lambda i, j: (i, j)),
        out_shape=jax.ShapeDtypeStruct((M, N), F32),
        scratch_shapes=[pltpu.VMEM((tm, D), BF16)],
        compiler_params=_cparams("parallel", "arbitrary"),
        name="glu_res",
    )(x, w, w, res)


def _mixer_b(xp, xs, g, ops, d, w_glu, h0r, h0i):
    taps, inc_re, inc_im, out_re, out_im, lam = ops
    n = S5_CHUNK
    GP = lam[0].shape[-1]
    g, d = g[None], d.astype(F32)[None]
    w_glu = w_glu.astype(BF16)

    def stream(x, hr0, hi0):
        B, T, D = x.shape
        nck = T // n
        Nc = B * nck
        x2 = x.reshape(B * T, D)
        u = _rmsnorm(x2, g)
        tc = _pick(Nc, 256, 8)
        sr, si = _s5_inc(u, inc_re, inc_im, tc=tc)
        pr, pi, fr, fi = _s5_scan(sr.reshape(B, nck, GP), si.reshape(B, nck, GP), lam, hr0, hi0)
        gy = _s5_out(u, taps, pr.reshape(Nc, GP), pi.reshape(Nc, GP), out_re, out_im, d, tc=tc)
        y = _glu_res(gy, w_glu, x2)
        shp = (B, GP // S5_STATE, S5_STATE)
        return y.reshape(B, T, D), fr.reshape(shp), fi.reshape(shp)

    B = xp.shape[0]
    zero = jnp.zeros((B, 1, GP), F32)
    yp, rp, ip = stream(xp, zero, zero)
    Bs = xs.shape[0]
    ys, rs, is_ = stream(xs, h0r.astype(F32).reshape(Bs, 1, GP), h0i.astype(F32).reshape(Bs, 1, GP))
    return yp, ys, (rp, ip, rs, is_)


def _ffn_kernel(*refs, seq_tiles, seq_len, halo):
    if halo:
        (x_ref, xh_ref, g_ref, wa_ref, wv_ref, cwa_ref, cwv_ref, cba_ref, cbv_ref, wd_ref,
         o_ref, xn_ref, xhn_ref) = refs
    else:
        (x_ref, p1a_ref, p1v_ref, p2a_ref, p2v_ref, g_ref, wa_ref, wv_ref, cwa_ref, cwv_ref,
         cba_ref, cbv_ref, wd_ref, o_ref, xn_ref) = refs
    i, f = pl.program_id(0), pl.program_id(1)
    tm = x_ref.shape[0]

    @pl.when(f == 0)
    def _():
        x = x_ref[...]
        xn_ref[...] = _rms(x, g_ref[...]).astype(BF16)
        o_ref[...] = x
        if halo:
            xhn_ref[...] = _rms(xh_ref[...], g_ref[...]).astype(BF16)

    xn = xn_ref[...]
    tf = wa_ref.shape[1]
    r = lax.broadcasted_iota(I32, (tm, tf), 0)

    def conv(w_ref, cw_ref, cb_ref, p1_ref, p2_ref):
        h = _dot(xn, w_ref[...])
        h1 = pltpu.roll(h, 1, axis=0)
        h2 = pltpu.roll(h, 2, axis=0)
        if halo:
            hh = _dot(xhn_ref[...], w_ref[...])
            hh = hh * jnp.where(i % seq_tiles == 0, 0.0, 1.0)
            last, prev = hh[-1:, :], hh[-2:-1, :]
            h1 = jnp.where(r == 0, last, h1)
            h2 = jnp.where(r == 0, prev, jnp.where(r == 1, last, h2))
        else:
            t = r % seq_len
            h1 = jnp.where(t >= 1, h1, p1_ref[...])
            h2 = jnp.where(t >= 2, h2, p2_ref[...])
        cw = cw_ref[...]
        return cb_ref[...] + cw[0:1] * h2 + cw[1:2] * h1 + cw[2:3] * h

    if halo:
        ca = conv(wa_ref, cwa_ref, cba_ref, None, None)
        cv = conv(wv_ref, cwv_ref, cbv_ref, None, None)
    else:
        ca = conv(wa_ref, cwa_ref, cba_ref, p1a_ref, p2a_ref)
        cv = conv(wv_ref, cwv_ref, cbv_ref, p1v_ref, p2v_ref)
    act = (jax.nn.silu(ca) * cv).astype(BF16)
    o_ref[...] += _dot(act, wd_ref[...])


FFN_HALO = 16


def _ffn(x, g, w_up, conv_w, conv_b, w_down, *, seq_len, hist=None, tm=512, tf=1024):
    M, D = x.shape
    F = w_down.shape[0]
    halo = hist is None
    tm = _pick(seq_len, tm, FFN_HALO) if halo else M
    tf = _pick(F, tf, LANES)
    nf = F // tf
    xs = pl.BlockSpec((tm, D), lambda i, f: (i, 0))
    a_col = lambda i, f: (0, f)
    v_col = lambda i, f: (0, nf + f)
    wspecs = [pl.BlockSpec((1, D), lambda i, f: (0, 0)),
              pl.BlockSpec((D, tf), a_col), pl.BlockSpec((D, tf), v_col),
              pl.BlockSpec((CONV_W, tf), a_col), pl.BlockSpec((CONV_W, tf), v_col),
              pl.BlockSpec((1, tf), a_col), pl.BlockSpec((1, tf), v_col),
              pl.BlockSpec((tf, D), lambda i, f: (f, 0))]
    wargs = [g, w_up, w_up, conv_w, conv_w, conv_b, conv_b, w_down]
    scratch = [pltpu.VMEM((tm, D), BF16)]
    if halo:
        per = tm // FFN_HALO
        extra = [pl.BlockSpec((FFN_HALO, D), lambda i, f: (jnp.maximum(i * per - 1, 0), 0))]
        eargs = [x]
        scratch.append(pltpu.VMEM((FFN_HALO, D), BF16))
    else:
        ha = pl.BlockSpec((tm, tf), lambda i, f: (i, f))
        hv = pl.BlockSpec((tm, tf), lambda i, f: (i, nf + f))
        extra = [ha, hv, ha, hv]
        eargs = [hist[0], hist[0], hist[1], hist[1]]
    kern = functools.partial(_ffn_kernel, seq_tiles=max(seq_len // tm, 1), seq_len=seq_len,
                             halo=halo)
    return pl.pallas_call(
        kern,
        grid=(M // tm, nf),
        in_specs=[xs] + extra + wspecs,
        out_specs=pl.BlockSpec((tm, D), lambda i, f: (i, 0)),
        out_shape=jax.ShapeDtypeStruct((M, D), F32),
        scratch_shapes=scratch,
        compiler_params=_cparams("parallel", "arbitrary"),
        name="conv_ffn" if halo else "conv_ffn_hist",
    )(x, *eargs, *wargs)


def _conv_ffn(xp, xs, g, w_up, conv_w, conv_b, w_down, hist_s):
    B, S, D = xp.shape
    Bs, T, _ = xs.shape
    F2 = w_up.shape[1]
    g = g[None]
    w_up = w_up.astype(BF16)
    w_down = w_down.astype(BF16)
    conv_b = conv_b[None]
    yp = _ffn(xp.reshape(B * S, D), g, w_up, conv_w, conv_b, w_down, seq_len=S)
    hs = hist_s.astype(F32)
    z = jnp.zeros((Bs, T, F2), F32)
    p1 = z.at[:, 0].set(hs[:, -1]).reshape(Bs * T, F2)
    p2 = z.at[:, 0].set(hs[:, -2]).at[:, 1].set(hs[:, -1]).reshape(Bs * T, F2)
    ys = _ffn(xs.reshape(Bs * T, D), g, w_up, conv_w, conv_b, w_down, seq_len=T, hist=(p1, p2))
    tail = jnp.concatenate([xp[:, S - (CONV_W - 1):].reshape(-1, D),
                            xs[:, T - (CONV_W - 1):].reshape(-1, D)], axis=0)
    n_tail = tail.shape[0]
    tail = jnp.pad(tail, ((0, -n_tail % 16), (0, 0)))
    ht = _rms_matmul(tail, g, w_up)
    nb = B * (CONV_W - 1)
    return (yp.reshape(B, S, D), ys.reshape(Bs, T, D),
            ht[:nb].reshape(B, CONV_W - 1, F2), ht[nb:n_tail].reshape(Bs, CONV_W - 1, F2))


def kernel(x_prompt, x_sample, cache_a_k, cache_a_v, cache_a_kidx, cache_c_k, cache_c_v, state_b_re, state_b_im, state_ffn_conv, norm_mix, norm_ffn, a_w_in, a_w_out, a_q_norm, a_k_norm, b_a_re, b_a_im, b_log_dt, b_b_re, b_b_im, b_c_re, b_c_im, b_d, b_w_glu, c_w_in, c_w_out, c_q_norm, c_k_norm, c_rel_bias, ffn_w_up, ffn_conv_w, ffn_conv_b, ffn_w_down):
    xp, xs = x_prompt, x_sample
    depth = norm_mix.shape[0]
    outs_a, outs_b, outs_c, f_p, f_s = [], [], [], [], []
    for i in range(depth):
        li, kind = i // 3, i % 3
        if kind == 0:
            xp, xs, o = _mixer_a(xp, xs, norm_mix[i], a_w_in[li], a_w_out[li], a_q_norm[li],
                                 a_k_norm[li], cache_a_k[li], cache_a_v[li], cache_a_kidx[li])
            outs_a.append(o)
        elif kind == 1:
            ops = _s5_operators(b_a_re[li], b_a_im[li], b_log_dt[li], b_b_re[li], b_b_im[li],
                                b_c_re[li], b_c_im[li])
            xp, xs, o = _mixer_b(xp, xs, norm_mix[i], ops, b_d[li], b_w_glu[li],
                                 state_b_re[li], state_b_im[li])
            outs_b.append(o)
        else:
            xp, xs, o = _mixer_c(xp, xs, norm_mix[i], c_w_in[li], c_w_out[li], c_q_norm[li],
                                 c_k_norm[li], c_rel_bias[li], cache_c_k[li], cache_c_v[li])
            outs_c.append(o)
        xp, xs, hp, hs = _conv_ffn(xp, xs, norm_ffn[i], ffn_w_up[i], ffn_conv_w[i],
                                   ffn_conv_b[i], ffn_w_down[i], state_ffn_conv[i])
        f_p.append(hp)
        f_s.append(hs)
    stack = lambda outs, j: jnp.stack([o[j] for o in outs])
    return (xp, xs,
            *[stack(outs_a, j) for j in range(6)],
            *[stack(outs_c, j) for j in range(4)],
            *[stack(outs_b, j) for j in range(4)],
            jnp.stack(f_p), jnp.stack(f_s))
```

```python
import functools
import math

import jax
import jax.numpy as jnp
from jax import lax
from jax.experimental import pallas as pl
from jax.experimental.pallas import tpu as pltpu

F32 = jnp.float32
BF16 = jnp.bfloat16
I32 = jnp.int32

EPS = 1e-6
NEG_INF = -1e30
ROW_MAX_INIT = -1e29
ROPE_THETA = 10000.0
CHUNK = 64
HEAD_DIM = 128
LANES = 128

A_HEADS = 16
A_KV_HEADS = 4
A_GROUP = A_HEADS // A_KV_HEADS
IDX_HEADS = 8
IDX_DIM = 64
TOPK_MAX = 256
A_Q = A_HEADS * HEAD_DIM
A_KV = A_KV_HEADS * HEAD_DIM
A_QI = IDX_HEADS * IDX_DIM
IDX_W_SCALE = (IDX_HEADS * IDX_DIM) ** -0.5
A_OFF_K = A_Q
A_OFF_V = A_OFF_K + A_KV
A_OFF_QI = A_OFF_V + A_KV
A_OFF_KI = A_OFF_QI + A_QI
A_OFF_WI = A_OFF_KI + LANES
A_IN_PAD = A_OFF_WI + LANES

S5_GROUP = 16
S5_STATE = 64
S5_CHUNK = 16
S5_LB_GROUPS = LANES // S5_GROUP

C_HEADS = 16
C_BAND_CHUNKS = 8
C_REACH = C_BAND_CHUNKS * CHUNK
REL_CLIP = 128
CONV_W = 3

VMEM_LIMIT = 56 * 1024 * 1024


def _cparams(*sem):
    return pltpu.CompilerParams(dimension_semantics=sem, vmem_limit_bytes=VMEM_LIMIT)


def _pick(n, target, mult):
    best = None
    for t in range(mult, min(n, target) + 1, mult):
        if n % t == 0:
            best = t
    return best if best is not None else n


def _dot(a, b):
    return jnp.dot(a, b, preferred_element_type=F32)


def _dot_nt(a, b):
    return lax.dot_general(a, b, (((1,), (1,)), ((), ())), preferred_element_type=F32)


def _rms(x, g):
    return x * lax.rsqrt(jnp.mean(x * x, axis=-1, keepdims=True) + EPS) * g


def _rms_matmul_kernel(x_ref, g_ref, w_ref, o_ref, xn_ref):
    @pl.when(pl.program_id(1) == 0)
    def _():
        xn_ref[...] = _rms(x_ref[...], g_ref[...]).astype(BF16)

    o_ref[...] = _dot(xn_ref[...], w_ref[...])


def _rms_matmul(x, g, w, *, tm=1024, tn=1024):
    M, D = x.shape
    N = w.shape[1]
    tm = _pick(M, tm, 16)
    tn = _pick(N, tn, LANES)
    return pl.pallas_call(
        _rms_matmul_kernel,
        grid=(M // tm, N // tn),
        in_specs=[pl.BlockSpec((tm, D), lambda i, j: (i, 0)),
                  pl.BlockSpec((1, D), lambda i, j: (0, 0)),
                  pl.BlockSpec((D, tn), lambda i, j: (0, j))],
        out_specs=pl.BlockSpec((tm, tn), lambda i, j: (i, j)),
        out_shape=jax.ShapeDtypeStruct((M, N), F32),
        scratch_shapes=[pltpu.VMEM((tm, D), BF16)],
        compiler_params=_cparams("parallel", "arbitrary"),
        name="rms_matmul",
    )(x, g, w)


def _rmsnorm_kernel(x_ref, g_ref, o_ref):
    o_ref[...] = _rms(x_ref[...], g_ref[...])


def _rmsnorm(x, g, *, tm=512):
    M, D = x.shape
    tm = _pick(M, tm, 8)
    return pl.pallas_call(
        _rmsnorm_kernel,
        grid=(M // tm,),
        in_specs=[pl.BlockSpec((tm, D), lambda i: (i, 0)),
                  pl.BlockSpec((1, D), lambda i: (0, 0))],
        out_specs=pl.BlockSpec((tm, D), lambda i: (i, 0)),
        out_shape=jax.ShapeDtypeStruct((M, D), F32),
        compiler_params=_cparams("parallel"),
        name="rmsnorm",
    )(x, g)


def _matmul_res_kernel(x_ref, w_ref, r_ref, o_ref):
    o_ref[...] = r_ref[...] + _dot(x_ref[...], w_ref[...])


def _matmul_res(x, w, res, *, tm=512, tn=2048):
    M, K = x.shape
    N = w.shape[1]
    tm = _pick(M, tm, 16)
    tn = _pick(N, tn, LANES)
    return pl.pallas_call(
        _matmul_res_kernel,
        grid=(M // tm, N // tn),
        in_specs=[pl.BlockSpec((tm, K), lambda i, j: (i, 0)),
                  pl.BlockSpec((K, tn), lambda i, j: (0, j)),
                  pl.BlockSpec((tm, tn), lambda i, j: (i, j))],
        out_specs=pl.BlockSpec((tm, tn), lambda i, j: (i, j)),
        out_shape=jax.ShapeDtypeStruct((M, N), F32),
        compiler_params=_cparams("parallel", "arbitrary"),
        name="matmul_res",
    )(x, w, res)


def _rope_tables(pos, dim):
    half = dim // 2
    inv = jnp.power(ROPE_THETA, -jnp.arange(half, dtype=F32) / half)
    ang = pos.astype(F32)[:, None] * inv[None, :]
    cos, sin = jnp.cos(ang), jnp.sin(ang)
    reps = LANES // dim
    return (jnp.tile(jnp.concatenate([cos, cos], axis=-1), (1, reps)),
            jnp.tile(jnp.concatenate([-sin, sin], axis=-1), (1, reps)))


def _a_post_kernel(h_ref, c128_ref, s128_ref, c64_ref, s64_ref, qg_ref, kg_ref,
                   q_ref, kf_ref, kb_ref, vb_ref, qi_ref, kif_ref, kib_ref, wi_ref):
    c128, s128 = c128_ref[...], s128_ref[...]
    c64, s64 = c64_ref[...], s64_ref[...]
    tm = h_ref.shape[0]
    lane = lax.broadcasted_iota(I32, (tm, LANES), 1)

    def rope128(x):
        return x * c128 + pltpu.roll(x, HEAD_DIM // 2, axis=1) * s128

    def rope64(x):
        lo = (lane % IDX_DIM) < (IDX_DIM // 2)
        partner = jnp.where(lo, pltpu.roll(x, LANES - IDX_DIM // 2, axis=1),
                            pltpu.roll(x, IDX_DIM // 2, axis=1))
        return x * c64 + partner * s64

    for h in range(A_HEADS):
        x = h_ref[:, h * LANES:(h + 1) * LANES]
        q_ref[:, h * LANES:(h + 1) * LANES] = (
            rope128(_rms(x, qg_ref[...])) * (HEAD_DIM ** -0.5)).astype(BF16)
    for h in range(A_KV_HEADS):
        x = h_ref[:, A_OFF_K + h * LANES:A_OFF_K + (h + 1) * LANES]
        k = rope128(_rms(x, kg_ref[...]))
        kf_ref[:, h * LANES:(h + 1) * LANES] = k
        kb_ref[:, h * LANES:(h + 1) * LANES] = k.astype(BF16)
    ones = jnp.ones((tm, LANES), BF16)
    for h in range(A_KV_HEADS):
        v = h_ref[:, A_OFF_V + h * LANES:A_OFF_V + (h + 1) * LANES]
        vb_ref[:, 2 * h * LANES:(2 * h + 1) * LANES] = v.astype(BF16)
        vb_ref[:, (2 * h + 1) * LANES:(2 * h + 2) * LANES] = ones
    for p in range(A_QI // LANES):
        x = rope64(h_ref[:, A_OFF_QI + p * LANES:A_OFF_QI + (p + 1) * LANES])
        qi_ref[:, (2 * p) * LANES:(2 * p + 1) * LANES] = (
            jnp.where(lane < IDX_DIM, x, 0.0).astype(BF16))
        qi_ref[:, (2 * p + 1) * LANES:(2 * p + 2) * LANES] = (
            jnp.where(lane >= IDX_DIM, x, 0.0).astype(BF16))
    ki = rope64(h_ref[:, A_OFF_KI:A_OFF_KI + LANES])
    kif_ref[...] = ki
    kib_ref[...] = (ki + pltpu.roll(ki, IDX_DIM, axis=1)).astype(BF16)
    wi_ref[...] = h_ref[:, A_OFF_WI:A_OFF_WI + LANES] * IDX_W_SCALE


def _a_post(h, tabs, qg, kg, *, tm=256):
    M = h.shape[0]
    P = tabs[0].shape[0]
    tm = _pick(math.gcd(M, P), tm, 16)
    nt = P // tm
    row = lambda i: (i, 0)
    tab = lambda i: (i % nt, 0)
    one = lambda i: (0, 0)
    widths = [(A_Q, BF16), (A_KV, F32), (A_KV, BF16), (2 * A_KV, BF16),
              (IDX_HEADS * LANES, BF16), (LANES, F32), (LANES, BF16), (LANES, F32)]
    return pl.pallas_call(
        _a_post_kernel,
        grid=(M // tm,),
        in_specs=[pl.BlockSpec((tm, A_IN_PAD), row)]
        + [pl.BlockSpec((tm, LANES), tab)] * 4
        + [pl.BlockSpec((1, LANES), one)] * 2,
        out_specs=[pl.BlockSpec((tm, w), row) for w, _ in widths],
        out_shape=[jax.ShapeDtypeStruct((M, w), dt) for w, dt in widths],
        compiler_params=_cparams("parallel"),
        name="a_post",
    )(h, *tabs, qg, kg)


def _dsa_kernel(q_ref, qi_ref, wi_ref, k_ref, v_ref, ki_ref, o_ref,
                key_scr, bias_scr, qst_scr, s_scr, p_scr, m_scr, acc_scr,
                *, tq, tk, ksel, causal, l_valid, n_chunks, idx_bits):
    t0 = pl.program_id(1) * tq
    nch = (t0 + tq + tk - 1) // tk if causal else n_chunks
    nslab = tk // LANES
    row = lax.broadcasted_iota(I32, (tq, tk), 0)
    col = lax.broadcasted_iota(I32, (tq, tk), 1)
    lane = lax.broadcasted_iota(I32, (tq, LANES), 1)
    limit = ((t0 + row) // CHUNK + 1) * CHUNK if causal else l_valid

    def chunk_off(c):
        return pl.multiple_of(c * tk, tk)

    wi = wi_ref[0]

    def score_body(c, carry):
        off = chunk_off(c)
        kic = ki_ref[0, :, pl.ds(off, tk)]
        s = jnp.zeros((tq, tk), F32)
        for h in range(IDX_HEADS):
            sh = _dot(qi_ref[0, :, h * LANES:(h + 1) * LANES], kic)
            s = s + wi[:, h:h + 1] * jnp.maximum(sh, 0.0)
        s = jnp.where(s == 0.0, 0.0, s)
        s = jnp.where(off + col < limit, s, NEG_INF)
        bits = pltpu.bitcast(s, I32)
        key_scr[:, pl.ds(off, tk)] = bits ^ ((bits >> 31) & 0x7FFFFFFF)
        return carry

    lax.fori_loop(0, nch, score_body, 0)

    def count(pred):
        def body(c, acc):
            off = chunk_off(c)
            blk = key_scr[:, pl.ds(off, tk)]
            for s in range(nslab):
                sl = slice(s * LANES, (s + 1) * LANES)
                acc = acc + pred(blk[:, sl], lane + (off + s * LANES))
            return acc

        acc = lax.fori_loop(0, nch, body, jnp.zeros((tq, LANES), F32))
        return jnp.broadcast_to(jnp.sum(acc, axis=-1, keepdims=True), (tq, LANES))

    def thr_cond(st):
        i, _, n_ge = st
        return jnp.logical_and(i < 32, jnp.max(n_ge) > ksel)

    def thr_body(st):
        i, thr, n_ge = st
        cand = thr + lax.shift_left(jnp.int32(1), 31 - i)
        cnt = count(lambda key, idx: jnp.where(key >= cand, 1.0, 0.0))
        take = cnt >= ksel
        return i + 1, jnp.where(take, cand, thr), jnp.where(take, cnt, n_ge)

    n_keys = (nch * tk).astype(F32) if causal else float(nch * tk)
    _, thr, _ = lax.while_loop(
        thr_cond, thr_body,
        (jnp.int32(0), jnp.full((tq, LANES), -2 ** 31, I32), jnp.full((tq, LANES), n_keys, F32)))

    need = ksel - count(lambda key, idx: jnp.where(key > thr, 1.0, 0.0))
    n_eq = count(lambda key, idx: jnp.where(key == thr, 1.0, 0.0))

    def cut_search():
        def body(i, cut):
            cand = cut + lax.shift_left(jnp.int32(1), idx_bits - 1 - i)
            below = count(lambda key, idx: jnp.where(key == thr, jnp.where(idx < cand, 1.0, 0.0), 0.0))
            return jnp.where(below <= need, cand, cut)

        return lax.fori_loop(0, idx_bits, body, jnp.zeros((tq, LANES), I32))

    cut = lax.cond(jnp.max(n_eq - need) > 0.0, cut_search,
                   lambda: jnp.full((tq, LANES), 2 ** idx_bits, I32))

    thr_t = jnp.tile(thr, (1, nslab))
    cut_t = jnp.tile(cut, (1, nslab))

    def bias_body(c, carry):
        off = chunk_off(c)
        key = key_scr[:, pl.ds(off, tk)]
        idx = off + col
        tie = jnp.where(key == thr_t, jnp.where(idx < cut_t, 0.0, NEG_INF), NEG_INF)
        bias = jnp.where(key > thr_t, 0.0, tie)
        bias_scr[:, pl.ds(off, tk)] = jnp.where(idx < limit, bias, NEG_INF)
        return carry

    lax.fori_loop(0, nch, bias_body, 0)

    for j in range(A_KV_HEADS):
        for g in range(A_GROUP):
            h = j * A_GROUP + g
            qst_scr[j, g * tq:(g + 1) * tq, :] = q_ref[0, :, h * LANES:(h + 1) * LANES]
    m_scr[...] = jnp.full(m_scr.shape, ROW_MAX_INIT, F32)
    acc_scr[...] = jnp.zeros(acc_scr.shape, F32)
    gr = A_GROUP * tq

    def logits(c, slot):
        off = chunk_off(c)
        for j in range(A_KV_HEADS):
            s_scr[slot, j] = _dot(qst_scr[j], k_ref[0, j * LANES:(j + 1) * LANES, pl.ds(off, tk)])

    def softmax_pv(c, slot):
        off = chunk_off(c)
        bias = bias_scr[:, pl.ds(off, tk)]
        for j in range(A_KV_HEADS):
            vc = v_ref[0, pl.ds(off, tk), 2 * j * LANES:2 * (j + 1) * LANES]
            rows = slice(j * gr, (j + 1) * gr)
            m_old = m_scr[rows, :]
            alphas = []
            for g in range(A_GROUP):
                sl = slice(g * tq, (g + 1) * tq)
                sg = s_scr[slot, j, sl, :] + bias
                m_new = jnp.maximum(m_old[sl, :], jnp.max(sg, axis=-1, keepdims=True))
                alphas.append(jnp.exp(m_old[sl, :] - m_new))
                p_scr[j, sl, :] = jnp.exp((sg - jnp.tile(m_new, (1, nslab))).astype(BF16))
                m_scr[j * gr + g * tq:j * gr + (g + 1) * tq, :] = m_new
            alpha = jnp.concatenate(alphas, axis=0)
            acc_scr[rows, :] = jnp.tile(alpha, (1, 2)) * acc_scr[rows, :] + _dot(p_scr[j], vc)

    logits(0, 0)

    def pair_body(p, carry):
        logits(2 * p + 1, 1)
        softmax_pv(2 * p, 0)
        logits(2 * p + 2, 0)
        softmax_pv(2 * p + 1, 1)
        return carry

    n_pair = (nch - 1) // 2
    lax.fori_loop(0, n_pair, pair_body, 0)
    rest = nch - 2 * n_pair

    def tail_two():
        logits(nch - 1, 1)
        softmax_pv(nch - 2, 0)
        softmax_pv(nch - 1, 1)

    if causal:
        pl.when(rest == 2)(tail_two)
        pl.when(rest == 1)(lambda: softmax_pv(nch - 1, 0))
    elif rest == 2:
        tail_two()
    else:
        softmax_pv(nch - 1, 0)
    for h in range(A_HEADS):
        rows = slice(h * tq, (h + 1) * tq)
        o_ref[0, :, h * LANES:(h + 1) * LANES] = (
            acc_scr[rows, :LANES] / acc_scr[rows, LANES:]).astype(BF16)


def _dsa(q, qi, wi, k, v, ki, *, tq, tk, ksel, causal, l_valid):
    B, T, _ = q.shape
    L = k.shape[1]
    blk_q = lambda w: pl.BlockSpec((1, tq, w), lambda b, i: (b, i, 0))
    blk_k = lambda w: pl.BlockSpec((1, L, w), lambda b, i: (b, 0, 0))
    blk_t = lambda w: pl.BlockSpec((1, w, L), lambda b, i: (b, 0, 0))
    k_t, ki_t = jnp.swapaxes(k, 1, 2), jnp.swapaxes(ki, 1, 2)
    kern = functools.partial(
        _dsa_kernel, tq=tq, tk=tk, ksel=ksel, causal=causal, l_valid=l_valid,
        n_chunks=-(-l_valid // tk), idx_bits=max(1, (L - 1).bit_length()) + 1)
    return pl.pallas_call(
        kern,
        grid=(B, T // tq),
        in_specs=[blk_q(A_Q), blk_q(IDX_HEADS * LANES), blk_q(LANES),
                  blk_t(A_KV), blk_k(2 * A_KV), blk_t(LANES)],
        out_specs=blk_q(A_Q),
        out_shape=jax.ShapeDtypeStruct((B, T, A_Q), BF16),
        scratch_shapes=[pltpu.VMEM((tq, L), I32), pltpu.VMEM((tq, L), F32),
                        pltpu.VMEM((A_KV_HEADS, A_GROUP * tq, LANES), BF16),
                        pltpu.VMEM((2, A_KV_HEADS, A_GROUP * tq, tk), F32),
                        pltpu.VMEM((A_KV_HEADS, A_GROUP * tq, tk), BF16),
                        pltpu.VMEM((A_HEADS * tq, LANES), F32),
                        pltpu.VMEM((A_HEADS * tq, 2 * LANES), F32)],
        compiler_params=_cparams("parallel", "arbitrary"),
        name="dsa_causal" if causal else "dsa_cached",
    )(q, qi, wi, k_t, v, ki_t)


def _pack_a_w_in(w):
    D = w.shape[0]
    o_ki = A_Q + 2 * A_KV + A_QI
    pad = lambda n: jnp.zeros((D, n), w.dtype)
    return jnp.concatenate(
        [w[:, :o_ki], w[:, o_ki:o_ki + IDX_DIM], pad(LANES - IDX_DIM),
         w[:, o_ki + IDX_DIM:], pad(LANES - IDX_HEADS)], axis=1).astype(BF16)


def _mixer_a(xp, xs, g, w_in, w_out, qg, kg, ck, cv, cki):
    B, S, D = xp.shape
    Bs, T, _ = xs.shape
    past = ck.shape[1]
    w_in = _pack_a_w_in(w_in)
    w_out = w_out.astype(BF16)
    g, qg, kg = g[None], qg[None], kg[None]

    def project(x, pos):
        b, t, _ = x.shape
        h = _rms_matmul(x.reshape(b * t, D), g, w_in, tn=A_IN_PAD // 5)
        tabs = _rope_tables(pos, HEAD_DIM) + _rope_tables(pos, IDX_DIM)
        outs = _a_post(h, tabs, qg, kg)
        v = h[:, A_OFF_V:A_OFF_V + A_KV]
        return [o.reshape(b, t, -1) for o in outs] + [v.reshape(b, t, A_KV_HEADS, HEAD_DIM)]

    q, kf, kb, vb, qi, kif, kib, wi, v_p = project(xp, jnp.arange(S))
    tk = _pick(S, 512, LANES)
    o = _dsa(q, qi, wi, kb, vb, kib, tq=min(128, S), tk=tk, ksel=min(TOPK_MAX, S // 4),
             causal=True, l_valid=S)
    yp = _matmul_res(o.reshape(B * S, A_Q), w_out, xp.reshape(B * S, D)).reshape(B, S, D)
    k_p = kf.reshape(B, S, A_KV_HEADS, HEAD_DIM)
    ki_p = kif[..., :IDX_DIM]

    pos_s = jnp.tile(past + jnp.arange(T), Bs)
    q, kf, kb, vb, qi, kif, kib, wi, v_s = project(xs, pos_s)
    L = past + T
    tk = 512
    pad = -(-L // tk) * tk - L
    cat = lambda c, n, w: jnp.concatenate(
        [c.reshape(Bs, past, w).astype(BF16), n, jnp.zeros((Bs, pad, w), BF16)], axis=1)
    cv1 = jnp.concatenate([cv, jnp.ones_like(cv)], axis=-1)
    o = _dsa(q, qi, wi, cat(ck, kb, A_KV), cat(cv1, vb, 2 * A_KV),
             cat(jnp.concatenate([cki, cki], axis=-1), kib, LANES),
             tq=T, tk=tk, ksel=min(TOPK_MAX, L // 4), causal=False, l_valid=L)
    ys = _matmul_res(o.reshape(Bs * T, A_Q), w_out, xs.reshape(Bs * T, D)).reshape(Bs, T, D)
    k_s = kf.reshape(Bs, T, A_KV_HEADS, HEAD_DIM)
    return yp, ys, (k_p, v_p, ki_p, k_s, v_s, kif[..., :IDX_DIM])


def _c_post_kernel(h_ref, qg_ref, kg_ref, q_ref, kf_ref, kb_ref, vb_ref):
    W = C_HEADS * HEAD_DIM
    for h in range(C_HEADS):
        sl = slice(h * LANES, (h + 1) * LANES)
        q_ref[:, sl] = (_rms(h_ref[:, sl], qg_ref[...]) * (HEAD_DIM ** -0.5)).astype(BF16)
        k = _rms(h_ref[:, W + h * LANES:W + (h + 1) * LANES], kg_ref[...])
        kf_ref[:, sl] = k
        kb_ref[:, sl] = k.astype(BF16)
    vb_ref[...] = h_ref[:, 2 * W:3 * W].astype(BF16)


def _c_post(h, qg, kg, *, tm=256):
    M = h.shape[0]
    W = C_HEADS * HEAD_DIM
    tm = _pick(M, tm, 16)
    row = lambda i: (i, 0)
    dts = [BF16, F32, BF16, BF16]
    return pl.pallas_call(
        _c_post_kernel,
        grid=(M // tm,),
        in_specs=[pl.BlockSpec((tm, 3 * W), row),
                  pl.BlockSpec((1, LANES), lambda i: (0, 0)),
                  pl.BlockSpec((1, LANES), lambda i: (0, 0))],
        out_specs=[pl.BlockSpec((tm, W), row)] * 4,
        out_shape=[jax.ShapeDtypeStruct((M, W), dt) for dt in dts],
        compiler_params=_cparams("parallel"),
        name="c_post",
    )(h, qg, kg)


def _band_kernel(q_ref, k_ref, v_ref, b_ref, o_ref, *, sub, nsub, win, first_key):
    t0 = pl.program_id(2) * (sub * nsub)
    for s in range(nsub):
        start = pl.multiple_of(t0 + s * sub, sub)
        kw = k_ref[0, pl.ds(start, win), :]
        vw = v_ref[0, pl.ds(start, win), :]
        lg = _dot_nt(q_ref[0, s * sub:(s + 1) * sub, :], kw) + b_ref[0]
        if first_key is not None:
            kk = lax.broadcasted_iota(I32, (sub, win), 1)
            lg = jnp.where(kk >= first_key - start, lg, NEG_INF)
        p = jnp.exp(lg - jnp.max(lg, axis=-1, keepdims=True))
        o = _dot(p.astype(BF16), vw) / jnp.sum(p, axis=-1, keepdims=True)
        o_ref[0, s * sub:(s + 1) * sub, :] = o.astype(BF16)


def _band(q, k, v, bias, *, sub, nsub, first_key):
    B, T, W = q.shape
    Lk = k.shape[1]
    win = bias.shape[2]
    tq = sub * nsub
    kern = functools.partial(_band_kernel, sub=sub, nsub=nsub, win=win, first_key=first_key)
    return pl.pallas_call(
        kern,
        grid=(B, C_HEADS, T // tq),
        in_specs=[pl.BlockSpec((1, tq, LANES), lambda b, h, i: (b, i, h)),
                  pl.BlockSpec((1, Lk, LANES), lambda b, h, i: (b, 0, h)),
                  pl.BlockSpec((1, Lk, LANES), lambda b, h, i: (b, 0, h)),
                  pl.BlockSpec((1, sub, win), lambda b, h, i: (h, 0, 0))],
        out_specs=pl.BlockSpec((1, tq, LANES), lambda b, h, i: (b, i, h)),
        out_shape=jax.ShapeDtypeStruct((B, T, W), BF16),
        compiler_params=_cparams("parallel", "parallel", "arbitrary"),
        name="band_attn",
    )(q, k, v, bias)


def _mixer_c(xp, xs, g, w_in, w_out, qg, kg, rel, ck, cv):
    B, S, D = xp.shape
    Bs, T, _ = xs.shape
    W = C_HEADS * HEAD_DIM
    cp = ck.shape[1]
    w_in = w_in.astype(BF16)
    w_out = w_out.astype(BF16)
    g, qg, kg = g[None], qg[None], kg[None]
    relf = rel.astype(F32)

    def project(x, keep):
        b, t, _ = x.shape
        h = _rms_matmul(x.reshape(b * t, D), g, w_in)
        q, kf, kb, vb = _c_post(h, qg, kg)
        vf = h.reshape(b, t, 3 * W)[:, t - keep:, 2 * W:].reshape(b, keep, C_HEADS, HEAD_DIM)
        kf = kf.reshape(b, t, C_HEADS, HEAD_DIM)[:, t - keep:]
        return q.reshape(b, t, W), kf, kb.reshape(b, t, W), vb.reshape(b, t, W), vf

    def rel_bias(rows, win, shift):
        j = jnp.arange(rows + win - 1)
        vec = relf[:, jnp.clip(shift + rows - 1 - j, -REL_CLIP, REL_CLIP) + REL_CLIP]
        return jnp.stack([vec[:, rows - 1 - r:rows - 1 - r + win] for r in range(rows)], axis=1)

    sub = min(2 * CHUNK, S)
    win = C_REACH + sub
    r = jnp.arange(sub)[:, None]
    kk = jnp.arange(win)[None, :]
    lo = (r // CHUNK) * CHUNK
    bias = jnp.where((kk >= lo) & (kk < lo + C_REACH + CHUNK), rel_bias(sub, win, C_REACH),
                     NEG_INF)
    q, k_p, kb, vb, v_p = project(xp, min(C_REACH, S))
    padk = lambda a: jnp.pad(a, ((0, 0), (C_REACH, 0), (0, 0)))
    o = _band(q, padk(kb), padk(vb), bias, sub=sub, nsub=_pick(S // sub, 4, 1),
              first_key=C_REACH)
    yp = _matmul_res(o.reshape(B * S, W), w_out, xp.reshape(B * S, D)).reshape(B, S, D)

    L = cp + T
    win = -(-L // LANES) * LANES
    bias = jnp.where(jnp.arange(win)[None, :] < L, rel_bias(T, win, cp), NEG_INF)
    q, kf, kb, vb, vf = project(xs, T)
    cat = lambda c, n: jnp.concatenate(
        [c.reshape(Bs, cp, W).astype(BF16), n, jnp.zeros((Bs, win - L, W), BF16)], axis=1)
    o = _band(q, cat(ck, kb), cat(cv, vb), bias, sub=T, nsub=1, first_key=None)
    ys = _matmul_res(o.reshape(Bs * T, W), w_out, xs.reshape(Bs * T, D)).reshape(Bs, T, D)
    return yp, ys, (k_p, v_p, kf, vf)


def _s5_ops_kernel(ar_ref, ai_ref, ldt_ref, br_ref, bi_ref, cr_ref, ci_ref, cr4_ref, ci4_ref,
                   lag_ref, xr_ref, xi_ref, or_ref, oi_ref, lnr_ref, lni_ref, sr_ref, si_ref):
    n, P = xr_ref.shape[:2]
    dt = jnp.exp(ldt_ref[...])
    ar, ai = ar_ref[...], ai_ref[...]
    mag = jnp.exp(ar * dt)
    lr, li = mag * jnp.cos(ai * dt), mag * jnp.sin(ai * dt)
    den = ar * ar + ai * ai
    nr = lr - 1.0
    fr = (nr * ar + li * ai) / den
    fi = (li * ar - nr * ai) / den
    br, bi = br_ref[...], bi_ref[...]
    bbr = fr * br - fi * bi
    bbi = fr * bi + fi * br
    cr, ci = cr_ref[...], ci_ref[...]
    pr, pi = jnp.ones_like(lr), jnp.zeros_like(lr)
    for tau in range(n):
        sr_ref[...] = pr * bbr - pi * bbi
        si_ref[...] = pr * bbi + pi * bbr
        xr_ref[tau] = sr_ref[...].astype(BF16)
        xi_ref[tau] = si_ref[...].astype(BF16)

        def body(p, acc):
            return acc + cr4_ref[p] * sr_ref[p][None] - ci4_ref[p] * si_ref[p][None]

        lag_ref[tau] = lax.fori_loop(0, P, body, jnp.zeros(lag_ref.shape[1:], F32))
        pr, pi = pr * lr - pi * li, pr * li + pi * lr
        or_ref[tau] = (cr * pr - ci * pi).astype(BF16)
        oi_ref[tau] = (-(cr * pi + ci * pr)).astype(BF16)
    lnr_ref[...] = pr
    lni_ref[...] = pi


def _s5_operators(a_re, a_im, log_dt, b_re, b_im, c_re, c_im):
    G, P = a_re.shape
    C = S5_GROUP
    n = S5_CHUNK
    nlb = G // S5_LB_GROUPS
    f = lambda a: a.astype(F32)
    a_t = lambda a: f(a).T.reshape(P, 1, G)
    b_t = lambda a: jnp.transpose(f(a), (1, 2, 0))
    c_t = lambda a: jnp.transpose(f(a), (2, 1, 0))
    crt, cit = c_t(c_re), c_t(c_im)
    sds = lambda *shape: jax.ShapeDtypeStruct(shape, F32)
    lag, xr, xi, o_re, o_im, lnr, lni = pl.pallas_call(
        _s5_ops_kernel,
        out_shape=[sds(n, C, C, G)] + [jax.ShapeDtypeStruct((n, P, C, G), BF16)] * 4
        + [sds(P, 1, G)] * 2,
        scratch_shapes=[pltpu.VMEM((P, C, G), F32)] * 2,
        compiler_params=pltpu.CompilerParams(vmem_limit_bytes=VMEM_LIMIT),
        name="s5_ops",
    )(a_t(a_re), a_t(a_im), f(log_dt).reshape(1, 1, G), b_t(b_re), b_t(b_im), crt, cit,
      crt.reshape(P, C, 1, G), cit.reshape(P, C, 1, G))

    eye = jnp.eye(S5_LB_GROUPS, dtype=F32)
    split = lambda a: a.reshape(a.shape[:-1] + (nlb, S5_LB_GROUPS))
    taps = jnp.einsum('tdclg,gh->ltgchd', split(lag), eye).reshape(nlb, n, LANES, LANES)

    def inc(x):
        return jnp.einsum('tpclg,gh->ltgchp', split(x[::-1]), eye).reshape(
            nlb, n * LANES, S5_LB_GROUPS * P)

    def out(x):
        return jnp.einsum('tpclg,gh->lgpthc', split(x), eye).reshape(
            nlb, S5_LB_GROUPS * P, n * LANES)

    lam_n = tuple(a.reshape(P, G).T.reshape(1, 1, G * P) for a in (lnr, lni))
    bf = lambda a: a.astype(BF16)
    return bf(taps), bf(inc(xr)), bf(inc(xi)), bf(out(o_re)), bf(out(o_im)), lam_n


def _chunk_rows(u_ref, tc):
    return jnp.concatenate(
        [u_ref[pl.ds(t, tc, stride=S5_CHUNK), :] for t in range(S5_CHUNK)], axis=1)


def _s5_inc_kernel(u_ref, wr_ref, wi_ref, sr_ref, si_ref):
    u = _chunk_rows(u_ref, sr_ref.shape[0]).astype(BF16)
    sr_ref[...] = _dot(u, wr_ref[0])
    si_ref[...] = _dot(u, wi_ref[0])


def _s5_inc(u, w_re, w_im, *, tc):
    Nc = u.shape[0] // S5_CHUNK
    nlb, K, Wn = w_re.shape
    wspec = pl.BlockSpec((1, K, Wn), lambda lb, i: (lb, 0, 0))
    ospec = pl.BlockSpec((tc, Wn), lambda lb, i: (i, lb))
    return pl.pallas_call(
        _s5_inc_kernel,
        grid=(nlb, Nc // tc),
        in_specs=[pl.BlockSpec((tc * S5_CHUNK, LANES), lambda lb, i: (i, lb)), wspec, wspec],
        out_specs=[ospec, ospec],
        out_shape=[jax.ShapeDtypeStruct((Nc, nlb * Wn), F32)] * 2,
        compiler_params=_cparams("parallel", "arbitrary"),
        name="s5_inc",
    )(u, w_re, w_im)


def _s5_scan_kernel(sr_ref, si_ref, lr_ref, li_ref, h0r_ref, h0i_ref,
                    pr_ref, pi_ref, fr_ref, fi_ref):
    lr, li = lr_ref[...], li_ref[...]
    nck = sr_ref.shape[1]

    def body(k, carry):
        hr, hi = carry
        pr_ref[:, pl.ds(k, 1), :] = hr
        pi_ref[:, pl.ds(k, 1), :] = hi
        sr = sr_ref[:, pl.ds(k, 1), :]
        si = si_ref[:, pl.ds(k, 1), :]
        return lr * hr - li * hi + sr, lr * hi + li * hr + si

    hr, hi = lax.fori_loop(0, nck, body, (h0r_ref[...], h0i_ref[...]))
    fr_ref[...] = hr
    fi_ref[...] = hi


def _s5_scan(sr, si, lam, h0r, h0i, *, cb=1024):
    B, nck, GP = sr.shape
    cb = _pick(GP, cb, LANES)
    full = pl.BlockSpec((B, nck, cb), lambda j: (0, 0, j))
    vec = pl.BlockSpec((1, 1, cb), lambda j: (0, 0, j))
    st = pl.BlockSpec((B, 1, cb), lambda j: (0, 0, j))
    return pl.pallas_call(
        _s5_scan_kernel,
        grid=(GP // cb,),
        in_specs=[full, full, vec, vec, st, st],
        out_specs=[full, full, st, st],
        out_shape=[jax.ShapeDtypeStruct((B, nck, GP), F32)] * 2
        + [jax.ShapeDtypeStruct((B, 1, GP), F32)] * 2,
        compiler_params=_cparams("parallel"),
        name="s5_scan",
    )(sr, si, lam[0], lam[1], h0r, h0i)


def _s5_out_kernel(u_ref, taps_ref, hr_ref, hi_ref, wr_ref, wi_ref, d_ref, o_ref, toep_ref):
    n = S5_CHUNK
    tc = hr_ref.shape[0]

    @pl.when(pl.program_id(1) == 0)
    def _():
        zero = jnp.zeros((LANES, LANES), BF16)
        for t in range(n):
            for v in range(n):
                blk = taps_ref[0, v - t] if v >= t else zero
                toep_ref[t * LANES:(t + 1) * LANES, v * LANES:(v + 1) * LANES] = blk

    u = _chunk_rows(u_ref, tc)
    y = _dot(u.astype(BF16), toep_ref[...])
    y = y + _dot(hr_ref[...].astype(BF16), wr_ref[0]) + _dot(hi_ref[...].astype(BF16), wi_ref[0])
    y = jax.nn.gelu(y + jnp.tile(d_ref[...], (1, n)) * u)
    for t in range(n):
        o_ref[pl.ds(t, tc, stride=n), :] = y[:, t * LANES:(t + 1) * LANES]


def _s5_out(u, taps, hr, hi, w_re, w_im, d, *, tc):
    M, D = u.shape
    n = S5_CHUNK
    nlb = taps.shape[0]
    Wn, K = w_re.shape[1:]
    hspec = pl.BlockSpec((tc, Wn), lambda lb, i: (i, lb))
    uspec = pl.BlockSpec((tc * n, LANES), lambda lb, i: (i, lb))
    return pl.pallas_call(
        _s5_out_kernel,
        grid=(nlb, M // (tc * n)),
        in_specs=[uspec,
                  pl.BlockSpec((1, n, LANES, LANES), lambda lb, i: (lb, 0, 0, 0)), hspec, hspec,
                  pl.BlockSpec((1, Wn, K), lambda lb, i: (lb, 0, 0)),
                  pl.BlockSpec((1, Wn, K), lambda lb, i: (lb, 0, 0)),
                  pl.BlockSpec((1, LANES), lambda lb, i: (0, lb))],
        out_specs=uspec,
        out_shape=jax.ShapeDtypeStruct((M, D), F32),
        scratch_shapes=[pltpu.VMEM((K, K), BF16)],
        compiler_params=_cparams("parallel", "arbitrary"),
        name="s5_out",
    )(u, taps, hr, hi, w_re, w_im, d)


def _glu_res_kernel(x_ref, wa_ref, wb_ref, r_ref, o_ref, xc_ref):
    @pl.when(pl.program_id(1) == 0)
    def _():
        xc_ref[...] = x_ref[...].astype(BF16)

    x = xc_ref[...]
    a = _dot(x, wa_ref[...])
    b = _dot(x, wb_ref[...])
    o_ref[...] = r_ref[...] + a * jax.nn.sigmoid(b)


def _glu_res(x, w, res, *, tm=1024, tn=512):
    M, D = x.shape
    N = w.shape[1] // 2
    tm = _pick(M, tm, 16)
    tn = _pick(N, tn, LANES)
    nj = N // tn
    return pl.pallas_call(
        _glu_res_kernel,
        grid=(M // tm, nj),
        in_specs=[pl.BlockSpec((tm, D), lambda i, j: (i, 0)),
                  pl.BlockSpec((D, tn), lambda i, j: (0, j)),
                  pl.BlockSpec((D, tn), lambda i, j: (0, nj + j)),
                  pl.BlockSpec((tm, tn), lambda i, j: (i, j))],
        out_specs=pl.BlockSpec((tm, tn), lambda i, j: (i, j)),
        out_shape=jax.ShapeDtypeStruct((M, N), F32),
        scratch_shapes=[pltpu.VMEM((tm, D), BF16)],
        compiler_params=_cparams("parallel", "arbitrary"),
        name="glu_res",
    )(x, w, w, res)


def _mixer_b(xp, xs, g, ops, d, w_glu, h0r, h0i):
    taps, inc_re, inc_im, out_re, out_im, lam = ops
    n = S5_CHUNK
    GP = lam[0].shape[-1]
    g, d = g[None], d.astype(F32)[None]
    w_glu = w_glu.astype(BF16)

    def stream(x, hr0, hi0):
        B, T, D = x.shape
        nck = T // n
        Nc = B * nck
        x2 = x.reshape(B * T, D)
        u = _rmsnorm(x2, g)
        tc = _pick(Nc, 256, 8)
        sr, si = _s5_inc(u, inc_re, inc_im, tc=tc)
        pr, pi, fr, fi = _s5_scan(sr.reshape(B, nck, GP), si.reshape(B, nck, GP), lam, hr0, hi0)
        gy = _s5_out(u, taps, pr.reshape(Nc, GP), pi.reshape(Nc, GP), out_re, out_im, d, tc=tc)
        y = _glu_res(gy, w_glu, x2)
        shp = (B, GP // S5_STATE, S5_STATE)
        return y.reshape(B, T, D), fr.reshape(shp), fi.reshape(shp)

    B = xp.shape[0]
    zero = jnp.zeros((B, 1, GP), F32)
    yp, rp, ip = stream(xp, zero, zero)
    Bs = xs.shape[0]
    ys, rs, is_ = stream(xs, h0r.astype(F32).reshape(Bs, 1, GP), h0i.astype(F32).reshape(Bs, 1, GP))
    return yp, ys, (rp, ip, rs, is_)


def _ffn_kernel(*refs, seq_tiles, seq_len, halo, n_sub):
    if halo:
        (x_ref, xh_ref, g_ref, wa_ref, wv_ref, cwa_ref, cwv_ref, cba_ref, cbv_ref, wd_ref,
         o_ref, xn_ref, xhn_ref) = refs
    else:
        (x_ref, p1a_ref, p1v_ref, p2a_ref, p2v_ref, g_ref, wa_ref, wv_ref, cwa_ref, cwv_ref,
         cba_ref, cbv_ref, wd_ref, o_ref, xn_ref) = refs
    i, f = pl.program_id(0), pl.program_id(1)
    tm = x_ref.shape[0]

    @pl.when(f == 0)
    def _():
        x = x_ref[...]
        xn_ref[...] = _rms(x, g_ref[...]).astype(BF16)
        o_ref[...] = x
        if halo:
            xhn_ref[...] = _rms(xh_ref[...], g_ref[...]).astype(BF16)

    xn = xn_ref[...]
    tf = wa_ref.shape[1]
    sw = tf // n_sub
    r = lax.broadcasted_iota(I32, (tm, sw), 0)

    def conv(cs, w_ref, cw_ref, cb_ref, p1_ref, p2_ref):
        h = _dot(xn, w_ref[:, cs])
        h1 = pltpu.roll(h, 1, axis=0)
        h2 = pltpu.roll(h, 2, axis=0)
        if halo:
            hh = _dot(xhn_ref[...], w_ref[:, cs])
            hh = hh * jnp.where(i % seq_tiles == 0, 0.0, 1.0)
            last, prev = hh[-1:, :], hh[-2:-1, :]
            h1 = jnp.where(r == 0, last, h1)
            h2 = jnp.where(r == 0, prev, jnp.where(r == 1, last, h2))
        else:
            t = r % seq_len
            h1 = jnp.where(t >= 1, h1, p1_ref[:, cs])
            h2 = jnp.where(t >= 2, h2, p2_ref[:, cs])
        cw = cw_ref[:, cs]
        return cb_ref[:, cs] + cw[0:1] * h2 + cw[1:2] * h1 + cw[2:3] * h

    acc = None
    for c in range(n_sub):
        cs = slice(c * sw, (c + 1) * sw)
        if halo:
            ca = conv(cs, wa_ref, cwa_ref, cba_ref, None, None)
            cv = conv(cs, wv_ref, cwv_ref, cbv_ref, None, None)
        else:
            ca = conv(cs, wa_ref, cwa_ref, cba_ref, p1a_ref, p2a_ref)
            cv = conv(cs, wv_ref, cwv_ref, cbv_ref, p1v_ref, p2v_ref)
        act = (jax.nn.silu(ca) * cv).astype(BF16)
        part = _dot(act, wd_ref[cs, :])
        acc = part if acc is None else acc + part
    o_ref[...] += acc


FFN_HALO = 16
FFN_SUB = 512


def _ffn(x, g, w_up, conv_w, conv_b, w_down, *, seq_len, hist=None, tm=512, tf=1024):
    M, D = x.shape
    F = w_down.shape[0]
    halo = hist is None
    tm = _pick(seq_len, tm, FFN_HALO) if halo else M
    tf = _pick(F, tf, LANES)
    nf = F // tf
    xs = pl.BlockSpec((tm, D), lambda i, f: (i, 0))
    a_col = lambda i, f: (0, f)
    v_col = lambda i, f: (0, nf + f)
    wspecs = [pl.BlockSpec((1, D), lambda i, f: (0, 0)),
              pl.BlockSpec((D, tf), a_col), pl.BlockSpec((D, tf), v_col),
              pl.BlockSpec((CONV_W, tf), a_col), pl.BlockSpec((CONV_W, tf), v_col),
              pl.BlockSpec((1, tf), a_col), pl.BlockSpec((1, tf), v_col),
              pl.BlockSpec((tf, D), lambda i, f: (f, 0))]
    wargs = [g, w_up, w_up, conv_w, conv_w, conv_b, conv_b, w_down]
    scratch = [pltpu.VMEM((tm, D), BF16)]
    if halo:
        per = tm // FFN_HALO
        extra = [pl.BlockSpec((FFN_HALO, D), lambda i, f: (jnp.maximum(i * per - 1, 0), 0))]
        eargs = [x]
        scratch.append(pltpu.VMEM((FFN_HALO, D), BF16))
    else:
        ha = pl.BlockSpec((tm, tf), lambda i, f: (i, f))
        hv = pl.BlockSpec((tm, tf), lambda i, f: (i, nf + f))
        extra = [ha, hv, ha, hv]
        eargs = [hist[0], hist[0], hist[1], hist[1]]
    kern = functools.partial(_ffn_kernel, seq_tiles=max(seq_len // tm, 1), seq_len=seq_len,
                             halo=halo, n_sub=max(tf // FFN_SUB, 1))
    return pl.pallas_call(
        kern,
        grid=(M // tm, nf),
        in_specs=[xs] + extra + wspecs,
        out_specs=pl.BlockSpec((tm, D), lambda i, f: (i, 0)),
        out_shape=jax.ShapeDtypeStruct((M, D), F32),
        scratch_shapes=scratch,
        compiler_params=_cparams("parallel", "arbitrary"),
        name="conv_ffn" if halo else "conv_ffn_hist",
    )(x, *eargs, *wargs)


def _conv_ffn(xp, xs, g, w_up, conv_w, conv_b, w_down, hist_s):
    B, S, D = xp.shape
    Bs, T, _ = xs.shape
    F2 = w_up.shape[1]
    g = g[None]
    w_up = w_up.astype(BF16)
    w_down = w_down.astype(BF16)
    conv_b = conv_b[None]
    yp = _ffn(xp.reshape(B * S, D), g, w_up, conv_w, conv_b, w_down, seq_len=S)
    hs = hist_s.astype(F32)
    z = jnp.zeros((Bs, T, F2), F32)
    p1 = z.at[:, 0].set(hs[:, -1]).reshape(Bs * T, F2)
    p2 = z.at[:, 0].set(hs[:, -2]).at[:, 1].set(hs[:, -1]).reshape(Bs * T, F2)
    ys = _ffn(xs.reshape(Bs * T, D), g, w_up, conv_w, conv_b, w_down, seq_len=T, hist=(p1, p2))
    tail = jnp.concatenate([xp[:, S - (CONV_W - 1):].reshape(-1, D),
                            xs[:, T - (CONV_W - 1):].reshape(-1, D)], axis=0)
    n_tail = tail.shape[0]
    tail = jnp.pad(tail, ((0, -n_tail % 16), (0, 0)))
    ht = _rms_matmul(tail, g, w_up)
    nb = B * (CONV_W - 1)
    return (yp.reshape(B, S, D), ys.reshape(Bs, T, D),
            ht[:nb].reshape(B, CONV_W - 1, F2), ht[nb:n_tail].reshape(Bs, CONV_W - 1, F2))


def kernel(x_prompt, x_sample, cache_a_k, cache_a_v, cache_a_kidx, cache_c_k, cache_c_v, state_b_re, state_b_im, state_ffn_conv, norm_mix, norm_ffn, a_w_in, a_w_out, a_q_norm, a_k_norm, b_a_re, b_a_im, b_log_dt, b_b_re, b_b_im, b_c_re, b_c_im, b_d, b_w_glu, c_w_in, c_w_out, c_q_norm, c_k_norm, c_rel_bias, ffn_w_up, ffn_conv_w, ffn_conv_b, ffn_w_down):
    xp, xs = x_prompt, x_sample
    depth = norm_mix.shape[0]
    outs_a, outs_b, outs_c, f_p, f_s = [], [], [], [], []
    for i in range(depth):
        li, kind = i // 3, i % 3
        if kind == 0:
            xp, xs, o = _mixer_a(xp, xs, norm_mix[i], a_w_in[li], a_w_out[li], a_q_norm[li],
                                 a_k_norm[li], cache_a_k[li], cache_a_v[li], cache_a_kidx[li])
            outs_a.append(o)
        elif kind == 1:
            ops = _s5_operators(b_a_re[li], b_a_im[li], b_log_dt[li], b_b_re[li], b_b_im[li],
                                b_c_re[li], b_c_im[li])
            xp, xs, o = _mixer_b(xp, xs, norm_mix[i], ops, b_d[li], b_w_glu[li],
                                 state_b_re[li], state_b_im[li])
            outs_b.append(o)
        else:
            xp, xs, o = _mixer_c(xp, xs, norm_mix[i], c_w_in[li], c_w_out[li], c_q_norm[li],
                                 c_k_norm[li], c_rel_bias[li], cache_c_k[li], cache_c_v[li])
            outs_c.append(o)
        xp, xs, hp, hs = _conv_ffn(xp, xs, norm_ffn[i], ffn_w_up[i], ffn_conv_w[i],
                                   ffn_conv_b[i], ffn_w_down[i], state_ffn_conv[i])
        f_p.append(hp)
        f_s.append(hs)
    stack = lambda outs, j: jnp.stack([o[j] for o in outs])
    return (xp, xs,
            *[stack(outs_a, j) for j in range(6)],
            *[stack(outs_c, j) for j in range(4)],
            *[stack(outs_b, j) for j in range(4)],
            jnp.stack(f_p), jnp.stack(f_s))
```

```python
import functools
import math

import jax
import jax.numpy as jnp
from jax import lax
from jax.experimental import pallas as pl
from jax.experimental.pallas import tpu as pltpu

F32 = jnp.float32
BF16 = jnp.bfloat16
I32 = jnp.int32

EPS = 1e-6
NEG_INF = -1e30
ROW_MAX_INIT = -1e29
ROPE_THETA = 10000.0
CHUNK = 64
HEAD_DIM = 128
LANES = 128

A_HEADS = 16
A_KV_HEADS = 4
A_GROUP = A_HEADS // A_KV_HEADS
IDX_HEADS = 8
IDX_DIM = 64
TOPK_MAX = 256
A_Q = A_HEADS * HEAD_DIM
A_KV = A_KV_HEADS * HEAD_DIM
A_QI = IDX_HEADS * IDX_DIM
IDX_W_SCALE = (IDX_HEADS * IDX_DIM) ** -0.5
A_OFF_K = A_Q
A_OFF_V = A_OFF_K + A_KV
A_OFF_QI = A_OFF_V + A_KV
A_OFF_KI = A_OFF_QI + A_QI
A_OFF_WI = A_OFF_KI + LANES
A_IN_PAD = A_OFF_WI + LANES

S5_GROUP = 16
S5_STATE = 64
S5_CHUNK = 16
S5_LB_GROUPS = LANES // S5_GROUP

C_HEADS = 16
C_BAND_CHUNKS = 8
C_REACH = C_BAND_CHUNKS * CHUNK
REL_CLIP = 128
CONV_W = 3

VMEM_LIMIT = 56 * 1024 * 1024


def _cparams(*sem):
    return pltpu.CompilerParams(dimension_semantics=sem, vmem_limit_bytes=VMEM_LIMIT)


def _pick(n, target, mult):
    best = None
    for t in range(mult, min(n, target) + 1, mult):
        if n % t == 0:
            best = t
    return best if best is not None else n


def _dot(a, b):
    return jnp.dot(a, b, preferred_element_type=F32)


def _dot_nt(a, b):
    return lax.dot_general(a, b, (((1,), (1,)), ((), ())), preferred_element_type=F32)


def _rms(x, g):
    return x * lax.rsqrt(jnp.mean(x * x, axis=-1, keepdims=True) + EPS) * g


def _rms_matmul_kernel(x_ref, g_ref, w_ref, o_ref, xn_ref):
    @pl.when(pl.program_id(1) == 0)
    def _():
        xn_ref[...] = _rms(x_ref[...], g_ref[...]).astype(BF16)

    o_ref[...] = _dot(xn_ref[...], w_ref[...])


def _rms_matmul(x, g, w, *, tm=1024, tn=1024):
    M, D = x.shape
    N = w.shape[1]
    tm = _pick(M, tm, 16)
    tn = _pick(N, tn, LANES)
    return pl.pallas_call(
        _rms_matmul_kernel,
        grid=(M // tm, N // tn),
        in_specs=[pl.BlockSpec((tm, D), lambda i, j: (i, 0)),
                  pl.BlockSpec((1, D), lambda i, j: (0, 0)),
                  pl.BlockSpec((D, tn), lambda i, j: (0, j))],
        out_specs=pl.BlockSpec((tm, tn), lambda i, j: (i, j)),
        out_shape=jax.ShapeDtypeStruct((M, N), F32),
        scratch_shapes=[pltpu.VMEM((tm, D), BF16)],
        compiler_params=_cparams("parallel", "arbitrary"),
        name="rms_matmul",
    )(x, g, w)


def _rmsnorm_kernel(x_ref, g_ref, o_ref):
    o_ref[...] = _rms(x_ref[...], g_ref[...])


def _rmsnorm(x, g, *, tm=512):
    M, D = x.shape
    tm = _pick(M, tm, 8)
    return pl.pallas_call(
        _rmsnorm_kernel,
        grid=(M // tm,),
        in_specs=[pl.BlockSpec((tm, D), lambda i: (i, 0)),
                  pl.BlockSpec((1, D), lambda i: (0, 0))],
        out_specs=pl.BlockSpec((tm, D), lambda i: (i, 0)),
        out_shape=jax.ShapeDtypeStruct((M, D), F32),
        compiler_params=_cparams("parallel"),
        name="rmsnorm",
    )(x, g)


def _matmul_res_kernel(x_ref, w_ref, r_ref, o_ref):
    o_ref[...] = r_ref[...] + _dot(x_ref[...], w_ref[...])


def _matmul_res(x, w, res, *, tm=512, tn=2048):
    M, K = x.shape
    N = w.shape[1]
    tm = _pick(M, tm, 16)
    tn = _pick(N, tn, LANES)
    return pl.pallas_call(
        _matmul_res_kernel,
        grid=(M // tm, N // tn),
        in_specs=[pl.BlockSpec((tm, K), lambda i, j: (i, 0)),
                  pl.BlockSpec((K, tn), lambda i, j: (0, j)),
                  pl.BlockSpec((tm, tn), lambda i, j: (i, j))],
        out_specs=pl.BlockSpec((tm, tn), lambda i, j: (i, j)),
        out_shape=jax.ShapeDtypeStruct((M, N), F32),
        compiler_params=_cparams("parallel", "arbitrary"),
        name="matmul_res",
    )(x, w, res)


def _rope_tables(pos, dim):
    half = dim // 2
    inv = jnp.power(ROPE_THETA, -jnp.arange(half, dtype=F32) / half)
    ang = pos.astype(F32)[:, None] * inv[None, :]
    cos, sin = jnp.cos(ang), jnp.sin(ang)
    reps = LANES // dim
    return (jnp.tile(jnp.concatenate([cos, cos], axis=-1), (1, reps)),
            jnp.tile(jnp.concatenate([-sin, sin], axis=-1), (1, reps)))


def _a_post_kernel(h_ref, c128_ref, s128_ref, c64_ref, s64_ref, qg_ref, kg_ref,
                   q_ref, kf_ref, kb_ref, vb_ref, qi_ref, kif_ref, kib_ref, wi_ref):
    c128, s128 = c128_ref[...], s128_ref[...]
    c64, s64 = c64_ref[...], s64_ref[...]
    tm = h_ref.shape[0]
    lane = lax.broadcasted_iota(I32, (tm, LANES), 1)

    def rope128(x):
        return x * c128 + pltpu.roll(x, HEAD_DIM // 2, axis=1) * s128

    def rope64(x):
        lo = (lane % IDX_DIM) < (IDX_DIM // 2)
        partner = jnp.where(lo, pltpu.roll(x, LANES - IDX_DIM // 2, axis=1),
                            pltpu.roll(x, IDX_DIM // 2, axis=1))
        return x * c64 + partner * s64

    for h in range(A_HEADS):
        x = h_ref[:, h * LANES:(h + 1) * LANES]
        q_ref[:, h * LANES:(h + 1) * LANES] = (
            rope128(_rms(x, qg_ref[...])) * (HEAD_DIM ** -0.5)).astype(BF16)
    for h in range(A_KV_HEADS):
        x = h_ref[:, A_OFF_K + h * LANES:A_OFF_K + (h + 1) * LANES]
        k = rope128(_rms(x, kg_ref[...]))
        kf_ref[:, h * LANES:(h + 1) * LANES] = k
        kb_ref[:, h * LANES:(h + 1) * LANES] = k.astype(BF16)
    ones = jnp.ones((tm, LANES), BF16)
    for h in range(A_KV_HEADS):
        v = h_ref[:, A_OFF_V + h * LANES:A_OFF_V + (h + 1) * LANES]
        vb_ref[:, 2 * h * LANES:(2 * h + 1) * LANES] = v.astype(BF16)
        vb_ref[:, (2 * h + 1) * LANES:(2 * h + 2) * LANES] = ones
    for p in range(A_QI // LANES):
        x = rope64(h_ref[:, A_OFF_QI + p * LANES:A_OFF_QI + (p + 1) * LANES])
        qi_ref[:, (2 * p) * LANES:(2 * p + 1) * LANES] = (
            jnp.where(lane < IDX_DIM, x, 0.0).astype(BF16))
        qi_ref[:, (2 * p + 1) * LANES:(2 * p + 2) * LANES] = (
            jnp.where(lane >= IDX_DIM, x, 0.0).astype(BF16))
    ki = rope64(h_ref[:, A_OFF_KI:A_OFF_KI + LANES])
    kif_ref[...] = ki
    kib_ref[...] = (ki + pltpu.roll(ki, IDX_DIM, axis=1)).astype(BF16)
    wi_ref[...] = h_ref[:, A_OFF_WI:A_OFF_WI + LANES] * IDX_W_SCALE


def _a_post(h, tabs, qg, kg, *, tm=256):
    M = h.shape[0]
    P = tabs[0].shape[0]
    tm = _pick(math.gcd(M, P), tm, 16)
    nt = P // tm
    row = lambda i: (i, 0)
    tab = lambda i: (i % nt, 0)
    one = lambda i: (0, 0)
    widths = [(A_Q, BF16), (A_KV, F32), (A_KV, BF16), (2 * A_KV, BF16),
              (IDX_HEADS * LANES, BF16), (LANES, F32), (LANES, BF16), (LANES, F32)]
    return pl.pallas_call(
        _a_post_kernel,
        grid=(M // tm,),
        in_specs=[pl.BlockSpec((tm, A_IN_PAD), row)]
        + [pl.BlockSpec((tm, LANES), tab)] * 4
        + [pl.BlockSpec((1, LANES), one)] * 2,
        out_specs=[pl.BlockSpec((tm, w), row) for w, _ in widths],
        out_shape=[jax.ShapeDtypeStruct((M, w), dt) for w, dt in widths],
        compiler_params=_cparams("parallel"),
        name="a_post",
    )(h, *tabs, qg, kg)


def _dsa_kernel(*refs, tq, tk, ksel, causal, l_valid, n_chunks, idx_bits):
    if causal:
        q_ref, qi_ref, wi_ref, k_ref, v_ref, ki_ref, o_ref = refs[:7]
    else:
        q_ref, qi_ref, wi_ref, k_ref, v_ref, ki_ref, ck_ref, cv_ref, cki_ref, o_ref = refs[:10]
    key_scr, bias_scr, qst_scr, p_scr, m_scr, acc_scr = refs[-6:]
    t0 = pl.program_id(1) * tq
    nch = (t0 + tq + tk - 1) // tk if causal else n_chunks
    nslab = tk // LANES
    row = lax.broadcasted_iota(I32, (tq, tk), 0)
    col = lax.broadcasted_iota(I32, (tq, tk), 1)
    lane = lax.broadcasted_iota(I32, (tq, LANES), 1)
    limit = ((t0 + row) // CHUNK + 1) * CHUNK if causal else l_valid

    def chunk_off(c):
        return c * tk if isinstance(c, int) else pl.multiple_of(c * tk, tk)

    def cached(c):
        return not causal and c < n_chunks - 1

    def new_off(c):
        return chunk_off(c) if causal else 0

    def load_ki(c):
        if cached(c):
            x = cki_ref[0, c * tk:(c + 1) * tk, :]
            return jnp.concatenate([x, x], axis=1).astype(BF16)
        return ki_ref[0, pl.ds(new_off(c), tk), :]

    def load_k(c, j):
        if cached(c):
            rows = pl.ds(c * tk * A_KV_HEADS + j, tk, stride=A_KV_HEADS)
            return ck_ref.at[0][rows, :].astype(BF16)
        return k_ref[0, pl.ds(new_off(c), tk), j * LANES:(j + 1) * LANES]

    def load_v(c, j):
        if cached(c):
            rows = pl.ds(c * tk * A_KV_HEADS + j, tk, stride=A_KV_HEADS)
            v = cv_ref.at[0][rows, :].astype(BF16)
            return jnp.concatenate([v, jnp.ones((tk, LANES), BF16)], axis=1)
        return v_ref[0, pl.ds(new_off(c), tk), 2 * j * LANES:2 * (j + 1) * LANES]

    def chunk_loop(body):
        if causal:
            lax.fori_loop(0, nch, lambda c, carry: (body(c), carry)[1], 0)
        else:
            for c in range(n_chunks):
                body(c)

    wi = wi_ref[0]

    def score_body(c):
        off = chunk_off(c)
        kic = load_ki(c)
        s = jnp.zeros((tq, tk), F32)
        for h in range(IDX_HEADS):
            sh = _dot_nt(qi_ref[0, :, h * LANES:(h + 1) * LANES], kic)
            s = s + wi[:, h:h + 1] * jnp.maximum(sh, 0.0)
        s = jnp.where(s == 0.0, 0.0, s)
        s = jnp.where(off + col < limit, s, NEG_INF)
        bits = pltpu.bitcast(s, I32)
        key_scr[:, pl.ds(off, tk)] = bits ^ ((bits >> 31) & 0x7FFFFFFF)

    chunk_loop(score_body)

    def count(pred):
        def body(c, acc):
            off = chunk_off(c)
            blk = key_scr[:, pl.ds(off, tk)]
            for s in range(nslab):
                sl = slice(s * LANES, (s + 1) * LANES)
                acc = acc + pred(blk[:, sl], lane + (off + s * LANES))
            return acc

        acc = lax.fori_loop(0, nch, body, jnp.zeros((tq, LANES), F32))
        return jnp.broadcast_to(jnp.sum(acc, axis=-1, keepdims=True), (tq, LANES))

    def thr_cond(st):
        i, _, n_ge = st
        return jnp.logical_and(i < 32, jnp.max(n_ge) > ksel)

    def thr_body(st):
        i, thr, n_ge = st
        cand = thr + lax.shift_left(jnp.int32(1), 31 - i)
        cnt = count(lambda key, idx: jnp.where(key >= cand, 1.0, 0.0))
        take = cnt >= ksel
        return i + 1, jnp.where(take, cand, thr), jnp.where(take, cnt, n_ge)

    n_keys = (nch * tk).astype(F32) if causal else float(nch * tk)
    _, thr, _ = lax.while_loop(
        thr_cond, thr_body,
        (jnp.int32(0), jnp.full((tq, LANES), -2 ** 31, I32), jnp.full((tq, LANES), n_keys, F32)))

    need = ksel - count(lambda key, idx: jnp.where(key > thr, 1.0, 0.0))
    n_eq = count(lambda key, idx: jnp.where(key == thr, 1.0, 0.0))

    def cut_search():
        def body(i, cut):
            cand = cut + lax.shift_left(jnp.int32(1), idx_bits - 1 - i)
            below = count(lambda key, idx: jnp.where(key == thr, jnp.where(idx < cand, 1.0, 0.0), 0.0))
            return jnp.where(below <= need, cand, cut)

        return lax.fori_loop(0, idx_bits, body, jnp.zeros((tq, LANES), I32))

    cut = lax.cond(jnp.max(n_eq - need) > 0.0, cut_search,
                   lambda: jnp.full((tq, LANES), 2 ** idx_bits, I32))

    thr_t = jnp.tile(thr, (1, nslab))
    cut_t = jnp.tile(cut, (1, nslab))

    def bias_body(c, carry):
        off = chunk_off(c)
        key = key_scr[:, pl.ds(off, tk)]
        idx = off + col
        tie = jnp.where(key == thr_t, jnp.where(idx < cut_t, 0.0, NEG_INF), NEG_INF)
        bias = jnp.where(key > thr_t, 0.0, tie)
        bias_scr[:, pl.ds(off, tk)] = jnp.where(idx < limit, bias, NEG_INF)
        return carry

    lax.fori_loop(0, nch, bias_body, 0)

    for j in range(A_KV_HEADS):
        for g in range(A_GROUP):
            h = j * A_GROUP + g
            qst_scr[j, g * tq:(g + 1) * tq, :] = q_ref[0, :, h * LANES:(h + 1) * LANES]
    m_scr[...] = jnp.full(m_scr.shape, ROW_MAX_INIT, F32)
    acc_scr[...] = jnp.zeros(acc_scr.shape, F32)
    gr = A_GROUP * tq

    def attn_body(c):
        off = chunk_off(c)
        bias = bias_scr[:, pl.ds(off, tk)]
        for j in range(A_KV_HEADS):
            kc = load_k(c, j)
            vc = load_v(c, j)
            s = _dot_nt(qst_scr[j], kc)
            rows = slice(j * gr, (j + 1) * gr)
            m_old = m_scr[rows, :]
            alphas = []
            for g in range(A_GROUP):
                sl = slice(g * tq, (g + 1) * tq)
                sg = s[sl, :] + bias
                m_new = jnp.maximum(m_old[sl, :], jnp.max(sg, axis=-1, keepdims=True))
                alphas.append(jnp.exp(m_old[sl, :] - m_new))
                p_scr[sl, :] = jnp.exp((sg - jnp.tile(m_new, (1, nslab))).astype(BF16))
                m_scr[j * gr + g * tq:j * gr + (g + 1) * tq, :] = m_new
            alpha = jnp.concatenate(alphas, axis=0)
            acc_scr[rows, :] = jnp.tile(alpha, (1, 2)) * acc_scr[rows, :] + _dot(p_scr[...], vc)

    chunk_loop(attn_body)
    for h in range(A_HEADS):
        rows = slice(h * tq, (h + 1) * tq)
        o_ref[0, :, h * LANES:(h + 1) * LANES] = (
            acc_scr[rows, :LANES] / acc_scr[rows, LANES:]).astype(BF16)


def _dsa(q, qi, wi, k, v, ki, cache=None, *, tq, tk, ksel, causal, l_valid):
    B, T, _ = q.shape
    Lk = k.shape[1]
    blk_q = lambda w: pl.BlockSpec((1, tq, w), lambda b, i: (b, i, 0))
    blk_k = lambda n, w: pl.BlockSpec((1, n, w), lambda b, i: (b, 0, 0))
    specs, args = [], []
    L = Lk
    if cache is not None:
        ck, cv, cki = cache
        past = ck.shape[1]
        L = past + Lk
        rows = past * A_KV_HEADS
        specs = [blk_k(rows, HEAD_DIM), blk_k(rows, HEAD_DIM), blk_k(past, IDX_DIM)]
        args = [ck.reshape(B, rows, HEAD_DIM), cv.reshape(B, rows, HEAD_DIM), cki]
    kern = functools.partial(
        _dsa_kernel, tq=tq, tk=tk, ksel=ksel, causal=causal, l_valid=l_valid,
        n_chunks=-(-l_valid // tk), idx_bits=max(1, (L - 1).bit_length()) + 1)
    return pl.pallas_call(
        kern,
        grid=(B, T // tq),
        in_specs=[blk_q(A_Q), blk_q(IDX_HEADS * LANES), blk_q(LANES),
                  blk_k(Lk, A_KV), blk_k(Lk, 2 * A_KV), blk_k(Lk, LANES)] + specs,
        out_specs=blk_q(A_Q),
        out_shape=jax.ShapeDtypeStruct((B, T, A_Q), BF16),
        scratch_shapes=[pltpu.VMEM((tq, L), I32), pltpu.VMEM((tq, L), F32),
                        pltpu.VMEM((A_KV_HEADS, A_GROUP * tq, LANES), BF16),
                        pltpu.VMEM((A_GROUP * tq, tk), BF16),
                        pltpu.VMEM((A_HEADS * tq, LANES), F32),
                        pltpu.VMEM((A_HEADS * tq, 2 * LANES), F32)],
        compiler_params=_cparams("parallel", "arbitrary"),
        name="dsa_causal" if causal else "dsa_cached",
    )(q, qi, wi, k, v, ki, *args)


def _pack_a_w_in(w):
    D = w.shape[0]
    o_ki = A_Q + 2 * A_KV + A_QI
    pad = lambda n: jnp.zeros((D, n), w.dtype)
    return jnp.concatenate(
        [w[:, :o_ki], w[:, o_ki:o_ki + IDX_DIM], pad(LANES - IDX_DIM),
         w[:, o_ki + IDX_DIM:], pad(LANES - IDX_HEADS)], axis=1).astype(BF16)


def _mixer_a(xp, xs, g, w_in, w_out, qg, kg, ck, cv, cki):
    B, S, D = xp.shape
    Bs, T, _ = xs.shape
    past = ck.shape[1]
    w_in = _pack_a_w_in(w_in)
    w_out = w_out.astype(BF16)
    g, qg, kg = g[None], qg[None], kg[None]

    def project(x, pos):
        b, t, _ = x.shape
        h = _rms_matmul(x.reshape(b * t, D), g, w_in, tn=A_IN_PAD // 5)
        tabs = _rope_tables(pos, HEAD_DIM) + _rope_tables(pos, IDX_DIM)
        outs = _a_post(h, tabs, qg, kg)
        v = h[:, A_OFF_V:A_OFF_V + A_KV]
        return [o.reshape(b, t, -1) for o in outs] + [v.reshape(b, t, A_KV_HEADS, HEAD_DIM)]

    q, kf, kb, vb, qi, kif, kib, wi, v_p = project(xp, jnp.arange(S))
    tk = _pick(S, 512, LANES)
    o = _dsa(q, qi, wi, kb, vb, kib, tq=min(128, S), tk=tk, ksel=min(TOPK_MAX, S // 4),
             causal=True, l_valid=S)
    yp = _matmul_res(o.reshape(B * S, A_Q), w_out, xp.reshape(B * S, D)).reshape(B, S, D)
    k_p = kf.reshape(B, S, A_KV_HEADS, HEAD_DIM)
    ki_p = kif[..., :IDX_DIM]

    pos_s = jnp.tile(past + jnp.arange(T), Bs)
    q, kf, kb, vb, qi, kif, kib, wi, v_s = project(xs, pos_s)
    L = past + T
    tk = _pick(past, 512, LANES)
    padk = lambda a: jnp.pad(a, ((0, 0), (0, tk - T), (0, 0)))
    o = _dsa(q, qi, wi, padk(kb), padk(vb), padk(kib), (ck, cv, cki),
             tq=T, tk=tk, ksel=min(TOPK_MAX, L // 4), causal=False, l_valid=L)
    ys = _matmul_res(o.reshape(Bs * T, A_Q), w_out, xs.reshape(Bs * T, D)).reshape(Bs, T, D)
    k_s = kf.reshape(Bs, T, A_KV_HEADS, HEAD_DIM)
    return yp, ys, (k_p, v_p, ki_p, k_s, v_s, kif[..., :IDX_DIM])


def _c_post_kernel(h_ref, qg_ref, kg_ref, q_ref, kf_ref, kb_ref, vb_ref):
    W = C_HEADS * HEAD_DIM
    for h in range(C_HEADS):
        sl = slice(h * LANES, (h + 1) * LANES)
        q_ref[:, sl] = (_rms(h_ref[:, sl], qg_ref[...]) * (HEAD_DIM ** -0.5)).astype(BF16)
        k = _rms(h_ref[:, W + h * LANES:W + (h + 1) * LANES], kg_ref[...])
        kf_ref[:, sl] = k
        kb_ref[:, sl] = k.astype(BF16)
    vb_ref[...] = h_ref[:, 2 * W:3 * W].astype(BF16)


def _c_post(h, qg, kg, *, tm=256):
    M = h.shape[0]
    W = C_HEADS * HEAD_DIM
    tm = _pick(M, tm, 16)
    row = lambda i: (i, 0)
    dts = [BF16, F32, BF16, BF16]
    return pl.pallas_call(
        _c_post_kernel,
        grid=(M // tm,),
        in_specs=[pl.BlockSpec((tm, 3 * W), row),
                  pl.BlockSpec((1, LANES), lambda i: (0, 0)),
                  pl.BlockSpec((1, LANES), lambda i: (0, 0))],
        out_specs=[pl.BlockSpec((tm, W), row)] * 4,
        out_shape=[jax.ShapeDtypeStruct((M, W), dt) for dt in dts],
        compiler_params=_cparams("parallel"),
        name="c_post",
    )(h, qg, kg)


def _band_kernel(q_ref, k_ref, v_ref, b_ref, o_ref, *, sub, nsub, win, first_key):
    t0 = pl.program_id(2) * (sub * nsub)
    for s in range(nsub):
        start = pl.multiple_of(t0 + s * sub, sub)
        kw = k_ref[0, pl.ds(start, win), :]
        vw = v_ref[0, pl.ds(start, win), :]
        lg = _dot_nt(q_ref[0, s * sub:(s + 1) * sub, :], kw) + b_ref[0]
        if first_key is not None:
            kk = lax.broadcasted_iota(I32, (sub, win), 1)
            lg = jnp.where(kk >= first_key - start, lg, NEG_INF)
        p = jnp.exp(lg - jnp.max(lg, axis=-1, keepdims=True))
        o = _dot(p.astype(BF16), vw) / jnp.sum(p, axis=-1, keepdims=True)
        o_ref[0, s * sub:(s + 1) * sub, :] = o.astype(BF16)


def _band(q, k, v, bias, *, sub, nsub, first_key):
    B, T, W = q.shape
    Lk = k.shape[1]
    win = bias.shape[2]
    tq = sub * nsub
    kern = functools.partial(_band_kernel, sub=sub, nsub=nsub, win=win, first_key=first_key)
    return pl.pallas_call(
        kern,
        grid=(B, C_HEADS, T // tq),
        in_specs=[pl.BlockSpec((1, tq, LANES), lambda b, h, i: (b, i, h)),
                  pl.BlockSpec((1, Lk, LANES), lambda b, h, i: (b, 0, h)),
                  pl.BlockSpec((1, Lk, LANES), lambda b, h, i: (b, 0, h)),
                  pl.BlockSpec((1, sub, win), lambda b, h, i: (h, 0, 0))],
        out_specs=pl.BlockSpec((1, tq, LANES), lambda b, h, i: (b, i, h)),
        out_shape=jax.ShapeDtypeStruct((B, T, W), BF16),
        compiler_params=_cparams("parallel", "parallel", "arbitrary"),
        name="band_attn",
    )(q, k, v, bias)


def _mixer_c(xp, xs, g, w_in, w_out, qg, kg, rel, ck, cv):
    B, S, D = xp.shape
    Bs, T, _ = xs.shape
    W = C_HEADS * HEAD_DIM
    cp = ck.shape[1]
    w_in = w_in.astype(BF16)
    w_out = w_out.astype(BF16)
    g, qg, kg = g[None], qg[None], kg[None]
    relf = rel.astype(F32)

    def project(x, keep):
        b, t, _ = x.shape
        h = _rms_matmul(x.reshape(b * t, D), g, w_in)
        q, kf, kb, vb = _c_post(h, qg, kg)
        vf = h.reshape(b, t, 3 * W)[:, t - keep:, 2 * W:].reshape(b, keep, C_HEADS, HEAD_DIM)
        kf = kf.reshape(b, t, C_HEADS, HEAD_DIM)[:, t - keep:]
        return q.reshape(b, t, W), kf, kb.reshape(b, t, W), vb.reshape(b, t, W), vf

    def rel_bias(rows, win, shift):
        n = rows + win - 1
        j = jnp.arange(n)
        vec = relf[:, jnp.clip(shift + rows - 1 - j, -REL_CLIP, REL_CLIP) + REL_CLIP]
        flat = jnp.tile(jnp.pad(vec, ((0, 0), (0, 1))), (1, rows))
        toep = flat[:, :rows * n].reshape(-1, rows, n)
        return toep[:, :, rows - 1:rows - 1 + win]

    sub = min(2 * CHUNK, S)
    win = C_REACH + sub
    r = jnp.arange(sub)[:, None]
    kk = jnp.arange(win)[None, :]
    lo = (r // CHUNK) * CHUNK
    bias = jnp.where((kk >= lo) & (kk < lo + C_REACH + CHUNK), rel_bias(sub, win, C_REACH),
                     NEG_INF)
    q, k_p, kb, vb, v_p = project(xp, min(C_REACH, S))
    padk = lambda a: jnp.pad(a, ((0, 0), (C_REACH, 0), (0, 0)))
    o = _band(q, padk(kb), padk(vb), bias, sub=sub, nsub=_pick(S // sub, 4, 1),
              first_key=C_REACH)
    yp = _matmul_res(o.reshape(B * S, W), w_out, xp.reshape(B * S, D)).reshape(B, S, D)

    L = cp + T
    win = -(-L // LANES) * LANES
    bias = jnp.where(jnp.arange(win)[None, :] < L, rel_bias(T, win, cp), NEG_INF)
    q, kf, kb, vb, vf = project(xs, T)
    cat = lambda c, n: jnp.concatenate(
        [c.reshape(Bs, cp, W).astype(BF16), n, jnp.zeros((Bs, win - L, W), BF16)], axis=1)
    o = _band(q, cat(ck, kb), cat(cv, vb), bias, sub=T, nsub=1, first_key=None)
    ys = _matmul_res(o.reshape(Bs * T, W), w_out, xs.reshape(Bs * T, D)).reshape(Bs, T, D)
    return yp, ys, (k_p, v_p, kf, vf)


def _s5_ops_kernel(ar_ref, ai_ref, ldt_ref, br_ref, bi_ref, cr_ref, ci_ref, cr4_ref, ci4_ref,
                   lag_ref, xr_ref, xi_ref, or_ref, oi_ref, lnr_ref, lni_ref, sr_ref, si_ref):
    n, P = xr_ref.shape[:2]
    dt = jnp.exp(ldt_ref[...])
    ar, ai = ar_ref[...], ai_ref[...]
    mag = jnp.exp(ar * dt)
    lr, li = mag * jnp.cos(ai * dt), mag * jnp.sin(ai * dt)
    den = ar * ar + ai * ai
    nr = lr - 1.0
    fr = (nr * ar + li * ai) / den
    fi = (li * ar - nr * ai) / den
    br, bi = br_ref[...], bi_ref[...]
    bbr = fr * br - fi * bi
    bbi = fr * bi + fi * br
    cr, ci = cr_ref[...], ci_ref[...]
    pr, pi = jnp.ones_like(lr), jnp.zeros_like(lr)
    for tau in range(n):
        sr_ref[...] = pr * bbr - pi * bbi
        si_ref[...] = pr * bbi + pi * bbr
        xr_ref[tau] = sr_ref[...].astype(BF16)
        xi_ref[tau] = si_ref[...].astype(BF16)

        def body(p, acc):
            return acc + cr4_ref[p] * sr_ref[p][None] - ci4_ref[p] * si_ref[p][None]

        lag_ref[tau] = lax.fori_loop(0, P, body, jnp.zeros(lag_ref.shape[1:], F32))
        pr, pi = pr * lr - pi * li, pr * li + pi * lr
        or_ref[tau] = (cr * pr - ci * pi).astype(BF16)
        oi_ref[tau] = (-(cr * pi + ci * pr)).astype(BF16)
    lnr_ref[...] = pr
    lni_ref[...] = pi


def _s5_operators(a_re, a_im, log_dt, b_re, b_im, c_re, c_im):
    G, P = a_re.shape
    C = S5_GROUP
    n = S5_CHUNK
    nlb = G // S5_LB_GROUPS
    f = lambda a: a.astype(F32)
    a_t = lambda a: f(a).T.reshape(P, 1, G)
    b_t = lambda a: jnp.transpose(f(a), (1, 2, 0))
    c_t = lambda a: jnp.transpose(f(a), (2, 1, 0))
    crt, cit = c_t(c_re), c_t(c_im)
    sds = lambda *shape: jax.ShapeDtypeStruct(shape, F32)
    lag, xr, xi, o_re, o_im, lnr, lni = pl.pallas_call(
        _s5_ops_kernel,
        out_shape=[sds(n, C, C, G)] + [jax.ShapeDtypeStruct((n, P, C, G), BF16)] * 4
        + [sds(P, 1, G)] * 2,
        scratch_shapes=[pltpu.VMEM((P, C, G), F32)] * 2,
        compiler_params=pltpu.CompilerParams(vmem_limit_bytes=VMEM_LIMIT),
        name="s5_ops",
    )(a_t(a_re), a_t(a_im), f(log_dt).reshape(1, 1, G), b_t(b_re), b_t(b_im), crt, cit,
      crt.reshape(P, C, 1, G), cit.reshape(P, C, 1, G))

    eye = jnp.eye(S5_LB_GROUPS, dtype=F32)
    split = lambda a: a.reshape(a.shape[:-1] + (nlb, S5_LB_GROUPS))
    taps = jnp.einsum('tdclg,gh->ltgchd', split(lag), eye).reshape(nlb, n, LANES, LANES)

    def inc(x):
        return jnp.einsum('tpclg,gh->ltgchp', split(x[::-1]), eye).reshape(
            nlb, n * LANES, S5_LB_GROUPS * P)

    def out(x):
        return jnp.einsum('tpclg,gh->lgpthc', split(x), eye).reshape(
            nlb, S5_LB_GROUPS * P, n * LANES)

    lam_n = tuple(a.reshape(P, G).T.reshape(1, 1, G * P) for a in (lnr, lni))
    bf = lambda a: a.astype(BF16)
    return bf(taps), bf(inc(xr)), bf(inc(xi)), bf(out(o_re)), bf(out(o_im)), lam_n


def _chunk_rows(u_ref, tc):
    return jnp.concatenate(
        [u_ref[pl.ds(t, tc, stride=S5_CHUNK), :] for t in range(S5_CHUNK)], axis=1)


def _s5_inc_kernel(u_ref, wr_ref, wi_ref, sr_ref, si_ref):
    u = _chunk_rows(u_ref, sr_ref.shape[0]).astype(BF16)
    sr_ref[...] = _dot(u, wr_ref[0])
    si_ref[...] = _dot(u, wi_ref[0])


def _s5_inc(u, w_re, w_im, *, tc):
    Nc = u.shape[0] // S5_CHUNK
    nlb, K, Wn = w_re.shape
    wspec = pl.BlockSpec((1, K, Wn), lambda lb, i: (lb, 0, 0))
    ospec = pl.BlockSpec((tc, Wn), lambda lb, i: (i, lb))
    return pl.pallas_call(
        _s5_inc_kernel,
        grid=(nlb, Nc // tc),
        in_specs=[pl.BlockSpec((tc * S5_CHUNK, LANES), lambda lb, i: (i, lb)), wspec, wspec],
        out_specs=[ospec, ospec],
        out_shape=[jax.ShapeDtypeStruct((Nc, nlb * Wn), F32)] * 2,
        compiler_params=_cparams("parallel", "arbitrary"),
        name="s5_inc",
    )(u, w_re, w_im)


def _s5_scan_kernel(sr_ref, si_ref, lr_ref, li_ref, h0r_ref, h0i_ref,
                    pr_ref, pi_ref, fr_ref, fi_ref):
    lr, li = lr_ref[...], li_ref[...]
    nck = sr_ref.shape[1]

    def body(k, carry):
        hr, hi = carry
        pr_ref[:, pl.ds(k, 1), :] = hr
        pi_ref[:, pl.ds(k, 1), :] = hi
        sr = sr_ref[:, pl.ds(k, 1), :]
        si = si_ref[:, pl.ds(k, 1), :]
        return lr * hr - li * hi + sr, lr * hi + li * hr + si

    hr, hi = lax.fori_loop(0, nck, body, (h0r_ref[...], h0i_ref[...]))
    fr_ref[...] = hr
    fi_ref[...] = hi


def _s5_scan(sr, si, lam, h0r, h0i, *, cb=1024):
    B, nck, GP = sr.shape
    cb = _pick(GP, cb, LANES)
    full = pl.BlockSpec((B, nck, cb), lambda j: (0, 0, j))
    vec = pl.BlockSpec((1, 1, cb), lambda j: (0, 0, j))
    st = pl.BlockSpec((B, 1, cb), lambda j: (0, 0, j))
    return pl.pallas_call(
        _s5_scan_kernel,
        grid=(GP // cb,),
        in_specs=[full, full, vec, vec, st, st],
        out_specs=[full, full, st, st],
        out_shape=[jax.ShapeDtypeStruct((B, nck, GP), F32)] * 2
        + [jax.ShapeDtypeStruct((B, 1, GP), F32)] * 2,
        compiler_params=_cparams("parallel"),
        name="s5_scan",
    )(sr, si, lam[0], lam[1], h0r, h0i)


def _s5_out_kernel(u_ref, taps_ref, hr_ref, hi_ref, wr_ref, wi_ref, d_ref, o_ref, toep_ref):
    n = S5_CHUNK
    tc = hr_ref.shape[0]

    @pl.when(pl.program_id(1) == 0)
    def _():
        zero = jnp.zeros((LANES, LANES), BF16)
        for t in range(n):
            for v in range(n):
                blk = taps_ref[0, v - t] if v >= t else zero
                toep_ref[t * LANES:(t + 1) * LANES, v * LANES:(v + 1) * LANES] = blk

    u = _chunk_rows(u_ref, tc)
    y = _dot(u.astype(BF16), toep_ref[...])
    y = y + _dot(hr_ref[...].astype(BF16), wr_ref[0]) + _dot(hi_ref[...].astype(BF16), wi_ref[0])
    y = jax.nn.gelu(y + jnp.tile(d_ref[...], (1, n)) * u)
    for t in range(n):
        o_ref[pl.ds(t, tc, stride=n), :] = y[:, t * LANES:(t + 1) * LANES]


def _s5_out(u, taps, hr, hi, w_re, w_im, d, *, tc):
    M, D = u.shape
    n = S5_CHUNK
    nlb = taps.shape[0]
    Wn, K = w_re.shape[1:]
    hspec = pl.BlockSpec((tc, Wn), lambda lb, i: (i, lb))
    uspec = pl.BlockSpec((tc * n, LANES), lambda lb, i: (i, lb))
    return pl.pallas_call(
        _s5_out_kernel,
        grid=(nlb, M // (tc * n)),
        in_specs=[uspec,
                  pl.BlockSpec((1, n, LANES, LANES), lambda lb, i: (lb, 0, 0, 0)), hspec, hspec,
                  pl.BlockSpec((1, Wn, K), lambda lb, i: (lb, 0, 0)),
                  pl.BlockSpec((1, Wn, K), lambda lb, i: (lb, 0, 0)),
                  pl.BlockSpec((1, LANES), lambda lb, i: (0, lb))],
        out_specs=uspec,
        out_shape=jax.ShapeDtypeStruct((M, D), F32),
        scratch_shapes=[pltpu.VMEM((K, K), BF16)],
        compiler_params=_cparams("parallel", "arbitrary"),
        name="s5_out",
    )(u, taps, hr, hi, w_re, w_im, d)


def _glu_res_kernel(x_ref, wa_ref, wb_ref, r_ref, o_ref, xc_ref):
    @pl.when(pl.program_id(1) == 0)
    def _():
        xc_ref[...] = x_ref[...].astype(BF16)

    x = xc_ref[...]
    a = _dot(x, wa_ref[...])
    b = _dot(x, wb_ref[...])
    o_ref[...] = r_ref[...] + a * jax.nn.sigmoid(b)


def _glu_res(x, w, res, *, tm=1024, tn=512):
    M, D = x.shape
    N = w.shape[1] // 2
    tm = _pick(M, tm, 16)
    tn = _pick(N, tn, LANES)
    nj = N // tn
    return pl.pallas_call(
        _glu_res_kernel,
        grid=(M // tm, nj),
        in_specs=[pl.BlockSpec((tm, D), lambda i, j: (i, 0)),
                  pl.BlockSpec((D, tn), lambda i, j: (0, j)),
                  pl.BlockSpec((D, tn), lambda i, j: (0, nj + j)),
                  pl.BlockSpec((tm, tn), lambda i, j: (i, j))],
        out_specs=pl.BlockSpec((tm, tn), lambda i, j: (i, j)),
        out_shape=jax.ShapeDtypeStruct((M, N), F32),
        scratch_shapes=[pltpu.VMEM((tm, D), BF16)],
        compiler_params=_cparams("parallel", "arbitrary"),
        name="glu_res",
    )(x, w, w, res)


def _mixer_b(xp, xs, g, ops, d, w_glu, h0r, h0i):
    taps, inc_re, inc_im, out_re, out_im, lam = ops
    n = S5_CHUNK
    GP = lam[0].shape[-1]
    g, d = g[None], d.astype(F32)[None]
    w_glu = w_glu.astype(BF16)

    def stream(x, hr0, hi0):
        B, T, D = x.shape
        nck = T // n
        Nc = B * nck
        x2 = x.reshape(B * T, D)
        u = _rmsnorm(x2, g)
        tc = _pick(Nc, 256, 8)
        sr, si = _s5_inc(u, inc_re, inc_im, tc=tc)
        pr, pi, fr, fi = _s5_scan(sr.reshape(B, nck, GP), si.reshape(B, nck, GP), lam, hr0, hi0)
        gy = _s5_out(u, taps, pr.reshape(Nc, GP), pi.reshape(Nc, GP), out_re, out_im, d, tc=tc)
        y = _glu_res(gy, w_glu, x2)
        shp = (B, GP // S5_STATE, S5_STATE)
        return y.reshape(B, T, D), fr.reshape(shp), fi.reshape(shp)

    B = xp.shape[0]
    zero = jnp.zeros((B, 1, GP), F32)
    yp, rp, ip = stream(xp, zero, zero)
    Bs = xs.shape[0]
    ys, rs, is_ = stream(xs, h0r.astype(F32).reshape(Bs, 1, GP), h0i.astype(F32).reshape(Bs, 1, GP))
    return yp, ys, (rp, ip, rs, is_)


FFN_HALO = 16
FFN_TAIL = 8


def _ffn_kernel(*refs, seq_tiles, seq_len, halo):
    if halo:
        (x_ref, xh_ref, g_ref, wa_ref, wv_ref, cwa_ref, cwv_ref, cba_ref, cbv_ref, wd_ref,
         o_ref, ta_ref, tv_ref, xn_ref, xhn_ref) = refs
    else:
        (x_ref, p1a_ref, p1v_ref, p2a_ref, p2v_ref, g_ref, wa_ref, wv_ref, cwa_ref, cwv_ref,
         cba_ref, cbv_ref, wd_ref, o_ref, ta_ref, tv_ref, xn_ref, h_ref) = refs
    i, f = pl.program_id(0), pl.program_id(1)
    tm = x_ref.shape[0]

    @pl.when(f == 0)
    def _():
        x = x_ref[...]
        xn_ref[...] = _rms(x, g_ref[...]).astype(BF16)
        o_ref[...] = x
        if halo:
            xhn_ref[...] = _rms(xh_ref[...], g_ref[...]).astype(BF16)

    xn = xn_ref[...]
    tf = wa_ref.shape[1]
    r = lax.broadcasted_iota(I32, (tm, tf), 0)

    def conv(w_ref, cw_ref, cb_ref, p1_ref, p2_ref, t_ref):
        h = _dot(xn, w_ref[...])
        h1 = pltpu.roll(h, 1, axis=0)
        h2 = pltpu.roll(h, 2, axis=0)
        if halo:
            t_ref[...] = h[tm - FFN_TAIL:, :]
            hh = _dot(xhn_ref[...], w_ref[...])
            hh = hh * jnp.where(i % seq_tiles == 0, 0.0, 1.0)
            last, prev = hh[-1:, :], hh[-2:-1, :]
            h1 = jnp.where(r == 0, last, h1)
            h2 = jnp.where(r == 0, prev, jnp.where(r == 1, last, h2))
        else:
            for s in range(tf // LANES):
                sl = slice(s * LANES, (s + 1) * LANES)
                h_ref[s] = h[:, sl]
                for k in range(CONV_W - 1):
                    rows = pl.ds(seq_len - (CONV_W - 1) + k, tm // seq_len, stride=seq_len)
                    t_ref[k, :, sl] = h_ref.at[s][rows, :]
            t = r % seq_len
            h1 = jnp.where(t >= 1, h1, p1_ref[...])
            h2 = jnp.where(t >= 2, h2, p2_ref[...])
        cw = cw_ref[...]
        return cb_ref[...] + cw[0:1] * h2 + cw[1:2] * h1 + cw[2:3] * h

    if halo:
        ca = conv(wa_ref, cwa_ref, cba_ref, None, None, ta_ref)
        cv = conv(wv_ref, cwv_ref, cbv_ref, None, None, tv_ref)
    else:
        ca = conv(wa_ref, cwa_ref, cba_ref, p1a_ref, p2a_ref, ta_ref)
        cv = conv(wv_ref, cwv_ref, cbv_ref, p1v_ref, p2v_ref, tv_ref)
    act = (jax.nn.silu(ca) * cv).astype(BF16)
    o_ref[...] += _dot(act, wd_ref[...])


def _ffn(x, g, w_up, conv_w, conv_b, w_down, *, seq_len, hist=None, tm=512, tf=1024):
    M, D = x.shape
    F = w_down.shape[0]
    halo = hist is None
    n_seq = M // seq_len
    tm = _pick(seq_len, tm, FFN_HALO) if halo else M
    tf = _pick(F, tf, LANES)
    nf = F // tf
    xs = pl.BlockSpec((tm, D), lambda i, f: (i, 0))
    a_col = lambda i, f: (0, f)
    v_col = lambda i, f: (0, nf + f)
    wspecs = [pl.BlockSpec((1, D), lambda i, f: (0, 0)),
              pl.BlockSpec((D, tf), a_col), pl.BlockSpec((D, tf), v_col),
              pl.BlockSpec((CONV_W, tf), a_col), pl.BlockSpec((CONV_W, tf), v_col),
              pl.BlockSpec((1, tf), a_col), pl.BlockSpec((1, tf), v_col),
              pl.BlockSpec((tf, D), lambda i, f: (f, 0))]
    wargs = [g, w_up, w_up, conv_w, conv_w, conv_b, conv_b, w_down]
    scratch = [pltpu.VMEM((tm, D), BF16)]
    seq_tiles = max(seq_len // tm, 1)
    if halo:
        per = tm // FFN_HALO
        extra = [pl.BlockSpec((FFN_HALO, D), lambda i, f: (jnp.maximum(i * per - 1, 0), 0))]
        eargs = [x]
        scratch.append(pltpu.VMEM((FFN_HALO, D), BF16))
        tspec = pl.BlockSpec((FFN_TAIL, tf), lambda i, f: (i // seq_tiles, f))
        tshape = jax.ShapeDtypeStruct((n_seq * FFN_TAIL, F), F32)
    else:
        ha = pl.BlockSpec((tm, tf), lambda i, f: (i, f))
        hv = pl.BlockSpec((tm, tf), lambda i, f: (i, nf + f))
        extra = [ha, hv, ha, hv]
        eargs = [hist[0], hist[0], hist[1], hist[1]]
        scratch.append(pltpu.VMEM((tf // LANES, tm, LANES), F32))
        tspec = pl.BlockSpec((CONV_W - 1, n_seq, tf), lambda i, f: (0, 0, f))
        tshape = jax.ShapeDtypeStruct((CONV_W - 1, n_seq, F), F32)
    kern = functools.partial(_ffn_kernel, seq_tiles=seq_tiles, seq_len=seq_len, halo=halo)
    y, ta, tv = pl.pallas_call(
        kern,
        grid=(M // tm, nf),
        in_specs=[xs] + extra + wspecs,
        out_specs=[pl.BlockSpec((tm, D), lambda i, f: (i, 0)), tspec, tspec],
        out_shape=[jax.ShapeDtypeStruct((M, D), F32), tshape, tshape],
        scratch_shapes=scratch,
        compiler_params=_cparams("arbitrary", "arbitrary"),
        name="conv_ffn" if halo else "conv_ffn_hist",
    )(x, *eargs, *wargs)
    if halo:
        tail = lambda t: t.reshape(n_seq, FFN_TAIL, F)[:, FFN_TAIL - (CONV_W - 1):]
    else:
        tail = lambda t: jnp.swapaxes(t, 0, 1)
    return y, jnp.concatenate([tail(ta), tail(tv)], axis=-1)


def _conv_ffn(xp, xs, g, w_up, conv_w, conv_b, w_down, hist_s):
    B, S, D = xp.shape
    Bs, T, _ = xs.shape
    F2 = w_up.shape[1]
    g = g[None]
    w_up = w_up.astype(BF16)
    w_down = w_down.astype(BF16)
    conv_b = conv_b[None]
    yp, hp = _ffn(xp.reshape(B * S, D), g, w_up, conv_w, conv_b, w_down, seq_len=S)
    hs = hist_s.astype(F32)
    t = jnp.arange(T)[None, :, None]
    p1 = jnp.where(t == 0, hs[:, -1:], 0.0).reshape(Bs * T, F2)
    p2 = jnp.where(t == 0, hs[:, -2:-1], jnp.where(t == 1, hs[:, -1:], 0.0)).reshape(Bs * T, F2)
    ys, hs_new = _ffn(xs.reshape(Bs * T, D), g, w_up, conv_w, conv_b, w_down, seq_len=T,
                      hist=(p1, p2))
    return yp.reshape(B, S, D), ys.reshape(Bs, T, D), hp, hs_new


def kernel(x_prompt, x_sample, cache_a_k, cache_a_v, cache_a_kidx, cache_c_k, cache_c_v, state_b_re, state_b_im, state_ffn_conv, norm_mix, norm_ffn, a_w_in, a_w_out, a_q_norm, a_k_norm, b_a_re, b_a_im, b_log_dt, b_b_re, b_b_im, b_c_re, b_c_im, b_d, b_w_glu, c_w_in, c_w_out, c_q_norm, c_k_norm, c_rel_bias, ffn_w_up, ffn_conv_w, ffn_conv_b, ffn_w_down):
    xp, xs = x_prompt, x_sample
    depth = norm_mix.shape[0]
    outs_a, outs_b, outs_c, f_p, f_s = [], [], [], [], []
    for i in range(depth):
        li, kind = i // 3, i % 3
        if kind == 0:
            xp, xs, o = _mixer_a(xp, xs, norm_mix[i], a_w_in[li], a_w_out[li], a_q_norm[li],
                                 a_k_norm[li], cache_a_k[li], cache_a_v[li], cache_a_kidx[li])
            outs_a.append(o)
        elif kind == 1:
            ops = _s5_operators(b_a_re[li], b_a_im[li], b_log_dt[li], b_b_re[li], b_b_im[li],
                                b_c_re[li], b_c_im[li])
            xp, xs, o = _mixer_b(xp, xs, norm_mix[i], ops, b_d[li], b_w_glu[li],
                                 state_b_re[li], state_b_im[li])
            outs_b.append(o)
        else:
            xp, xs, o = _mixer_c(xp, xs, norm_mix[i], c_w_in[li], c_w_out[li], c_q_norm[li],
                                 c_k_norm[li], c_rel_bias[li], cache_c_k[li], cache_c_v[li])
            outs_c.append(o)
        xp, xs, hp, hs = _conv_ffn(xp, xs, norm_ffn[i], ffn_w_up[i], ffn_conv_w[i],
                                   ffn_conv_b[i], ffn_w_down[i], state_ffn_conv[i])
        f_p.append(hp)
        f_s.append(hs)
    stack = lambda outs, j: jnp.stack([o[j] for o in outs])
    return (xp, xs,
            *[stack(outs_a, j) for j in range(6)],
            *[stack(outs_c, j) for j in range(4)],
            *[stack(outs_b, j) for j in range(4)],
            jnp.stack(f_p), jnp.stack(f_s))
```

```python
import functools
import math

import jax
import jax.numpy as jnp
from jax import lax
from jax.experimental import pallas as pl
from jax.experimental.pallas import tpu as pltpu

F32 = jnp.float32
BF16 = jnp.bfloat16
I32 = jnp.int32

EPS = 1e-6
NEG_INF = -1e30
ROW_MAX_INIT = -1e29
ROPE_THETA = 10000.0
CHUNK = 64
HEAD_DIM = 128
LANES = 128

A_HEADS = 16
A_KV_HEADS = 4
A_GROUP = A_HEADS // A_KV_HEADS
IDX_HEADS = 8
IDX_DIM = 64
TOPK_MAX = 256
A_Q = A_HEADS * HEAD_DIM
A_KV = A_KV_HEADS * HEAD_DIM
A_QI = IDX_HEADS * IDX_DIM
IDX_W_SCALE = (IDX_HEADS * IDX_DIM) ** -0.5
A_OFF_K = A_Q
A_OFF_V = A_OFF_K + A_KV
A_OFF_QI = A_OFF_V + A_KV
A_OFF_KI = A_OFF_QI + A_QI
A_OFF_WI = A_OFF_KI + LANES
A_IN_PAD = A_OFF_WI + LANES

S5_GROUP = 16
S5_STATE = 64
S5_CHUNK = 16
S5_LB_GROUPS = LANES // S5_GROUP

C_HEADS = 16
C_BAND_CHUNKS = 8
C_REACH = C_BAND_CHUNKS * CHUNK
REL_CLIP = 128
CONV_W = 3

VMEM_LIMIT = 56 * 1024 * 1024


def _cparams(*sem):
    return pltpu.CompilerParams(dimension_semantics=sem, vmem_limit_bytes=VMEM_LIMIT)


def _pick(n, target, mult):
    best = None
    for t in range(mult, min(n, target) + 1, mult):
        if n % t == 0:
            best = t
    return best if best is not None else n


def _dot(a, b):
    return jnp.dot(a, b, preferred_element_type=F32)


def _dot_nt(a, b):
    return lax.dot_general(a, b, (((1,), (1,)), ((), ())), preferred_element_type=F32)


def _rms(x, g):
    return x * lax.rsqrt(jnp.mean(x * x, axis=-1, keepdims=True) + EPS) * g


def _rms_matmul_kernel(x_ref, g_ref, w_ref, o_ref, xn_ref):
    @pl.when(pl.program_id(1) == 0)
    def _():
        xn_ref[...] = _rms(x_ref[...], g_ref[...]).astype(BF16)

    o_ref[...] = _dot(xn_ref[...], w_ref[...])


def _rms_matmul(x, g, w, *, tm=1024, tn=1024):
    M, D = x.shape
    N = w.shape[1]
    tm = _pick(M, tm, 16)
    tn = _pick(N, tn, LANES)
    return pl.pallas_call(
        _rms_matmul_kernel,
        grid=(M // tm, N // tn),
        in_specs=[pl.BlockSpec((tm, D), lambda i, j: (i, 0)),
                  pl.BlockSpec((1, D), lambda i, j: (0, 0)),
                  pl.BlockSpec((D, tn), lambda i, j: (0, j))],
        out_specs=pl.BlockSpec((tm, tn), lambda i, j: (i, j)),
        out_shape=jax.ShapeDtypeStruct((M, N), F32),
        scratch_shapes=[pltpu.VMEM((tm, D), BF16)],
        compiler_params=_cparams("parallel", "arbitrary"),
        name="rms_matmul",
    )(x, g, w)


def _rmsnorm_kernel(x_ref, g_ref, o_ref):
    o_ref[...] = _rms(x_ref[...], g_ref[...])


def _rmsnorm(x, g, *, tm=512):
    M, D = x.shape
    tm = _pick(M, tm, 8)
    return pl.pallas_call(
        _rmsnorm_kernel,
        grid=(M // tm,),
        in_specs=[pl.BlockSpec((tm, D), lambda i: (i, 0)),
                  pl.BlockSpec((1, D), lambda i: (0, 0))],
        out_specs=pl.BlockSpec((tm, D), lambda i: (i, 0)),
        out_shape=jax.ShapeDtypeStruct((M, D), F32),
        compiler_params=_cparams("parallel"),
        name="rmsnorm",
    )(x, g)


def _matmul_res_kernel(x_ref, w_ref, r_ref, o_ref):
    o_ref[...] = r_ref[...] + _dot(x_ref[...], w_ref[...])


def _matmul_res(x, w, res, *, tm=512, tn=2048):
    M, K = x.shape
    N = w.shape[1]
    tm = _pick(M, tm, 16)
    tn = _pick(N, tn, LANES)
    return pl.pallas_call(
        _matmul_res_kernel,
        grid=(M // tm, N // tn),
        in_specs=[pl.BlockSpec((tm, K), lambda i, j: (i, 0)),
                  pl.BlockSpec((K, tn), lambda i, j: (0, j)),
                  pl.BlockSpec((tm, tn), lambda i, j: (i, j))],
        out_specs=pl.BlockSpec((tm, tn), lambda i, j: (i, j)),
        out_shape=jax.ShapeDtypeStruct((M, N), F32),
        compiler_params=_cparams("parallel", "arbitrary"),
        name="matmul_res",
    )(x, w, res)


def _rope_tables(pos, dim):
    half = dim // 2
    inv = jnp.power(ROPE_THETA, -jnp.arange(half, dtype=F32) / half)
    ang = pos.astype(F32)[:, None] * inv[None, :]
    cos, sin = jnp.cos(ang), jnp.sin(ang)
    reps = LANES // dim
    return (jnp.tile(jnp.concatenate([cos, cos], axis=-1), (1, reps)),
            jnp.tile(jnp.concatenate([-sin, sin], axis=-1), (1, reps)))


def _a_post_kernel(h_ref, c128_ref, s128_ref, c64_ref, s64_ref, qg_ref, kg_ref,
                   q_ref, kf_ref, kb_ref, vb_ref, qi_ref, kif_ref, kib_ref, wi_ref):
    c128, s128 = c128_ref[...], s128_ref[...]
    c64, s64 = c64_ref[...], s64_ref[...]
    tm = h_ref.shape[0]
    lane = lax.broadcasted_iota(I32, (tm, LANES), 1)

    def rope128(x):
        return x * c128 + pltpu.roll(x, HEAD_DIM // 2, axis=1) * s128

    def rope64(x):
        lo = (lane % IDX_DIM) < (IDX_DIM // 2)
        partner = jnp.where(lo, pltpu.roll(x, LANES - IDX_DIM // 2, axis=1),
                            pltpu.roll(x, IDX_DIM // 2, axis=1))
        return x * c64 + partner * s64

    for h in range(A_HEADS):
        x = h_ref[:, h * LANES:(h + 1) * LANES]
        q_ref[:, h * LANES:(h + 1) * LANES] = (
            rope128(_rms(x, qg_ref[...])) * (HEAD_DIM ** -0.5)).astype(BF16)
    for h in range(A_KV_HEADS):
        x = h_ref[:, A_OFF_K + h * LANES:A_OFF_K + (h + 1) * LANES]
        k = rope128(_rms(x, kg_ref[...]))
        kf_ref[:, h * LANES:(h + 1) * LANES] = k
        kb_ref[:, h * LANES:(h + 1) * LANES] = k.astype(BF16)
    ones = jnp.ones((tm, LANES), BF16)
    for h in range(A_KV_HEADS):
        v = h_ref[:, A_OFF_V + h * LANES:A_OFF_V + (h + 1) * LANES]
        vb_ref[:, 2 * h * LANES:(2 * h + 1) * LANES] = v.astype(BF16)
        vb_ref[:, (2 * h + 1) * LANES:(2 * h + 2) * LANES] = ones
    for p in range(A_QI // LANES):
        x = rope64(h_ref[:, A_OFF_QI + p * LANES:A_OFF_QI + (p + 1) * LANES])
        qi_ref[:, (2 * p) * LANES:(2 * p + 1) * LANES] = (
            jnp.where(lane < IDX_DIM, x, 0.0).astype(BF16))
        qi_ref[:, (2 * p + 1) * LANES:(2 * p + 2) * LANES] = (
            jnp.where(lane >= IDX_DIM, x, 0.0).astype(BF16))
    ki = rope64(h_ref[:, A_OFF_KI:A_OFF_KI + LANES])
    kif_ref[...] = ki
    kib_ref[...] = (ki + pltpu.roll(ki, IDX_DIM, axis=1)).astype(BF16)
    wi_ref[...] = h_ref[:, A_OFF_WI:A_OFF_WI + LANES] * IDX_W_SCALE


def _a_post(h, tabs, qg, kg, *, tm=256):
    M = h.shape[0]
    P = tabs[0].shape[0]
    tm = _pick(math.gcd(M, P), tm, 16)
    nt = P // tm
    row = lambda i: (i, 0)
    tab = lambda i: (i % nt, 0)
    one = lambda i: (0, 0)
    widths = [(A_Q, BF16), (A_KV, F32), (A_KV, BF16), (2 * A_KV, BF16),
              (IDX_HEADS * LANES, BF16), (LANES, F32), (LANES, BF16), (LANES, F32)]
    return pl.pallas_call(
        _a_post_kernel,
        grid=(M // tm,),
        in_specs=[pl.BlockSpec((tm, A_IN_PAD), row)]
        + [pl.BlockSpec((tm, LANES), tab)] * 4
        + [pl.BlockSpec((1, LANES), one)] * 2,
        out_specs=[pl.BlockSpec((tm, w), row) for w, _ in widths],
        out_shape=[jax.ShapeDtypeStruct((M, w), dt) for w, dt in widths],
        compiler_params=_cparams("parallel"),
        name="a_post",
    )(h, *tabs, qg, kg)


def _dsa_kernel(*refs, tq, tk, ksel, causal, l_valid, n_chunks, idx_bits):
    if causal:
        q_ref, qi_ref, wi_ref, k_ref, v_ref, ki_ref, o_ref = refs[:7]
    else:
        q_ref, qi_ref, wi_ref, k_ref, v_ref, ki_ref, ck_ref, cv_ref, cki_ref, o_ref = refs[:10]
    key_scr, bias_scr, qst_scr, m_scr, acc_scr = refs[-5:]
    t0 = pl.program_id(1) * tq
    nch = (t0 + tq + tk - 1) // tk if causal else n_chunks
    nslab = tk // LANES
    row = lax.broadcasted_iota(I32, (tq, tk), 0)
    col = lax.broadcasted_iota(I32, (tq, tk), 1)
    lane = lax.broadcasted_iota(I32, (tq, LANES), 1)
    limit = ((t0 + row) // CHUNK + 1) * CHUNK if causal else l_valid

    def chunk_off(c):
        return c * tk if isinstance(c, int) else pl.multiple_of(c * tk, tk)

    def cached(c):
        return not causal and c < n_chunks - 1

    def new_off(c):
        return chunk_off(c) if causal else 0

    def load_ki(c):
        if cached(c):
            x = cki_ref[0, c * tk:(c + 1) * tk, :]
            return jnp.concatenate([x, x], axis=1).astype(BF16)
        return ki_ref[0, pl.ds(new_off(c), tk), :]

    def load_k(c, j):
        if cached(c):
            return ck_ref[0, c * tk:(c + 1) * tk, j, :].astype(BF16)
        return k_ref[0, pl.ds(new_off(c), tk), j * LANES:(j + 1) * LANES]

    def load_v(c, j):
        if cached(c):
            v = cv_ref[0, c * tk:(c + 1) * tk, j, :].astype(BF16)
            return jnp.concatenate([v, jnp.ones((tk, LANES), BF16)], axis=1)
        return v_ref[0, pl.ds(new_off(c), tk), 2 * j * LANES:2 * (j + 1) * LANES]

    def chunk_loop(body):
        if causal:
            lax.fori_loop(0, nch, lambda c, carry: (body(c), carry)[1], 0)
        else:
            for c in range(n_chunks):
                body(c)

    wi = wi_ref[0]

    def score_body(c):
        off = chunk_off(c)
        kic = load_ki(c)
        shs = [_dot_nt(qi_ref[0, :, h * LANES:(h + 1) * LANES], kic) for h in range(IDX_HEADS)]
        s = jnp.zeros((tq, tk), F32)
        for h in range(IDX_HEADS):
            s = s + wi[:, h:h + 1] * jnp.maximum(shs[h], 0.0)
        s = jnp.where(s == 0.0, 0.0, s)
        s = jnp.where(off + col < limit, s, NEG_INF)
        bits = pltpu.bitcast(s, I32)
        key_scr[:, pl.ds(off, tk)] = bits ^ ((bits >> 31) & 0x7FFFFFFF)

    chunk_loop(score_body)

    def count(pred):
        def body(c, acc):
            off = chunk_off(c)
            blk = key_scr[:, pl.ds(off, tk)]
            for s in range(nslab):
                sl = slice(s * LANES, (s + 1) * LANES)
                acc = acc + pred(blk[:, sl], lane + (off + s * LANES))
            return acc

        acc = lax.fori_loop(0, nch, body, jnp.zeros((tq, LANES), F32))
        return jnp.broadcast_to(jnp.sum(acc, axis=-1, keepdims=True), (tq, LANES))

    def thr_cond(st):
        i, _, n_ge = st
        return jnp.logical_and(i < 32, jnp.max(n_ge) > ksel)

    def thr_body(st):
        i, thr, n_ge = st
        cand = thr + lax.shift_left(jnp.int32(1), 31 - i)
        cnt = count(lambda key, idx: jnp.where(key >= cand, 1.0, 0.0))
        take = cnt >= ksel
        return i + 1, jnp.where(take, cand, thr), jnp.where(take, cnt, n_ge)

    n_keys = (nch * tk).astype(F32) if causal else float(nch * tk)
    _, thr, _ = lax.while_loop(
        thr_cond, thr_body,
        (jnp.int32(0), jnp.full((tq, LANES), -2 ** 31, I32), jnp.full((tq, LANES), n_keys, F32)))

    need = ksel - count(lambda key, idx: jnp.where(key > thr, 1.0, 0.0))
    n_eq = count(lambda key, idx: jnp.where(key == thr, 1.0, 0.0))

    def cut_search():
        def body(i, cut):
            cand = cut + lax.shift_left(jnp.int32(1), idx_bits - 1 - i)
            below = count(lambda key, idx: jnp.where(key == thr, jnp.where(idx < cand, 1.0, 0.0), 0.0))
            return jnp.where(below <= need, cand, cut)

        return lax.fori_loop(0, idx_bits, body, jnp.zeros((tq, LANES), I32))

    cut = lax.cond(jnp.max(n_eq - need) > 0.0, cut_search,
                   lambda: jnp.full((tq, LANES), 2 ** idx_bits, I32))

    thr_t = jnp.tile(thr, (1, nslab))
    cut_t = jnp.tile(cut, (1, nslab))

    def bias_body(c, carry):
        off = chunk_off(c)
        key = key_scr[:, pl.ds(off, tk)]
        idx = off + col
        tie = jnp.where(key == thr_t, jnp.where(idx < cut_t, 0.0, NEG_INF), NEG_INF)
        bias = jnp.where(key > thr_t, 0.0, tie)
        bias_scr[:, pl.ds(off, tk)] = jnp.where(idx < limit, bias, NEG_INF)
        return carry

    lax.fori_loop(0, nch, bias_body, 0)

    for j in range(A_KV_HEADS):
        for g in range(A_GROUP):
            h = j * A_GROUP + g
            qst_scr[j, g * tq:(g + 1) * tq, :] = q_ref[0, :, h * LANES:(h + 1) * LANES]
    m_scr[...] = jnp.full(m_scr.shape, ROW_MAX_INIT, F32)
    acc_scr[...] = jnp.zeros(acc_scr.shape, F32)
    gr = A_GROUP * tq

    def attn_body(c):
        off = chunk_off(c)
        bias = bias_scr[:, pl.ds(off, tk)]
        logits = [_dot_nt(qst_scr[j], load_k(c, j)) for j in range(A_KV_HEADS)]
        for j in range(A_KV_HEADS):
            rows = slice(j * gr, (j + 1) * gr)
            m_old = m_scr[rows, :]
            alphas, ps = [], []
            for g in range(A_GROUP):
                sl = slice(g * tq, (g + 1) * tq)
                sg = logits[j][sl, :] + bias
                m_new = jnp.maximum(m_old[sl, :], jnp.max(sg, axis=-1, keepdims=True))
                alphas.append(jnp.exp(m_old[sl, :] - m_new))
                ps.append(jnp.exp((sg - jnp.tile(m_new, (1, nslab))).astype(BF16)))
                m_scr[j * gr + g * tq:j * gr + (g + 1) * tq, :] = m_new
            alpha = jnp.tile(jnp.concatenate(alphas, axis=0), (1, 2))
            pv = _dot(jnp.concatenate(ps, axis=0), load_v(c, j))
            acc_scr[rows, :] = alpha * acc_scr[rows, :] + pv

    chunk_loop(attn_body)
    for h in range(A_HEADS):
        rows = slice(h * tq, (h + 1) * tq)
        o_ref[0, :, h * LANES:(h + 1) * LANES] = (
            acc_scr[rows, :LANES] / acc_scr[rows, LANES:]).astype(BF16)


def _dsa(q, qi, wi, k, v, ki, cache=None, *, tq, tk, ksel, causal, l_valid):
    B, T, _ = q.shape
    Lk = k.shape[1]
    blk_q = lambda w: pl.BlockSpec((1, tq, w), lambda b, i: (b, i, 0))
    blk_k = lambda n, w: pl.BlockSpec((1, n, w), lambda b, i: (b, 0, 0))
    specs, args = [], []
    L = Lk
    if cache is not None:
        ck, cv, cki = cache
        past = ck.shape[1]
        L = past + Lk
        blk_c = pl.BlockSpec((1, past, A_KV_HEADS, HEAD_DIM), lambda b, i: (b, 0, 0, 0))
        specs = [blk_c, blk_c, blk_k(past, IDX_DIM)]
        args = [ck, cv, cki]
    kern = functools.partial(
        _dsa_kernel, tq=tq, tk=tk, ksel=ksel, causal=causal, l_valid=l_valid,
        n_chunks=-(-l_valid // tk), idx_bits=max(1, (L - 1).bit_length()) + 1)
    return pl.pallas_call(
        kern,
        grid=(B, T // tq),
        in_specs=[blk_q(A_Q), blk_q(IDX_HEADS * LANES), blk_q(LANES),
                  blk_k(Lk, A_KV), blk_k(Lk, 2 * A_KV), blk_k(Lk, LANES)] + specs,
        out_specs=blk_q(A_Q),
        out_shape=jax.ShapeDtypeStruct((B, T, A_Q), BF16),
        scratch_shapes=[pltpu.VMEM((tq, L), I32), pltpu.VMEM((tq, L), F32),
                        pltpu.VMEM((A_KV_HEADS, A_GROUP * tq, LANES), BF16),
                        pltpu.VMEM((A_HEADS * tq, LANES), F32),
                        pltpu.VMEM((A_HEADS * tq, 2 * LANES), F32)],
        compiler_params=_cparams("parallel", "arbitrary"),
        name="dsa_causal" if causal else "dsa_cached",
    )(q, qi, wi, k, v, ki, *args)


def _pack_a_w_in(w):
    D = w.shape[0]
    o_ki = A_Q + 2 * A_KV + A_QI
    pad = lambda n: jnp.zeros((D, n), w.dtype)
    return jnp.concatenate(
        [w[:, :o_ki], w[:, o_ki:o_ki + IDX_DIM], pad(LANES - IDX_DIM),
         w[:, o_ki + IDX_DIM:], pad(LANES - IDX_HEADS)], axis=1).astype(BF16)


def _mixer_a(xp, xs, g, w_in, w_out, qg, kg, ck, cv, cki):
    B, S, D = xp.shape
    Bs, T, _ = xs.shape
    past = ck.shape[1]
    w_in = _pack_a_w_in(w_in)
    w_out = w_out.astype(BF16)
    g, qg, kg = g[None], qg[None], kg[None]

    def project(x, pos):
        b, t, _ = x.shape
        h = _rms_matmul(x.reshape(b * t, D), g, w_in, tn=A_IN_PAD // 5)
        tabs = _rope_tables(pos, HEAD_DIM) + _rope_tables(pos, IDX_DIM)
        outs = _a_post(h, tabs, qg, kg)
        v = h[:, A_OFF_V:A_OFF_V + A_KV]
        return [o.reshape(b, t, -1) for o in outs] + [v.reshape(b, t, A_KV_HEADS, HEAD_DIM)]

    q, kf, kb, vb, qi, kif, kib, wi, v_p = project(xp, jnp.arange(S))
    tk = _pick(S, 512, LANES)
    o = _dsa(q, qi, wi, kb, vb, kib, tq=min(128, S), tk=tk, ksel=min(TOPK_MAX, S // 4),
             causal=True, l_valid=S)
    yp = _matmul_res(o.reshape(B * S, A_Q), w_out, xp.reshape(B * S, D)).reshape(B, S, D)
    k_p = kf.reshape(B, S, A_KV_HEADS, HEAD_DIM)
    ki_p = kif[..., :IDX_DIM]

    pos_s = jnp.tile(past + jnp.arange(T), Bs)
    q, kf, kb, vb, qi, kif, kib, wi, v_s = project(xs, pos_s)
    L = past + T
    tk = _pick(past, 512, LANES)
    padk = lambda a: jnp.pad(a, ((0, 0), (0, tk - T), (0, 0)))
    o = _dsa(q, qi, wi, padk(kb), padk(vb), padk(kib), (ck, cv, cki),
             tq=T, tk=tk, ksel=min(TOPK_MAX, L // 4), causal=False, l_valid=L)
    ys = _matmul_res(o.reshape(Bs * T, A_Q), w_out, xs.reshape(Bs * T, D)).reshape(Bs, T, D)
    k_s = kf.reshape(Bs, T, A_KV_HEADS, HEAD_DIM)
    return yp, ys, (k_p, v_p, ki_p, k_s, v_s, kif[..., :IDX_DIM])


def _c_post_kernel(h_ref, qg_ref, kg_ref, q_ref, kf_ref, kb_ref, vb_ref):
    W = C_HEADS * HEAD_DIM
    for h in range(C_HEADS):
        sl = slice(h * LANES, (h + 1) * LANES)
        q_ref[:, sl] = (_rms(h_ref[:, sl], qg_ref[...]) * (HEAD_DIM ** -0.5)).astype(BF16)
        k = _rms(h_ref[:, W + h * LANES:W + (h + 1) * LANES], kg_ref[...])
        kf_ref[:, sl] = k
        kb_ref[:, sl] = k.astype(BF16)
    vb_ref[...] = h_ref[:, 2 * W:3 * W].astype(BF16)


def _c_post(h, qg, kg, *, tm=256):
    M = h.shape[0]
    W = C_HEADS * HEAD_DIM
    tm = _pick(M, tm, 16)
    row = lambda i: (i, 0)
    dts = [BF16, F32, BF16, BF16]
    return pl.pallas_call(
        _c_post_kernel,
        grid=(M // tm,),
        in_specs=[pl.BlockSpec((tm, 3 * W), row),
                  pl.BlockSpec((1, LANES), lambda i: (0, 0)),
                  pl.BlockSpec((1, LANES), lambda i: (0, 0))],
        out_specs=[pl.BlockSpec((tm, W), row)] * 4,
        out_shape=[jax.ShapeDtypeStruct((M, W), dt) for dt in dts],
        compiler_params=_cparams("parallel"),
        name="c_post",
    )(h, qg, kg)


def _band_kernel(q_ref, k_ref, v_ref, b_ref, o_ref, *, sub, nsub, win, first_key):
    t0 = pl.program_id(2) * (sub * nsub)
    starts = [pl.multiple_of(t0 + s * sub, sub) for s in range(nsub)]
    lgs = [_dot_nt(q_ref[0, s * sub:(s + 1) * sub, :], k_ref[0, pl.ds(starts[s], win), :])
           + b_ref[0] for s in range(nsub)]
    if first_key is not None:
        kk = lax.broadcasted_iota(I32, (sub, win), 1)
        lgs = [jnp.where(kk >= first_key - starts[s], lgs[s], NEG_INF) for s in range(nsub)]
    ms = [jnp.max(lg, axis=-1, keepdims=True) for lg in lgs]
    ps = [jnp.exp(lg - m) for lg, m in zip(lgs, ms)]
    ls = [jnp.sum(p, axis=-1, keepdims=True) for p in ps]
    os_ = [_dot(ps[s].astype(BF16), v_ref[0, pl.ds(starts[s], win), :]) for s in range(nsub)]
    for s in range(nsub):
        o_ref[0, s * sub:(s + 1) * sub, :] = (os_[s] / ls[s]).astype(BF16)


def _band(q, k, v, bias, *, sub, nsub, first_key):
    B, T, W = q.shape
    Lk = k.shape[1]
    win = bias.shape[2]
    tq = sub * nsub
    kern = functools.partial(_band_kernel, sub=sub, nsub=nsub, win=win, first_key=first_key)
    return pl.pallas_call(
        kern,
        grid=(B, C_HEADS, T // tq),
        in_specs=[pl.BlockSpec((1, tq, LANES), lambda b, h, i: (b, i, h)),
                  pl.BlockSpec((1, Lk, LANES), lambda b, h, i: (b, 0, h)),
                  pl.BlockSpec((1, Lk, LANES), lambda b, h, i: (b, 0, h)),
                  pl.BlockSpec((1, sub, win), lambda b, h, i: (h, 0, 0))],
        out_specs=pl.BlockSpec((1, tq, LANES), lambda b, h, i: (b, i, h)),
        out_shape=jax.ShapeDtypeStruct((B, T, W), BF16),
        compiler_params=_cparams("parallel", "parallel", "arbitrary"),
        name="band_attn",
    )(q, k, v, bias)


def _mixer_c(xp, xs, g, w_in, w_out, qg, kg, rel, ck, cv):
    B, S, D = xp.shape
    Bs, T, _ = xs.shape
    W = C_HEADS * HEAD_DIM
    cp = ck.shape[1]
    w_in = w_in.astype(BF16)
    w_out = w_out.astype(BF16)
    g, qg, kg = g[None], qg[None], kg[None]
    relf = rel.astype(F32)

    def project(x, keep):
        b, t, _ = x.shape
        h = _rms_matmul(x.reshape(b * t, D), g, w_in)
        q, kf, kb, vb = _c_post(h, qg, kg)
        vf = h.reshape(b, t, 3 * W)[:, t - keep:, 2 * W:].reshape(b, keep, C_HEADS, HEAD_DIM)
        kf = kf.reshape(b, t, C_HEADS, HEAD_DIM)[:, t - keep:]
        return q.reshape(b, t, W), kf, kb.reshape(b, t, W), vb.reshape(b, t, W), vf

    def rel_bias(rows, win, shift):
        n = rows + win - 1
        j = jnp.arange(n)
        vec = relf[:, jnp.clip(shift + rows - 1 - j, -REL_CLIP, REL_CLIP) + REL_CLIP]
        flat = jnp.tile(jnp.pad(vec, ((0, 0), (0, 1))), (1, rows))
        toep = flat[:, :rows * n].reshape(-1, rows, n)
        return toep[:, :, rows - 1:rows - 1 + win]

    sub = min(2 * CHUNK, S)
    win = C_REACH + sub
    r = jnp.arange(sub)[:, None]
    kk = jnp.arange(win)[None, :]
    lo = (r // CHUNK) * CHUNK
    bias = jnp.where((kk >= lo) & (kk < lo + C_REACH + CHUNK), rel_bias(sub, win, C_REACH),
                     NEG_INF)
    q, k_p, kb, vb, v_p = project(xp, min(C_REACH, S))
    padk = lambda a: jnp.pad(a, ((0, 0), (C_REACH, 0), (0, 0)))
    o = _band(q, padk(kb), padk(vb), bias, sub=sub, nsub=_pick(S // sub, 8, 1),
              first_key=C_REACH)
    yp = _matmul_res(o.reshape(B * S, W), w_out, xp.reshape(B * S, D)).reshape(B, S, D)

    L = cp + T
    win = -(-L // LANES) * LANES
    bias = jnp.where(jnp.arange(win)[None, :] < L, rel_bias(T, win, cp), NEG_INF)
    q, kf, kb, vb, vf = project(xs, T)
    cat = lambda c, n: jnp.concatenate(
        [c.reshape(Bs, cp, W).astype(BF16), n, jnp.zeros((Bs, win - L, W), BF16)], axis=1)
    o = _band(q, cat(ck, kb), cat(cv, vb), bias, sub=T, nsub=1, first_key=None)
    ys = _matmul_res(o.reshape(Bs * T, W), w_out, xs.reshape(Bs * T, D)).reshape(Bs, T, D)
    return yp, ys, (k_p, v_p, kf, vf)


def _s5_ops_kernel(ar_ref, ai_ref, ldt_ref, br_ref, bi_ref, cr_ref, ci_ref, cr4_ref, ci4_ref,
                   lag_ref, xr_ref, xi_ref, or_ref, oi_ref, lnr_ref, lni_ref, sr_ref, si_ref):
    n, P = xr_ref.shape[:2]
    dt = jnp.exp(ldt_ref[...])
    ar, ai = ar_ref[...], ai_ref[...]
    mag = jnp.exp(ar * dt)
    lr, li = mag * jnp.cos(ai * dt), mag * jnp.sin(ai * dt)
    den = ar * ar + ai * ai
    nr = lr - 1.0
    fr = (nr * ar + li * ai) / den
    fi = (li * ar - nr * ai) / den
    br, bi = br_ref[...], bi_ref[...]
    bbr = fr * br - fi * bi
    bbi = fr * bi + fi * br
    cr, ci = cr_ref[...], ci_ref[...]
    pr, pi = jnp.ones_like(lr), jnp.zeros_like(lr)
    for tau in range(n):
        sr_ref[...] = pr * bbr - pi * bbi
        si_ref[...] = pr * bbi + pi * bbr
        xr_ref[tau] = sr_ref[...].astype(BF16)
        xi_ref[tau] = si_ref[...].astype(BF16)

        def body(p, acc):
            return acc + cr4_ref[p] * sr_ref[p][None] - ci4_ref[p] * si_ref[p][None]

        lag_ref[tau] = lax.fori_loop(0, P, body, jnp.zeros(lag_ref.shape[1:], F32))
        pr, pi = pr * lr - pi * li, pr * li + pi * lr
        or_ref[tau] = (cr * pr - ci * pi).astype(BF16)
        oi_ref[tau] = (-(cr * pi + ci * pr)).astype(BF16)
    lnr_ref[...] = pr
    lni_ref[...] = pi


def _s5_operators(a_re, a_im, log_dt, b_re, b_im, c_re, c_im):
    G, P = a_re.shape
    C = S5_GROUP
    n = S5_CHUNK
    nlb = G // S5_LB_GROUPS
    f = lambda a: a.astype(F32)
    a_t = lambda a: f(a).T.reshape(P, 1, G)
    b_t = lambda a: jnp.transpose(f(a), (1, 2, 0))
    c_t = lambda a: jnp.transpose(f(a), (2, 1, 0))
    crt, cit = c_t(c_re), c_t(c_im)
    sds = lambda *shape: jax.ShapeDtypeStruct(shape, F32)
    lag, xr, xi, o_re, o_im, lnr, lni = pl.pallas_call(
        _s5_ops_kernel,
        out_shape=[sds(n, C, C, G)] + [jax.ShapeDtypeStruct((n, P, C, G), BF16)] * 4
        + [sds(P, 1, G)] * 2,
        scratch_shapes=[pltpu.VMEM((P, C, G), F32)] * 2,
        compiler_params=pltpu.CompilerParams(vmem_limit_bytes=VMEM_LIMIT),
        name="s5_ops",
    )(a_t(a_re), a_t(a_im), f(log_dt).reshape(1, 1, G), b_t(b_re), b_t(b_im), crt, cit,
      crt.reshape(P, C, 1, G), cit.reshape(P, C, 1, G))

    eye = jnp.eye(S5_LB_GROUPS, dtype=F32)
    split = lambda a: a.reshape(a.shape[:-1] + (nlb, S5_LB_GROUPS))
    taps = jnp.einsum('tdclg,gh->ltgchd', split(lag), eye).reshape(nlb, n, LANES, LANES)

    def inc(x):
        return jnp.einsum('tpclg,gh->ltgchp', split(x[::-1]), eye).reshape(
            nlb, n * LANES, S5_LB_GROUPS * P)

    def out(x):
        return jnp.einsum('tpclg,gh->lgpthc', split(x), eye).reshape(
            nlb, S5_LB_GROUPS * P, n * LANES)

    lam_n = tuple(a.reshape(P, G).T.reshape(1, 1, G * P) for a in (lnr, lni))
    bf = lambda a: a.astype(BF16)
    return bf(taps), bf(inc(xr)), bf(inc(xi)), bf(out(o_re)), bf(out(o_im)), lam_n


def _chunk_rows(u_ref, tc):
    return jnp.concatenate(
        [u_ref[pl.ds(t, tc, stride=S5_CHUNK), :] for t in range(S5_CHUNK)], axis=1)


def _s5_inc_kernel(u_ref, wr_ref, wi_ref, sr_ref, si_ref):
    u = _chunk_rows(u_ref, sr_ref.shape[0]).astype(BF16)
    sr_ref[...] = _dot(u, wr_ref[0])
    si_ref[...] = _dot(u, wi_ref[0])


def _s5_inc(u, w_re, w_im, *, tc):
    Nc = u.shape[0] // S5_CHUNK
    nlb, K, Wn = w_re.shape
    wspec = pl.BlockSpec((1, K, Wn), lambda lb, i: (lb, 0, 0))
    ospec = pl.BlockSpec((tc, Wn), lambda lb, i: (i, lb))
    return pl.pallas_call(
        _s5_inc_kernel,
        grid=(nlb, Nc // tc),
        in_specs=[pl.BlockSpec((tc * S5_CHUNK, LANES), lambda lb, i: (i, lb)), wspec, wspec],
        out_specs=[ospec, ospec],
        out_shape=[jax.ShapeDtypeStruct((Nc, nlb * Wn), F32)] * 2,
        compiler_params=_cparams("parallel", "arbitrary"),
        name="s5_inc",
    )(u, w_re, w_im)


def _s5_scan_kernel(sr_ref, si_ref, lr_ref, li_ref, h0r_ref, h0i_ref,
                    pr_ref, pi_ref, fr_ref, fi_ref):
    lr, li = lr_ref[...], li_ref[...]
    nck = sr_ref.shape[1]

    def body(k, carry):
        hr, hi = carry
        pr_ref[:, pl.ds(k, 1), :] = hr
        pi_ref[:, pl.ds(k, 1), :] = hi
        sr = sr_ref[:, pl.ds(k, 1), :]
        si = si_ref[:, pl.ds(k, 1), :]
        return lr * hr - li * hi + sr, lr * hi + li * hr + si

    hr, hi = lax.fori_loop(0, nck, body, (h0r_ref[...], h0i_ref[...]))
    fr_ref[...] = hr
    fi_ref[...] = hi


def _s5_scan(sr, si, lam, h0r, h0i, *, cb=1024):
    B, nck, GP = sr.shape
    cb = _pick(GP, cb, LANES)
    full = pl.BlockSpec((B, nck, cb), lambda j: (0, 0, j))
    vec = pl.BlockSpec((1, 1, cb), lambda j: (0, 0, j))
    st = pl.BlockSpec((B, 1, cb), lambda j: (0, 0, j))
    return pl.pallas_call(
        _s5_scan_kernel,
        grid=(GP // cb,),
        in_specs=[full, full, vec, vec, st, st],
        out_specs=[full, full, st, st],
        out_shape=[jax.ShapeDtypeStruct((B, nck, GP), F32)] * 2
        + [jax.ShapeDtypeStruct((B, 1, GP), F32)] * 2,
        compiler_params=_cparams("parallel"),
        name="s5_scan",
    )(sr, si, lam[0], lam[1], h0r, h0i)


def _s5_out_kernel(u_ref, taps_ref, hr_ref, hi_ref, wr_ref, wi_ref, d_ref, o_ref, toep_ref):
    n = S5_CHUNK
    tc = hr_ref.shape[0]

    @pl.when(pl.program_id(1) == 0)
    def _():
        zero = jnp.zeros((LANES, LANES), BF16)
        for t in range(n):
            for v in range(n):
                blk = taps_ref[0, v - t] if v >= t else zero
                toep_ref[t * LANES:(t + 1) * LANES, v * LANES:(v + 1) * LANES] = blk

    u = _chunk_rows(u_ref, tc)
    y = _dot(u.astype(BF16), toep_ref[...])
    y = y + _dot(hr_ref[...].astype(BF16), wr_ref[0]) + _dot(hi_ref[...].astype(BF16), wi_ref[0])
    y = jax.nn.gelu(y + jnp.tile(d_ref[...], (1, n)) * u)
    for t in range(n):
        o_ref[pl.ds(t, tc, stride=n), :] = y[:, t * LANES:(t + 1) * LANES]


def _s5_out(u, taps, hr, hi, w_re, w_im, d, *, tc):
    M, D = u.shape
    n = S5_CHUNK
    nlb = taps.shape[0]
    Wn, K = w_re.shape[1:]
    hspec = pl.BlockSpec((tc, Wn), lambda lb, i: (i, lb))
    uspec = pl.BlockSpec((tc * n, LANES), lambda lb, i: (i, lb))
    return pl.pallas_call(
        _s5_out_kernel,
        grid=(nlb, M // (tc * n)),
        in_specs=[uspec,
                  pl.BlockSpec((1, n, LANES, LANES), lambda lb, i: (lb, 0, 0, 0)), hspec, hspec,
                  pl.BlockSpec((1, Wn, K), lambda lb, i: (lb, 0, 0)),
                  pl.BlockSpec((1, Wn, K), lambda lb, i: (lb, 0, 0)),
                  pl.BlockSpec((1, LANES), lambda lb, i: (0, lb))],
        out_specs=uspec,
        out_shape=jax.ShapeDtypeStruct((M, D), F32),
        scratch_shapes=[pltpu.VMEM((K, K), BF16)],
        compiler_params=_cparams("parallel", "arbitrary"),
        name="s5_out",
    )(u, taps, hr, hi, w_re, w_im, d)


def _glu_res_kernel(x_ref, wa_ref, wb_ref, r_ref, o_ref, xc_ref):
    @pl.when(pl.program_id(1) == 0)
    def _():
        xc_ref[...] = x_ref[...].astype(BF16)

    x = xc_ref[...]
    a = _dot(x, wa_ref[...])
    b = _dot(x, wb_ref[...])
    o_ref[...] = r_ref[...] + a * jax.nn.sigmoid(b)


def _glu_res(x, w, res, *, tm=1024, tn=512):
    M, D = x.shape
    N = w.shape[1] // 2
    tm = _pick(M, tm, 16)
    tn = _pick(N, tn, LANES)
    nj = N // tn
    return pl.pallas_call(
        _glu_res_kernel,
        grid=(M // tm, nj),
        in_specs=[pl.BlockSpec((tm, D), lambda i, j: (i, 0)),
                  pl.BlockSpec((D, tn), lambda i, j: (0, j)),
                  pl.BlockSpec((D, tn), lambda i, j: (0, nj + j)),
                  pl.BlockSpec((tm, tn), lambda i, j: (i, j))],
        out_specs=pl.BlockSpec((tm, tn), lambda i, j: (i, j)),
        out_shape=jax.ShapeDtypeStruct((M, N), F32),
        scratch_shapes=[pltpu.VMEM((tm, D), BF16)],
        compiler_params=_cparams("parallel", "arbitrary"),
        name="glu_res",
    )(x, w, w, res)


def _mixer_b(xp, xs, g, ops, d, w_glu, h0r, h0i):
    taps, inc_re, inc_im, out_re, out_im, lam = ops
    n = S5_CHUNK
    GP = lam[0].shape[-1]
    g, d = g[None], d.astype(F32)[None]
    w_glu = w_glu.astype(BF16)

    def stream(x, hr0, hi0):
        B, T, D = x.shape
        nck = T // n
        Nc = B * nck
        x2 = x.reshape(B * T, D)
        u = _rmsnorm(x2, g)
        tc = _pick(Nc, 256, 8)
        sr, si = _s5_inc(u, inc_re, inc_im, tc=tc)
        pr, pi, fr, fi = _s5_scan(sr.reshape(B, nck, GP), si.reshape(B, nck, GP), lam, hr0, hi0)
        gy = _s5_out(u, taps, pr.reshape(Nc, GP), pi.reshape(Nc, GP), out_re, out_im, d, tc=tc)
        y = _glu_res(gy, w_glu, x2)
        shp = (B, GP // S5_STATE, S5_STATE)
        return y.reshape(B, T, D), fr.reshape(shp), fi.reshape(shp)

    B = xp.shape[0]
    zero = jnp.zeros((B, 1, GP), F32)
    yp, rp, ip = stream(xp, zero, zero)
    Bs = xs.shape[0]
    ys, rs, is_ = stream(xs, h0r.astype(F32).reshape(Bs, 1, GP), h0i.astype(F32).reshape(Bs, 1, GP))
    return yp, ys, (rp, ip, rs, is_)


FFN_HALO = 16
FFN_TAIL = 8


def _ffn_kernel(*refs, seq_tiles, seq_len, halo):
    if halo:
        (x_ref, xh_ref, g_ref, wa_ref, wv_ref, cwa_ref, cwv_ref, cba_ref, cbv_ref, wd_ref,
         o_ref, ta_ref, tv_ref, xn_ref, xhn_ref) = refs
    else:
        (x_ref, p1a_ref, p1v_ref, p2a_ref, p2v_ref, g_ref, wa_ref, wv_ref, cwa_ref, cwv_ref,
         cba_ref, cbv_ref, wd_ref, o_ref, ta_ref, tv_ref, xn_ref, h_ref) = refs
    i, f = pl.program_id(0), pl.program_id(1)
    tm = x_ref.shape[0]

    @pl.when(f == 0)
    def _():
        x = x_ref[...]
        xn_ref[...] = _rms(x, g_ref[...]).astype(BF16)
        o_ref[...] = x
        if halo:
            xhn_ref[...] = _rms(xh_ref[...], g_ref[...]).astype(BF16)

    xn = xn_ref[...]
    tf = wa_ref.shape[1]
    r = lax.broadcasted_iota(I32, (tm, tf), 0)

    def conv(w_ref, cw_ref, cb_ref, p1_ref, p2_ref, t_ref):
        h = _dot(xn, w_ref[...])
        h1 = pltpu.roll(h, 1, axis=0)
        h2 = pltpu.roll(h, 2, axis=0)
        if halo:
            t_ref[...] = h[tm - FFN_TAIL:, :]
            hh = _dot(xhn_ref[...], w_ref[...])
            hh = hh * jnp.where(i % seq_tiles == 0, 0.0, 1.0)
            last, prev = hh[-1:, :], hh[-2:-1, :]
            h1 = jnp.where(r == 0, last, h1)
            h2 = jnp.where(r == 0, prev, jnp.where(r == 1, last, h2))
        else:
            for s in range(tf // LANES):
                sl = slice(s * LANES, (s + 1) * LANES)
                h_ref[s] = h[:, sl]
                for k in range(CONV_W - 1):
                    rows = pl.ds(seq_len - (CONV_W - 1) + k, tm // seq_len, stride=seq_len)
                    t_ref[k, :, sl] = h_ref.at[s][rows, :]
            t = r % seq_len
            h1 = jnp.where(t >= 1, h1, p1_ref[...])
            h2 = jnp.where(t >= 2, h2, p2_ref[...])
        cw = cw_ref[...]
        return cb_ref[...] + cw[0:1] * h2 + cw[1:2] * h1 + cw[2:3] * h

    if halo:
        ca = conv(wa_ref, cwa_ref, cba_ref, None, None, ta_ref)
        cv = conv(wv_ref, cwv_ref, cbv_ref, None, None, tv_ref)
    else:
        ca = conv(wa_ref, cwa_ref, cba_ref, p1a_ref, p2a_ref, ta_ref)
        cv = conv(wv_ref, cwv_ref, cbv_ref, p1v_ref, p2v_ref, tv_ref)
    act = (jax.nn.silu(ca) * cv).astype(BF16)
    o_ref[...] += _dot(act, wd_ref[...])


def _ffn(x, g, w_up, conv_w, conv_b, w_down, *, seq_len, hist=None, tm=512, tf=1024):
    M, D = x.shape
    F = w_down.shape[0]
    halo = hist is None
    n_seq = M // seq_len
    tm = _pick(seq_len, tm, FFN_HALO) if halo else M
    tf = _pick(F, tf, LANES)
    nf = F // tf
    xs = pl.BlockSpec((tm, D), lambda i, f: (i, 0))
    a_col = lambda i, f: (0, f)
    v_col = lambda i, f: (0, nf + f)
    wspecs = [pl.BlockSpec((1, D), lambda i, f: (0, 0)),
              pl.BlockSpec((D, tf), a_col), pl.BlockSpec((D, tf), v_col),
              pl.BlockSpec((CONV_W, tf), a_col), pl.BlockSpec((CONV_W, tf), v_col),
              pl.BlockSpec((1, tf), a_col), pl.BlockSpec((1, tf), v_col),
              pl.BlockSpec((tf, D), lambda i, f: (f, 0))]
    wargs = [g, w_up, w_up, conv_w, conv_w, conv_b, conv_b, w_down]
    scratch = [pltpu.VMEM((tm, D), BF16)]
    seq_tiles = max(seq_len // tm, 1)
    if halo:
        per = tm // FFN_HALO
        extra = [pl.BlockSpec((FFN_HALO, D), lambda i, f: (jnp.maximum(i * per - 1, 0), 0))]
        eargs = [x]
        scratch.append(pltpu.VMEM((FFN_HALO, D), BF16))
        tspec = pl.BlockSpec((FFN_TAIL, tf), lambda i, f: (i // seq_tiles, f))
        tshape = jax.ShapeDtypeStruct((n_seq * FFN_TAIL, F), F32)
    else:
        ha = pl.BlockSpec((tm, tf), lambda i, f: (i, f))
        hv = pl.BlockSpec((tm, tf), lambda i, f: (i, nf + f))
        extra = [ha, hv, ha, hv]
        eargs = [hist[0], hist[0], hist[1], hist[1]]
        scratch.append(pltpu.VMEM((tf // LANES, tm, LANES), F32))
        tspec = pl.BlockSpec((CONV_W - 1, n_seq, tf), lambda i, f: (0, 0, f))
        tshape = jax.ShapeDtypeStruct((CONV_W - 1, n_seq, F), F32)
    kern = functools.partial(_ffn_kernel, seq_tiles=seq_tiles, seq_len=seq_len, halo=halo)
    y, ta, tv = pl.pallas_call(
        kern,
        grid=(M // tm, nf),
        in_specs=[xs] + extra + wspecs,
        out_specs=[pl.BlockSpec((tm, D), lambda i, f: (i, 0)), tspec, tspec],
        out_shape=[jax.ShapeDtypeStruct((M, D), F32), tshape, tshape],
        scratch_shapes=scratch,
        compiler_params=_cparams("arbitrary", "arbitrary"),
        name="conv_ffn" if halo else "conv_ffn_hist",
    )(x, *eargs, *wargs)
    if halo:
        tail = lambda t: t.reshape(n_seq, FFN_TAIL, F)[:, FFN_TAIL - (CONV_W - 1):]
    else:
        tail = lambda t: jnp.swapaxes(t, 0, 1)
    return y, jnp.concatenate([tail(ta), tail(tv)], axis=-1)


def _conv_ffn(xp, xs, g, w_up, conv_w, conv_b, w_down, hist_s):
    B, S, D = xp.shape
    Bs, T, _ = xs.shape
    F2 = w_up.shape[1]
    g = g[None]
    w_up = w_up.astype(BF16)
    w_down = w_down.astype(BF16)
    conv_b = conv_b[None]
    yp, hp = _ffn(xp.reshape(B * S, D), g, w_up, conv_w, conv_b, w_down, seq_len=S)
    hs = hist_s.astype(F32)
    t = jnp.arange(T)[None, :, None]
    p1 = jnp.where(t == 0, hs[:, -1:], 0.0).reshape(Bs * T, F2)
    p2 = jnp.where(t == 0, hs[:, -2:-1], jnp.where(t == 1, hs[:, -1:], 0.0)).reshape(Bs * T, F2)
    ys, hs_new = _ffn(xs.reshape(Bs * T, D), g, w_up, conv_w, conv_b, w_down, seq_len=T,
                      hist=(p1, p2))
    return yp.reshape(B, S, D), ys.reshape(Bs, T, D), hp, hs_new


def kernel(x_prompt, x_sample, cache_a_k, cache_a_v, cache_a_kidx, cache_c_k, cache_c_v, state_b_re, state_b_im, state_ffn_conv, norm_mix, norm_ffn, a_w_in, a_w_out, a_q_norm, a_k_norm, b_a_re, b_a_im, b_log_dt, b_b_re, b_b_im, b_c_re, b_c_im, b_d, b_w_glu, c_w_in, c_w_out, c_q_norm, c_k_norm, c_rel_bias, ffn_w_up, ffn_conv_w, ffn_conv_b, ffn_w_down):
    xp, xs = x_prompt, x_sample
    depth = norm_mix.shape[0]
    outs_a, outs_b, outs_c, f_p, f_s = [], [], [], [], []
    for i in range(depth):
        li, kind = i // 3, i % 3
        if kind == 0:
            xp, xs, o = _mixer_a(xp, xs, norm_mix[i], a_w_in[li], a_w_out[li], a_q_norm[li],
                                 a_k_norm[li], cache_a_k[li], cache_a_v[li], cache_a_kidx[li])
            outs_a.append(o)
        elif kind == 1:
            ops = _s5_operators(b_a_re[li], b_a_im[li], b_log_dt[li], b_b_re[li], b_b_im[li],
                                b_c_re[li], b_c_im[li])
            xp, xs, o = _mixer_b(xp, xs, norm_mix[i], ops, b_d[li], b_w_glu[li],
                                 state_b_re[li], state_b_im[li])
            outs_b.append(o)
        else:
            xp, xs, o = _mixer_c(xp, xs, norm_mix[i], c_w_in[li], c_w_out[li], c_q_norm[li],
                                 c_k_norm[li], c_rel_bias[li], cache_c_k[li], cache_c_v[li])
            outs_c.append(o)
        xp, xs, hp, hs = _conv_ffn(xp, xs, norm_ffn[i], ffn_w_up[i], ffn_conv_w[i],
                                   ffn_conv_b[i], ffn_w_down[i], state_ffn_conv[i])
        f_p.append(hp)
        f_s.append(hs)
    stack = lambda outs, j: jnp.stack([o[j] for o in outs])
    return (xp, xs,
            *[stack(outs_a, j) for j in range(6)],
            *[stack(outs_c, j) for j in range(4)],
            *[stack(outs_b, j) for j in range(4)],
            jnp.stack(f_p), jnp.stack(f_s))
```

```python
import functools
import math

import jax
import jax.numpy as jnp
from jax import lax
from jax.experimental import pallas as pl
from jax.experimental.pallas import tpu as pltpu

F32 = jnp.float32
BF16 = jnp.bfloat16
I32 = jnp.int32

EPS = 1e-6
NEG_INF = -1e30
ROW_MAX_INIT = -1e29
HI16 = -(1 << 16)
F32_TINY_BITS = 0x00800000
F32_TINY = 2.0 ** -126
ROPE_THETA = 10000.0
CHUNK = 64
HEAD_DIM = 128
LANES = 128

A_HEADS = 16
A_KV_HEADS = 4
A_GROUP = A_HEADS // A_KV_HEADS
IDX_HEADS = 8
IDX_DIM = 64
TOPK_MAX = 256
A_Q = A_HEADS * HEAD_DIM
A_KV = A_KV_HEADS * HEAD_DIM
A_QI = IDX_HEADS * IDX_DIM
IDX_W_SCALE = (IDX_HEADS * IDX_DIM) ** -0.5
A_OFF_K = A_Q
A_OFF_V = A_OFF_K + A_KV
A_OFF_QI = A_OFF_V + A_KV
A_OFF_KI = A_OFF_QI + A_QI
A_OFF_WI = A_OFF_KI + LANES
A_IN_PAD = A_OFF_WI + LANES

S5_GROUP = 16
S5_STATE = 64
S5_CHUNK = 16
S5_LB_GROUPS = LANES // S5_GROUP

C_HEADS = 16
C_BAND_CHUNKS = 8
C_REACH = C_BAND_CHUNKS * CHUNK
REL_CLIP = 128
CONV_W = 3

VMEM_LIMIT = 56 * 1024 * 1024


def _cparams(*sem):
    return pltpu.CompilerParams(dimension_semantics=sem, vmem_limit_bytes=VMEM_LIMIT)


def _pick(n, target, mult):
    best = None
    for t in range(mult, min(n, target) + 1, mult):
        if n % t == 0:
            best = t
    return best if best is not None else n


def _dot(a, b):
    return jnp.dot(a, b, preferred_element_type=F32)


def _dot_nt(a, b):
    return lax.dot_general(a, b, (((1,), (1,)), ((), ())), preferred_element_type=F32)


def _rms(x, g):
    return x * lax.rsqrt(jnp.mean(x * x, axis=-1, keepdims=True) + EPS) * g


def _rms_matmul_kernel(x_ref, g_ref, w_ref, o_ref, xn_ref):
    @pl.when(pl.program_id(1) == 0)
    def _():
        xn_ref[...] = _rms(x_ref[...], g_ref[...]).astype(BF16)

    o_ref[...] = _dot(xn_ref[...], w_ref[...]).astype(o_ref.dtype)


def _rms_matmul(x, g, w, *, tm=1024, tn=1024, out_dtype=F32):
    M, D = x.shape
    N = w.shape[1]
    tm = _pick(M, tm, 16)
    tn = _pick(N, tn, LANES)
    return pl.pallas_call(
        _rms_matmul_kernel,
        grid=(M // tm, N // tn),
        in_specs=[pl.BlockSpec((tm, D), lambda i, j: (i, 0)),
                  pl.BlockSpec((1, D), lambda i, j: (0, 0)),
                  pl.BlockSpec((D, tn), lambda i, j: (0, j))],
        out_specs=pl.BlockSpec((tm, tn), lambda i, j: (i, j)),
        out_shape=jax.ShapeDtypeStruct((M, N), out_dtype),
        scratch_shapes=[pltpu.VMEM((tm, D), BF16)],
        compiler_params=_cparams("parallel", "arbitrary"),
        name="rms_matmul",
    )(x, g, w)


def _rmsnorm_kernel(x_ref, g_ref, o_ref):
    o_ref[...] = _rms(x_ref[...], g_ref[...])


def _rmsnorm(x, g, *, tm=512):
    M, D = x.shape
    tm = _pick(M, tm, 8)
    return pl.pallas_call(
        _rmsnorm_kernel,
        grid=(M // tm,),
        in_specs=[pl.BlockSpec((tm, D), lambda i: (i, 0)),
                  pl.BlockSpec((1, D), lambda i: (0, 0))],
        out_specs=pl.BlockSpec((tm, D), lambda i: (i, 0)),
        out_shape=jax.ShapeDtypeStruct((M, D), F32),
        compiler_params=_cparams("parallel"),
        name="rmsnorm",
    )(x, g)


def _matmul_res_kernel(x_ref, w_ref, r_ref, o_ref):
    o_ref[...] = r_ref[...] + _dot(x_ref[...], w_ref[...])


def _matmul_res(x, w, res, *, tm=512, tn=2048):
    M, K = x.shape
    N = w.shape[1]
    tm = _pick(M, tm, 16)
    tn = _pick(N, tn, LANES)
    return pl.pallas_call(
        _matmul_res_kernel,
        grid=(M // tm, N // tn),
        in_specs=[pl.BlockSpec((tm, K), lambda i, j: (i, 0)),
                  pl.BlockSpec((K, tn), lambda i, j: (0, j)),
                  pl.BlockSpec((tm, tn), lambda i, j: (i, j))],
        out_specs=pl.BlockSpec((tm, tn), lambda i, j: (i, j)),
        out_shape=jax.ShapeDtypeStruct((M, N), F32),
        compiler_params=_cparams("parallel", "arbitrary"),
        name="matmul_res",
    )(x, w, res)


def _rope_tables(pos, dim):
    half = dim // 2
    inv = jnp.power(ROPE_THETA, -jnp.arange(half, dtype=F32) / half)
    ang = pos.astype(F32)[:, None] * inv[None, :]
    cos, sin = jnp.cos(ang), jnp.sin(ang)
    reps = LANES // dim
    return (jnp.tile(jnp.concatenate([cos, cos], axis=-1), (1, reps)),
            jnp.tile(jnp.concatenate([-sin, sin], axis=-1), (1, reps)))


def _a_post_kernel(h_ref, c128_ref, s128_ref, c64_ref, s64_ref, qg_ref, kg_ref,
                   q_ref, kf_ref, kb_ref, vb_ref, qi_ref, kif_ref, kib_ref, wi_ref):
    c128, s128 = c128_ref[...], s128_ref[...]
    c64, s64 = c64_ref[...], s64_ref[...]
    tm = h_ref.shape[0]
    lane = lax.broadcasted_iota(I32, (tm, LANES), 1)

    def rope128(x):
        return x * c128 + pltpu.roll(x, HEAD_DIM // 2, axis=1) * s128

    def rope64(x):
        lo = (lane % IDX_DIM) < (IDX_DIM // 2)
        partner = jnp.where(lo, pltpu.roll(x, LANES - IDX_DIM // 2, axis=1),
                            pltpu.roll(x, IDX_DIM // 2, axis=1))
        return x * c64 + partner * s64

    for h in range(A_HEADS):
        x = h_ref[:, h * LANES:(h + 1) * LANES]
        q_ref[:, h * LANES:(h + 1) * LANES] = (
            rope128(_rms(x, qg_ref[...])) * (HEAD_DIM ** -0.5)).astype(BF16)
    for h in range(A_KV_HEADS):
        x = h_ref[:, A_OFF_K + h * LANES:A_OFF_K + (h + 1) * LANES]
        k = rope128(_rms(x, kg_ref[...]))
        kf_ref[:, h * LANES:(h + 1) * LANES] = k
        kb_ref[:, h * LANES:(h + 1) * LANES] = k.astype(BF16)
    ones = jnp.ones((tm, LANES), BF16)
    for h in range(A_KV_HEADS):
        v = h_ref[:, A_OFF_V + h * LANES:A_OFF_V + (h + 1) * LANES]
        vb_ref[:, 2 * h * LANES:(2 * h + 1) * LANES] = v.astype(BF16)
        vb_ref[:, (2 * h + 1) * LANES:(2 * h + 2) * LANES] = ones
    for p in range(A_QI // LANES):
        x = rope64(h_ref[:, A_OFF_QI + p * LANES:A_OFF_QI + (p + 1) * LANES])
        qi_ref[:, (2 * p) * LANES:(2 * p + 1) * LANES] = (
            jnp.where(lane < IDX_DIM, x, 0.0).astype(BF16))
        qi_ref[:, (2 * p + 1) * LANES:(2 * p + 2) * LANES] = (
            jnp.where(lane >= IDX_DIM, x, 0.0).astype(BF16))
    ki = rope64(h_ref[:, A_OFF_KI:A_OFF_KI + LANES])
    kif_ref[...] = ki
    kib_ref[...] = (ki + pltpu.roll(ki, IDX_DIM, axis=1)).astype(BF16)
    wi_ref[...] = h_ref[:, A_OFF_WI:A_OFF_WI + LANES] * IDX_W_SCALE


def _a_post(h, tabs, qg, kg, *, tm=256):
    M = h.shape[0]
    P = tabs[0].shape[0]
    tm = _pick(math.gcd(M, P), tm, 16)
    nt = P // tm
    row = lambda i: (i, 0)
    tab = lambda i: (i % nt, 0)
    one = lambda i: (0, 0)
    widths = [(A_Q, BF16), (A_KV, F32), (A_KV, BF16), (2 * A_KV, BF16),
              (IDX_HEADS * LANES, BF16), (LANES, F32), (LANES, BF16), (LANES, F32)]
    return pl.pallas_call(
        _a_post_kernel,
        grid=(M // tm,),
        in_specs=[pl.BlockSpec((tm, A_IN_PAD), row)]
        + [pl.BlockSpec((tm, LANES), tab)] * 4
        + [pl.BlockSpec((1, LANES), one)] * 2,
        out_specs=[pl.BlockSpec((tm, w), row) for w, _ in widths],
        out_shape=[jax.ShapeDtypeStruct((M, w), dt) for w, dt in widths],
        compiler_params=_cparams("parallel"),
        name="a_post",
    )(h, *tabs, qg, kg)


def _dsa_kernel(*refs, tq, tk, ksel, causal, l_valid, n_chunks, idx_bits):
    if causal:
        q_ref, qi_ref, wi_ref, k_ref, v_ref, ki_ref, o_ref = refs[:7]
    else:
        q_ref, qi_ref, wi_ref, k_ref, v_ref, ki_ref, ck_ref, cv_ref, cki_ref, o_ref = refs[:10]
    key_scr, top_scr, bias_scr, qst_scr, m_scr, acc_scr = refs[-6:]
    t0 = pl.program_id(1) * tq
    nch = (t0 + tq + tk - 1) // tk if causal else n_chunks
    nslab = tk // LANES
    row = lax.broadcasted_iota(I32, (tq, tk), 0)
    col = lax.broadcasted_iota(I32, (tq, tk), 1)
    lane = lax.broadcasted_iota(I32, (tq, LANES), 1)
    limit = ((t0 + row) // CHUNK + 1) * CHUNK if causal else l_valid

    def chunk_off(c):
        return c * tk if isinstance(c, int) else pl.multiple_of(c * tk, tk)

    def cached(c):
        return not causal and c < n_chunks - 1

    def new_off(c):
        return chunk_off(c) if causal else 0

    def load_ki(c):
        if cached(c):
            x = cki_ref[0, c * tk:(c + 1) * tk, :]
            return jnp.concatenate([x, x], axis=1).astype(BF16)
        return ki_ref[0, pl.ds(new_off(c), tk), :]

    def load_k(c, j):
        if cached(c):
            return ck_ref[0, c * tk:(c + 1) * tk, j, :].astype(BF16)
        return k_ref[0, pl.ds(new_off(c), tk), j * LANES:(j + 1) * LANES]

    def load_v(c, j):
        if cached(c):
            v = cv_ref[0, c * tk:(c + 1) * tk, j, :].astype(BF16)
            return jnp.concatenate([v, jnp.ones((tk, LANES), BF16)], axis=1)
        return v_ref[0, pl.ds(new_off(c), tk), 2 * j * LANES:2 * (j + 1) * LANES]

    def chunk_loop(body):
        if causal:
            lax.fori_loop(0, nch, lambda c, carry: (body(c), carry)[1], 0)
        else:
            for c in range(n_chunks):
                body(c)

    wi = wi_ref[0]

    def score_body(c):
        off = chunk_off(c)
        kic = load_ki(c)
        shs = [_dot_nt(qi_ref[0, :, h * LANES:(h + 1) * LANES], kic) for h in range(IDX_HEADS)]
        s = jnp.zeros((tq, tk), F32)
        for h in range(IDX_HEADS):
            s = s + wi[:, h:h + 1] * jnp.maximum(shs[h], 0.0)
        s = jnp.where(jnp.abs(s) < F32_TINY, 0.0, s)
        s = jnp.where(off + col < limit, s, NEG_INF)
        bits = pltpu.bitcast(s, I32)
        key_scr[:, pl.ds(off, tk)] = bits ^ ((bits >> 31) & 0x7FFFFFFF)
        top_scr[:, pl.ds(off, tk)] = pltpu.bitcast(bits & HI16, F32).astype(BF16)

    chunk_loop(score_body)

    def count_top(cand):
        cbits = (cand ^ ((cand >> 31) & 0x7FFFFFFF)) & HI16
        cbits = jnp.where(cbits > 0, jnp.maximum(cbits, F32_TINY_BITS), cbits)
        cf = pltpu.bitcast(cbits, F32).astype(BF16)
        one, zero = jnp.ones((), BF16), jnp.zeros((), BF16)

        def body(c, acc):
            blk = top_scr[:, pl.ds(chunk_off(c), tk)]
            for s in range(nslab):
                acc = acc + jnp.where(blk[:, s * LANES:(s + 1) * LANES] >= cf, one, zero)
            return acc

        acc = lax.fori_loop(0, nch, body, jnp.zeros((tq, LANES), BF16))
        tot = jnp.sum(acc.astype(F32), axis=-1, keepdims=True)
        return jnp.broadcast_to(tot, (tq, LANES))

    def count(pred):
        def body(c, acc):
            off = chunk_off(c)
            blk = key_scr[:, pl.ds(off, tk)]
            for s in range(nslab):
                sl = slice(s * LANES, (s + 1) * LANES)
                acc = acc + pred(blk[:, sl], lane + (off + s * LANES))
            return acc

        acc = lax.fori_loop(0, nch, body, jnp.zeros((tq, LANES), F32))
        return jnp.broadcast_to(jnp.sum(acc, axis=-1, keepdims=True), (tq, LANES))

    def search(count_ge, last_bit):
        def cond(st):
            i, _, n_ge = st
            return jnp.logical_and(i < last_bit, jnp.max(n_ge) > ksel)

        def body(st):
            i, thr, n_ge = st
            cand = thr + lax.shift_left(jnp.int32(1), 31 - i)
            cnt = count_ge(cand)
            take = cnt >= ksel
            return i + 1, jnp.where(take, cand, thr), jnp.where(take, cnt, n_ge)

        return cond, body

    n_keys = (nch * tk).astype(F32) if causal else float(nch * tk)
    st = (jnp.int32(0), jnp.full((tq, LANES), -2 ** 31, I32), jnp.full((tq, LANES), n_keys, F32))
    st = lax.while_loop(*search(count_top, 16), st)
    full = search(lambda cand: count(lambda key, idx: jnp.where(key >= cand, 1.0, 0.0)), 32)
    _, thr, _ = lax.while_loop(*full, (jnp.int32(16),) + st[1:])

    need = ksel - count(lambda key, idx: jnp.where(key > thr, 1.0, 0.0))
    n_eq = count(lambda key, idx: jnp.where(key == thr, 1.0, 0.0))

    def cut_search():
        def body(i, cut):
            cand = cut + lax.shift_left(jnp.int32(1), idx_bits - 1 - i)
            below = count(lambda key, idx: jnp.where(key == thr, jnp.where(idx < cand, 1.0, 0.0), 0.0))
            return jnp.where(below <= need, cand, cut)

        return lax.fori_loop(0, idx_bits, body, jnp.zeros((tq, LANES), I32))

    cut = lax.cond(jnp.max(n_eq - need) > 0.0, cut_search,
                   lambda: jnp.full((tq, LANES), 2 ** idx_bits, I32))

    thr_t = jnp.tile(thr, (1, nslab))
    cut_t = jnp.tile(cut, (1, nslab))

    def bias_body(c, carry):
        off = chunk_off(c)
        key = key_scr[:, pl.ds(off, tk)]
        idx = off + col
        tie = jnp.where(key == thr_t, jnp.where(idx < cut_t, 0.0, NEG_INF), NEG_INF)
        bias = jnp.where(key > thr_t, 0.0, tie)
        bias_scr[:, pl.ds(off, tk)] = jnp.where(idx < limit, bias, NEG_INF)
        return carry

    lax.fori_loop(0, nch, bias_body, 0)

    for j in range(A_KV_HEADS):
        for g in range(A_GROUP):
            h = j * A_GROUP + g
            qst_scr[j, g * tq:(g + 1) * tq, :] = q_ref[0, :, h * LANES:(h + 1) * LANES]
    m_scr[...] = jnp.full(m_scr.shape, ROW_MAX_INIT, F32)
    acc_scr[...] = jnp.zeros(acc_scr.shape, F32)
    gr = A_GROUP * tq

    def attn_body(c):
        off = chunk_off(c)
        bias = bias_scr[:, pl.ds(off, tk)]
        logits = [_dot_nt(qst_scr[j], load_k(c, j)) for j in range(A_KV_HEADS)]
        for j in range(A_KV_HEADS):
            rows = slice(j * gr, (j + 1) * gr)
            m_old = m_scr[rows, :]
            alphas, ps = [], []
            for g in range(A_GROUP):
                sl = slice(g * tq, (g + 1) * tq)
                sg = logits[j][sl, :] + bias
                m_new = jnp.maximum(m_old[sl, :], jnp.max(sg, axis=-1, keepdims=True))
                alphas.append(jnp.exp(m_old[sl, :] - m_new))
                ps.append(jnp.exp((sg - jnp.tile(m_new, (1, nslab))).astype(BF16)))
                m_scr[j * gr + g * tq:j * gr + (g + 1) * tq, :] = m_new
            alpha = jnp.tile(jnp.concatenate(alphas, axis=0), (1, 2))
            pv = _dot(jnp.concatenate(ps, axis=0), load_v(c, j))
            acc_scr[rows, :] = alpha * acc_scr[rows, :] + pv

    chunk_loop(attn_body)
    for h in range(A_HEADS):
        rows = slice(h * tq, (h + 1) * tq)
        o_ref[0, :, h * LANES:(h + 1) * LANES] = (
            acc_scr[rows, :LANES] / acc_scr[rows, LANES:]).astype(BF16)


def _dsa(q, qi, wi, k, v, ki, cache=None, *, tq, tk, ksel, causal, l_valid):
    B, T, _ = q.shape
    Lk = k.shape[1]
    blk_q = lambda w: pl.BlockSpec((1, tq, w), lambda b, i: (b, i, 0))
    blk_k = lambda n, w: pl.BlockSpec((1, n, w), lambda b, i: (b, 0, 0))
    specs, args = [], []
    L = Lk
    if cache is not None:
        ck, cv, cki = cache
        past = ck.shape[1]
        L = past + Lk
        blk_c = pl.BlockSpec((1, past, A_KV_HEADS, HEAD_DIM), lambda b, i: (b, 0, 0, 0))
        specs = [blk_c, blk_c, blk_k(past, IDX_DIM)]
        args = [ck, cv, cki]
    kern = functools.partial(
        _dsa_kernel, tq=tq, tk=tk, ksel=ksel, causal=causal, l_valid=l_valid,
        n_chunks=-(-l_valid // tk), idx_bits=max(1, (L - 1).bit_length()) + 1)
    return pl.pallas_call(
        kern,
        grid=(B, T // tq),
        in_specs=[blk_q(A_Q), blk_q(IDX_HEADS * LANES), blk_q(LANES),
                  blk_k(Lk, A_KV), blk_k(Lk, 2 * A_KV), blk_k(Lk, LANES)] + specs,
        out_specs=blk_q(A_Q),
        out_shape=jax.ShapeDtypeStruct((B, T, A_Q), BF16),
        scratch_shapes=[pltpu.VMEM((tq, L), I32), pltpu.VMEM((tq, L), BF16),
                        pltpu.VMEM((tq, L), F32),
                        pltpu.VMEM((A_KV_HEADS, A_GROUP * tq, LANES), BF16),
                        pltpu.VMEM((A_HEADS * tq, LANES), F32),
                        pltpu.VMEM((A_HEADS * tq, 2 * LANES), F32)],
        compiler_params=_cparams("parallel", "arbitrary"),
        name="dsa_causal" if causal else "dsa_cached",
    )(q, qi, wi, k, v, ki, *args)


def _pack_a_w_in(w):
    D = w.shape[0]
    o_ki = A_Q + 2 * A_KV + A_QI
    pad = lambda n: jnp.zeros((D, n), w.dtype)
    return jnp.concatenate(
        [w[:, :o_ki], w[:, o_ki:o_ki + IDX_DIM], pad(LANES - IDX_DIM),
         w[:, o_ki + IDX_DIM:], pad(LANES - IDX_HEADS)], axis=1).astype(BF16)


def _mixer_a(xp, xs, g, w_in, w_out, qg, kg, ck, cv, cki):
    B, S, D = xp.shape
    Bs, T, _ = xs.shape
    past = ck.shape[1]
    w_in = _pack_a_w_in(w_in)
    w_out = w_out.astype(BF16)
    g, qg, kg = g[None], qg[None], kg[None]

    def project(x, pos):
        b, t, _ = x.shape
        h = _rms_matmul(x.reshape(b * t, D), g, w_in, tn=A_IN_PAD // 5)
        tabs = _rope_tables(pos, HEAD_DIM) + _rope_tables(pos, IDX_DIM)
        outs = _a_post(h, tabs, qg, kg)
        v = h[:, A_OFF_V:A_OFF_V + A_KV]
        return [o.reshape(b, t, -1) for o in outs] + [v.reshape(b, t, A_KV_HEADS, HEAD_DIM)]

    q, kf, kb, vb, qi, kif, kib, wi, v_p = project(xp, jnp.arange(S))
    tk = _pick(S, 512, LANES)
    o = _dsa(q, qi, wi, kb, vb, kib, tq=min(128, S), tk=tk, ksel=min(TOPK_MAX, S // 4),
             causal=True, l_valid=S)
    yp = _matmul_res(o.reshape(B * S, A_Q), w_out, xp.reshape(B * S, D)).reshape(B, S, D)
    k_p = kf.reshape(B, S, A_KV_HEADS, HEAD_DIM)
    ki_p = kif[..., :IDX_DIM]

    pos_s = jnp.tile(past + jnp.arange(T), Bs)
    q, kf, kb, vb, qi, kif, kib, wi, v_s = project(xs, pos_s)
    L = past + T
    tk = _pick(past, 512, LANES)
    padk = lambda a: jnp.pad(a, ((0, 0), (0, tk - T), (0, 0)))
    o = _dsa(q, qi, wi, padk(kb), padk(vb), padk(kib), (ck, cv, cki),
             tq=T, tk=tk, ksel=min(TOPK_MAX, L // 4), causal=False, l_valid=L)
    ys = _matmul_res(o.reshape(Bs * T, A_Q), w_out, xs.reshape(Bs * T, D)).reshape(Bs, T, D)
    k_s = kf.reshape(Bs, T, A_KV_HEADS, HEAD_DIM)
    return yp, ys, (k_p, v_p, ki_p, k_s, v_s, kif[..., :IDX_DIM])


def _c_post_kernel(h_ref, qg_ref, kg_ref, q_ref, kb_ref, kf_ref, vf_ref):
    W = C_HEADS * HEAD_DIM
    for h in range(C_HEADS):
        sl = slice(h * LANES, (h + 1) * LANES)
        q = _rms(h_ref[:, sl].astype(F32), qg_ref[...])
        q_ref[:, sl] = (q * (HEAD_DIM ** -0.5)).astype(BF16)
        k = _rms(h_ref[:, W + h * LANES:W + (h + 1) * LANES].astype(F32), kg_ref[...])
        kf_ref[:, sl] = k
        kb_ref[:, sl] = k.astype(BF16)
    vf_ref[...] = h_ref[:, 2 * W:3 * W].astype(F32)


def _c_post(h, qg, kg, *, t, keep, tm=256):
    M = h.shape[0]
    W = C_HEADS * HEAD_DIM
    tm = _pick(math.gcd(t, keep), tm, 16) if keep < t else _pick(M, tm, 16)
    row = lambda i: (i, 0)
    if keep < t:
        tps, kps = t // tm, keep // tm
        tail = lambda i: ((i // tps) * kps + jnp.maximum(i % tps - (tps - kps), 0), 0)
    else:
        tail = row
    n_tail = M // t * keep
    return pl.pallas_call(
        _c_post_kernel,
        grid=(M // tm,),
        in_specs=[pl.BlockSpec((tm, 3 * W), row),
                  pl.BlockSpec((1, LANES), lambda i: (0, 0)),
                  pl.BlockSpec((1, LANES), lambda i: (0, 0))],
        out_specs=[pl.BlockSpec((tm, W), row)] * 2 + [pl.BlockSpec((tm, W), tail)] * 2,
        out_shape=[jax.ShapeDtypeStruct((M, W), BF16)] * 2
        + [jax.ShapeDtypeStruct((n_tail, W), F32)] * 2,
        compiler_params=_cparams("arbitrary"),
        name="c_post",
    )(h, qg, kg)


def _band_kernel(q_ref, k_ref, v_ref, b_ref, o_ref, *, sub, nsub, win, first_key):
    t0 = pl.program_id(2) * (sub * nsub)
    starts = [pl.multiple_of(t0 + s * sub, sub) for s in range(nsub)]
    lgs = [_dot_nt(q_ref[0, s * sub:(s + 1) * sub, :], k_ref[0, pl.ds(starts[s], win), :])
           + b_ref[0] for s in range(nsub)]
    if first_key is not None:
        kk = lax.broadcasted_iota(I32, (sub, win), 1)
        lgs = [jnp.where(kk >= first_key - starts[s], lgs[s], NEG_INF) for s in range(nsub)]
    ms = [jnp.max(lg, axis=-1, keepdims=True) for lg in lgs]
    ps = [jnp.exp(lg - m) for lg, m in zip(lgs, ms)]
    ls = [jnp.sum(p, axis=-1, keepdims=True) for p in ps]
    os_ = [_dot(ps[s].astype(BF16), v_ref[0, pl.ds(starts[s], win), :]) for s in range(nsub)]
    for s in range(nsub):
        o_ref[0, s * sub:(s + 1) * sub, :] = (os_[s] / ls[s]).astype(BF16)


def _band(q, k, v, bias, *, sub, nsub, first_key):
    B, T, W = q.shape
    Lk = k.shape[1]
    win = bias.shape[2]
    tq = sub * nsub
    kern = functools.partial(_band_kernel, sub=sub, nsub=nsub, win=win, first_key=first_key)
    return pl.pallas_call(
        kern,
        grid=(B, C_HEADS, T // tq),
        in_specs=[pl.BlockSpec((1, tq, LANES), lambda b, h, i: (b, i, h)),
                  pl.BlockSpec((1, Lk, LANES), lambda b, h, i: (b, 0, h)),
                  pl.BlockSpec((1, Lk, LANES), lambda b, h, i: (b, 0, h)),
                  pl.BlockSpec((1, sub, win), lambda b, h, i: (h, 0, 0))],
        out_specs=pl.BlockSpec((1, tq, LANES), lambda b, h, i: (b, i, h)),
        out_shape=jax.ShapeDtypeStruct((B, T, W), BF16),
        compiler_params=_cparams("parallel", "parallel", "arbitrary"),
        name="band_attn",
    )(q, k, v, bias)


def _mixer_c(xp, xs, g, w_in, w_out, qg, kg, rel, ck, cv):
    B, S, D = xp.shape
    Bs, T, _ = xs.shape
    W = C_HEADS * HEAD_DIM
    cp = ck.shape[1]
    w_in = w_in.astype(BF16)
    w_out = w_out.astype(BF16)
    g, qg, kg = g[None], qg[None], kg[None]
    relf = rel.astype(F32)

    def project(x, keep):
        b, t, _ = x.shape
        h = _rms_matmul(x.reshape(b * t, D), g, w_in, out_dtype=BF16)
        q, kb, kf, vf = _c_post(h, qg, kg, t=t, keep=keep)
        vb = h.reshape(b, t, 3 * W)[:, :, 2 * W:]
        tail = lambda a: a.reshape(b, keep, C_HEADS, HEAD_DIM)
        return q.reshape(b, t, W), tail(kf), kb.reshape(b, t, W), vb, tail(vf)

    def rel_bias(rows, win, shift):
        n = rows + win - 1
        j = jnp.arange(n)
        vec = relf[:, jnp.clip(shift + rows - 1 - j, -REL_CLIP, REL_CLIP) + REL_CLIP]
        flat = jnp.tile(jnp.pad(vec, ((0, 0), (0, 1))), (1, rows))
        toep = flat[:, :rows * n].reshape(-1, rows, n)
        return toep[:, :, rows - 1:rows - 1 + win]

    sub = min(2 * CHUNK, S)
    win = C_REACH + sub
    r = jnp.arange(sub)[:, None]
    kk = jnp.arange(win)[None, :]
    lo = (r // CHUNK) * CHUNK
    bias = jnp.where((kk >= lo) & (kk < lo + C_REACH + CHUNK), rel_bias(sub, win, C_REACH),
                     NEG_INF)
    q, k_p, kb, vb, v_p = project(xp, min(C_REACH, S))
    padk = lambda a: jnp.pad(a, ((0, 0), (C_REACH, 0), (0, 0)))
    o = _band(q, padk(kb), padk(vb), bias, sub=sub, nsub=_pick(S // sub, 8, 1),
              first_key=C_REACH)
    yp = _matmul_res(o.reshape(B * S, W), w_out, xp.reshape(B * S, D)).reshape(B, S, D)

    L = cp + T
    win = -(-L // LANES) * LANES
    bias = jnp.where(jnp.arange(win)[None, :] < L, rel_bias(T, win, cp), NEG_INF)
    q, kf, kb, vb, vf = project(xs, T)
    cat = lambda c, n: jnp.concatenate(
        [c.reshape(Bs, cp, W).astype(BF16), n, jnp.zeros((Bs, win - L, W), BF16)], axis=1)
    o = _band(q, cat(ck, kb), cat(cv, vb), bias, sub=T, nsub=1, first_key=None)
    ys = _matmul_res(o.reshape(Bs * T, W), w_out, xs.reshape(Bs * T, D)).reshape(Bs, T, D)
    return yp, ys, (k_p, v_p, kf, vf)


def _s5_ops_kernel(ar_ref, ai_ref, ldt_ref, br_ref, bi_ref, cr_ref, ci_ref, cr4_ref, ci4_ref,
                   lag_ref, xr_ref, xi_ref, or_ref, oi_ref, lnr_ref, lni_ref, sr_ref, si_ref):
    n, P = xr_ref.shape[:2]
    dt = jnp.exp(ldt_ref[...])
    ar, ai = ar_ref[...], ai_ref[...]
    mag = jnp.exp(ar * dt)
    lr, li = mag * jnp.cos(ai * dt), mag * jnp.sin(ai * dt)
    den = ar * ar + ai * ai
    nr = lr - 1.0
    fr = (nr * ar + li * ai) / den
    fi = (li * ar - nr * ai) / den
    br, bi = br_ref[...], bi_ref[...]
    bbr = fr * br - fi * bi
    bbi = fr * bi + fi * br
    cr, ci = cr_ref[...], ci_ref[...]
    pr, pi = jnp.ones_like(lr), jnp.zeros_like(lr)
    for tau in range(n):
        sr_ref[...] = pr * bbr - pi * bbi
        si_ref[...] = pr * bbi + pi * bbr
        xr_ref[tau] = sr_ref[...].astype(BF16)
        xi_ref[tau] = si_ref[...].astype(BF16)

        def body(p, acc):
            return acc + cr4_ref[p] * sr_ref[p][None] - ci4_ref[p] * si_ref[p][None]

        lag_ref[tau] = lax.fori_loop(0, P, body, jnp.zeros(lag_ref.shape[1:], F32))
        pr, pi = pr * lr - pi * li, pr * li + pi * lr
        or_ref[tau] = (cr * pr - ci * pi).astype(BF16)
        oi_ref[tau] = (-(cr * pi + ci * pr)).astype(BF16)
    lnr_ref[...] = pr
    lni_ref[...] = pi


def _s5_operators(a_re, a_im, log_dt, b_re, b_im, c_re, c_im):
    G, P = a_re.shape
    C = S5_GROUP
    n = S5_CHUNK
    nlb = G // S5_LB_GROUPS
    f = lambda a: a.astype(F32)
    a_t = lambda a: f(a).T.reshape(P, 1, G)
    b_t = lambda a: jnp.transpose(f(a), (1, 2, 0))
    c_t = lambda a: jnp.transpose(f(a), (2, 1, 0))
    crt, cit = c_t(c_re), c_t(c_im)
    sds = lambda *shape: jax.ShapeDtypeStruct(shape, F32)
    lag, xr, xi, o_re, o_im, lnr, lni = pl.pallas_call(
        _s5_ops_kernel,
        out_shape=[sds(n, C, C, G)] + [jax.ShapeDtypeStruct((n, P, C, G), BF16)] * 4
        + [sds(P, 1, G)] * 2,
        scratch_shapes=[pltpu.VMEM((P, C, G), F32)] * 2,
        compiler_params=pltpu.CompilerParams(vmem_limit_bytes=VMEM_LIMIT),
        name="s5_ops",
    )(a_t(a_re), a_t(a_im), f(log_dt).reshape(1, 1, G), b_t(b_re), b_t(b_im), crt, cit,
      crt.reshape(P, C, 1, G), cit.reshape(P, C, 1, G))

    eye = jnp.eye(S5_LB_GROUPS, dtype=F32)
    split = lambda a: a.reshape(a.shape[:-1] + (nlb, S5_LB_GROUPS))
    taps = jnp.einsum('tdclg,gh->ltgchd', split(lag), eye).reshape(nlb, n, LANES, LANES)

    def inc(x):
        return jnp.einsum('tpclg,gh->ltgchp', split(x[::-1]), eye).reshape(
            nlb, n * LANES, S5_LB_GROUPS * P)

    def out(x):
        return jnp.einsum('tpclg,gh->lgpthc', split(x), eye).reshape(
            nlb, S5_LB_GROUPS * P, n * LANES)

    lam_n = tuple(a.reshape(P, G).T.reshape(1, 1, G * P) for a in (lnr, lni))
    bf = lambda a: a.astype(BF16)
    return bf(taps), bf(inc(xr)), bf(inc(xi)), bf(out(o_re)), bf(out(o_im)), lam_n


def _chunk_rows(u_ref, tc):
    return jnp.concatenate(
        [u_ref[pl.ds(t, tc, stride=S5_CHUNK), :] for t in range(S5_CHUNK)], axis=1)


def _s5_inc_kernel(u_ref, wr_ref, wi_ref, sr_ref, si_ref):
    u = _chunk_rows(u_ref, sr_ref.shape[0]).astype(BF16)
    sr_ref[...] = _dot(u, wr_ref[0])
    si_ref[...] = _dot(u, wi_ref[0])


def _s5_inc(u, w_re, w_im, *, tc):
    Nc = u.shape[0] // S5_CHUNK
    nlb, K, Wn = w_re.shape
    wspec = pl.BlockSpec((1, K, Wn), lambda lb, i: (lb, 0, 0))
    ospec = pl.BlockSpec((tc, Wn), lambda lb, i: (i, lb))
    return pl.pallas_call(
        _s5_inc_kernel,
        grid=(nlb, Nc // tc),
        in_specs=[pl.BlockSpec((tc * S5_CHUNK, LANES), lambda lb, i: (i, lb)), wspec, wspec],
        out_specs=[ospec, ospec],
        out_shape=[jax.ShapeDtypeStruct((Nc, nlb * Wn), F32)] * 2,
        compiler_params=_cparams("parallel", "arbitrary"),
        name="s5_inc",
    )(u, w_re, w_im)


def _s5_scan_kernel(sr_ref, si_ref, lr_ref, li_ref, h0r_ref, h0i_ref,
                    pr_ref, pi_ref, fr_ref, fi_ref):
    lr, li = lr_ref[...], li_ref[...]
    nck = sr_ref.shape[1]

    def body(k, carry):
        hr, hi = carry
        pr_ref[:, pl.ds(k, 1), :] = hr
        pi_ref[:, pl.ds(k, 1), :] = hi
        sr = sr_ref[:, pl.ds(k, 1), :]
        si = si_ref[:, pl.ds(k, 1), :]
        return lr * hr - li * hi + sr, lr * hi + li * hr + si

    hr, hi = lax.fori_loop(0, nck, body, (h0r_ref[...], h0i_ref[...]))
    fr_ref[...] = hr
    fi_ref[...] = hi


def _s5_scan(sr, si, lam, h0r, h0i, *, cb=1024):
    B, nck, GP = sr.shape
    cb = _pick(GP, cb, LANES)
    full = pl.BlockSpec((B, nck, cb), lambda j: (0, 0, j))
    vec = pl.BlockSpec((1, 1, cb), lambda j: (0, 0, j))
    st = pl.BlockSpec((B, 1, cb), lambda j: (0, 0, j))
    return pl.pallas_call(
        _s5_scan_kernel,
        grid=(GP // cb,),
        in_specs=[full, full, vec, vec, st, st],
        out_specs=[full, full, st, st],
        out_shape=[jax.ShapeDtypeStruct((B, nck, GP), F32)] * 2
        + [jax.ShapeDtypeStruct((B, 1, GP), F32)] * 2,
        compiler_params=_cparams("parallel"),
        name="s5_scan",
    )(sr, si, lam[0], lam[1], h0r, h0i)


def _s5_out_kernel(u_ref, taps_ref, hr_ref, hi_ref, wr_ref, wi_ref, d_ref, o_ref, toep_ref):
    n = S5_CHUNK
    tc = hr_ref.shape[0]

    @pl.when(pl.program_id(1) == 0)
    def _():
        zero = jnp.zeros((LANES, LANES), BF16)
        for t in range(n):
            for v in range(n):
                blk = taps_ref[0, v - t] if v >= t else zero
                toep_ref[t * LANES:(t + 1) * LANES, v * LANES:(v + 1) * LANES] = blk

    u = _chunk_rows(u_ref, tc)
    y = _dot(u.astype(BF16), toep_ref[...])
    y = y + _dot(hr_ref[...].astype(BF16), wr_ref[0]) + _dot(hi_ref[...].astype(BF16), wi_ref[0])
    y = jax.nn.gelu(y + jnp.tile(d_ref[...], (1, n)) * u)
    for t in range(n):
        o_ref[pl.ds(t, tc, stride=n), :] = y[:, t * LANES:(t + 1) * LANES]


def _s5_out(u, taps, hr, hi, w_re, w_im, d, *, tc):
    M, D = u.shape
    n = S5_CHUNK
    nlb = taps.shape[0]
    Wn, K = w_re.shape[1:]
    hspec = pl.BlockSpec((tc, Wn), lambda lb, i: (i, lb))
    uspec = pl.BlockSpec((tc * n, LANES), lambda lb, i: (i, lb))
    return pl.pallas_call(
        _s5_out_kernel,
        grid=(nlb, M // (tc * n)),
        in_specs=[uspec,
                  pl.BlockSpec((1, n, LANES, LANES), lambda lb, i: (lb, 0, 0, 0)), hspec, hspec,
                  pl.BlockSpec((1, Wn, K), lambda lb, i: (lb, 0, 0)),
                  pl.BlockSpec((1, Wn, K), lambda lb, i: (lb, 0, 0)),
                  pl.BlockSpec((1, LANES), lambda lb, i: (0, lb))],
        out_specs=uspec,
        out_shape=jax.ShapeDtypeStruct((M, D), F32),
        scratch_shapes=[pltpu.VMEM((K, K), BF16)],
        compiler_params=_cparams("parallel", "arbitrary"),
        name="s5_out",
    )(u, taps, hr, hi, w_re, w_im, d)


def _glu_res_kernel(x_ref, wa_ref, wb_ref, r_ref, o_ref, xc_ref):
    @pl.when(pl.program_id(1) == 0)
    def _():
        xc_ref[...] = x_ref[...].astype(BF16)

    x = xc_ref[...]
    a = _dot(x, wa_ref[...])
    b = _dot(x, wb_ref[...])
    o_ref[...] = r_ref[...] + a * jax.nn.sigmoid(b)


def _glu_res(x, w, res, *, tm=1024, tn=512):
    M, D = x.shape
    N = w.shape[1] // 2
    tm = _pick(M, tm, 16)
    tn = _pick(N, tn, LANES)
    nj = N // tn
    return pl.pallas_call(
        _glu_res_kernel,
        grid=(M // tm, nj),
        in_specs=[pl.BlockSpec((tm, D), lambda i, j: (i, 0)),
                  pl.BlockSpec((D, tn), lambda i, j: (0, j)),
                  pl.BlockSpec((D, tn), lambda i, j: (0, nj + j)),
                  pl.BlockSpec((tm, tn), lambda i, j: (i, j))],
        out_specs=pl.BlockSpec((tm, tn), lambda i, j: (i, j)),
        out_shape=jax.ShapeDtypeStruct((M, N), F32),
        scratch_shapes=[pltpu.VMEM((tm, D), BF16)],
        compiler_params=_cparams("parallel", "arbitrary"),
        name="glu_res",
    )(x, w, w, res)


def _mixer_b(xp, xs, g, ops, d, w_glu, h0r, h0i):
    taps, inc_re, inc_im, out_re, out_im, lam = ops
    n = S5_CHUNK
    GP = lam[0].shape[-1]
    g, d = g[None], d.astype(F32)[None]
    w_glu = w_glu.astype(BF16)

    def stream(x, hr0, hi0):
        B, T, D = x.shape
        nck = T // n
        Nc = B * nck
        x2 = x.reshape(B * T, D)
        u = _rmsnorm(x2, g)
        tc = _pick(Nc, 256, 8)
        sr, si = _s5_inc(u, inc_re, inc_im, tc=tc)
        pr, pi, fr, fi = _s5_scan(sr.reshape(B, nck, GP), si.reshape(B, nck, GP), lam, hr0, hi0)
        gy = _s5_out(u, taps, pr.reshape(Nc, GP), pi.reshape(Nc, GP), out_re, out_im, d, tc=tc)
        y = _glu_res(gy, w_glu, x2)
        shp = (B, GP // S5_STATE, S5_STATE)
        return y.reshape(B, T, D), fr.reshape(shp), fi.reshape(shp)

    B = xp.shape[0]
    zero = jnp.zeros((B, 1, GP), F32)
    yp, rp, ip = stream(xp, zero, zero)
    Bs = xs.shape[0]
    ys, rs, is_ = stream(xs, h0r.astype(F32).reshape(Bs, 1, GP), h0i.astype(F32).reshape(Bs, 1, GP))
    return yp, ys, (rp, ip, rs, is_)


FFN_HALO = 16
FFN_TAIL = 8


def _ffn_kernel(*refs, seq_tiles, seq_len, halo):
    if halo:
        (x_ref, xh_ref, g_ref, wa_ref, wv_ref, cwa_ref, cwv_ref, cba_ref, cbv_ref, wd_ref,
         o_ref, ta_ref, tv_ref, xn_ref, xhn_ref) = refs
    else:
        (x_ref, p1a_ref, p1v_ref, p2a_ref, p2v_ref, g_ref, wa_ref, wv_ref, cwa_ref, cwv_ref,
         cba_ref, cbv_ref, wd_ref, o_ref, ta_ref, tv_ref, xn_ref, h_ref) = refs
    i, f = pl.program_id(0), pl.program_id(1)
    tm = x_ref.shape[0]

    @pl.when(f == 0)
    def _():
        x = x_ref[...]
        xn_ref[...] = _rms(x, g_ref[...]).astype(BF16)
        o_ref[...] = x
        if halo:
            xhn_ref[...] = _rms(xh_ref[...], g_ref[...]).astype(BF16)

    xn = xn_ref[...]
    tf = wa_ref.shape[1]
    r = lax.broadcasted_iota(I32, (tm, tf), 0)

    def conv(w_ref, cw_ref, cb_ref, p1_ref, p2_ref, t_ref):
        h = _dot(xn, w_ref[...])
        h1 = pltpu.roll(h, 1, axis=0)
        h2 = pltpu.roll(h, 2, axis=0)
        if halo:
            t_ref[...] = h[tm - FFN_TAIL:, :]
            hh = _dot(xhn_ref[...], w_ref[...])
            hh = hh * jnp.where(i % seq_tiles == 0, 0.0, 1.0)
            last, prev = hh[-1:, :], hh[-2:-1, :]
            h1 = jnp.where(r == 0, last, h1)
            h2 = jnp.where(r == 0, prev, jnp.where(r == 1, last, h2))
        else:
            for s in range(tf // LANES):
                sl = slice(s * LANES, (s + 1) * LANES)
                h_ref[s] = h[:, sl]
                for k in range(CONV_W - 1):
                    rows = pl.ds(seq_len - (CONV_W - 1) + k, tm // seq_len, stride=seq_len)
                    t_ref[k, :, sl] = h_ref.at[s][rows, :]
            t = r % seq_len
            h1 = jnp.where(t >= 1, h1, p1_ref[...])
            h2 = jnp.where(t >= 2, h2, p2_ref[...])
        cw = cw_ref[...]
        return cb_ref[...] + cw[0:1] * h2 + cw[1:2] * h1 + cw[2:3] * h

    if halo:
        ca = conv(wa_ref, cwa_ref, cba_ref, None, None, ta_ref)
        cv = conv(wv_ref, cwv_ref, cbv_ref, None, None, tv_ref)
    else:
        ca = conv(wa_ref, cwa_ref, cba_ref, p1a_ref, p2a_ref, ta_ref)
        cv = conv(wv_ref, cwv_ref, cbv_ref, p1v_ref, p2v_ref, tv_ref)
    act = (jax.nn.silu(ca) * cv).astype(BF16)
    o_ref[...] += _dot(act, wd_ref[...])


def _ffn(x, g, w_up, conv_w, conv_b, w_down, *, seq_len, hist=None, tm=512, tf=1024):
    M, D = x.shape
    F = w_down.shape[0]
    halo = hist is None
    n_seq = M // seq_len
    tm = _pick(seq_len, tm, FFN_HALO) if halo else M
    tf = _pick(F, tf, LANES)
    nf = F // tf
    xs = pl.BlockSpec((tm, D), lambda i, f: (i, 0))
    a_col = lambda i, f: (0, f)
    v_col = lambda i, f: (0, nf + f)
    wspecs = [pl.BlockSpec((1, D), lambda i, f: (0, 0)),
              pl.BlockSpec((D, tf), a_col), pl.BlockSpec((D, tf), v_col),
              pl.BlockSpec((CONV_W, tf), a_col), pl.BlockSpec((CONV_W, tf), v_col),
              pl.BlockSpec((1, tf), a_col), pl.BlockSpec((1, tf), v_col),
              pl.BlockSpec((tf, D), lambda i, f: (f, 0))]
    wargs = [g, w_up, w_up, conv_w, conv_w, conv_b, conv_b, w_down]
    scratch = [pltpu.VMEM((tm, D), BF16)]
    seq_tiles = max(seq_len // tm, 1)
    if halo:
        per = tm // FFN_HALO
        extra = [pl.BlockSpec((FFN_HALO, D), lambda i, f: (jnp.maximum(i * per - 1, 0), 0))]
        eargs = [x]
        scratch.append(pltpu.VMEM((FFN_HALO, D), BF16))
        tspec = pl.BlockSpec((FFN_TAIL, tf), lambda i, f: (i // seq_tiles, f))
        tshape = jax.ShapeDtypeStruct((n_seq * FFN_TAIL, F), F32)
    else:
        ha = pl.BlockSpec((tm, tf), lambda i, f: (i, f))
        hv = pl.BlockSpec((tm, tf), lambda i, f: (i, nf + f))
        extra = [ha, hv, ha, hv]
        eargs = [hist[0], hist[0], hist[1], hist[1]]
        scratch.append(pltpu.VMEM((tf // LANES, tm, LANES), F32))
        tspec = pl.BlockSpec((CONV_W - 1, n_seq, tf), lambda i, f: (0, 0, f))
        tshape = jax.ShapeDtypeStruct((CONV_W - 1, n_seq, F), F32)
    kern = functools.partial(_ffn_kernel, seq_tiles=seq_tiles, seq_len=seq_len, halo=halo)
    y, ta, tv = pl.pallas_call(
        kern,
        grid=(M // tm, nf),
        in_specs=[xs] + extra + wspecs,
        out_specs=[pl.BlockSpec((tm, D), lambda i, f: (i, 0)), tspec, tspec],
        out_shape=[jax.ShapeDtypeStruct((M, D), F32), tshape, tshape],
        scratch_shapes=scratch,
        compiler_params=_cparams("arbitrary", "arbitrary"),
        name="conv_ffn" if halo else "conv_ffn_hist",
    )(x, *eargs, *wargs)
    if halo:
        tail = lambda t: t.reshape(n_seq, FFN_TAIL, F)[:, FFN_TAIL - (CONV_W - 1):]
    else:
        tail = lambda t: jnp.swapaxes(t, 0, 1)
    return y, jnp.concatenate([tail(ta), tail(tv)], axis=-1)


def _conv_ffn(xp, xs, g, w_up, conv_w, conv_b, w_down, hist_s):
    B, S, D = xp.shape
    Bs, T, _ = xs.shape
    F2 = w_up.shape[1]
    g = g[None]
    w_up = w_up.astype(BF16)
    w_down = w_down.astype(BF16)
    conv_b = conv_b[None]
    yp, hp = _ffn(xp.reshape(B * S, D), g, w_up, conv_w, conv_b, w_down, seq_len=S)
    hs = hist_s.astype(F32)
    t = jnp.arange(T)[None, :, None]
    p1 = jnp.where(t == 0, hs[:, -1:], 0.0).reshape(Bs * T, F2)
    p2 = jnp.where(t == 0, hs[:, -2:-1], jnp.where(t == 1, hs[:, -1:], 0.0)).reshape(Bs * T, F2)
    ys, hs_new = _ffn(xs.reshape(Bs * T, D), g, w_up, conv_w, conv_b, w_down, seq_len=T,
                      hist=(p1, p2))
    return yp.reshape(B, S, D), ys.reshape(Bs, T, D), hp, hs_new


def kernel(x_prompt, x_sample, cache_a_k, cache_a_v, cache_a_kidx, cache_c_k, cache_c_v, state_b_re, state_b_im, state_ffn_conv, norm_mix, norm_ffn, a_w_in, a_w_out, a_q_norm, a_k_norm, b_a_re, b_a_im, b_log_dt, b_b_re, b_b_im, b_c_re, b_c_im, b_d, b_w_glu, c_w_in, c_w_out, c_q_norm, c_k_norm, c_rel_bias, ffn_w_up, ffn_conv_w, ffn_conv_b, ffn_w_down):
    xp, xs = x_prompt, x_sample
    depth = norm_mix.shape[0]
    outs_a, outs_b, outs_c, f_p, f_s = [], [], [], [], []
    for i in range(depth):
        li, kind = i // 3, i % 3
        if kind == 0:
            xp, xs, o = _mixer_a(xp, xs, norm_mix[i], a_w_in[li], a_w_out[li], a_q_norm[li],
                                 a_k_norm[li], cache_a_k[li], cache_a_v[li], cache_a_kidx[li])
            outs_a.append(o)
        elif kind == 1:
            ops = _s5_operators(b_a_re[li], b_a_im[li], b_log_dt[li], b_b_re[li], b_b_im[li],
                                b_c_re[li], b_c_im[li])
            xp, xs, o = _mixer_b(xp, xs, norm_mix[i], ops, b_d[li], b_w_glu[li],
                                 state_b_re[li], state_b_im[li])
            outs_b.append(o)
        else:
            xp, xs, o = _mixer_c(xp, xs, norm_mix[i], c_w_in[li], c_w_out[li], c_q_norm[li],
                                 c_k_norm[li], c_rel_bias[li], cache_c_k[li], cache_c_v[li])
            outs_c.append(o)
        xp, xs, hp, hs = _conv_ffn(xp, xs, norm_ffn[i], ffn_w_up[i], ffn_conv_w[i],
                                   ffn_conv_b[i], ffn_w_down[i], state_ffn_conv[i])
        f_p.append(hp)
        f_s.append(hs)
    stack = lambda outs, j: jnp.stack([o[j] for o in outs])
    return (xp, xs,
            *[stack(outs_a, j) for j in range(6)],
            *[stack(outs_c, j) for j in range(4)],
            *[stack(outs_b, j) for j in range(4)],
            jnp.stack(f_p), jnp.stack(f_s))
```

```python
import functools
import math

import jax
import jax.numpy as jnp
from jax import lax
from jax.experimental import pallas as pl
from jax.experimental.pallas import tpu as pltpu

F32 = jnp.float32
BF16 = jnp.bfloat16
I32 = jnp.int32

EPS = 1e-6
NEG_INF = -1e30
ROW_MAX_INIT = -1e29
ROPE_THETA = 10000.0
CHUNK = 64
HEAD_DIM = 128
LANES = 128

A_HEADS = 16
A_KV_HEADS = 4
A_GROUP = A_HEADS // A_KV_HEADS
IDX_HEADS = 8
IDX_DIM = 64
TOPK_MAX = 256
A_Q = A_HEADS * HEAD_DIM
A_KV = A_KV_HEADS * HEAD_DIM
A_QI = IDX_HEADS * IDX_DIM
IDX_W_SCALE = (IDX_HEADS * IDX_DIM) ** -0.5
A_OFF_K = A_Q
A_OFF_V = A_OFF_K + A_KV
A_OFF_QI = A_OFF_V + A_KV
A_OFF_KI = A_OFF_QI + A_QI
A_OFF_WI = A_OFF_KI + LANES
A_IN_PAD = A_OFF_WI + LANES

S5_GROUP = 16
S5_STATE = 64
S5_CHUNK = 16
S5_LB_GROUPS = LANES // S5_GROUP

C_HEADS = 16
C_BAND_CHUNKS = 8
C_REACH = C_BAND_CHUNKS * CHUNK
REL_CLIP = 128
CONV_W = 3

VMEM_LIMIT = 56 * 1024 * 1024


def _cparams(*sem):
    return pltpu.CompilerParams(dimension_semantics=sem, vmem_limit_bytes=VMEM_LIMIT)


def _pick(n, target, mult):
    best = None
    for t in range(mult, min(n, target) + 1, mult):
        if n % t == 0:
            best = t
    return best if best is not None else n


def _dot(a, b):
    return jnp.dot(a, b, preferred_element_type=F32)


def _dot_nt(a, b):
    return lax.dot_general(a, b, (((1,), (1,)), ((), ())), preferred_element_type=F32)


def _rms(x, g):
    return x * lax.rsqrt(jnp.mean(x * x, axis=-1, keepdims=True) + EPS) * g


def _rms_matmul_kernel(x_ref, g_ref, w_ref, o_ref, xn_ref):
    @pl.when(pl.program_id(1) == 0)
    def _():
        xn_ref[...] = _rms(x_ref[...], g_ref[...]).astype(BF16)

    o_ref[...] = _dot(xn_ref[...], w_ref[...]).astype(o_ref.dtype)


def _rms_matmul(x, g, w, *, tm=1024, tn=1024, out_dtype=F32):
    M, D = x.shape
    N = w.shape[1]
    tm = _pick(M, tm, 16)
    tn = _pick(N, tn, LANES)
    return pl.pallas_call(
        _rms_matmul_kernel,
        grid=(M // tm, N // tn),
        in_specs=[pl.BlockSpec((tm, D), lambda i, j: (i, 0)),
                  pl.BlockSpec((1, D), lambda i, j: (0, 0)),
                  pl.BlockSpec((D, tn), lambda i, j: (0, j))],
        out_specs=pl.BlockSpec((tm, tn), lambda i, j: (i, j)),
        out_shape=jax.ShapeDtypeStruct((M, N), out_dtype),
        scratch_shapes=[pltpu.VMEM((tm, D), BF16)],
        compiler_params=_cparams("parallel", "arbitrary"),
        name="rms_matmul",
    )(x, g, w)


def _rmsnorm_kernel(x_ref, g_ref, o_ref):
    o_ref[...] = _rms(x_ref[...], g_ref[...])


def _rmsnorm(x, g, *, tm=512):
    M, D = x.shape
    tm = _pick(M, tm, 8)
    return pl.pallas_call(
        _rmsnorm_kernel,
        grid=(M // tm,),
        in_specs=[pl.BlockSpec((tm, D), lambda i: (i, 0)),
                  pl.BlockSpec((1, D), lambda i: (0, 0))],
        out_specs=pl.BlockSpec((tm, D), lambda i: (i, 0)),
        out_shape=jax.ShapeDtypeStruct((M, D), F32),
        compiler_params=_cparams("parallel"),
        name="rmsnorm",
    )(x, g)


def _matmul_res_kernel(x_ref, w_ref, r_ref, o_ref):
    o_ref[...] = r_ref[...] + _dot(x_ref[...], w_ref[...])


def _matmul_res(x, w, res, *, tm=512, tn=2048):
    M, K = x.shape
    N = w.shape[1]
    tm = _pick(M, tm, 16)
    tn = _pick(N, tn, LANES)
    return pl.pallas_call(
        _matmul_res_kernel,
        grid=(M // tm, N // tn),
        in_specs=[pl.BlockSpec((tm, K), lambda i, j: (i, 0)),
                  pl.BlockSpec((K, tn), lambda i, j: (0, j)),
                  pl.BlockSpec((tm, tn), lambda i, j: (i, j))],
        out_specs=pl.BlockSpec((tm, tn), lambda i, j: (i, j)),
        out_shape=jax.ShapeDtypeStruct((M, N), F32),
        compiler_params=_cparams("parallel", "arbitrary"),
        name="matmul_res",
    )(x, w, res)


def _rope_tables(pos, dim):
    half = dim // 2
    inv = jnp.power(ROPE_THETA, -jnp.arange(half, dtype=F32) / half)
    ang = pos.astype(F32)[:, None] * inv[None, :]
    cos, sin = jnp.cos(ang), jnp.sin(ang)
    reps = LANES // dim
    return (jnp.tile(jnp.concatenate([cos, cos], axis=-1), (1, reps)),
            jnp.tile(jnp.concatenate([-sin, sin], axis=-1), (1, reps)))


def _a_post_kernel(h_ref, c128_ref, s128_ref, c64_ref, s64_ref, qg_ref, kg_ref,
                   q_ref, kf_ref, kb_ref, vb_ref, qi_ref, kif_ref, kib_ref, wi_ref):
    c128, s128 = c128_ref[...], s128_ref[...]
    c64, s64 = c64_ref[...], s64_ref[...]
    tm = h_ref.shape[0]
    lane = lax.broadcasted_iota(I32, (tm, LANES), 1)

    def rope128(x):
        return x * c128 + pltpu.roll(x, HEAD_DIM // 2, axis=1) * s128

    def rope64(x):
        lo = (lane % IDX_DIM) < (IDX_DIM // 2)
        partner = jnp.where(lo, pltpu.roll(x, LANES - IDX_DIM // 2, axis=1),
                            pltpu.roll(x, IDX_DIM // 2, axis=1))
        return x * c64 + partner * s64

    for h in range(A_HEADS):
        x = h_ref[:, h * LANES:(h + 1) * LANES]
        q_ref[:, h * LANES:(h + 1) * LANES] = (
            rope128(_rms(x, qg_ref[...])) * (HEAD_DIM ** -0.5)).astype(BF16)
    for h in range(A_KV_HEADS):
        x = h_ref[:, A_OFF_K + h * LANES:A_OFF_K + (h + 1) * LANES]
        k = rope128(_rms(x, kg_ref[...]))
        kf_ref[:, h * LANES:(h + 1) * LANES] = k
        kb_ref[:, h * LANES:(h + 1) * LANES] = k.astype(BF16)
    ones = jnp.ones((tm, LANES), BF16)
    for h in range(A_KV_HEADS):
        v = h_ref[:, A_OFF_V + h * LANES:A_OFF_V + (h + 1) * LANES]
        vb_ref[:, 2 * h * LANES:(2 * h + 1) * LANES] = v.astype(BF16)
        vb_ref[:, (2 * h + 1) * LANES:(2 * h + 2) * LANES] = ones
    for p in range(A_QI // LANES):
        x = rope64(h_ref[:, A_OFF_QI + p * LANES:A_OFF_QI + (p + 1) * LANES])
        qi_ref[:, (2 * p) * LANES:(2 * p + 1) * LANES] = (
            jnp.where(lane < IDX_DIM, x, 0.0).astype(BF16))
        qi_ref[:, (2 * p + 1) * LANES:(2 * p + 2) * LANES] = (
            jnp.where(lane >= IDX_DIM, x, 0.0).astype(BF16))
    ki = rope64(h_ref[:, A_OFF_KI:A_OFF_KI + LANES])
    kif_ref[...] = ki
    kib_ref[...] = (ki + pltpu.roll(ki, IDX_DIM, axis=1)).astype(BF16)
    wi_ref[...] = h_ref[:, A_OFF_WI:A_OFF_WI + LANES] * IDX_W_SCALE


def _a_post(h, tabs, qg, kg, *, tm=256):
    M = h.shape[0]
    P = tabs[0].shape[0]
    tm = _pick(math.gcd(M, P), tm, 16)
    nt = P // tm
    row = lambda i: (i, 0)
    tab = lambda i: (i % nt, 0)
    one = lambda i: (0, 0)
    widths = [(A_Q, BF16), (A_KV, F32), (A_KV, BF16), (2 * A_KV, BF16),
              (IDX_HEADS * LANES, BF16), (LANES, F32), (LANES, BF16), (LANES, F32)]
    return pl.pallas_call(
        _a_post_kernel,
        grid=(M // tm,),
        in_specs=[pl.BlockSpec((tm, A_IN_PAD), row)]
        + [pl.BlockSpec((tm, LANES), tab)] * 4
        + [pl.BlockSpec((1, LANES), one)] * 2,
        out_specs=[pl.BlockSpec((tm, w), row) for w, _ in widths],
        out_shape=[jax.ShapeDtypeStruct((M, w), dt) for w, dt in widths],
        compiler_params=_cparams("parallel"),
        name="a_post",
    )(h, *tabs, qg, kg)


def _dsa_kernel(*refs, tq, tk, ksel, causal, l_valid, n_chunks, idx_bits):
    if causal:
        q_ref, qi_ref, wi_ref, k_ref, v_ref, ki_ref, o_ref = refs[:7]
    else:
        q_ref, qi_ref, wi_ref, k_ref, v_ref, ki_ref, ck_ref, cv_ref, cki_ref, o_ref = refs[:10]
    key_scr, bias_scr, qst_scr, m_scr, acc_scr = refs[-5:]
    t0 = pl.program_id(1) * tq
    nch = (t0 + tq + tk - 1) // tk if causal else n_chunks
    nslab = tk // LANES
    row = lax.broadcasted_iota(I32, (tq, tk), 0)
    col = lax.broadcasted_iota(I32, (tq, tk), 1)
    lane = lax.broadcasted_iota(I32, (tq, LANES), 1)
    limit = ((t0 + row) // CHUNK + 1) * CHUNK if causal else l_valid

    def chunk_off(c):
        return c * tk if isinstance(c, int) else pl.multiple_of(c * tk, tk)

    def cached(c):
        return not causal and c < n_chunks - 1

    def new_off(c):
        return chunk_off(c) if causal else 0

    def load_ki(c):
        if cached(c):
            x = cki_ref[0, c * tk:(c + 1) * tk, :]
            return jnp.concatenate([x, x], axis=1).astype(BF16)
        return ki_ref[0, pl.ds(new_off(c), tk), :]

    def load_k(c, j):
        if cached(c):
            return ck_ref[0, c * tk:(c + 1) * tk, j, :].astype(BF16)
        return k_ref[0, pl.ds(new_off(c), tk), j * LANES:(j + 1) * LANES]

    def load_v(c, j):
        if cached(c):
            v = cv_ref[0, c * tk:(c + 1) * tk, j, :].astype(BF16)
            return jnp.concatenate([v, jnp.ones((tk, LANES), BF16)], axis=1)
        return v_ref[0, pl.ds(new_off(c), tk), 2 * j * LANES:2 * (j + 1) * LANES]

    def chunk_loop(body):
        if causal:
            lax.fori_loop(0, nch, lambda c, carry: (body(c), carry)[1], 0)
        else:
            for c in range(n_chunks):
                body(c)

    wi = wi_ref[0]

    def score_body(c):
        off = chunk_off(c)
        kic = load_ki(c)
        shs = [_dot_nt(qi_ref[0, :, h * LANES:(h + 1) * LANES], kic) for h in range(IDX_HEADS)]
        s = jnp.zeros((tq, tk), F32)
        for h in range(IDX_HEADS):
            s = s + wi[:, h:h + 1] * jnp.maximum(shs[h], 0.0)
        s = jnp.where(s == 0.0, 0.0, s)
        s = jnp.where(off + col < limit, s, NEG_INF)
        bits = pltpu.bitcast(s, I32)
        key_scr[:, pl.ds(off, tk)] = bits ^ ((bits >> 31) & 0x7FFFFFFF)

    chunk_loop(score_body)

    def count(pred):
        def body(c, acc):
            off = chunk_off(c)
            blk = key_scr[:, pl.ds(off, tk)]
            for s in range(nslab):
                sl = slice(s * LANES, (s + 1) * LANES)
                acc = acc + pred(blk[:, sl], lane + (off + s * LANES))
            return acc

        acc = lax.fori_loop(0, nch, body, jnp.zeros((tq, LANES), F32))
        return jnp.broadcast_to(jnp.sum(acc, axis=-1, keepdims=True), (tq, LANES))

    def thr_cond(st):
        i, _, n_ge = st
        return jnp.logical_and(i < 32, jnp.max(n_ge) > ksel)

    def thr_body(st):
        i, thr, n_ge = st
        cand = thr + lax.shift_left(jnp.int32(1), 31 - i)
        cnt = count(lambda key, idx: jnp.where(key >= cand, 1.0, 0.0))
        take = cnt >= ksel
        return i + 1, jnp.where(take, cand, thr), jnp.where(take, cnt, n_ge)

    n_keys = (nch * tk).astype(F32) if causal else float(nch * tk)
    _, thr, _ = lax.while_loop(
        thr_cond, thr_body,
        (jnp.int32(0), jnp.full((tq, LANES), -2 ** 31, I32), jnp.full((tq, LANES), n_keys, F32)))

    need = ksel - count(lambda key, idx: jnp.where(key > thr, 1.0, 0.0))
    n_eq = count(lambda key, idx: jnp.where(key == thr, 1.0, 0.0))

    def cut_search():
        def body(i, cut):
            cand = cut + lax.shift_left(jnp.int32(1), idx_bits - 1 - i)
            below = count(lambda key, idx: jnp.where(key == thr, jnp.where(idx < cand, 1.0, 0.0), 0.0))
            return jnp.where(below <= need, cand, cut)

        return lax.fori_loop(0, idx_bits, body, jnp.zeros((tq, LANES), I32))

    cut = lax.cond(jnp.max(n_eq - need) > 0.0, cut_search,
                   lambda: jnp.full((tq, LANES), 2 ** idx_bits, I32))

    thr_t = jnp.tile(thr, (1, nslab))
    cut_t = jnp.tile(cut, (1, nslab))

    def bias_body(c, carry):
        off = chunk_off(c)
        key = key_scr[:, pl.ds(off, tk)]
        idx = off + col
        tie = jnp.where(key == thr_t, jnp.where(idx < cut_t, 0.0, NEG_INF), NEG_INF)
        bias = jnp.where(key > thr_t, 0.0, tie)
        bias_scr[:, pl.ds(off, tk)] = jnp.where(idx < limit, bias, NEG_INF)
        return carry

    lax.fori_loop(0, nch, bias_body, 0)

    for j in range(A_KV_HEADS):
        for g in range(A_GROUP):
            h = j * A_GROUP + g
            qst_scr[j, g * tq:(g + 1) * tq, :] = q_ref[0, :, h * LANES:(h + 1) * LANES]
    m_scr[...] = jnp.full(m_scr.shape, ROW_MAX_INIT, F32)
    acc_scr[...] = jnp.zeros(acc_scr.shape, F32)
    gr = A_GROUP * tq

    def attn_body(c):
        off = chunk_off(c)
        bias = bias_scr[:, pl.ds(off, tk)]
        logits = [_dot_nt(qst_scr[j], load_k(c, j)) for j in range(A_KV_HEADS)]
        for j in range(A_KV_HEADS):
            rows = slice(j * gr, (j + 1) * gr)
            m_old = m_scr[rows, :]
            alphas, ps = [], []
            for g in range(A_GROUP):
                sl = slice(g * tq, (g + 1) * tq)
                sg = logits[j][sl, :] + bias
                m_new = jnp.maximum(m_old[sl, :], jnp.max(sg, axis=-1, keepdims=True))
                alphas.append(jnp.exp(m_old[sl, :] - m_new))
                ps.append(jnp.exp((sg - jnp.tile(m_new, (1, nslab))).astype(BF16)))
                m_scr[j * gr + g * tq:j * gr + (g + 1) * tq, :] = m_new
            alpha = jnp.tile(jnp.concatenate(alphas, axis=0), (1, 2))
            pv = _dot(jnp.concatenate(ps, axis=0), load_v(c, j))
            acc_scr[rows, :] = alpha * acc_scr[rows, :] + pv

    chunk_loop(attn_body)
    for h in range(A_HEADS):
        rows = slice(h * tq, (h + 1) * tq)
        o_ref[0, :, h * LANES:(h + 1) * LANES] = (
            acc_scr[rows, :LANES] / acc_scr[rows, LANES:]).astype(BF16)


def _dsa(q, qi, wi, k, v, ki, cache=None, *, tq, tk, ksel, causal, l_valid):
    B, T, _ = q.shape
    Lk = k.shape[1]
    blk_q = lambda w: pl.BlockSpec((1, tq, w), lambda b, i: (b, i, 0))
    blk_k = lambda n, w: pl.BlockSpec((1, n, w), lambda b, i: (b, 0, 0))
    specs, args = [], []
    L = Lk
    if cache is not None:
        ck, cv, cki, li = cache
        past = ck.shape[2]
        L = past + Lk
        blk_c = pl.BlockSpec((None, 1, past, A_KV_HEADS, HEAD_DIM),
                             lambda b, i: (li, b, 0, 0, 0))
        specs = [blk_c, blk_c,
                 pl.BlockSpec((None, 1, past, IDX_DIM), lambda b, i: (li, b, 0, 0))]
        args = [ck, cv, cki]
    kern = functools.partial(
        _dsa_kernel, tq=tq, tk=tk, ksel=ksel, causal=causal, l_valid=l_valid,
        n_chunks=-(-l_valid // tk), idx_bits=max(1, (L - 1).bit_length()) + 1)
    return pl.pallas_call(
        kern,
        grid=(B, T // tq),
        in_specs=[blk_q(A_Q), blk_q(IDX_HEADS * LANES), blk_q(LANES),
                  blk_k(Lk, A_KV), blk_k(Lk, 2 * A_KV), blk_k(Lk, LANES)] + specs,
        out_specs=blk_q(A_Q),
        out_shape=jax.ShapeDtypeStruct((B, T, A_Q), BF16),
        scratch_shapes=[pltpu.VMEM((tq, L), I32), pltpu.VMEM((tq, L), F32),
                        pltpu.VMEM((A_KV_HEADS, A_GROUP * tq, LANES), BF16),
                        pltpu.VMEM((A_HEADS * tq, LANES), F32),
                        pltpu.VMEM((A_HEADS * tq, 2 * LANES), F32)],
        compiler_params=_cparams("parallel", "arbitrary"),
        name="dsa_causal" if causal else "dsa_cached",
    )(q, qi, wi, k, v, ki, *args)


def _pack_a_w_in(w):
    D = w.shape[0]
    o_ki = A_Q + 2 * A_KV + A_QI
    pad = lambda n: jnp.zeros((D, n), w.dtype)
    return jnp.concatenate(
        [w[:, :o_ki], w[:, o_ki:o_ki + IDX_DIM], pad(LANES - IDX_DIM),
         w[:, o_ki + IDX_DIM:], pad(LANES - IDX_HEADS)], axis=1).astype(BF16)


def _mixer_a(xp, xs, g, w_in, w_out, qg, kg, caches, li):
    B, S, D = xp.shape
    Bs, T, _ = xs.shape
    past = caches[0].shape[2]
    w_in = _pack_a_w_in(w_in)
    w_out = w_out.astype(BF16)
    g, qg, kg = g[None], qg[None], kg[None]

    def project(x, pos):
        b, t, _ = x.shape
        h = _rms_matmul(x.reshape(b * t, D), g, w_in, tn=A_IN_PAD // 5)
        tabs = _rope_tables(pos, HEAD_DIM) + _rope_tables(pos, IDX_DIM)
        outs = _a_post(h, tabs, qg, kg)
        v = h[:, A_OFF_V:A_OFF_V + A_KV]
        return [o.reshape(b, t, -1) for o in outs] + [v.reshape(b, t, A_KV_HEADS, HEAD_DIM)]

    q, kf, kb, vb, qi, kif, kib, wi, v_p = project(xp, jnp.arange(S))
    tk = _pick(S, 512, LANES)
    o = _dsa(q, qi, wi, kb, vb, kib, tq=min(128, S), tk=tk, ksel=min(TOPK_MAX, S // 4),
             causal=True, l_valid=S)
    yp = _matmul_res(o.reshape(B * S, A_Q), w_out, xp.reshape(B * S, D)).reshape(B, S, D)
    k_p = kf.reshape(B, S, A_KV_HEADS, HEAD_DIM)
    ki_p = kif[..., :IDX_DIM]

    pos_s = jnp.tile(past + jnp.arange(T), Bs)
    q, kf, kb, vb, qi, kif, kib, wi, v_s = project(xs, pos_s)
    L = past + T
    tk = _pick(past, 512, LANES)
    padk = lambda a: jnp.pad(a, ((0, 0), (0, tk - T), (0, 0)))
    o = _dsa(q, qi, wi, padk(kb), padk(vb), padk(kib), caches + (li,),
             tq=T, tk=tk, ksel=min(TOPK_MAX, L // 4), causal=False, l_valid=L)
    ys = _matmul_res(o.reshape(Bs * T, A_Q), w_out, xs.reshape(Bs * T, D)).reshape(Bs, T, D)
    k_s = kf.reshape(Bs, T, A_KV_HEADS, HEAD_DIM)
    return yp, ys, (k_p, v_p, ki_p, k_s, v_s, kif[..., :IDX_DIM])


def _c_post_kernel(h_ref, qg_ref, kg_ref, q_ref, kb_ref, kf_ref, vf_ref):
    W = C_HEADS * HEAD_DIM
    for h in range(C_HEADS):
        sl = slice(h * LANES, (h + 1) * LANES)
        q = _rms(h_ref[:, sl].astype(F32), qg_ref[...])
        q_ref[:, sl] = (q * (HEAD_DIM ** -0.5)).astype(BF16)
        k = _rms(h_ref[:, W + h * LANES:W + (h + 1) * LANES].astype(F32), kg_ref[...])
        kf_ref[:, sl] = k
        kb_ref[:, sl] = k.astype(BF16)
    vf_ref[...] = h_ref[:, 2 * W:3 * W].astype(F32)


def _c_post(h, qg, kg, *, t, keep, tm=256):
    M = h.shape[0]
    W = C_HEADS * HEAD_DIM
    tm = _pick(math.gcd(t, keep), tm, 16) if keep < t else _pick(M, tm, 16)
    row = lambda i: (i, 0)
    if keep < t:
        tps, kps = t // tm, keep // tm
        tail = lambda i: ((i // tps) * kps + jnp.maximum(i % tps - (tps - kps), 0), 0)
    else:
        tail = row
    n_tail = M // t * keep
    return pl.pallas_call(
        _c_post_kernel,
        grid=(M // tm,),
        in_specs=[pl.BlockSpec((tm, 3 * W), row),
                  pl.BlockSpec((1, LANES), lambda i: (0, 0)),
                  pl.BlockSpec((1, LANES), lambda i: (0, 0))],
        out_specs=[pl.BlockSpec((tm, W), row)] * 2 + [pl.BlockSpec((tm, W), tail)] * 2,
        out_shape=[jax.ShapeDtypeStruct((M, W), BF16)] * 2
        + [jax.ShapeDtypeStruct((n_tail, W), F32)] * 2,
        compiler_params=_cparams("arbitrary"),
        name="c_post",
    )(h, qg, kg)


def _band_kernel(q_ref, k_ref, v_ref, b_ref, o_ref, *, sub, nsub, win, first_key):
    t0 = pl.program_id(2) * (sub * nsub)
    starts = [pl.multiple_of(t0 + s * sub, sub) for s in range(nsub)]
    lgs = [_dot_nt(q_ref[0, s * sub:(s + 1) * sub, :], k_ref[0, pl.ds(starts[s], win), :])
           + b_ref[0] for s in range(nsub)]
    if first_key is not None:
        kk = lax.broadcasted_iota(I32, (sub, win), 1)
        lgs = [jnp.where(kk >= first_key - starts[s], lgs[s], NEG_INF) for s in range(nsub)]
    ms = [jnp.max(lg, axis=-1, keepdims=True) for lg in lgs]
    ps = [jnp.exp(lg - m) for lg, m in zip(lgs, ms)]
    ls = [jnp.sum(p, axis=-1, keepdims=True) for p in ps]
    os_ = [_dot(ps[s].astype(BF16), v_ref[0, pl.ds(starts[s], win), :]) for s in range(nsub)]
    for s in range(nsub):
        o_ref[0, s * sub:(s + 1) * sub, :] = (os_[s] / ls[s]).astype(BF16)


def _band(q, k, v, bias, *, sub, nsub, first_key):
    B, T, W = q.shape
    Lk = k.shape[1]
    win = bias.shape[2]
    tq = sub * nsub
    kern = functools.partial(_band_kernel, sub=sub, nsub=nsub, win=win, first_key=first_key)
    return pl.pallas_call(
        kern,
        grid=(B, C_HEADS, T // tq),
        in_specs=[pl.BlockSpec((1, tq, LANES), lambda b, h, i: (b, i, h)),
                  pl.BlockSpec((1, Lk, LANES), lambda b, h, i: (b, 0, h)),
                  pl.BlockSpec((1, Lk, LANES), lambda b, h, i: (b, 0, h)),
                  pl.BlockSpec((1, sub, win), lambda b, h, i: (h, 0, 0))],
        out_specs=pl.BlockSpec((1, tq, LANES), lambda b, h, i: (b, i, h)),
        out_shape=jax.ShapeDtypeStruct((B, T, W), BF16),
        compiler_params=_cparams("parallel", "parallel", "arbitrary"),
        name="band_attn",
    )(q, k, v, bias)


def _band_cached_kernel(q_ref, k_ref, v_ref, ck_ref, cv_ref, b_ref, o_ref):
    T = q_ref.shape[1]
    cp = ck_ref.shape[1]
    win = b_ref.shape[2]
    pad = jnp.zeros((win - cp - T, LANES), BF16)

    def window(c_ref, n_ref, h):
        new = n_ref[0, :, h * LANES:(h + 1) * LANES]
        return jnp.concatenate([c_ref[0, :, h, :].astype(BF16), new, pad], axis=0)

    heads = range(C_HEADS)
    lgs = [_dot_nt(q_ref[0, :, h * LANES:(h + 1) * LANES], window(ck_ref, k_ref, h)) + b_ref[h]
           for h in heads]
    ms = [jnp.max(lg, axis=-1, keepdims=True) for lg in lgs]
    ps = [jnp.exp(lg - m) for lg, m in zip(lgs, ms)]
    ls = [jnp.sum(p, axis=-1, keepdims=True) for p in ps]
    for h in heads:
        o = _dot(ps[h].astype(BF16), window(cv_ref, v_ref, h)) / ls[h]
        o_ref[0, :, h * LANES:(h + 1) * LANES] = o.astype(BF16)


def _band_cached(q, k, v, ck, cv, bias):
    B, T, W = q.shape
    cp = ck.shape[1]
    row = pl.BlockSpec((1, T, W), lambda b: (b, 0, 0))
    cache = pl.BlockSpec((1, cp, C_HEADS, HEAD_DIM), lambda b: (b, 0, 0, 0))
    return pl.pallas_call(
        _band_cached_kernel,
        grid=(B,),
        in_specs=[row, row, row, cache, cache,
                  pl.BlockSpec(bias.shape, lambda b: (0, 0, 0))],
        out_specs=row,
        out_shape=jax.ShapeDtypeStruct((B, T, W), BF16),
        compiler_params=_cparams("parallel"),
        name="band_cached",
    )(q, k, v, ck, cv, bias)


def _mixer_c(xp, xs, g, w_in, w_out, qg, kg, rel, ck, cv):
    B, S, D = xp.shape
    Bs, T, _ = xs.shape
    W = C_HEADS * HEAD_DIM
    cp = ck.shape[1]
    w_in = w_in.astype(BF16)
    w_out = w_out.astype(BF16)
    g, qg, kg = g[None], qg[None], kg[None]
    relf = rel.astype(F32)

    def project(x, keep):
        b, t, _ = x.shape
        h = _rms_matmul(x.reshape(b * t, D), g, w_in, out_dtype=BF16)
        q, kb, kf, vf = _c_post(h, qg, kg, t=t, keep=keep)
        vb = h.reshape(b, t, 3 * W)[:, :, 2 * W:]
        tail = lambda a: a.reshape(b, keep, C_HEADS, HEAD_DIM)
        return q.reshape(b, t, W), tail(kf), kb.reshape(b, t, W), vb, tail(vf)

    def rel_bias(rows, win, shift):
        n = rows + win - 1
        j = jnp.arange(n)
        vec = relf[:, jnp.clip(shift + rows - 1 - j, -REL_CLIP, REL_CLIP) + REL_CLIP]
        flat = jnp.tile(jnp.pad(vec, ((0, 0), (0, 1))), (1, rows))
        toep = flat[:, :rows * n].reshape(-1, rows, n)
        return toep[:, :, rows - 1:rows - 1 + win]

    sub = min(2 * CHUNK, S)
    win = C_REACH + sub
    r = jnp.arange(sub)[:, None]
    kk = jnp.arange(win)[None, :]
    lo = (r // CHUNK) * CHUNK
    bias = jnp.where((kk >= lo) & (kk < lo + C_REACH + CHUNK), rel_bias(sub, win, C_REACH),
                     NEG_INF)
    q, k_p, kb, vb, v_p = project(xp, min(C_REACH, S))
    padk = lambda a: jnp.pad(a, ((0, 0), (C_REACH, 0), (0, 0)))
    o = _band(q, padk(kb), padk(vb), bias, sub=sub, nsub=_pick(S // sub, 8, 1),
              first_key=C_REACH)
    yp = _matmul_res(o.reshape(B * S, W), w_out, xp.reshape(B * S, D)).reshape(B, S, D)

    L = cp + T
    win = -(-L // LANES) * LANES
    bias = jnp.where(jnp.arange(win)[None, :] < L, rel_bias(T, win, cp), NEG_INF)
    q, kf, kb, vb, vf = project(xs, T)
    o = _band_cached(q, kb, vb, ck, cv, bias)
    ys = _matmul_res(o.reshape(Bs * T, W), w_out, xs.reshape(Bs * T, D)).reshape(Bs, T, D)
    return yp, ys, (k_p, v_p, kf, vf)


def _s5_ops_kernel(ar_ref, ai_ref, ldt_ref, br_ref, bi_ref, cr_ref, ci_ref, cr4_ref, ci4_ref,
                   lag_ref, xr_ref, xi_ref, or_ref, oi_ref, lnr_ref, lni_ref, sr_ref, si_ref):
    n, P = xr_ref.shape[:2]
    dt = jnp.exp(ldt_ref[...])
    ar, ai = ar_ref[...], ai_ref[...]
    mag = jnp.exp(ar * dt)
    lr, li = mag * jnp.cos(ai * dt), mag * jnp.sin(ai * dt)
    den = ar * ar + ai * ai
    nr = lr - 1.0
    fr = (nr * ar + li * ai) / den
    fi = (li * ar - nr * ai) / den
    br, bi = br_ref[...], bi_ref[...]
    bbr = fr * br - fi * bi
    bbi = fr * bi + fi * br
    cr, ci = cr_ref[...], ci_ref[...]
    pr, pi = jnp.ones_like(lr), jnp.zeros_like(lr)
    for tau in range(n):
        sr_ref[...] = pr * bbr - pi * bbi
        si_ref[...] = pr * bbi + pi * bbr
        xr_ref[tau] = sr_ref[...].astype(BF16)
        xi_ref[tau] = si_ref[...].astype(BF16)

        def body(p, acc):
            return acc + cr4_ref[p] * sr_ref[p][None] - ci4_ref[p] * si_ref[p][None]

        lag_ref[tau] = lax.fori_loop(0, P, body, jnp.zeros(lag_ref.shape[1:], F32))
        pr, pi = pr * lr - pi * li, pr * li + pi * lr
        or_ref[tau] = (cr * pr - ci * pi).astype(BF16)
        oi_ref[tau] = (-(cr * pi + ci * pr)).astype(BF16)
    lnr_ref[...] = pr
    lni_ref[...] = pi


def _s5_operators(a_re, a_im, log_dt, b_re, b_im, c_re, c_im):
    G, P = a_re.shape
    C = S5_GROUP
    n = S5_CHUNK
    nlb = G // S5_LB_GROUPS
    f = lambda a: a.astype(F32)
    a_t = lambda a: f(a).T.reshape(P, 1, G)
    b_t = lambda a: jnp.transpose(f(a), (1, 2, 0))
    c_t = lambda a: jnp.transpose(f(a), (2, 1, 0))
    crt, cit = c_t(c_re), c_t(c_im)
    sds = lambda *shape: jax.ShapeDtypeStruct(shape, F32)
    lag, xr, xi, o_re, o_im, lnr, lni = pl.pallas_call(
        _s5_ops_kernel,
        out_shape=[sds(n, C, C, G)] + [jax.ShapeDtypeStruct((n, P, C, G), BF16)] * 4
        + [sds(P, 1, G)] * 2,
        scratch_shapes=[pltpu.VMEM((P, C, G), F32)] * 2,
        compiler_params=pltpu.CompilerParams(vmem_limit_bytes=VMEM_LIMIT),
        name="s5_ops",
    )(a_t(a_re), a_t(a_im), f(log_dt).reshape(1, 1, G), b_t(b_re), b_t(b_im), crt, cit,
      crt.reshape(P, C, 1, G), cit.reshape(P, C, 1, G))

    eye = jnp.eye(S5_LB_GROUPS, dtype=F32)
    split = lambda a: a.reshape(a.shape[:-1] + (nlb, S5_LB_GROUPS))
    taps = jnp.einsum('tdclg,gh->ltgchd', split(lag), eye).reshape(nlb, n, LANES, LANES)

    def inc(x):
        return jnp.einsum('tpclg,gh->ltgchp', split(x[::-1]), eye).reshape(
            nlb, n * LANES, S5_LB_GROUPS * P)

    def out(x):
        return jnp.einsum('tpclg,gh->lgpthc', split(x), eye).reshape(
            nlb, S5_LB_GROUPS * P, n * LANES)

    lam_n = tuple(a.reshape(P, G).T.reshape(1, 1, G * P) for a in (lnr, lni))
    bf = lambda a: a.astype(BF16)
    return bf(taps), bf(inc(xr)), bf(inc(xi)), bf(out(o_re)), bf(out(o_im)), lam_n


def _chunk_rows(u_ref, tc):
    return jnp.concatenate(
        [u_ref[pl.ds(t, tc, stride=S5_CHUNK), :] for t in range(S5_CHUNK)], axis=1)


def _s5_inc_kernel(u_ref, wr_ref, wi_ref, sr_ref, si_ref):
    u = _chunk_rows(u_ref, sr_ref.shape[0]).astype(BF16)
    sr_ref[...] = _dot(u, wr_ref[0])
    si_ref[...] = _dot(u, wi_ref[0])


def _s5_inc(u, w_re, w_im, *, tc):
    Nc = u.shape[0] // S5_CHUNK
    nlb, K, Wn = w_re.shape
    wspec = pl.BlockSpec((1, K, Wn), lambda lb, i: (lb, 0, 0))
    ospec = pl.BlockSpec((tc, Wn), lambda lb, i: (i, lb))
    return pl.pallas_call(
        _s5_inc_kernel,
        grid=(nlb, Nc // tc),
        in_specs=[pl.BlockSpec((tc * S5_CHUNK, LANES), lambda lb, i: (i, lb)), wspec, wspec],
        out_specs=[ospec, ospec],
        out_shape=[jax.ShapeDtypeStruct((Nc, nlb * Wn), F32)] * 2,
        compiler_params=_cparams("parallel", "arbitrary"),
        name="s5_inc",
    )(u, w_re, w_im)


def _s5_scan_kernel(sr_ref, si_ref, lr_ref, li_ref, h0r_ref, h0i_ref,
                    pr_ref, pi_ref, fr_ref, fi_ref):
    lr, li = lr_ref[...], li_ref[...]
    nck = sr_ref.shape[1]

    def body(k, carry):
        hr, hi = carry
        pr_ref[:, pl.ds(k, 1), :] = hr
        pi_ref[:, pl.ds(k, 1), :] = hi
        sr = sr_ref[:, pl.ds(k, 1), :]
        si = si_ref[:, pl.ds(k, 1), :]
        return lr * hr - li * hi + sr, lr * hi + li * hr + si

    hr, hi = lax.fori_loop(0, nck, body, (h0r_ref[...], h0i_ref[...]))
    fr_ref[...] = hr
    fi_ref[...] = hi


def _s5_scan(sr, si, lam, h0r, h0i, *, cb=1024):
    B, nck, GP = sr.shape
    cb = _pick(GP, cb, LANES)
    full = pl.BlockSpec((B, nck, cb), lambda j: (0, 0, j))
    vec = pl.BlockSpec((1, 1, cb), lambda j: (0, 0, j))
    st = pl.BlockSpec((B, 1, cb), lambda j: (0, 0, j))
    return pl.pallas_call(
        _s5_scan_kernel,
        grid=(GP // cb,),
        in_specs=[full, full, vec, vec, st, st],
        out_specs=[full, full, st, st],
        out_shape=[jax.ShapeDtypeStruct((B, nck, GP), F32)] * 2
        + [jax.ShapeDtypeStruct((B, 1, GP), F32)] * 2,
        compiler_params=_cparams("parallel"),
        name="s5_scan",
    )(sr, si, lam[0], lam[1], h0r, h0i)


def _s5_out_kernel(u_ref, taps_ref, hr_ref, hi_ref, wr_ref, wi_ref, d_ref, o_ref, toep_ref):
    n = S5_CHUNK
    tc = hr_ref.shape[0]

    @pl.when(pl.program_id(1) == 0)
    def _():
        zero = jnp.zeros((LANES, LANES), BF16)
        for t in range(n):
            for v in range(n):
                blk = taps_ref[0, v - t] if v >= t else zero
                toep_ref[t * LANES:(t + 1) * LANES, v * LANES:(v + 1) * LANES] = blk

    u = _chunk_rows(u_ref, tc)
    y = _dot(u.astype(BF16), toep_ref[...])
    y = y + _dot(hr_ref[...].astype(BF16), wr_ref[0]) + _dot(hi_ref[...].astype(BF16), wi_ref[0])
    y = jax.nn.gelu(y + jnp.tile(d_ref[...], (1, n)) * u)
    for t in range(n):
        o_ref[pl.ds(t, tc, stride=n), :] = y[:, t * LANES:(t + 1) * LANES]


def _s5_out(u, taps, hr, hi, w_re, w_im, d, *, tc):
    M, D = u.shape
    n = S5_CHUNK
    nlb = taps.shape[0]
    Wn, K = w_re.shape[1:]
    hspec = pl.BlockSpec((tc, Wn), lambda lb, i: (i, lb))
    uspec = pl.BlockSpec((tc * n, LANES), lambda lb, i: (i, lb))
    return pl.pallas_call(
        _s5_out_kernel,
        grid=(nlb, M // (tc * n)),
        in_specs=[uspec,
                  pl.BlockSpec((1, n, LANES, LANES), lambda lb, i: (lb, 0, 0, 0)), hspec, hspec,
                  pl.BlockSpec((1, Wn, K), lambda lb, i: (lb, 0, 0)),
                  pl.BlockSpec((1, Wn, K), lambda lb, i: (lb, 0, 0)),
                  pl.BlockSpec((1, LANES), lambda lb, i: (0, lb))],
        out_specs=uspec,
        out_shape=jax.ShapeDtypeStruct((M, D), F32),
        scratch_shapes=[pltpu.VMEM((K, K), BF16)],
        compiler_params=_cparams("parallel", "arbitrary"),
        name="s5_out",
    )(u, taps, hr, hi, w_re, w_im, d)


def _glu_res_kernel(x_ref, wa_ref, wb_ref, r_ref, o_ref, xc_ref):
    @pl.when(pl.program_id(1) == 0)
    def _():
        xc_ref[...] = x_ref[...].astype(BF16)

    x = xc_ref[...]
    a = _dot(x, wa_ref[...])
    b = _dot(x, wb_ref[...])
    o_ref[...] = r_ref[...] + a * jax.nn.sigmoid(b)


def _glu_res(x, w, res, *, tm=1024, tn=512):
    M, D = x.shape
    N = w.shape[1] // 2
    tm = _pick(M, tm, 16)
    tn = _pick(N, tn, LANES)
    nj = N // tn
    return pl.pallas_call(
        _glu_res_kernel,
        grid=(M // tm, nj),
        in_specs=[pl.BlockSpec((tm, D), lambda i, j: (i, 0)),
                  pl.BlockSpec((D, tn), lambda i, j: (0, j)),
                  pl.BlockSpec((D, tn), lambda i, j: (0, nj + j)),
                  pl.BlockSpec((tm, tn), lambda i, j: (i, j))],
        out_specs=pl.BlockSpec((tm, tn), lambda i, j: (i, j)),
        out_shape=jax.ShapeDtypeStruct((M, N), F32),
        scratch_shapes=[pltpu.VMEM((tm, D), BF16)],
        compiler_params=_cparams("parallel", "arbitrary"),
        name="glu_res",
    )(x, w, w, res)


def _mixer_b(xp, xs, g, ops, d, w_glu, h0r, h0i):
    taps, inc_re, inc_im, out_re, out_im, lam = ops
    n = S5_CHUNK
    GP = lam[0].shape[-1]
    g, d = g[None], d.astype(F32)[None]
    w_glu = w_glu.astype(BF16)

    def stream(x, hr0, hi0):
        B, T, D = x.shape
        nck = T // n
        Nc = B * nck
        x2 = x.reshape(B * T, D)
        u = _rmsnorm(x2, g)
        tc = _pick(Nc, 256, 8)
        sr, si = _s5_inc(u, inc_re, inc_im, tc=tc)
        pr, pi, fr, fi = _s5_scan(sr.reshape(B, nck, GP), si.reshape(B, nck, GP), lam, hr0, hi0)
        gy = _s5_out(u, taps, pr.reshape(Nc, GP), pi.reshape(Nc, GP), out_re, out_im, d, tc=tc)
        y = _glu_res(gy, w_glu, x2)
        shp = (B, GP // S5_STATE, S5_STATE)
        return y.reshape(B, T, D), fr.reshape(shp), fi.reshape(shp)

    B = xp.shape[0]
    zero = jnp.zeros((B, 1, GP), F32)
    yp, rp, ip = stream(xp, zero, zero)
    Bs = xs.shape[0]
    ys, rs, is_ = stream(xs, h0r.astype(F32).reshape(Bs, 1, GP), h0i.astype(F32).reshape(Bs, 1, GP))
    return yp, ys, (rp, ip, rs, is_)


FFN_HALO = 16
FFN_TAIL = 8


def _ffn_kernel(*refs, seq_tiles, seq_len, halo):
    if halo:
        (x_ref, xh_ref, g_ref, wa_ref, wv_ref, cwa_ref, cwv_ref, cba_ref, cbv_ref, wd_ref,
         o_ref, ta_ref, tv_ref, xn_ref, xhn_ref) = refs
    else:
        (x_ref, p1a_ref, p1v_ref, p2a_ref, p2v_ref, g_ref, wa_ref, wv_ref, cwa_ref, cwv_ref,
         cba_ref, cbv_ref, wd_ref, o_ref, ta_ref, tv_ref, xn_ref, h_ref) = refs
    i, f = pl.program_id(0), pl.program_id(1)
    tm = x_ref.shape[0]

    @pl.when(f == 0)
    def _():
        x = x_ref[...]
        xn_ref[...] = _rms(x, g_ref[...]).astype(BF16)
        o_ref[...] = x
        if halo:
            xhn_ref[...] = _rms(xh_ref[...], g_ref[...]).astype(BF16)

    xn = xn_ref[...]
    tf = wa_ref.shape[1]
    r = lax.broadcasted_iota(I32, (tm, tf), 0)

    def conv(w_ref, cw_ref, cb_ref, p1_ref, p2_ref, t_ref):
        h = _dot(xn, w_ref[...])
        h1 = pltpu.roll(h, 1, axis=0)
        h2 = pltpu.roll(h, 2, axis=0)
        if halo:
            t_ref[...] = h[tm - FFN_TAIL:, :]
            hh = _dot(xhn_ref[...], w_ref[...])
            hh = hh * jnp.where(i % seq_tiles == 0, 0.0, 1.0)
            last, prev = hh[-1:, :], hh[-2:-1, :]
            h1 = jnp.where(r == 0, last, h1)
            h2 = jnp.where(r == 0, prev, jnp.where(r == 1, last, h2))
        else:
            for s in range(tf // LANES):
                sl = slice(s * LANES, (s + 1) * LANES)
                h_ref[s] = h[:, sl]
                for k in range(CONV_W - 1):
                    rows = pl.ds(seq_len - (CONV_W - 1) + k, tm // seq_len, stride=seq_len)
                    t_ref[k, :, sl] = h_ref.at[s][rows, :]
            t = r % seq_len
            h1 = jnp.where(t >= 1, h1, p1_ref[...])
            h2 = jnp.where(t >= 2, h2, p2_ref[...])
        cw = cw_ref[...]
        return cb_ref[...] + cw[0:1] * h2 + cw[1:2] * h1 + cw[2:3] * h

    if halo:
        ca = conv(wa_ref, cwa_ref, cba_ref, None, None, ta_ref)
        cv = conv(wv_ref, cwv_ref, cbv_ref, None, None, tv_ref)
    else:
        ca = conv(wa_ref, cwa_ref, cba_ref, p1a_ref, p2a_ref, ta_ref)
        cv = conv(wv_ref, cwv_ref, cbv_ref, p1v_ref, p2v_ref, tv_ref)
    act = (jax.nn.silu(ca) * cv).astype(BF16)
    o_ref[...] += _dot(act, wd_ref[...])


def _ffn(x, g, w_up, conv_w, conv_b, w_down, *, layer, seq_len, hist=None, tm=512, tf=1024):
    M, D = x.shape
    F = w_down.shape[1]
    halo = hist is None
    n_seq = M // seq_len
    tm = _pick(seq_len, tm, FFN_HALO) if halo else M
    tf = _pick(F, tf, LANES)
    nf = F // tf
    xs = pl.BlockSpec((tm, D), lambda i, f: (i, 0))
    a_col = lambda i, f: (layer, 0, f)
    v_col = lambda i, f: (layer, 0, nf + f)
    wspecs = [pl.BlockSpec((1, D), lambda i, f: (0, 0)),
              pl.BlockSpec((None, D, tf), a_col), pl.BlockSpec((None, D, tf), v_col),
              pl.BlockSpec((None, CONV_W, tf), a_col), pl.BlockSpec((None, CONV_W, tf), v_col),
              pl.BlockSpec((None, 1, tf), a_col), pl.BlockSpec((None, 1, tf), v_col),
              pl.BlockSpec((None, tf, D), lambda i, f: (layer, f, 0))]
    wargs = [g, w_up, w_up, conv_w, conv_w, conv_b, conv_b, w_down]
    scratch = [pltpu.VMEM((tm, D), BF16)]
    seq_tiles = max(seq_len // tm, 1)
    if halo:
        per = tm // FFN_HALO
        extra = [pl.BlockSpec((FFN_HALO, D), lambda i, f: (jnp.maximum(i * per - 1, 0), 0))]
        eargs = [x]
        scratch.append(pltpu.VMEM((FFN_HALO, D), BF16))
        tspec = pl.BlockSpec((FFN_TAIL, tf), lambda i, f: (i // seq_tiles, f))
        tshape = jax.ShapeDtypeStruct((n_seq * FFN_TAIL, F), F32)
    else:
        ha = pl.BlockSpec((tm, tf), lambda i, f: (i, f))
        hv = pl.BlockSpec((tm, tf), lambda i, f: (i, nf + f))
        extra = [ha, hv, ha, hv]
        eargs = [hist[0], hist[0], hist[1], hist[1]]
        scratch.append(pltpu.VMEM((tf // LANES, tm, LANES), F32))
        tspec = pl.BlockSpec((CONV_W - 1, n_seq, tf), lambda i, f: (0, 0, f))
        tshape = jax.ShapeDtypeStruct((CONV_W - 1, n_seq, F), F32)
    kern = functools.partial(_ffn_kernel, seq_tiles=seq_tiles, seq_len=seq_len, halo=halo)
    y, ta, tv = pl.pallas_call(
        kern,
        grid=(M // tm, nf),
        in_specs=[xs] + extra + wspecs,
        out_specs=[pl.BlockSpec((tm, D), lambda i, f: (i, 0)), tspec, tspec],
        out_shape=[jax.ShapeDtypeStruct((M, D), F32), tshape, tshape],
        scratch_shapes=scratch,
        compiler_params=_cparams("arbitrary", "arbitrary"),
        name="conv_ffn" if halo else "conv_ffn_hist",
    )(x, *eargs, *wargs)
    if halo:
        tail = lambda t: t.reshape(n_seq, FFN_TAIL, F)[:, FFN_TAIL - (CONV_W - 1):]
    else:
        tail = lambda t: jnp.swapaxes(t, 0, 1)
    return y, jnp.concatenate([tail(ta), tail(tv)], axis=-1)


def _conv_ffn(xp, xs, g, w_up, conv_w, conv_b, w_down, hist_s, layer):
    B, S, D = xp.shape
    Bs, T, _ = xs.shape
    F2 = w_up.shape[2]
    g = g[None]
    yp, hp = _ffn(xp.reshape(B * S, D), g, w_up, conv_w, conv_b, w_down, layer=layer, seq_len=S)
    hs = hist_s.astype(F32)
    t = jnp.arange(T)[None, :, None]
    p1 = jnp.where(t == 0, hs[:, -1:], 0.0).reshape(Bs * T, F2)
    p2 = jnp.where(t == 0, hs[:, -2:-1], jnp.where(t == 1, hs[:, -1:], 0.0)).reshape(Bs * T, F2)
    ys, hs_new = _ffn(xs.reshape(Bs * T, D), g, w_up, conv_w, conv_b, w_down, layer=layer,
                      seq_len=T, hist=(p1, p2))
    return yp.reshape(B, S, D), ys.reshape(Bs, T, D), hp, hs_new


def kernel(x_prompt, x_sample, cache_a_k, cache_a_v, cache_a_kidx, cache_c_k, cache_c_v, state_b_re, state_b_im, state_ffn_conv, norm_mix, norm_ffn, a_w_in, a_w_out, a_q_norm, a_k_norm, b_a_re, b_a_im, b_log_dt, b_b_re, b_b_im, b_c_re, b_c_im, b_d, b_w_glu, c_w_in, c_w_out, c_q_norm, c_k_norm, c_rel_bias, ffn_w_up, ffn_conv_w, ffn_conv_b, ffn_w_down):
    xp, xs = x_prompt, x_sample
    depth = norm_mix.shape[0]
    outs_a, outs_b, outs_c, f_p, f_s = [], [], [], [], []
    ffn_w_up, ffn_w_down = ffn_w_up.astype(BF16), ffn_w_down.astype(BF16)
    ffn_conv_b = ffn_conv_b[:, None]
    for i in range(depth):
        li, kind = i // 3, i % 3
        if kind == 0:
            xp, xs, o = _mixer_a(xp, xs, norm_mix[i], a_w_in[li], a_w_out[li], a_q_norm[li],
                                 a_k_norm[li], (cache_a_k, cache_a_v, cache_a_kidx), li)
            outs_a.append(o)
        elif kind == 1:
            ops = _s5_operators(b_a_re[li], b_a_im[li], b_log_dt[li], b_b_re[li], b_b_im[li],
                                b_c_re[li], b_c_im[li])
            xp, xs, o = _mixer_b(xp, xs, norm_mix[i], ops, b_d[li], b_w_glu[li],
                                 state_b_re[li], state_b_im[li])
            outs_b.append(o)
        else:
            xp, xs, o = _mixer_c(xp, xs, norm_mix[i], c_w_in[li], c_w_out[li], c_q_norm[li],
                                 c_k_norm[li], c_rel_bias[li], cache_c_k[li], cache_c_v[li])
            outs_c.append(o)
        xp, xs, hp, hs = _conv_ffn(xp, xs, norm_ffn[i], ffn_w_up, ffn_conv_w, ffn_conv_b,
                                   ffn_w_down, state_ffn_conv[i], i)
        f_p.append(hp)
        f_s.append(hs)
    stack = lambda outs, j: jnp.stack([o[j] for o in outs])
    return (xp, xs,
            *[stack(outs_a, j) for j in range(6)],
            *[stack(outs_c, j) for j in range(4)],
            *[stack(outs_b, j) for j in range(4)],
            jnp.stack(f_p), jnp.stack(f_s))
```

```python
import functools
import math

import jax
import jax.numpy as jnp
from jax import lax
from jax.experimental import pallas as pl
from jax.experimental.pallas import tpu as pltpu

F32 = jnp.float32
BF16 = jnp.bfloat16
I32 = jnp.int32

EPS = 1e-6
NEG_INF = -1e30
ROW_MAX_INIT = -1e29
ROPE_THETA = 10000.0
CHUNK = 64
HEAD_DIM = 128
LANES = 128

A_HEADS = 16
A_KV_HEADS = 4
A_GROUP = A_HEADS // A_KV_HEADS
IDX_HEADS = 8
IDX_DIM = 64
TOPK_MAX = 256
A_Q = A_HEADS * HEAD_DIM
A_KV = A_KV_HEADS * HEAD_DIM
A_QI = IDX_HEADS * IDX_DIM
IDX_W_SCALE = (IDX_HEADS * IDX_DIM) ** -0.5
A_OFF_K = A_Q
A_OFF_V = A_OFF_K + A_KV
A_OFF_QI = A_OFF_V + A_KV
A_OFF_KI = A_OFF_QI + A_QI
A_OFF_WI = A_OFF_KI + LANES
A_IN_PAD = A_OFF_WI + LANES

S5_GROUP = 16
S5_STATE = 64
S5_CHUNK = 16
S5_LB_GROUPS = LANES // S5_GROUP

C_HEADS = 16
C_BAND_CHUNKS = 8
C_REACH = C_BAND_CHUNKS * CHUNK
REL_CLIP = 128
CONV_W = 3

VMEM_LIMIT = 56 * 1024 * 1024


def _cparams(*sem):
    return pltpu.CompilerParams(dimension_semantics=sem, vmem_limit_bytes=VMEM_LIMIT)


def _pick(n, target, mult):
    best = None
    for t in range(mult, min(n, target) + 1, mult):
        if n % t == 0:
            best = t
    return best if best is not None else n


def _dot(a, b):
    return jnp.dot(a, b, preferred_element_type=F32)


def _dot_nt(a, b):
    return lax.dot_general(a, b, (((1,), (1,)), ((), ())), preferred_element_type=F32)


def _rms(x, g):
    return x * lax.rsqrt(jnp.mean(x * x, axis=-1, keepdims=True) + EPS) * g


def _rms_matmul_kernel(x_ref, g_ref, w_ref, o_ref, xn_ref):
    @pl.when(pl.program_id(1) == 0)
    def _():
        xn_ref[...] = _rms(x_ref[...], g_ref[...]).astype(BF16)

    o_ref[...] = _dot(xn_ref[...], w_ref[...]).astype(o_ref.dtype)


def _rms_matmul(x, g, w, *, tm=1024, tn=1024, out_dtype=F32):
    M, D = x.shape
    N = w.shape[1]
    tm = _pick(M, tm, 16)
    tn = _pick(N, tn, LANES)
    return pl.pallas_call(
        _rms_matmul_kernel,
        grid=(M // tm, N // tn),
        in_specs=[pl.BlockSpec((tm, D), lambda i, j: (i, 0)),
                  pl.BlockSpec((1, D), lambda i, j: (0, 0)),
                  pl.BlockSpec((D, tn), lambda i, j: (0, j))],
        out_specs=pl.BlockSpec((tm, tn), lambda i, j: (i, j)),
        out_shape=jax.ShapeDtypeStruct((M, N), out_dtype),
        scratch_shapes=[pltpu.VMEM((tm, D), BF16)],
        compiler_params=_cparams("parallel", "arbitrary"),
        name="rms_matmul",
    )(x, g, w)


def _rmsnorm_kernel(x_ref, g_ref, o_ref):
    o_ref[...] = _rms(x_ref[...], g_ref[...])


def _rmsnorm(x, g, *, tm=512):
    M, D = x.shape
    tm = _pick(M, tm, 8)
    return pl.pallas_call(
        _rmsnorm_kernel,
        grid=(M // tm,),
        in_specs=[pl.BlockSpec((tm, D), lambda i: (i, 0)),
                  pl.BlockSpec((1, D), lambda i: (0, 0))],
        out_specs=pl.BlockSpec((tm, D), lambda i: (i, 0)),
        out_shape=jax.ShapeDtypeStruct((M, D), F32),
        compiler_params=_cparams("parallel"),
        name="rmsnorm",
    )(x, g)


def _matmul_res_kernel(x_ref, w_ref, r_ref, o_ref):
    o_ref[...] = r_ref[...] + _dot(x_ref[...], w_ref[...])


def _matmul_res(x, w, res, *, tm=512, tn=2048):
    M, K = x.shape
    N = w.shape[1]
    tm = _pick(M, tm, 16)
    tn = _pick(N, tn, LANES)
    return pl.pallas_call(
        _matmul_res_kernel,
        grid=(M // tm, N // tn),
        in_specs=[pl.BlockSpec((tm, K), lambda i, j: (i, 0)),
                  pl.BlockSpec((K, tn), lambda i, j: (0, j)),
                  pl.BlockSpec((tm, tn), lambda i, j: (i, j))],
        out_specs=pl.BlockSpec((tm, tn), lambda i, j: (i, j)),
        out_shape=jax.ShapeDtypeStruct((M, N), F32),
        compiler_params=_cparams("parallel", "arbitrary"),
        name="matmul_res",
    )(x, w, res)


def _rope_tables(pos, dim):
    half = dim // 2
    inv = jnp.power(ROPE_THETA, -jnp.arange(half, dtype=F32) / half)
    ang = pos.astype(F32)[:, None] * inv[None, :]
    cos, sin = jnp.cos(ang), jnp.sin(ang)
    reps = LANES // dim
    return (jnp.tile(jnp.concatenate([cos, cos], axis=-1), (1, reps)),
            jnp.tile(jnp.concatenate([-sin, sin], axis=-1), (1, reps)))


def _a_post_kernel(h_ref, c128_ref, s128_ref, c64_ref, s64_ref, qg_ref, kg_ref,
                   q_ref, kf_ref, kb_ref, vb_ref, qi_ref, kif_ref, kib_ref, wi_ref):
    c128, s128 = c128_ref[...], s128_ref[...]
    c64, s64 = c64_ref[...], s64_ref[...]
    tm = h_ref.shape[0]
    lane = lax.broadcasted_iota(I32, (tm, LANES), 1)

    def rope128(x):
        return x * c128 + pltpu.roll(x, HEAD_DIM // 2, axis=1) * s128

    def rope64(x):
        lo = (lane % IDX_DIM) < (IDX_DIM // 2)
        partner = jnp.where(lo, pltpu.roll(x, LANES - IDX_DIM // 2, axis=1),
                            pltpu.roll(x, IDX_DIM // 2, axis=1))
        return x * c64 + partner * s64

    for h in range(A_HEADS):
        x = h_ref[:, h * LANES:(h + 1) * LANES]
        q_ref[:, h * LANES:(h + 1) * LANES] = (
            rope128(_rms(x, qg_ref[...])) * (HEAD_DIM ** -0.5)).astype(BF16)
    for h in range(A_KV_HEADS):
        x = h_ref[:, A_OFF_K + h * LANES:A_OFF_K + (h + 1) * LANES]
        k = rope128(_rms(x, kg_ref[...]))
        kf_ref[:, h * LANES:(h + 1) * LANES] = k
        kb_ref[:, h * LANES:(h + 1) * LANES] = k.astype(BF16)
    ones = jnp.ones((tm, LANES), BF16)
    for h in range(A_KV_HEADS):
        v = h_ref[:, A_OFF_V + h * LANES:A_OFF_V + (h + 1) * LANES]
        vb_ref[:, 2 * h * LANES:(2 * h + 1) * LANES] = v.astype(BF16)
        vb_ref[:, (2 * h + 1) * LANES:(2 * h + 2) * LANES] = ones
    for p in range(A_QI // LANES):
        x = rope64(h_ref[:, A_OFF_QI + p * LANES:A_OFF_QI + (p + 1) * LANES])
        qi_ref[:, (2 * p) * LANES:(2 * p + 1) * LANES] = (
            jnp.where(lane < IDX_DIM, x, 0.0).astype(BF16))
        qi_ref[:, (2 * p + 1) * LANES:(2 * p + 2) * LANES] = (
            jnp.where(lane >= IDX_DIM, x, 0.0).astype(BF16))
    ki = rope64(h_ref[:, A_OFF_KI:A_OFF_KI + LANES])
    kif_ref[...] = ki
    kib_ref[...] = (ki + pltpu.roll(ki, IDX_DIM, axis=1)).astype(BF16)
    wi_ref[...] = h_ref[:, A_OFF_WI:A_OFF_WI + LANES] * IDX_W_SCALE


def _a_post(h, tabs, qg, kg, *, tm=256):
    M = h.shape[0]
    P = tabs[0].shape[0]
    tm = _pick(math.gcd(M, P), tm, 16)
    nt = P // tm
    row = lambda i: (i, 0)
    tab = lambda i: (i % nt, 0)
    one = lambda i: (0, 0)
    widths = [(A_Q, BF16), (A_KV, F32), (A_KV, BF16), (2 * A_KV, BF16),
              (IDX_HEADS * LANES, BF16), (LANES, F32), (LANES, BF16), (LANES, F32)]
    return pl.pallas_call(
        _a_post_kernel,
        grid=(M // tm,),
        in_specs=[pl.BlockSpec((tm, A_IN_PAD), row)]
        + [pl.BlockSpec((tm, LANES), tab)] * 4
        + [pl.BlockSpec((1, LANES), one)] * 2,
        out_specs=[pl.BlockSpec((tm, w), row) for w, _ in widths],
        out_shape=[jax.ShapeDtypeStruct((M, w), dt) for w, dt in widths],
        compiler_params=_cparams("parallel"),
        name="a_post",
    )(h, *tabs, qg, kg)


def _dsa_kernel(*refs, tq, tk, ksel, causal, l_valid, n_chunks, idx_bits):
    if causal:
        q_ref, qi_ref, wi_ref, k_ref, v_ref, ki_ref, o_ref = refs[:7]
    else:
        q_ref, qi_ref, wi_ref, k_ref, v_ref, ki_ref, ck_ref, cv_ref, cki_ref, o_ref = refs[:10]
    key_scr, bias_scr, qst_scr, m_scr, acc_scr = refs[-5:]
    t0 = pl.program_id(1) * tq
    nch = (t0 + tq + tk - 1) // tk if causal else n_chunks
    nslab = tk // LANES
    row = lax.broadcasted_iota(I32, (tq, tk), 0)
    col = lax.broadcasted_iota(I32, (tq, tk), 1)
    lane = lax.broadcasted_iota(I32, (tq, LANES), 1)
    limit = ((t0 + row) // CHUNK + 1) * CHUNK if causal else l_valid

    def chunk_off(c):
        return c * tk if isinstance(c, int) else pl.multiple_of(c * tk, tk)

    def cached(c):
        return not causal and c < n_chunks - 1

    def new_off(c):
        return chunk_off(c) if causal else 0

    def load_ki(c):
        if cached(c):
            x = cki_ref[0, c * tk:(c + 1) * tk, :]
            return jnp.concatenate([x, x], axis=1).astype(BF16)
        return ki_ref[0, pl.ds(new_off(c), tk), :]

    def load_k(c, j):
        if cached(c):
            return ck_ref[0, c * tk:(c + 1) * tk, j, :].astype(BF16)
        return k_ref[0, pl.ds(new_off(c), tk), j * LANES:(j + 1) * LANES]

    def load_v(c, j):
        if cached(c):
            v = cv_ref[0, c * tk:(c + 1) * tk, j, :].astype(BF16)
            return jnp.concatenate([v, jnp.ones((tk, LANES), BF16)], axis=1)
        return v_ref[0, pl.ds(new_off(c), tk), 2 * j * LANES:2 * (j + 1) * LANES]

    def chunk_loop(body):
        if causal:
            lax.fori_loop(0, nch, lambda c, carry: (body(c), carry)[1], 0)
        else:
            for c in range(n_chunks):
                body(c)

    wi = wi_ref[0]

    def score_body(c):
        off = chunk_off(c)
        kic = load_ki(c)
        shs = [_dot_nt(qi_ref[0, :, h * LANES:(h + 1) * LANES], kic) for h in range(IDX_HEADS)]
        s = jnp.zeros((tq, tk), F32)
        for h in range(IDX_HEADS):
            s = s + wi[:, h:h + 1] * jnp.maximum(shs[h], 0.0)
        s = jnp.where(s == 0.0, 0.0, s)
        s = jnp.where(off + col < limit, s, NEG_INF)
        bits = pltpu.bitcast(s, I32)
        key_scr[:, pl.ds(off, tk)] = bits ^ ((bits >> 31) & 0x7FFFFFFF)

    chunk_loop(score_body)

    def count(pred):
        def body(c, acc):
            off = chunk_off(c)
            blk = key_scr[:, pl.ds(off, tk)]
            for s in range(nslab):
                sl = slice(s * LANES, (s + 1) * LANES)
                acc = acc + pred(blk[:, sl], lane + (off + s * LANES))
            return acc

        acc = lax.fori_loop(0, nch, body, jnp.zeros((tq, LANES), F32))
        return jnp.broadcast_to(jnp.sum(acc, axis=-1, keepdims=True), (tq, LANES))

    def thr_cond(st):
        i, _, n_ge = st
        return jnp.logical_and(i < 32, jnp.max(n_ge) > ksel)

    def thr_body(st):
        i, thr, n_ge = st
        cand = thr + lax.shift_left(jnp.int32(1), 31 - i)
        cnt = count(lambda key, idx: jnp.where(key >= cand, 1.0, 0.0))
        take = cnt >= ksel
        return i + 1, jnp.where(take, cand, thr), jnp.where(take, cnt, n_ge)

    n_keys = (nch * tk).astype(F32) if causal else float(nch * tk)
    _, thr, _ = lax.while_loop(
        thr_cond, thr_body,
        (jnp.int32(0), jnp.full((tq, LANES), -2 ** 31, I32), jnp.full((tq, LANES), n_keys, F32)))

    need = ksel - count(lambda key, idx: jnp.where(key > thr, 1.0, 0.0))
    n_eq = count(lambda key, idx: jnp.where(key == thr, 1.0, 0.0))

    def cut_search():
        def body(i, cut):
            cand = cut + lax.shift_left(jnp.int32(1), idx_bits - 1 - i)
            below = count(lambda key, idx: jnp.where(key == thr, jnp.where(idx < cand, 1.0, 0.0), 0.0))
            return jnp.where(below <= need, cand, cut)

        return lax.fori_loop(0, idx_bits, body, jnp.zeros((tq, LANES), I32))

    cut = lax.cond(jnp.max(n_eq - need) > 0.0, cut_search,
                   lambda: jnp.full((tq, LANES), 2 ** idx_bits, I32))

    thr_t = jnp.tile(thr, (1, nslab))
    cut_t = jnp.tile(cut, (1, nslab))

    def bias_body(c, carry):
        off = chunk_off(c)
        key = key_scr[:, pl.ds(off, tk)]
        idx = off + col
        tie = jnp.where(key == thr_t, jnp.where(idx < cut_t, 0.0, NEG_INF), NEG_INF)
        bias = jnp.where(key > thr_t, 0.0, tie)
        bias_scr[:, pl.ds(off, tk)] = jnp.where(idx < limit, bias, NEG_INF)
        return carry

    lax.fori_loop(0, nch, bias_body, 0)

    for j in range(A_KV_HEADS):
        for g in range(A_GROUP):
            h = j * A_GROUP + g
            qst_scr[j, g * tq:(g + 1) * tq, :] = q_ref[0, :, h * LANES:(h + 1) * LANES]
    m_scr[...] = jnp.full(m_scr.shape, ROW_MAX_INIT, F32)
    acc_scr[...] = jnp.zeros(acc_scr.shape, F32)
    gr = A_GROUP * tq

    def attn_body(c):
        off = chunk_off(c)
        bias = bias_scr[:, pl.ds(off, tk)]
        logits = [_dot_nt(qst_scr[j], load_k(c, j)) for j in range(A_KV_HEADS)]
        for j in range(A_KV_HEADS):
            rows = slice(j * gr, (j + 1) * gr)
            m_old = m_scr[rows, :]
            alphas, ps = [], []
            for g in range(A_GROUP):
                sl = slice(g * tq, (g + 1) * tq)
                sg = logits[j][sl, :] + bias
                m_new = jnp.maximum(m_old[sl, :], jnp.max(sg, axis=-1, keepdims=True))
                alphas.append(jnp.exp(m_old[sl, :] - m_new))
                ps.append(jnp.exp((sg - jnp.tile(m_new, (1, nslab))).astype(BF16)))
                m_scr[j * gr + g * tq:j * gr + (g + 1) * tq, :] = m_new
            alpha = jnp.tile(jnp.concatenate(alphas, axis=0), (1, 2))
            pv = _dot(jnp.concatenate(ps, axis=0), load_v(c, j))
            acc_scr[rows, :] = alpha * acc_scr[rows, :] + pv

    chunk_loop(attn_body)
    for h in range(A_HEADS):
        rows = slice(h * tq, (h + 1) * tq)
        o_ref[0, :, h * LANES:(h + 1) * LANES] = (
            acc_scr[rows, :LANES] / acc_scr[rows, LANES:]).astype(BF16)


def _dsa(q, qi, wi, k, v, ki, cache=None, *, tq, tk, ksel, causal, l_valid):
    B, T, _ = q.shape
    Lk = k.shape[1]
    blk_q = lambda w: pl.BlockSpec((1, tq, w), lambda b, i: (b, i, 0))
    blk_k = lambda n, w: pl.BlockSpec((1, n, w), lambda b, i: (b, 0, 0))
    specs, args = [], []
    L = Lk
    if cache is not None:
        ck, cv, cki, li = cache
        past = ck.shape[2]
        L = past + Lk
        blk_c = pl.BlockSpec((None, 1, past, A_KV_HEADS, HEAD_DIM),
                             lambda b, i: (li, b, 0, 0, 0))
        specs = [blk_c, blk_c,
                 pl.BlockSpec((None, 1, past, IDX_DIM), lambda b, i: (li, b, 0, 0))]
        args = [ck, cv, cki]
    kern = functools.partial(
        _dsa_kernel, tq=tq, tk=tk, ksel=ksel, causal=causal, l_valid=l_valid,
        n_chunks=-(-l_valid // tk), idx_bits=max(1, (L - 1).bit_length()) + 1)
    return pl.pallas_call(
        kern,
        grid=(B, T // tq),
        in_specs=[blk_q(A_Q), blk_q(IDX_HEADS * LANES), blk_q(LANES),
                  blk_k(Lk, A_KV), blk_k(Lk, 2 * A_KV), blk_k(Lk, LANES)] + specs,
        out_specs=blk_q(A_Q),
        out_shape=jax.ShapeDtypeStruct((B, T, A_Q), BF16),
        scratch_shapes=[pltpu.VMEM((tq, L), I32), pltpu.VMEM((tq, L), F32),
                        pltpu.VMEM((A_KV_HEADS, A_GROUP * tq, LANES), BF16),
                        pltpu.VMEM((A_HEADS * tq, LANES), F32),
                        pltpu.VMEM((A_HEADS * tq, 2 * LANES), F32)],
        compiler_params=_cparams("parallel", "arbitrary"),
        name="dsa_causal" if causal else "dsa_cached",
    )(q, qi, wi, k, v, ki, *args)


def _pack_a_w_in(w):
    D = w.shape[0]
    o_ki = A_Q + 2 * A_KV + A_QI
    pad = lambda n: jnp.zeros((D, n), w.dtype)
    return jnp.concatenate(
        [w[:, :o_ki], w[:, o_ki:o_ki + IDX_DIM], pad(LANES - IDX_DIM),
         w[:, o_ki + IDX_DIM:], pad(LANES - IDX_HEADS)], axis=1).astype(BF16)


def _mixer_a(xp, xs, g, w_in, w_out, qg, kg, caches, li):
    B, S, D = xp.shape
    Bs, T, _ = xs.shape
    past = caches[0].shape[2]
    w_in = _pack_a_w_in(w_in)
    w_out = w_out.astype(BF16)
    g, qg, kg = g[None], qg[None], kg[None]

    def project(x, pos):
        b, t, _ = x.shape
        h = _rms_matmul(x.reshape(b * t, D), g, w_in, tn=A_IN_PAD // 5)
        tabs = _rope_tables(pos, HEAD_DIM) + _rope_tables(pos, IDX_DIM)
        outs = _a_post(h, tabs, qg, kg)
        v = h[:, A_OFF_V:A_OFF_V + A_KV]
        return [o.reshape(b, t, -1) for o in outs] + [v.reshape(b, t, A_KV_HEADS, HEAD_DIM)]

    q, kf, kb, vb, qi, kif, kib, wi, v_p = project(xp, jnp.arange(S))
    tk = _pick(S, 512, LANES)
    o = _dsa(q, qi, wi, kb, vb, kib, tq=min(128, S), tk=tk, ksel=min(TOPK_MAX, S // 4),
             causal=True, l_valid=S)
    yp = _matmul_res(o.reshape(B * S, A_Q), w_out, xp.reshape(B * S, D)).reshape(B, S, D)
    k_p = kf.reshape(B, S, A_KV_HEADS, HEAD_DIM)
    ki_p = kif[..., :IDX_DIM]

    pos_s = jnp.tile(past + jnp.arange(T), Bs)
    q, kf, kb, vb, qi, kif, kib, wi, v_s = project(xs, pos_s)
    L = past + T
    tk = _pick(past, 512, LANES)
    padk = lambda a: jnp.pad(a, ((0, 0), (0, tk - T), (0, 0)))
    o = _dsa(q, qi, wi, padk(kb), padk(vb), padk(kib), caches + (li,),
             tq=T, tk=tk, ksel=min(TOPK_MAX, L // 4), causal=False, l_valid=L)
    ys = _matmul_res(o.reshape(Bs * T, A_Q), w_out, xs.reshape(Bs * T, D)).reshape(Bs, T, D)
    k_s = kf.reshape(Bs, T, A_KV_HEADS, HEAD_DIM)
    return yp, ys, (k_p, v_p, ki_p, k_s, v_s, kif[..., :IDX_DIM])


def _c_post_kernel(h_ref, qg_ref, kg_ref, q_ref, kb_ref, kf_ref, vf_ref):
    W = C_HEADS * HEAD_DIM
    for h in range(C_HEADS):
        sl = slice(h * LANES, (h + 1) * LANES)
        q = _rms(h_ref[:, sl].astype(F32), qg_ref[...])
        q_ref[:, sl] = (q * (HEAD_DIM ** -0.5)).astype(BF16)
        k = _rms(h_ref[:, W + h * LANES:W + (h + 1) * LANES].astype(F32), kg_ref[...])
        kf_ref[:, sl] = k
        kb_ref[:, sl] = k.astype(BF16)
    vf_ref[...] = h_ref[:, 2 * W:3 * W].astype(F32)


def _c_post(h, qg, kg, *, t, keep, tm=256):
    M = h.shape[0]
    W = C_HEADS * HEAD_DIM
    tm = _pick(math.gcd(t, keep), tm, 16) if keep < t else _pick(M, tm, 16)
    row = lambda i: (i, 0)
    if keep < t:
        tps, kps = t // tm, keep // tm
        tail = lambda i: ((i // tps) * kps + jnp.maximum(i % tps - (tps - kps), 0), 0)
    else:
        tail = row
    n_tail = M // t * keep
    return pl.pallas_call(
        _c_post_kernel,
        grid=(M // tm,),
        in_specs=[pl.BlockSpec((tm, 3 * W), row),
                  pl.BlockSpec((1, LANES), lambda i: (0, 0)),
                  pl.BlockSpec((1, LANES), lambda i: (0, 0))],
        out_specs=[pl.BlockSpec((tm, W), row)] * 2 + [pl.BlockSpec((tm, W), tail)] * 2,
        out_shape=[jax.ShapeDtypeStruct((M, W), BF16)] * 2
        + [jax.ShapeDtypeStruct((n_tail, W), F32)] * 2,
        compiler_params=_cparams("arbitrary"),
        name="c_post",
    )(h, qg, kg)


def _band_kernel(q_ref, k_ref, v_ref, b_ref, o_ref, *, sub, nsub, win, first_key):
    t0 = pl.program_id(2) * (sub * nsub)
    starts = [pl.multiple_of(t0 + s * sub, sub) for s in range(nsub)]
    lgs = [_dot_nt(q_ref[0, s * sub:(s + 1) * sub, :], k_ref[0, pl.ds(starts[s], win), :])
           + b_ref[0] for s in range(nsub)]
    if first_key is not None:
        kk = lax.broadcasted_iota(I32, (sub, win), 1)
        lgs = [jnp.where(kk >= first_key - starts[s], lgs[s], NEG_INF) for s in range(nsub)]
    ms = [jnp.max(lg, axis=-1, keepdims=True) for lg in lgs]
    ps = [jnp.exp(lg - m) for lg, m in zip(lgs, ms)]
    ls = [jnp.sum(p, axis=-1, keepdims=True) for p in ps]
    os_ = [_dot(ps[s].astype(BF16), v_ref[0, pl.ds(starts[s], win), :]) for s in range(nsub)]
    for s in range(nsub):
        o_ref[0, s * sub:(s + 1) * sub, :] = (os_[s] / ls[s]).astype(BF16)


def _band(q, k, v, bias, *, sub, nsub, first_key):
    B, T, W = q.shape
    Lk = k.shape[1]
    win = bias.shape[2]
    tq = sub * nsub
    kern = functools.partial(_band_kernel, sub=sub, nsub=nsub, win=win, first_key=first_key)
    return pl.pallas_call(
        kern,
        grid=(B, C_HEADS, T // tq),
        in_specs=[pl.BlockSpec((1, tq, LANES), lambda b, h, i: (b, i, h)),
                  pl.BlockSpec((1, Lk, LANES), lambda b, h, i: (b, 0, h)),
                  pl.BlockSpec((1, Lk, LANES), lambda b, h, i: (b, 0, h)),
                  pl.BlockSpec((1, sub, win), lambda b, h, i: (h, 0, 0))],
        out_specs=pl.BlockSpec((1, tq, LANES), lambda b, h, i: (b, i, h)),
        out_shape=jax.ShapeDtypeStruct((B, T, W), BF16),
        compiler_params=_cparams("parallel", "parallel", "arbitrary"),
        name="band_attn",
    )(q, k, v, bias)


def _band_cached_kernel(q_ref, k_ref, v_ref, ck_ref, cv_ref, b_ref, o_ref):
    T = q_ref.shape[1]
    cp = ck_ref.shape[1]
    win = b_ref.shape[2]
    pad = jnp.zeros((win - cp - T, LANES), BF16)

    def window(c_ref, n_ref, h):
        new = n_ref[0, :, h * LANES:(h + 1) * LANES]
        return jnp.concatenate([c_ref[0, :, h, :].astype(BF16), new, pad], axis=0)

    heads = range(C_HEADS)
    lgs = [_dot_nt(q_ref[0, :, h * LANES:(h + 1) * LANES], window(ck_ref, k_ref, h)) + b_ref[h]
           for h in heads]
    ms = [jnp.max(lg, axis=-1, keepdims=True) for lg in lgs]
    ps = [jnp.exp(lg - m) for lg, m in zip(lgs, ms)]
    ls = [jnp.sum(p, axis=-1, keepdims=True) for p in ps]
    for h in heads:
        o = _dot(ps[h].astype(BF16), window(cv_ref, v_ref, h)) / ls[h]
        o_ref[0, :, h * LANES:(h + 1) * LANES] = o.astype(BF16)


def _band_cached(q, k, v, ck, cv, bias):
    B, T, W = q.shape
    cp = ck.shape[1]
    row = pl.BlockSpec((1, T, W), lambda b: (b, 0, 0))
    cache = pl.BlockSpec((1, cp, C_HEADS, HEAD_DIM), lambda b: (b, 0, 0, 0))
    return pl.pallas_call(
        _band_cached_kernel,
        grid=(B,),
        in_specs=[row, row, row, cache, cache,
                  pl.BlockSpec(bias.shape, lambda b: (0, 0, 0))],
        out_specs=row,
        out_shape=jax.ShapeDtypeStruct((B, T, W), BF16),
        compiler_params=_cparams("parallel"),
        name="band_cached",
    )(q, k, v, ck, cv, bias)


def _mixer_c(xp, xs, g, w_in, w_out, qg, kg, rel, ck, cv):
    B, S, D = xp.shape
    Bs, T, _ = xs.shape
    W = C_HEADS * HEAD_DIM
    cp = ck.shape[1]
    w_in = w_in.astype(BF16)
    w_out = w_out.astype(BF16)
    g, qg, kg = g[None], qg[None], kg[None]
    relf = rel.astype(F32)

    def project(x, keep):
        b, t, _ = x.shape
        h = _rms_matmul(x.reshape(b * t, D), g, w_in, out_dtype=BF16)
        q, kb, kf, vf = _c_post(h, qg, kg, t=t, keep=keep)
        vb = h.reshape(b, t, 3 * W)[:, :, 2 * W:]
        tail = lambda a: a.reshape(b, keep, C_HEADS, HEAD_DIM)
        return q.reshape(b, t, W), tail(kf), kb.reshape(b, t, W), vb, tail(vf)

    def rel_bias(rows, win, shift):
        n = rows + win - 1
        j = jnp.arange(n)
        vec = relf[:, jnp.clip(shift + rows - 1 - j, -REL_CLIP, REL_CLIP) + REL_CLIP]
        flat = jnp.tile(jnp.pad(vec, ((0, 0), (0, 1))), (1, rows))
        toep = flat[:, :rows * n].reshape(-1, rows, n)
        return toep[:, :, rows - 1:rows - 1 + win]

    sub = min(2 * CHUNK, S)
    win = C_REACH + sub
    r = jnp.arange(sub)[:, None]
    kk = jnp.arange(win)[None, :]
    lo = (r // CHUNK) * CHUNK
    bias = jnp.where((kk >= lo) & (kk < lo + C_REACH + CHUNK), rel_bias(sub, win, C_REACH),
                     NEG_INF)
    q, k_p, kb, vb, v_p = project(xp, min(C_REACH, S))
    padk = lambda a: jnp.pad(a, ((0, 0), (C_REACH, 0), (0, 0)))
    o = _band(q, padk(kb), padk(vb), bias, sub=sub, nsub=_pick(S // sub, 8, 1),
              first_key=C_REACH)
    yp = _matmul_res(o.reshape(B * S, W), w_out, xp.reshape(B * S, D)).reshape(B, S, D)

    L = cp + T
    win = -(-L // LANES) * LANES
    bias = jnp.where(jnp.arange(win)[None, :] < L, rel_bias(T, win, cp), NEG_INF)
    q, kf, kb, vb, vf = project(xs, T)
    o = _band_cached(q, kb, vb, ck, cv, bias)
    ys = _matmul_res(o.reshape(Bs * T, W), w_out, xs.reshape(Bs * T, D)).reshape(Bs, T, D)
    return yp, ys, (k_p, v_p, kf, vf)


def _s5_ops_kernel(ar_ref, ai_ref, ldt_ref, br_ref, bi_ref, cr_ref, ci_ref, cr4_ref, ci4_ref,
                   lag_ref, xr_ref, xi_ref, or_ref, oi_ref, lnr_ref, lni_ref, sr_ref, si_ref):
    n, P = xr_ref.shape[:2]
    dt = jnp.exp(ldt_ref[...])
    ar, ai = ar_ref[...], ai_ref[...]
    mag = jnp.exp(ar * dt)
    lr, li = mag * jnp.cos(ai * dt), mag * jnp.sin(ai * dt)
    den = ar * ar + ai * ai
    nr = lr - 1.0
    fr = (nr * ar + li * ai) / den
    fi = (li * ar - nr * ai) / den
    br, bi = br_ref[...], bi_ref[...]
    bbr = fr * br - fi * bi
    bbi = fr * bi + fi * br
    cr, ci = cr_ref[...], ci_ref[...]
    pr, pi = jnp.ones_like(lr), jnp.zeros_like(lr)
    for tau in range(n):
        sr_ref[...] = pr * bbr - pi * bbi
        si_ref[...] = pr * bbi + pi * bbr
        xr_ref[tau] = sr_ref[...].astype(BF16)
        xi_ref[tau] = si_ref[...].astype(BF16)

        def body(p, acc):
            return acc + cr4_ref[p] * sr_ref[p][None] - ci4_ref[p] * si_ref[p][None]

        lag_ref[tau] = lax.fori_loop(0, P, body, jnp.zeros(lag_ref.shape[1:], F32))
        pr, pi = pr * lr - pi * li, pr * li + pi * lr
        or_ref[tau] = (cr * pr - ci * pi).astype(BF16)
        oi_ref[tau] = (-(cr * pi + ci * pr)).astype(BF16)
    lnr_ref[...] = pr
    lni_ref[...] = pi


def _s5_operators(a_re, a_im, log_dt, b_re, b_im, c_re, c_im):
    G, P = a_re.shape
    C = S5_GROUP
    n = S5_CHUNK
    nlb = G // S5_LB_GROUPS
    f = lambda a: a.astype(F32)
    a_t = lambda a: f(a).T.reshape(P, 1, G)
    b_t = lambda a: jnp.transpose(f(a), (1, 2, 0))
    c_t = lambda a: jnp.transpose(f(a), (2, 1, 0))
    crt, cit = c_t(c_re), c_t(c_im)
    sds = lambda *shape: jax.ShapeDtypeStruct(shape, F32)
    lag, xr, xi, o_re, o_im, lnr, lni = pl.pallas_call(
        _s5_ops_kernel,
        out_shape=[sds(n, C, C, G)] + [jax.ShapeDtypeStruct((n, P, C, G), BF16)] * 4
        + [sds(P, 1, G)] * 2,
        scratch_shapes=[pltpu.VMEM((P, C, G), F32)] * 2,
        compiler_params=pltpu.CompilerParams(vmem_limit_bytes=VMEM_LIMIT),
        name="s5_ops",
    )(a_t(a_re), a_t(a_im), f(log_dt).reshape(1, 1, G), b_t(b_re), b_t(b_im), crt, cit,
      crt.reshape(P, C, 1, G), cit.reshape(P, C, 1, G))

    split = lambda a: a.reshape(a.shape[:-1] + (nlb, S5_LB_GROUPS))
    taps = jnp.transpose(split(lag), (3, 0, 4, 2, 1)).reshape(nlb, n, LANES, C)
    taps = jnp.pad(taps.astype(BF16), ((0, 0), (0, 0), (0, 0), (0, LANES - C)))

    def inc(x):
        w = jnp.transpose(split(x[::-1]), (3, 4, 0, 2, 1))
        return jnp.concatenate([w, w], axis=-1)

    def out(x):
        return jnp.transpose(split(x), (3, 4, 1, 0, 2)).reshape(nlb, S5_LB_GROUPS, P, n * C)

    lam_n = tuple(a.reshape(P, G).T.reshape(1, 1, G * P) for a in (lnr, lni))
    return taps, inc(xr), inc(xi), out(o_re), out(o_im), lam_n


def _chunk_rows(u_ref, tc):
    return jnp.concatenate(
        [u_ref[pl.ds(t, tc, stride=S5_CHUNK), :] for t in range(S5_CHUNK)], axis=1)


def _s5_inc_kernel(u_ref, xr_ref, xi_ref, sr_ref, si_ref, wr_ref, wi_ref):
    n, C = S5_CHUNK, S5_GROUP
    P = wr_ref.shape[1] // S5_LB_GROUPS

    @pl.when(pl.program_id(1) == 0)
    def _():
        for x_ref, w_ref in ((xr_ref, wr_ref), (xi_ref, wi_ref)):
            w_ref[...] = jnp.zeros(w_ref.shape, BF16)
            for g in range(S5_LB_GROUPS):
                half = (g % 2) * P
                for t in range(n):
                    r0 = t * LANES + g * C
                    w_ref[r0:r0 + C, g * P:(g + 1) * P] = x_ref[0, g, t, :, half:half + P]

    u = _chunk_rows(u_ref, sr_ref.shape[0]).astype(BF16)
    sr_ref[...] = _dot(u, wr_ref[...])
    si_ref[...] = _dot(u, wi_ref[...])


def _s5_inc(u, x_re, x_im, *, tc):
    Nc = u.shape[0] // S5_CHUNK
    nlb = x_re.shape[0]
    K = S5_CHUNK * LANES
    Wn = S5_LB_GROUPS * (x_re.shape[-1] // 2)
    xspec = pl.BlockSpec((1,) + x_re.shape[1:], lambda lb, i: (lb, 0, 0, 0, 0))
    ospec = pl.BlockSpec((tc, Wn), lambda lb, i: (i, lb))
    return pl.pallas_call(
        _s5_inc_kernel,
        grid=(nlb, Nc // tc),
        in_specs=[pl.BlockSpec((tc * S5_CHUNK, LANES), lambda lb, i: (i, lb)), xspec, xspec],
        out_specs=[ospec, ospec],
        out_shape=[jax.ShapeDtypeStruct((Nc, nlb * Wn), F32)] * 2,
        scratch_shapes=[pltpu.VMEM((K, Wn), BF16)] * 2,
        compiler_params=_cparams("parallel", "arbitrary"),
        name="s5_inc",
    )(u, x_re, x_im)


def _s5_scan_kernel(sr_ref, si_ref, lr_ref, li_ref, h0r_ref, h0i_ref,
                    pr_ref, pi_ref, fr_ref, fi_ref):
    lr, li = lr_ref[...], li_ref[...]
    nck = sr_ref.shape[1]

    def body(k, carry):
        hr, hi = carry
        pr_ref[:, pl.ds(k, 1), :] = hr
        pi_ref[:, pl.ds(k, 1), :] = hi
        sr = sr_ref[:, pl.ds(k, 1), :]
        si = si_ref[:, pl.ds(k, 1), :]
        return lr * hr - li * hi + sr, lr * hi + li * hr + si

    hr, hi = lax.fori_loop(0, nck, body, (h0r_ref[...], h0i_ref[...]))
    fr_ref[...] = hr
    fi_ref[...] = hi


def _s5_scan(sr, si, lam, h0r, h0i, *, cb=1024):
    B, nck, GP = sr.shape
    cb = _pick(GP, cb, LANES)
    full = pl.BlockSpec((B, nck, cb), lambda j: (0, 0, j))
    vec = pl.BlockSpec((1, 1, cb), lambda j: (0, 0, j))
    st = pl.BlockSpec((B, 1, cb), lambda j: (0, 0, j))
    return pl.pallas_call(
        _s5_scan_kernel,
        grid=(GP // cb,),
        in_specs=[full, full, vec, vec, st, st],
        out_specs=[full, full, st, st],
        out_shape=[jax.ShapeDtypeStruct((B, nck, GP), F32)] * 2
        + [jax.ShapeDtypeStruct((B, 1, GP), F32)] * 2,
        compiler_params=_cparams("parallel"),
        name="s5_scan",
    )(sr, si, lam[0], lam[1], h0r, h0i)


def _s5_out_kernel(u_ref, taps_ref, hr_ref, hi_ref, ar_ref, ai_ref, d_ref, o_ref,
                   toep_ref, wr_ref, wi_ref):
    n, C = S5_CHUNK, S5_GROUP
    tc = hr_ref.shape[0]
    P = ar_ref.shape[2]

    @pl.when(pl.program_id(1) == 0)
    def _():
        row = lax.broadcasted_iota(I32, (LANES, LANES), 0)
        lane = lax.broadcasted_iota(I32, (LANES, LANES), 1)
        spread = jnp.where(lane % C == row, 1.0, 0.0).astype(BF16)
        same_group = row // C == lane // C
        zero = jnp.zeros((LANES, LANES), BF16)
        for tau in range(n):
            tap = jnp.where(same_group, _dot(taps_ref[0, tau], spread), 0.0).astype(BF16)
            for t in range(n - tau):
                v = t + tau
                toep_ref[t * LANES:(t + 1) * LANES, v * LANES:(v + 1) * LANES] = tap
            for t in range(tau):
                toep_ref[tau * LANES:(tau + 1) * LANES, t * LANES:(t + 1) * LANES] = zero
        r2 = lax.broadcasted_iota(I32, (n * C, n * LANES), 0)
        c2 = lax.broadcasted_iota(I32, (n * C, n * LANES), 1)
        for g in range(S5_LB_GROUPS):
            place = jnp.where(c2 == (r2 // C) * LANES + g * C + r2 % C, 1.0, 0.0).astype(BF16)
            wr_ref[g * P:(g + 1) * P, :] = _dot(ar_ref[0, g], place).astype(BF16)
            wi_ref[g * P:(g + 1) * P, :] = _dot(ai_ref[0, g], place).astype(BF16)

    u = _chunk_rows(u_ref, tc)
    y = _dot(u.astype(BF16), toep_ref[...])
    y = y + _dot(hr_ref[...].astype(BF16), wr_ref[...]) + _dot(hi_ref[...].astype(BF16), wi_ref[...])
    y = jax.nn.gelu(y + jnp.tile(d_ref[...], (1, n)) * u)
    for t in range(n):
        o_ref[pl.ds(t, tc, stride=n), :] = y[:, t * LANES:(t + 1) * LANES]


def _s5_out(u, taps, hr, hi, a_re, a_im, d, *, tc):
    M, D = u.shape
    n = S5_CHUNK
    nlb, ng, P, _ = a_re.shape
    Wn, K = ng * P, n * LANES
    hspec = pl.BlockSpec((tc, Wn), lambda lb, i: (i, lb))
    uspec = pl.BlockSpec((tc * n, LANES), lambda lb, i: (i, lb))
    aspec = pl.BlockSpec((1,) + a_re.shape[1:], lambda lb, i: (lb, 0, 0, 0))
    return pl.pallas_call(
        _s5_out_kernel,
        grid=(nlb, M // (tc * n)),
        in_specs=[uspec,
                  pl.BlockSpec((1, n, LANES, LANES), lambda lb, i: (lb, 0, 0, 0)), hspec, hspec,
                  aspec, aspec,
                  pl.BlockSpec((1, LANES), lambda lb, i: (0, lb))],
        out_specs=uspec,
        out_shape=jax.ShapeDtypeStruct((M, D), F32),
        scratch_shapes=[pltpu.VMEM((K, K), BF16), pltpu.VMEM((Wn, K), BF16),
                        pltpu.VMEM((Wn, K), BF16)],
        compiler_params=_cparams("parallel", "arbitrary"),
        name="s5_out",
    )(u, taps, hr, hi, a_re, a_im, d)


def _glu_res_kernel(x_ref, wa_ref, wb_ref, r_ref, o_ref, xc_ref):
    @pl.when(pl.program_id(1) == 0)
    def _():
        xc_ref[...] = x_ref[...].astype(BF16)

    x = xc_ref[...]
    a = _dot(x, wa_ref[...])
    b = _dot(x, wb_ref[...])
    o_ref[...] = r_ref[...] + a * jax.nn.sigmoid(b)


def _glu_res(x, w, res, *, tm=1024, tn=512):
    M, D = x.shape
    N = w.shape[1] // 2
    tm = _pick(M, tm, 16)
    tn = _pick(N, tn, LANES)
    nj = N // tn
    return pl.pallas_call(
        _glu_res_kernel,
        grid=(M // tm, nj),
        in_specs=[pl.BlockSpec((tm, D), lambda i, j: (i, 0)),
                  pl.BlockSpec((D, tn), lambda i, j: (0, j)),
                  pl.BlockSpec((D, tn), lambda i, j: (0, nj + j)),
                  pl.BlockSpec((tm, tn), lambda i, j: (i, j))],
        out_specs=pl.BlockSpec((tm, tn), lambda i, j: (i, j)),
        out_shape=jax.ShapeDtypeStruct((M, N), F32),
        scratch_shapes=[pltpu.VMEM((tm, D), BF16)],
        compiler_params=_cparams("parallel", "arbitrary"),
        name="glu_res",
    )(x, w, w, res)


def _mixer_b(xp, xs, g, ops, d, w_glu, h0r, h0i):
    taps, inc_re, inc_im, out_re, out_im, lam = ops
    n = S5_CHUNK
    GP = lam[0].shape[-1]
    g, d = g[None], d.astype(F32)[None]
    w_glu = w_glu.astype(BF16)

    def stream(x, hr0, hi0):
        B, T, D = x.shape
        nck = T // n
        Nc = B * nck
        x2 = x.reshape(B * T, D)
        u = _rmsnorm(x2, g)
        tc = _pick(Nc, 256, 8)
        sr, si = _s5_inc(u, inc_re, inc_im, tc=tc)
        pr, pi, fr, fi = _s5_scan(sr.reshape(B, nck, GP), si.reshape(B, nck, GP), lam, hr0, hi0)
        gy = _s5_out(u, taps, pr.reshape(Nc, GP), pi.reshape(Nc, GP), out_re, out_im, d, tc=tc)
        y = _glu_res(gy, w_glu, x2)
        shp = (B, GP // S5_STATE, S5_STATE)
        return y.reshape(B, T, D), fr.reshape(shp), fi.reshape(shp)

    B = xp.shape[0]
    zero = jnp.zeros((B, 1, GP), F32)
    yp, rp, ip = stream(xp, zero, zero)
    Bs = xs.shape[0]
    ys, rs, is_ = stream(xs, h0r.astype(F32).reshape(Bs, 1, GP), h0i.astype(F32).reshape(Bs, 1, GP))
    return yp, ys, (rp, ip, rs, is_)


FFN_HALO = 16
FFN_TAIL = 8


def _ffn_kernel(*refs, seq_tiles, seq_len, halo):
    if halo:
        (x_ref, xh_ref, g_ref, wa_ref, wv_ref, cwa_ref, cwv_ref, cba_ref, cbv_ref, wd_ref,
         o_ref, ta_ref, tv_ref, xn_ref, xhn_ref) = refs
    else:
        (x_ref, p1a_ref, p1v_ref, p2a_ref, p2v_ref, g_ref, wa_ref, wv_ref, cwa_ref, cwv_ref,
         cba_ref, cbv_ref, wd_ref, o_ref, ta_ref, tv_ref, xn_ref, h_ref) = refs
    i, f = pl.program_id(0), pl.program_id(1)
    tm = x_ref.shape[0]

    @pl.when(f == 0)
    def _():
        x = x_ref[...]
        xn_ref[...] = _rms(x, g_ref[...]).astype(BF16)
        o_ref[...] = x
        if halo:
            xhn_ref[...] = _rms(xh_ref[...], g_ref[...]).astype(BF16)

    xn = xn_ref[...]
    tf = wa_ref.shape[1]
    r = lax.broadcasted_iota(I32, (tm, tf), 0)

    def conv(w_ref, cw_ref, cb_ref, p1_ref, p2_ref, t_ref):
        h = _dot(xn, w_ref[...])
        h1 = pltpu.roll(h, 1, axis=0)
        h2 = pltpu.roll(h, 2, axis=0)
        if halo:
            t_ref[...] = h[tm - FFN_TAIL:, :]
            hh = _dot(xhn_ref[...], w_ref[...])
            hh = hh * jnp.where(i % seq_tiles == 0, 0.0, 1.0)
            last, prev = hh[-1:, :], hh[-2:-1, :]
            h1 = jnp.where(r == 0, last, h1)
            h2 = jnp.where(r == 0, prev, jnp.where(r == 1, last, h2))
        else:
            for s in range(tf // LANES):
                sl = slice(s * LANES, (s + 1) * LANES)
                h_ref[s] = h[:, sl]
                for k in range(CONV_W - 1):
                    rows = pl.ds(seq_len - (CONV_W - 1) + k, tm // seq_len, stride=seq_len)
                    t_ref[k, :, sl] = h_ref.at[s][rows, :]
            t = r % seq_len
            h1 = jnp.where(t >= 1, h1, p1_ref[...])
            h2 = jnp.where(t >= 2, h2, p2_ref[...])
        cw = cw_ref[...]
        return cb_ref[...] + cw[0:1] * h2 + cw[1:2] * h1 + cw[2:3] * h

    if halo:
        ca = conv(wa_ref, cwa_ref, cba_ref, None, None, ta_ref)
        cv = conv(wv_ref, cwv_ref, cbv_ref, None, None, tv_ref)
    else:
        ca = conv(wa_ref, cwa_ref, cba_ref, p1a_ref, p2a_ref, ta_ref)
        cv = conv(wv_ref, cwv_ref, cbv_ref, p1v_ref, p2v_ref, tv_ref)
    act = (jax.nn.silu(ca) * cv).astype(BF16)
    o_ref[...] += _dot(act, wd_ref[...])


def _ffn(x, g, w_up, conv_w, conv_b, w_down, *, layer, seq_len, hist=None, tm=512, tf=1024):
    M, D = x.shape
    F = w_down.shape[1]
    halo = hist is None
    n_seq = M // seq_len
    tm = _pick(seq_len, tm, FFN_HALO) if halo else M
    tf = _pick(F, tf, LANES)
    nf = F // tf
    xs = pl.BlockSpec((tm, D), lambda i, f: (i, 0))
    a_col = lambda i, f: (layer, 0, f)
    v_col = lambda i, f: (layer, 0, nf + f)
    wspecs = [pl.BlockSpec((1, D), lambda i, f: (0, 0)),
              pl.BlockSpec((None, D, tf), a_col), pl.BlockSpec((None, D, tf), v_col),
              pl.BlockSpec((None, CONV_W, tf), a_col), pl.BlockSpec((None, CONV_W, tf), v_col),
              pl.BlockSpec((None, 1, tf), a_col), pl.BlockSpec((None, 1, tf), v_col),
              pl.BlockSpec((None, tf, D), lambda i, f: (layer, f, 0))]
    wargs = [g, w_up, w_up, conv_w, conv_w, conv_b, conv_b, w_down]
    scratch = [pltpu.VMEM((tm, D), BF16)]
    seq_tiles = max(seq_len // tm, 1)
    if halo:
        per = tm // FFN_HALO
        extra = [pl.BlockSpec((FFN_HALO, D), lambda i, f: (jnp.maximum(i * per - 1, 0), 0))]
        eargs = [x]
        scratch.append(pltpu.VMEM((FFN_HALO, D), BF16))
        tspec = pl.BlockSpec((FFN_TAIL, tf), lambda i, f: (i // seq_tiles, f))
        tshape = jax.ShapeDtypeStruct((n_seq * FFN_TAIL, F), F32)
    else:
        ha = pl.BlockSpec((tm, tf), lambda i, f: (i, f))
        hv = pl.BlockSpec((tm, tf), lambda i, f: (i, nf + f))
        extra = [ha, hv, ha, hv]
        eargs = [hist[0], hist[0], hist[1], hist[1]]
        scratch.append(pltpu.VMEM((tf // LANES, tm, LANES), F32))
        tspec = pl.BlockSpec((CONV_W - 1, n_seq, tf), lambda i, f: (0, 0, f))
        tshape = jax.ShapeDtypeStruct((CONV_W - 1, n_seq, F), F32)
    kern = functools.partial(_ffn_kernel, seq_tiles=seq_tiles, seq_len=seq_len, halo=halo)
    y, ta, tv = pl.pallas_call(
        kern,
        grid=(M // tm, nf),
        in_specs=[xs] + extra + wspecs,
        out_specs=[pl.BlockSpec((tm, D), lambda i, f: (i, 0)), tspec, tspec],
        out_shape=[jax.ShapeDtypeStruct((M, D), F32), tshape, tshape],
        scratch_shapes=scratch,
        compiler_params=_cparams("arbitrary", "arbitrary"),
        name="conv_ffn" if halo else "conv_ffn_hist",
    )(x, *eargs, *wargs)
    if halo:
        tail = lambda t: t.reshape(n_seq, FFN_TAIL, F)[:, FFN_TAIL - (CONV_W - 1):]
    else:
        tail = lambda t: jnp.swapaxes(t, 0, 1)
    return y, jnp.concatenate([tail(ta), tail(tv)], axis=-1)


def _conv_ffn(xp, xs, g, w_up, conv_w, conv_b, w_down, hist_s, layer):
    B, S, D = xp.shape
    Bs, T, _ = xs.shape
    F2 = w_up.shape[2]
    g = g[None]
    yp, hp = _ffn(xp.reshape(B * S, D), g, w_up, conv_w, conv_b, w_down, layer=layer, seq_len=S)
    hs = hist_s.astype(F32)
    t = jnp.arange(T)[None, :, None]
    p1 = jnp.where(t == 0, hs[:, -1:], 0.0).reshape(Bs * T, F2)
    p2 = jnp.where(t == 0, hs[:, -2:-1], jnp.where(t == 1, hs[:, -1:], 0.0)).reshape(Bs * T, F2)
    ys, hs_new = _ffn(xs.reshape(Bs * T, D), g, w_up, conv_w, conv_b, w_down, layer=layer,
                      seq_len=T, hist=(p1, p2))
    return yp.reshape(B, S, D), ys.reshape(Bs, T, D), hp, hs_new


def kernel(x_prompt, x_sample, cache_a_k, cache_a_v, cache_a_kidx, cache_c_k, cache_c_v, state_b_re, state_b_im, state_ffn_conv, norm_mix, norm_ffn, a_w_in, a_w_out, a_q_norm, a_k_norm, b_a_re, b_a_im, b_log_dt, b_b_re, b_b_im, b_c_re, b_c_im, b_d, b_w_glu, c_w_in, c_w_out, c_q_norm, c_k_norm, c_rel_bias, ffn_w_up, ffn_conv_w, ffn_conv_b, ffn_w_down):
    xp, xs = x_prompt, x_sample
    depth = norm_mix.shape[0]
    outs_a, outs_b, outs_c, f_p, f_s = [], [], [], [], []
    ffn_w_up, ffn_w_down = ffn_w_up.astype(BF16), ffn_w_down.astype(BF16)
    ffn_conv_b = ffn_conv_b[:, None]
    for i in range(depth):
        li, kind = i // 3, i % 3
        if kind == 0:
            xp, xs, o = _mixer_a(xp, xs, norm_mix[i], a_w_in[li], a_w_out[li], a_q_norm[li],
                                 a_k_norm[li], (cache_a_k, cache_a_v, cache_a_kidx), li)
            outs_a.append(o)
        elif kind == 1:
            ops = _s5_operators(b_a_re[li], b_a_im[li], b_log_dt[li], b_b_re[li], b_b_im[li],
                                b_c_re[li], b_c_im[li])
            xp, xs, o = _mixer_b(xp, xs, norm_mix[i], ops, b_d[li], b_w_glu[li],
                                 state_b_re[li], state_b_im[li])
            outs_b.append(o)
        else:
            xp, xs, o = _mixer_c(xp, xs, norm_mix[i], c_w_in[li], c_w_out[li], c_q_norm[li],
                                 c_k_norm[li], c_rel_bias[li], cache_c_k[li], cache_c_v[li])
            outs_c.append(o)
        xp, xs, hp, hs = _conv_ffn(xp, xs, norm_ffn[i], ffn_w_up, ffn_conv_w, ffn_conv_b,
                                   ffn_w_down, state_ffn_conv[i], i)
        f_p.append(hp)
        f_s.append(hs)
    stack = lambda outs, j: jnp.stack([o[j] for o in outs])
    return (xp, xs,
            *[stack(outs_a, j) for j in range(6)],
            *[stack(outs_c, j) for j in range(4)],
            *[stack(outs_b, j) for j in range(4)],
            jnp.stack(f_p), jnp.stack(f_s))
```

```python
import functools
import math

import jax
import jax.numpy as jnp
from jax import lax
from jax.experimental import pallas as pl
from jax.experimental.pallas import tpu as pltpu

F32 = jnp.float32
BF16 = jnp.bfloat16
I32 = jnp.int32

EPS = 1e-6
NEG_INF = -1e30
ROW_MAX_INIT = -1e29
THR_BITS_PER_CHECK = 4
ROPE_THETA = 10000.0
CHUNK = 64
HEAD_DIM = 128
LANES = 128

A_HEADS = 16
A_KV_HEADS = 4
A_GROUP = A_HEADS // A_KV_HEADS
IDX_HEADS = 8
IDX_DIM = 64
TOPK_MAX = 256
A_Q = A_HEADS * HEAD_DIM
A_KV = A_KV_HEADS * HEAD_DIM
A_QI = IDX_HEADS * IDX_DIM
IDX_W_SCALE = (IDX_HEADS * IDX_DIM) ** -0.5
A_OFF_K = A_Q
A_OFF_V = A_OFF_K + A_KV
A_OFF_QI = A_OFF_V + A_KV
A_OFF_KI = A_OFF_QI + A_QI
A_OFF_WI = A_OFF_KI + LANES
A_IN_PAD = A_OFF_WI + LANES

S5_GROUP = 16
S5_STATE = 64
S5_CHUNK = 16
S5_LB_GROUPS = LANES // S5_GROUP

C_HEADS = 16
C_BAND_CHUNKS = 8
C_REACH = C_BAND_CHUNKS * CHUNK
REL_CLIP = 128
CONV_W = 3

VMEM_LIMIT = 56 * 1024 * 1024


def _cparams(*sem):
    return pltpu.CompilerParams(dimension_semantics=sem, vmem_limit_bytes=VMEM_LIMIT)


def _pick(n, target, mult):
    best = None
    for t in range(mult, min(n, target) + 1, mult):
        if n % t == 0:
            best = t
    return best if best is not None else n


def _dot(a, b):
    return jnp.dot(a, b, preferred_element_type=F32)


def _dot_nt(a, b):
    return lax.dot_general(a, b, (((1,), (1,)), ((), ())), preferred_element_type=F32)


def _rms(x, g):
    return x * lax.rsqrt(jnp.mean(x * x, axis=-1, keepdims=True) + EPS) * g


def _rms_matmul_kernel(x_ref, g_ref, w_ref, o_ref, xn_ref):
    @pl.when(pl.program_id(1) == 0)
    def _():
        xn_ref[...] = _rms(x_ref[...], g_ref[...]).astype(BF16)

    o_ref[...] = _dot(xn_ref[...], w_ref[...]).astype(o_ref.dtype)


def _rms_matmul(x, g, w, *, tm=1024, tn=1024, out_dtype=F32):
    M, D = x.shape
    N = w.shape[1]
    tm = _pick(M, tm, 16)
    tn = _pick(N, tn, LANES)
    return pl.pallas_call(
        _rms_matmul_kernel,
        grid=(M // tm, N // tn),
        in_specs=[pl.BlockSpec((tm, D), lambda i, j: (i, 0)),
                  pl.BlockSpec((1, D), lambda i, j: (0, 0)),
                  pl.BlockSpec((D, tn), lambda i, j: (0, j))],
        out_specs=pl.BlockSpec((tm, tn), lambda i, j: (i, j)),
        out_shape=jax.ShapeDtypeStruct((M, N), out_dtype),
        scratch_shapes=[pltpu.VMEM((tm, D), BF16)],
        compiler_params=_cparams("parallel", "arbitrary"),
        name="rms_matmul",
    )(x, g, w)


def _rmsnorm_kernel(x_ref, g_ref, o_ref):
    o_ref[...] = _rms(x_ref[...], g_ref[...])


def _rmsnorm(x, g, *, tm=512):
    M, D = x.shape
    tm = _pick(M, tm, 8)
    return pl.pallas_call(
        _rmsnorm_kernel,
        grid=(M // tm,),
        in_specs=[pl.BlockSpec((tm, D), lambda i: (i, 0)),
                  pl.BlockSpec((1, D), lambda i: (0, 0))],
        out_specs=pl.BlockSpec((tm, D), lambda i: (i, 0)),
        out_shape=jax.ShapeDtypeStruct((M, D), F32),
        compiler_params=_cparams("parallel"),
        name="rmsnorm",
    )(x, g)


def _matmul_res_kernel(x_ref, w_ref, r_ref, o_ref):
    o_ref[...] = r_ref[...] + _dot(x_ref[...], w_ref[...])


def _matmul_res(x, w, res, *, tm=512, tn=2048):
    M, K = x.shape
    N = w.shape[1]
    tm = _pick(M, tm, 16)
    tn = _pick(N, tn, LANES)
    return pl.pallas_call(
        _matmul_res_kernel,
        grid=(M // tm, N // tn),
        in_specs=[pl.BlockSpec((tm, K), lambda i, j: (i, 0)),
                  pl.BlockSpec((K, tn), lambda i, j: (0, j)),
                  pl.BlockSpec((tm, tn), lambda i, j: (i, j))],
        out_specs=pl.BlockSpec((tm, tn), lambda i, j: (i, j)),
        out_shape=jax.ShapeDtypeStruct((M, N), F32),
        compiler_params=_cparams("parallel", "arbitrary"),
        name="matmul_res",
    )(x, w, res)


def _rope_tables(pos, dim):
    half = dim // 2
    inv = jnp.power(ROPE_THETA, -jnp.arange(half, dtype=F32) / half)
    ang = pos.astype(F32)[:, None] * inv[None, :]
    cos, sin = jnp.cos(ang), jnp.sin(ang)
    reps = LANES // dim
    return (jnp.tile(jnp.concatenate([cos, cos], axis=-1), (1, reps)),
            jnp.tile(jnp.concatenate([-sin, sin], axis=-1), (1, reps)))


def _a_post_kernel(h_ref, c128_ref, s128_ref, c64_ref, s64_ref, qg_ref, kg_ref,
                   q_ref, kf_ref, vf_ref, kb_ref, vb_ref, qi_ref, kif_ref, kib_ref, wi_ref):
    c128, s128 = c128_ref[...], s128_ref[...]
    c64, s64 = c64_ref[...], s64_ref[...]
    tm = h_ref.shape[0]
    lane = lax.broadcasted_iota(I32, (tm, LANES), 1)

    def rope128(x):
        return x * c128 + pltpu.roll(x, HEAD_DIM // 2, axis=1) * s128

    def rope64(x):
        lo = (lane % IDX_DIM) < (IDX_DIM // 2)
        partner = jnp.where(lo, pltpu.roll(x, LANES - IDX_DIM // 2, axis=1),
                            pltpu.roll(x, IDX_DIM // 2, axis=1))
        return x * c64 + partner * s64

    for h in range(A_HEADS):
        x = h_ref[:, h * LANES:(h + 1) * LANES]
        q_ref[:, h * LANES:(h + 1) * LANES] = (
            rope128(_rms(x, qg_ref[...])) * (HEAD_DIM ** -0.5)).astype(BF16)
    for h in range(A_KV_HEADS):
        x = h_ref[:, A_OFF_K + h * LANES:A_OFF_K + (h + 1) * LANES]
        k = rope128(_rms(x, kg_ref[...]))
        kf_ref[:, h, :] = k
        kb_ref[:, h * LANES:(h + 1) * LANES] = k.astype(BF16)
    ones = jnp.ones((tm, LANES), BF16)
    for h in range(A_KV_HEADS):
        v = h_ref[:, A_OFF_V + h * LANES:A_OFF_V + (h + 1) * LANES]
        vf_ref[:, h, :] = v
        vb_ref[:, 2 * h * LANES:(2 * h + 1) * LANES] = v.astype(BF16)
        vb_ref[:, (2 * h + 1) * LANES:(2 * h + 2) * LANES] = ones
    for p in range(A_QI // LANES):
        x = rope64(h_ref[:, A_OFF_QI + p * LANES:A_OFF_QI + (p + 1) * LANES])
        qi_ref[:, (2 * p) * LANES:(2 * p + 1) * LANES] = (
            jnp.where(lane < IDX_DIM, x, 0.0).astype(BF16))
        qi_ref[:, (2 * p + 1) * LANES:(2 * p + 2) * LANES] = (
            jnp.where(lane >= IDX_DIM, x, 0.0).astype(BF16))
    ki = rope64(h_ref[:, A_OFF_KI:A_OFF_KI + LANES])
    kif_ref[...] = ki
    kib_ref[...] = (ki + pltpu.roll(ki, IDX_DIM, axis=1)).astype(BF16)
    wi_ref[...] = h_ref[:, A_OFF_WI:A_OFF_WI + LANES] * IDX_W_SCALE


def _a_post(h, tabs, qg, kg, *, tm=256):
    M = h.shape[0]
    P = tabs[0].shape[0]
    tm = _pick(math.gcd(M, P), tm, 16)
    nt = P // tm
    row = lambda i: (i, 0)
    tab = lambda i: (i % nt, 0)
    one = lambda i: (0, 0)
    widths = [(A_KV, BF16), (2 * A_KV, BF16),
              (IDX_HEADS * LANES, BF16), (LANES, F32), (LANES, BF16), (LANES, F32)]
    heads = (A_KV_HEADS, HEAD_DIM)
    return pl.pallas_call(
        _a_post_kernel,
        grid=(M // tm,),
        in_specs=[pl.BlockSpec((tm, A_IN_PAD), row)]
        + [pl.BlockSpec((tm, LANES), tab)] * 4
        + [pl.BlockSpec((1, LANES), one)] * 2,
        out_specs=[pl.BlockSpec((tm, A_Q), row)]
        + [pl.BlockSpec((tm,) + heads, lambda i: (i, 0, 0))] * 2
        + [pl.BlockSpec((tm, w), row) for w, _ in widths],
        out_shape=[jax.ShapeDtypeStruct((M, A_Q), BF16)]
        + [jax.ShapeDtypeStruct((M,) + heads, F32)] * 2
        + [jax.ShapeDtypeStruct((M, w), dt) for w, dt in widths],
        compiler_params=_cparams("parallel"),
        name="a_post",
    )(h, *tabs, qg, kg)


def _dsa_kernel(*refs, tq, tk, ksel, causal, l_valid, n_chunks, idx_bits):
    if causal:
        q_ref, qi_ref, wi_ref, k_ref, v_ref, ki_ref, o_ref = refs[:7]
    else:
        q_ref, qi_ref, wi_ref, k_ref, v_ref, ki_ref, ck_ref, cv_ref, cki_ref, o_ref = refs[:10]
    key_scr, bias_scr, qst_scr, m_scr, acc_scr = refs[-5:]
    t0 = pl.program_id(1) * tq
    nch = (t0 + tq + tk - 1) // tk if causal else n_chunks
    nslab = tk // LANES
    row = lax.broadcasted_iota(I32, (tq, tk), 0)
    col = lax.broadcasted_iota(I32, (tq, tk), 1)
    lane = lax.broadcasted_iota(I32, (tq, LANES), 1)
    limit = ((t0 + row) // CHUNK + 1) * CHUNK if causal else l_valid

    def chunk_off(c):
        return c * tk if isinstance(c, int) else pl.multiple_of(c * tk, tk)

    def cached(c):
        return not causal and c < n_chunks - 1

    def new_off(c):
        return chunk_off(c) if causal else 0

    def load_ki(c):
        if cached(c):
            x = cki_ref[0, c * tk:(c + 1) * tk, :]
            return jnp.concatenate([x, x], axis=1).astype(BF16)
        return ki_ref[0, pl.ds(new_off(c), tk), :]

    def load_k(c, j):
        if cached(c):
            return ck_ref[0, c * tk:(c + 1) * tk, j, :].astype(BF16)
        return k_ref[0, pl.ds(new_off(c), tk), j * LANES:(j + 1) * LANES]

    def load_v(c, j):
        if cached(c):
            v = cv_ref[0, c * tk:(c + 1) * tk, j, :].astype(BF16)
            return jnp.concatenate([v, jnp.ones((tk, LANES), BF16)], axis=1)
        return v_ref[0, pl.ds(new_off(c), tk), 2 * j * LANES:2 * (j + 1) * LANES]

    def chunk_loop(body):
        if causal:
            lax.fori_loop(0, nch, lambda c, carry: (body(c), carry)[1], 0)
        else:
            for c in range(n_chunks):
                body(c)

    wi = wi_ref[0]

    def score_body(c):
        off = chunk_off(c)
        kic = load_ki(c)
        shs = [_dot_nt(qi_ref[0, :, h * LANES:(h + 1) * LANES], kic) for h in range(IDX_HEADS)]
        s = jnp.zeros((tq, tk), F32)
        for h in range(IDX_HEADS):
            s = s + wi[:, h:h + 1] * jnp.maximum(shs[h], 0.0)
        s = jnp.where(s == 0.0, 0.0, s)
        s = jnp.where(off + col < limit, s, NEG_INF)
        bits = pltpu.bitcast(s, I32)
        key_scr[:, pl.ds(off, tk)] = bits ^ ((bits >> 31) & 0x7FFFFFFF)

    chunk_loop(score_body)

    def count(pred):
        def body(c, acc):
            off = chunk_off(c)
            blk = key_scr[:, pl.ds(off, tk)]
            for s in range(nslab):
                sl = slice(s * LANES, (s + 1) * LANES)
                acc = acc + pred(blk[:, sl], lane + (off + s * LANES))
            return acc

        acc = lax.fori_loop(0, nch, body, jnp.zeros((tq, LANES), F32))
        return jnp.broadcast_to(jnp.sum(acc, axis=-1, keepdims=True), (tq, LANES))

    def thr_cond(st):
        i, _, n_ge = st
        return jnp.logical_and(i < 32, jnp.max(n_ge) > ksel)

    def thr_body(st):
        i, thr, n_ge = st
        for b in range(THR_BITS_PER_CHECK):
            cand = thr + lax.shift_left(jnp.int32(1), 31 - i - b)
            cnt = count(lambda key, idx: jnp.where(key >= cand, 1.0, 0.0))
            take = cnt >= ksel
            thr, n_ge = jnp.where(take, cand, thr), jnp.where(take, cnt, n_ge)
        return i + THR_BITS_PER_CHECK, thr, n_ge

    n_keys = (nch * tk).astype(F32) if causal else float(nch * tk)
    _, thr, _ = lax.while_loop(
        thr_cond, thr_body,
        (jnp.int32(0), jnp.full((tq, LANES), -2 ** 31, I32), jnp.full((tq, LANES), n_keys, F32)))

    need = ksel - count(lambda key, idx: jnp.where(key > thr, 1.0, 0.0))
    n_eq = count(lambda key, idx: jnp.where(key == thr, 1.0, 0.0))

    def cut_search():
        def body(i, cut):
            cand = cut + lax.shift_left(jnp.int32(1), idx_bits - 1 - i)
            below = count(lambda key, idx: jnp.where(key == thr, jnp.where(idx < cand, 1.0, 0.0), 0.0))
            return jnp.where(below <= need, cand, cut)

        return lax.fori_loop(0, idx_bits, body, jnp.zeros((tq, LANES), I32))

    cut = lax.cond(jnp.max(n_eq - need) > 0.0, cut_search,
                   lambda: jnp.full((tq, LANES), 2 ** idx_bits, I32))

    thr_t = jnp.tile(thr, (1, nslab))
    cut_t = jnp.tile(cut, (1, nslab))

    def bias_body(c, carry):
        off = chunk_off(c)
        key = key_scr[:, pl.ds(off, tk)]
        idx = off + col
        tie = jnp.where(key == thr_t, jnp.where(idx < cut_t, 0.0, NEG_INF), NEG_INF)
        bias = jnp.where(key > thr_t, 0.0, tie)
        bias_scr[:, pl.ds(off, tk)] = jnp.where(idx < limit, bias, NEG_INF)
        return carry

    lax.fori_loop(0, nch, bias_body, 0)

    for j in range(A_KV_HEADS):
        for g in range(A_GROUP):
            h = j * A_GROUP + g
            qst_scr[j, g * tq:(g + 1) * tq, :] = q_ref[0, :, h * LANES:(h + 1) * LANES]
    m_scr[...] = jnp.full(m_scr.shape, ROW_MAX_INIT, F32)
    acc_scr[...] = jnp.zeros(acc_scr.shape, F32)
    gr = A_GROUP * tq

    def attn_body(c):
        off = chunk_off(c)
        bias = bias_scr[:, pl.ds(off, tk)]
        logits = [_dot_nt(qst_scr[j], load_k(c, j)) for j in range(A_KV_HEADS)]
        for j in range(A_KV_HEADS):
            rows = slice(j * gr, (j + 1) * gr)
            m_old = m_scr[rows, :]
            alphas, ps = [], []
            for g in range(A_GROUP):
                sl = slice(g * tq, (g + 1) * tq)
                sg = logits[j][sl, :] + bias
                m_new = jnp.maximum(m_old[sl, :], jnp.max(sg, axis=-1, keepdims=True))
                alphas.append(jnp.exp(m_old[sl, :] - m_new))
                ps.append(jnp.exp((sg - jnp.tile(m_new, (1, nslab))).astype(BF16)))
                m_scr[j * gr + g * tq:j * gr + (g + 1) * tq, :] = m_new
            alpha = jnp.tile(jnp.concatenate(alphas, axis=0), (1, 2))
            pv = _dot(jnp.concatenate(ps, axis=0), load_v(c, j))
            acc_scr[rows, :] = alpha * acc_scr[rows, :] + pv

    chunk_loop(attn_body)
    for h in range(A_HEADS):
        rows = slice(h * tq, (h + 1) * tq)
        o_ref[0, :, h * LANES:(h + 1) * LANES] = (
            acc_scr[rows, :LANES] / acc_scr[rows, LANES:]).astype(BF16)


def _dsa(q, qi, wi, k, v, ki, cache=None, *, tq, tk, ksel, causal, l_valid):
    B, T, _ = q.shape
    Lk = k.shape[1]
    blk_q = lambda w: pl.BlockSpec((1, tq, w), lambda b, i: (b, i, 0))
    blk_k = lambda n, w: pl.BlockSpec((1, n, w), lambda b, i: (b, 0, 0))
    specs, args = [], []
    L = Lk
    if cache is not None:
        ck, cv, cki, li = cache
        past = ck.shape[2]
        L = past + Lk
        blk_c = pl.BlockSpec((None, 1, past, A_KV_HEADS, HEAD_DIM),
                             lambda b, i: (li, b, 0, 0, 0))
        specs = [blk_c, blk_c,
                 pl.BlockSpec((None, 1, past, IDX_DIM), lambda b, i: (li, b, 0, 0))]
        args = [ck, cv, cki]
    kern = functools.partial(
        _dsa_kernel, tq=tq, tk=tk, ksel=ksel, causal=causal, l_valid=l_valid,
        n_chunks=-(-l_valid // tk), idx_bits=max(1, (L - 1).bit_length()) + 1)
    return pl.pallas_call(
        kern,
        grid=(B, T // tq),
        in_specs=[blk_q(A_Q), blk_q(IDX_HEADS * LANES), blk_q(LANES),
                  blk_k(Lk, A_KV), blk_k(Lk, 2 * A_KV), blk_k(Lk, LANES)] + specs,
        out_specs=blk_q(A_Q),
        out_shape=jax.ShapeDtypeStruct((B, T, A_Q), BF16),
        scratch_shapes=[pltpu.VMEM((tq, L), I32), pltpu.VMEM((tq, L), F32),
                        pltpu.VMEM((A_KV_HEADS, A_GROUP * tq, LANES), BF16),
                        pltpu.VMEM((A_HEADS * tq, LANES), F32),
                        pltpu.VMEM((A_HEADS * tq, 2 * LANES), F32)],
        compiler_params=_cparams("parallel", "arbitrary"),
        name="dsa_causal" if causal else "dsa_cached",
    )(q, qi, wi, k, v, ki, *args)


def _pack_a_w_in(w):
    D = w.shape[0]
    o_ki = A_Q + 2 * A_KV + A_QI
    pad = lambda n: jnp.zeros((D, n), w.dtype)
    return jnp.concatenate(
        [w[:, :o_ki], w[:, o_ki:o_ki + IDX_DIM], pad(LANES - IDX_DIM),
         w[:, o_ki + IDX_DIM:], pad(LANES - IDX_HEADS)], axis=1).astype(BF16)


def _mixer_a(xp, xs, g, w_in, w_out, qg, kg, caches, li):
    B, S, D = xp.shape
    Bs, T, _ = xs.shape
    past = caches[0].shape[2]
    w_in = _pack_a_w_in(w_in)
    w_out = w_out.astype(BF16)
    g, qg, kg = g[None], qg[None], kg[None]

    def project(x, pos):
        b, t, _ = x.shape
        h = _rms_matmul(x.reshape(b * t, D), g, w_in, tn=A_IN_PAD // 5)
        tabs = _rope_tables(pos, HEAD_DIM) + _rope_tables(pos, IDX_DIM)
        q, kf, vf, *rest = _a_post(h, tabs, qg, kg)
        heads = (b, t, A_KV_HEADS, HEAD_DIM)
        return [q.reshape(b, t, -1), kf.reshape(heads), vf.reshape(heads)] + [
            o.reshape(b, t, -1) for o in rest]

    q, k_p, v_p, kb, vb, qi, kif, kib, wi = project(xp, jnp.arange(S))
    tk = _pick(S, 512, LANES)
    o = _dsa(q, qi, wi, kb, vb, kib, tq=min(128, S), tk=tk, ksel=min(TOPK_MAX, S // 4),
             causal=True, l_valid=S)
    yp = _matmul_res(o.reshape(B * S, A_Q), w_out, xp.reshape(B * S, D)).reshape(B, S, D)
    ki_p = kif[..., :IDX_DIM]

    pos_s = jnp.tile(past + jnp.arange(T), Bs)
    q, k_s, v_s, kb, vb, qi, kif, kib, wi = project(xs, pos_s)
    L = past + T
    tk = _pick(past, 512, LANES)
    padk = lambda a: jnp.pad(a, ((0, 0), (0, tk - T), (0, 0)))
    o = _dsa(q, qi, wi, padk(kb), padk(vb), padk(kib), caches + (li,),
             tq=T, tk=tk, ksel=min(TOPK_MAX, L // 4), causal=False, l_valid=L)
    ys = _matmul_res(o.reshape(Bs * T, A_Q), w_out, xs.reshape(Bs * T, D)).reshape(Bs, T, D)
    return yp, ys, (k_p, v_p, ki_p, k_s, v_s, kif[..., :IDX_DIM])


def _c_post_kernel(h_ref, qg_ref, kg_ref, q_ref, kb_ref, kf_ref, vf_ref):
    W = C_HEADS * HEAD_DIM
    for h in range(C_HEADS):
        sl = slice(h * LANES, (h + 1) * LANES)
        q = _rms(h_ref[:, sl].astype(F32), qg_ref[...])
        q_ref[:, sl] = (q * (HEAD_DIM ** -0.5)).astype(BF16)
        k = _rms(h_ref[:, W + h * LANES:W + (h + 1) * LANES].astype(F32), kg_ref[...])
        kf_ref[:, sl] = k
        kb_ref[:, sl] = k.astype(BF16)
    vf_ref[...] = h_ref[:, 2 * W:3 * W].astype(F32)


def _c_post(h, qg, kg, *, t, keep, tm=256):
    M = h.shape[0]
    W = C_HEADS * HEAD_DIM
    tm = _pick(math.gcd(t, keep), tm, 16) if keep < t else _pick(M, tm, 16)
    row = lambda i: (i, 0)
    if keep < t:
        tps, kps = t // tm, keep // tm
        tail = lambda i: ((i // tps) * kps + jnp.maximum(i % tps - (tps - kps), 0), 0)
    else:
        tail = row
    n_tail = M // t * keep
    return pl.pallas_call(
        _c_post_kernel,
        grid=(M // tm,),
        in_specs=[pl.BlockSpec((tm, 3 * W), row),
                  pl.BlockSpec((1, LANES), lambda i: (0, 0)),
                  pl.BlockSpec((1, LANES), lambda i: (0, 0))],
        out_specs=[pl.BlockSpec((tm, W), row)] * 2 + [pl.BlockSpec((tm, W), tail)] * 2,
        out_shape=[jax.ShapeDtypeStruct((M, W), BF16)] * 2
        + [jax.ShapeDtypeStruct((n_tail, W), F32)] * 2,
        compiler_params=_cparams("arbitrary"),
        name="c_post",
    )(h, qg, kg)


def _band_kernel(q_ref, k_ref, v_ref, b_ref, o_ref, *, sub, nsub, win, first_key):
    t0 = pl.program_id(2) * (sub * nsub)
    starts = [pl.multiple_of(t0 + s * sub, sub) for s in range(nsub)]
    lgs = [_dot_nt(q_ref[0, s * sub:(s + 1) * sub, :], k_ref[0, pl.ds(starts[s], win), :])
           + b_ref[0] for s in range(nsub)]
    if first_key is not None:
        kk = lax.broadcasted_iota(I32, (sub, win), 1)
        lgs = [jnp.where(kk >= first_key - starts[s], lgs[s], NEG_INF) for s in range(nsub)]
    ms = [jnp.max(lg, axis=-1, keepdims=True) for lg in lgs]
    ps = [jnp.exp(lg - m) for lg, m in zip(lgs, ms)]
    ls = [jnp.sum(p, axis=-1, keepdims=True) for p in ps]
    os_ = [_dot(ps[s].astype(BF16), v_ref[0, pl.ds(starts[s], win), :]) for s in range(nsub)]
    for s in range(nsub):
        o_ref[0, s * sub:(s + 1) * sub, :] = (os_[s] / ls[s]).astype(BF16)


def _band(q, k, v, bias, *, sub, nsub, first_key):
    B, T, W = q.shape
    Lk = k.shape[1]
    win = bias.shape[2]
    tq = sub * nsub
    kern = functools.partial(_band_kernel, sub=sub, nsub=nsub, win=win, first_key=first_key)
    return pl.pallas_call(
        kern,
        grid=(B, C_HEADS, T // tq),
        in_specs=[pl.BlockSpec((1, tq, LANES), lambda b, h, i: (b, i, h)),
                  pl.BlockSpec((1, Lk, LANES), lambda b, h, i: (b, 0, h)),
                  pl.BlockSpec((1, Lk, LANES), lambda b, h, i: (b, 0, h)),
                  pl.BlockSpec((1, sub, win), lambda b, h, i: (h, 0, 0))],
        out_specs=pl.BlockSpec((1, tq, LANES), lambda b, h, i: (b, i, h)),
        out_shape=jax.ShapeDtypeStruct((B, T, W), BF16),
        compiler_params=_cparams("parallel", "parallel", "arbitrary"),
        name="band_attn",
    )(q, k, v, bias)


def _band_cached_kernel(q_ref, k_ref, v_ref, ck_ref, cv_ref, b_ref, o_ref):
    T = q_ref.shape[1]
    cp = ck_ref.shape[1]
    win = b_ref.shape[2]
    pad = jnp.zeros((win - cp - T, LANES), BF16)

    def window(c_ref, n_ref, h):
        new = n_ref[0, :, h * LANES:(h + 1) * LANES]
        return jnp.concatenate([c_ref[0, :, h, :].astype(BF16), new, pad], axis=0)

    heads = range(C_HEADS)
    lgs = [_dot_nt(q_ref[0, :, h * LANES:(h + 1) * LANES], window(ck_ref, k_ref, h)) + b_ref[h]
           for h in heads]
    ms = [jnp.max(lg, axis=-1, keepdims=True) for lg in lgs]
    ps = [jnp.exp(lg - m) for lg, m in zip(lgs, ms)]
    ls = [jnp.sum(p, axis=-1, keepdims=True) for p in ps]
    for h in heads:
        o = _dot(ps[h].astype(BF16), window(cv_ref, v_ref, h)) / ls[h]
        o_ref[0, :, h * LANES:(h + 1) * LANES] = o.astype(BF16)


def _band_cached(q, k, v, ck, cv, bias):
    B, T, W = q.shape
    cp = ck.shape[1]
    row = pl.BlockSpec((1, T, W), lambda b: (b, 0, 0))
    cache = pl.BlockSpec((1, cp, C_HEADS, HEAD_DIM), lambda b: (b, 0, 0, 0))
    return pl.pallas_call(
        _band_cached_kernel,
        grid=(B,),
        in_specs=[row, row, row, cache, cache,
                  pl.BlockSpec(bias.shape, lambda b: (0, 0, 0))],
        out_specs=row,
        out_shape=jax.ShapeDtypeStruct((B, T, W), BF16),
        compiler_params=_cparams("parallel"),
        name="band_cached",
    )(q, k, v, ck, cv, bias)


def _mixer_c(xp, xs, g, w_in, w_out, qg, kg, rel, ck, cv):
    B, S, D = xp.shape
    Bs, T, _ = xs.shape
    W = C_HEADS * HEAD_DIM
    cp = ck.shape[1]
    w_in = w_in.astype(BF16)
    w_out = w_out.astype(BF16)
    g, qg, kg = g[None], qg[None], kg[None]
    relf = rel.astype(F32)

    def project(x, keep):
        b, t, _ = x.shape
        h = _rms_matmul(x.reshape(b * t, D), g, w_in, out_dtype=BF16)
        q, kb, kf, vf = _c_post(h, qg, kg, t=t, keep=keep)
        vb = h.reshape(b, t, 3 * W)[:, :, 2 * W:]
        tail = lambda a: a.reshape(b, keep, C_HEADS, HEAD_DIM)
        return q.reshape(b, t, W), tail(kf), kb.reshape(b, t, W), vb, tail(vf)

    def rel_bias(rows, win, shift):
        n = rows + win - 1
        j = jnp.arange(n)
        vec = relf[:, jnp.clip(shift + rows - 1 - j, -REL_CLIP, REL_CLIP) + REL_CLIP]
        flat = jnp.tile(jnp.pad(vec, ((0, 0), (0, 1))), (1, rows))
        toep = flat[:, :rows * n].reshape(-1, rows, n)
        return toep[:, :, rows - 1:rows - 1 + win]

    sub = min(2 * CHUNK, S)
    win = C_REACH + sub
    r = jnp.arange(sub)[:, None]
    kk = jnp.arange(win)[None, :]
    lo = (r // CHUNK) * CHUNK
    bias = jnp.where((kk >= lo) & (kk < lo + C_REACH + CHUNK), rel_bias(sub, win, C_REACH),
                     NEG_INF)
    q, k_p, kb, vb, v_p = project(xp, min(C_REACH, S))
    padk = lambda a: jnp.pad(a, ((0, 0), (C_REACH, 0), (0, 0)))
    o = _band(q, padk(kb), padk(vb), bias, sub=sub, nsub=_pick(S // sub, 8, 1),
              first_key=C_REACH)
    yp = _matmul_res(o.reshape(B * S, W), w_out, xp.reshape(B * S, D)).reshape(B, S, D)

    L = cp + T
    win = -(-L // LANES) * LANES
    bias = jnp.where(jnp.arange(win)[None, :] < L, rel_bias(T, win, cp), NEG_INF)
    q, kf, kb, vb, vf = project(xs, T)
    o = _band_cached(q, kb, vb, ck, cv, bias)
    ys = _matmul_res(o.reshape(Bs * T, W), w_out, xs.reshape(Bs * T, D)).reshape(Bs, T, D)
    return yp, ys, (k_p, v_p, kf, vf)


def _s5_ops_kernel(ar_ref, ai_ref, ldt_ref, br_ref, bi_ref, cr_ref, ci_ref, cr4_ref, ci4_ref,
                   lag_ref, xr_ref, xi_ref, or_ref, oi_ref, lnr_ref, lni_ref, sr_ref, si_ref):
    n, P = xr_ref.shape[:2]
    dt = jnp.exp(ldt_ref[...])
    ar, ai = ar_ref[...], ai_ref[...]
    mag = jnp.exp(ar * dt)
    lr, li = mag * jnp.cos(ai * dt), mag * jnp.sin(ai * dt)
    den = ar * ar + ai * ai
    nr = lr - 1.0
    fr = (nr * ar + li * ai) / den
    fi = (li * ar - nr * ai) / den
    br, bi = br_ref[...], bi_ref[...]
    bbr = fr * br - fi * bi
    bbi = fr * bi + fi * br
    cr, ci = cr_ref[...], ci_ref[...]
    pr, pi = jnp.ones_like(lr), jnp.zeros_like(lr)
    for tau in range(n):
        sr_ref[...] = pr * bbr - pi * bbi
        si_ref[...] = pr * bbi + pi * bbr
        xr_ref[tau] = sr_ref[...].astype(BF16)
        xi_ref[tau] = si_ref[...].astype(BF16)

        def body(p, acc):
            return acc + cr4_ref[p] * sr_ref[p][None] - ci4_ref[p] * si_ref[p][None]

        lag_ref[tau] = lax.fori_loop(0, P, body, jnp.zeros(lag_ref.shape[1:], F32))
        pr, pi = pr * lr - pi * li, pr * li + pi * lr
        or_ref[tau] = (cr * pr - ci * pi).astype(BF16)
        oi_ref[tau] = (-(cr * pi + ci * pr)).astype(BF16)
    lnr_ref[...] = pr
    lni_ref[...] = pi


def _s5_operators(a_re, a_im, log_dt, b_re, b_im, c_re, c_im):
    G, P = a_re.shape
    C = S5_GROUP
    n = S5_CHUNK
    nlb = G // S5_LB_GROUPS
    f = lambda a: a.astype(F32)
    a_t = lambda a: f(a).T.reshape(P, 1, G)
    b_t = lambda a: jnp.transpose(f(a), (1, 2, 0))
    c_t = lambda a: jnp.transpose(f(a), (2, 1, 0))
    crt, cit = c_t(c_re), c_t(c_im)
    sds = lambda *shape: jax.ShapeDtypeStruct(shape, F32)
    lag, xr, xi, o_re, o_im, lnr, lni = pl.pallas_call(
        _s5_ops_kernel,
        out_shape=[sds(n, C, C, G)] + [jax.ShapeDtypeStruct((n, P, C, G), BF16)] * 4
        + [sds(P, 1, G)] * 2,
        scratch_shapes=[pltpu.VMEM((P, C, G), F32)] * 2,
        compiler_params=pltpu.CompilerParams(vmem_limit_bytes=VMEM_LIMIT),
        name="s5_ops",
    )(a_t(a_re), a_t(a_im), f(log_dt).reshape(1, 1, G), b_t(b_re), b_t(b_im), crt, cit,
      crt.reshape(P, C, 1, G), cit.reshape(P, C, 1, G))

    split = lambda a: a.reshape(a.shape[:-1] + (nlb, S5_LB_GROUPS))
    taps = jnp.transpose(split(lag), (3, 0, 4, 2, 1)).reshape(nlb, n, LANES, C)
    taps = jnp.pad(taps.astype(BF16), ((0, 0), (0, 0), (0, 0), (0, LANES - C)))

    def inc(x):
        w = jnp.transpose(split(x[::-1]), (3, 4, 0, 2, 1))
        return jnp.concatenate([w, w], axis=-1)

    def out(x):
        return jnp.transpose(split(x), (3, 4, 1, 0, 2)).reshape(nlb, S5_LB_GROUPS, P, n * C)

    lam_n = tuple(a.reshape(P, G).T.reshape(1, 1, G * P) for a in (lnr, lni))
    return taps, inc(xr), inc(xi), out(o_re), out(o_im), lam_n


def _chunk_rows(u_ref, tc):
    return jnp.concatenate(
        [u_ref[pl.ds(t, tc, stride=S5_CHUNK), :] for t in range(S5_CHUNK)], axis=1)


def _s5_inc_kernel(u_ref, xr_ref, xi_ref, sr_ref, si_ref, wr_ref, wi_ref):
    n, C = S5_CHUNK, S5_GROUP
    P = wr_ref.shape[1] // S5_LB_GROUPS

    @pl.when(pl.program_id(1) == 0)
    def _():
        for x_ref, w_ref in ((xr_ref, wr_ref), (xi_ref, wi_ref)):
            w_ref[...] = jnp.zeros(w_ref.shape, BF16)
            for g in range(S5_LB_GROUPS):
                half = (g % 2) * P
                for t in range(n):
                    r0 = t * LANES + g * C
                    w_ref[r0:r0 + C, g * P:(g + 1) * P] = x_ref[0, g, t, :, half:half + P]

    u = _chunk_rows(u_ref, sr_ref.shape[0]).astype(BF16)
    sr_ref[...] = _dot(u, wr_ref[...])
    si_ref[...] = _dot(u, wi_ref[...])


def _s5_inc(u, x_re, x_im, *, tc):
    Nc = u.shape[0] // S5_CHUNK
    nlb = x_re.shape[0]
    K = S5_CHUNK * LANES
    Wn = S5_LB_GROUPS * (x_re.shape[-1] // 2)
    xspec = pl.BlockSpec((1,) + x_re.shape[1:], lambda lb, i: (lb, 0, 0, 0, 0))
    ospec = pl.BlockSpec((tc, Wn), lambda lb, i: (i, lb))
    return pl.pallas_call(
        _s5_inc_kernel,
        grid=(nlb, Nc // tc),
        in_specs=[pl.BlockSpec((tc * S5_CHUNK, LANES), lambda lb, i: (i, lb)), xspec, xspec],
        out_specs=[ospec, ospec],
        out_shape=[jax.ShapeDtypeStruct((Nc, nlb * Wn), F32)] * 2,
        scratch_shapes=[pltpu.VMEM((K, Wn), BF16)] * 2,
        compiler_params=_cparams("parallel", "arbitrary"),
        name="s5_inc",
    )(u, x_re, x_im)


def _s5_scan_kernel(sr_ref, si_ref, lr_ref, li_ref, h0r_ref, h0i_ref,
                    pr_ref, pi_ref, fr_ref, fi_ref):
    lr, li = lr_ref[...], li_ref[...]
    nck = sr_ref.shape[1]

    def body(k, carry):
        hr, hi = carry
        pr_ref[:, pl.ds(k, 1), :] = hr
        pi_ref[:, pl.ds(k, 1), :] = hi
        sr = sr_ref[:, pl.ds(k, 1), :]
        si = si_ref[:, pl.ds(k, 1), :]
        return lr * hr - li * hi + sr, lr * hi + li * hr + si

    hr, hi = lax.fori_loop(0, nck, body, (h0r_ref[...], h0i_ref[...]))
    fr_ref[...] = hr
    fi_ref[...] = hi


def _s5_scan(sr, si, lam, h0r, h0i, *, cb=1024):
    B, nck, GP = sr.shape
    cb = _pick(GP, cb, LANES)
    full = pl.BlockSpec((B, nck, cb), lambda j: (0, 0, j))
    vec = pl.BlockSpec((1, 1, cb), lambda j: (0, 0, j))
    st = pl.BlockSpec((B, 1, cb), lambda j: (0, 0, j))
    return pl.pallas_call(
        _s5_scan_kernel,
        grid=(GP // cb,),
        in_specs=[full, full, vec, vec, st, st],
        out_specs=[full, full, st, st],
        out_shape=[jax.ShapeDtypeStruct((B, nck, GP), F32)] * 2
        + [jax.ShapeDtypeStruct((B, 1, GP), F32)] * 2,
        compiler_params=_cparams("parallel"),
        name="s5_scan",
    )(sr, si, lam[0], lam[1], h0r, h0i)


def _s5_out_kernel(u_ref, taps_ref, hr_ref, hi_ref, ar_ref, ai_ref, d_ref, o_ref,
                   toep_ref, wr_ref, wi_ref):
    n, C = S5_CHUNK, S5_GROUP
    tc = hr_ref.shape[0]
    P = ar_ref.shape[2]

    @pl.when(pl.program_id(1) == 0)
    def _():
        row = lax.broadcasted_iota(I32, (LANES, LANES), 0)
        lane = lax.broadcasted_iota(I32, (LANES, LANES), 1)
        spread = jnp.where(lane % C == row, 1.0, 0.0).astype(BF16)
        same_group = row // C == lane // C
        zero = jnp.zeros((LANES, LANES), BF16)
        for tau in range(n):
            tap = jnp.where(same_group, _dot(taps_ref[0, tau], spread), 0.0).astype(BF16)
            for t in range(n - tau):
                v = t + tau
                toep_ref[t * LANES:(t + 1) * LANES, v * LANES:(v + 1) * LANES] = tap
            for t in range(tau):
                toep_ref[tau * LANES:(tau + 1) * LANES, t * LANES:(t + 1) * LANES] = zero
        r2 = lax.broadcasted_iota(I32, (n * C, n * LANES), 0)
        c2 = lax.broadcasted_iota(I32, (n * C, n * LANES), 1)
        for g in range(S5_LB_GROUPS):
            place = jnp.where(c2 == (r2 // C) * LANES + g * C + r2 % C, 1.0, 0.0).astype(BF16)
            wr_ref[g * P:(g + 1) * P, :] = _dot(ar_ref[0, g], place).astype(BF16)
            wi_ref[g * P:(g + 1) * P, :] = _dot(ai_ref[0, g], place).astype(BF16)

    u = _chunk_rows(u_ref, tc)
    y = _dot(u.astype(BF16), toep_ref[...])
    y = y + _dot(hr_ref[...].astype(BF16), wr_ref[...]) + _dot(hi_ref[...].astype(BF16), wi_ref[...])
    y = jax.nn.gelu(y + jnp.tile(d_ref[...], (1, n)) * u)
    for t in range(n):
        o_ref[pl.ds(t, tc, stride=n), :] = y[:, t * LANES:(t + 1) * LANES]


def _s5_out(u, taps, hr, hi, a_re, a_im, d, *, tc):
    M, D = u.shape
    n = S5_CHUNK
    nlb, ng, P, _ = a_re.shape
    Wn, K = ng * P, n * LANES
    hspec = pl.BlockSpec((tc, Wn), lambda lb, i: (i, lb))
    uspec = pl.BlockSpec((tc * n, LANES), lambda lb, i: (i, lb))
    aspec = pl.BlockSpec((1,) + a_re.shape[1:], lambda lb, i: (lb, 0, 0, 0))
    return pl.pallas_call(
        _s5_out_kernel,
        grid=(nlb, M // (tc * n)),
        in_specs=[uspec,
                  pl.BlockSpec((1, n, LANES, LANES), lambda lb, i: (lb, 0, 0, 0)), hspec, hspec,
                  aspec, aspec,
                  pl.BlockSpec((1, LANES), lambda lb, i: (0, lb))],
        out_specs=uspec,
        out_shape=jax.ShapeDtypeStruct((M, D), F32),
        scratch_shapes=[pltpu.VMEM((K, K), BF16), pltpu.VMEM((Wn, K), BF16),
                        pltpu.VMEM((Wn, K), BF16)],
        compiler_params=_cparams("parallel", "arbitrary"),
        name="s5_out",
    )(u, taps, hr, hi, a_re, a_im, d)


def _glu_res_kernel(x_ref, wa_ref, wb_ref, r_ref, o_ref, xc_ref):
    @pl.when(pl.program_id(1) == 0)
    def _():
        xc_ref[...] = x_ref[...].astype(BF16)

    x = xc_ref[...]
    a = _dot(x, wa_ref[...])
    b = _dot(x, wb_ref[...])
    o_ref[...] = r_ref[...] + a * jax.nn.sigmoid(b)


def _glu_res(x, w, res, *, tm=1024, tn=512):
    M, D = x.shape
    N = w.shape[1] // 2
    tm = _pick(M, tm, 16)
    tn = _pick(N, tn, LANES)
    nj = N // tn
    return pl.pallas_call(
        _glu_res_kernel,
        grid=(M // tm, nj),
        in_specs=[pl.BlockSpec((tm, D), lambda i, j: (i, 0)),
                  pl.BlockSpec((D, tn), lambda i, j: (0, j)),
                  pl.BlockSpec((D, tn), lambda i, j: (0, nj + j)),
                  pl.BlockSpec((tm, tn), lambda i, j: (i, j))],
        out_specs=pl.BlockSpec((tm, tn), lambda i, j: (i, j)),
        out_shape=jax.ShapeDtypeStruct((M, N), F32),
        scratch_shapes=[pltpu.VMEM((tm, D), BF16)],
        compiler_params=_cparams("parallel", "arbitrary"),
        name="glu_res",
    )(x, w, w, res)


def _mixer_b(xp, xs, g, ops, d, w_glu, h0r, h0i):
    taps, inc_re, inc_im, out_re, out_im, lam = ops
    n = S5_CHUNK
    GP = lam[0].shape[-1]
    g, d = g[None], d.astype(F32)[None]
    w_glu = w_glu.astype(BF16)

    def stream(x, hr0, hi0):
        B, T, D = x.shape
        nck = T // n
        Nc = B * nck
        x2 = x.reshape(B * T, D)
        u = _rmsnorm(x2, g)
        tc = _pick(Nc, 256, 8)
        sr, si = _s5_inc(u, inc_re, inc_im, tc=tc)
        pr, pi, fr, fi = _s5_scan(sr.reshape(B, nck, GP), si.reshape(B, nck, GP), lam, hr0, hi0)
        gy = _s5_out(u, taps, pr.reshape(Nc, GP), pi.reshape(Nc, GP), out_re, out_im, d, tc=tc)
        y = _glu_res(gy, w_glu, x2)
        shp = (B, GP // S5_STATE, S5_STATE)
        return y.reshape(B, T, D), fr.reshape(shp), fi.reshape(shp)

    B = xp.shape[0]
    zero = jnp.zeros((B, 1, GP), F32)
    yp, rp, ip = stream(xp, zero, zero)
    Bs = xs.shape[0]
    ys, rs, is_ = stream(xs, h0r.astype(F32).reshape(Bs, 1, GP), h0i.astype(F32).reshape(Bs, 1, GP))
    return yp, ys, (rp, ip, rs, is_)


FFN_HALO = 16
FFN_TAIL = 8


def _ffn_kernel(*refs, seq_tiles, seq_len, halo):
    if halo:
        (x_ref, xh_ref, g_ref, wa_ref, wv_ref, cwa_ref, cwv_ref, cba_ref, cbv_ref, wd_ref,
         o_ref, ta_ref, tv_ref, xn_ref, xhn_ref) = refs
    else:
        (x_ref, p1a_ref, p1v_ref, p2a_ref, p2v_ref, g_ref, wa_ref, wv_ref, cwa_ref, cwv_ref,
         cba_ref, cbv_ref, wd_ref, o_ref, ta_ref, tv_ref, xn_ref, h_ref) = refs
    i, f = pl.program_id(0), pl.program_id(1)
    tm = x_ref.shape[0]

    @pl.when(f == 0)
    def _():
        x = x_ref[...]
        xn_ref[...] = _rms(x, g_ref[...]).astype(BF16)
        o_ref[...] = x
        if halo:
            xhn_ref[...] = _rms(xh_ref[...], g_ref[...]).astype(BF16)

    xn = xn_ref[...]
    tf = wa_ref.shape[1]
    r = lax.broadcasted_iota(I32, (tm, tf), 0)

    def conv(w_ref, cw_ref, cb_ref, p1_ref, p2_ref, t_ref):
        h = _dot(xn, w_ref[...])
        h1 = pltpu.roll(h, 1, axis=0)
        h2 = pltpu.roll(h, 2, axis=0)
        if halo:
            t_ref[...] = h[tm - FFN_TAIL:, :]
            hh = _dot(xhn_ref[...], w_ref[...])
            hh = hh * jnp.where(i % seq_tiles == 0, 0.0, 1.0)
            last, prev = hh[-1:, :], hh[-2:-1, :]
            h1 = jnp.where(r == 0, last, h1)
            h2 = jnp.where(r == 0, prev, jnp.where(r == 1, last, h2))
        else:
            for s in range(tf // LANES):
                sl = slice(s * LANES, (s + 1) * LANES)
                h_ref[s] = h[:, sl]
                for k in range(CONV_W - 1):
                    rows = pl.ds(seq_len - (CONV_W - 1) + k, tm // seq_len, stride=seq_len)
                    t_ref[k, :, sl] = h_ref.at[s][rows, :]
            t = r % seq_len
            h1 = jnp.where(t >= 1, h1, p1_ref[...])
            h2 = jnp.where(t >= 2, h2, p2_ref[...])
        cw = cw_ref[...]
        return cb_ref[...] + cw[0:1] * h2 + cw[1:2] * h1 + cw[2:3] * h

    if halo:
        ca = conv(wa_ref, cwa_ref, cba_ref, None, None, ta_ref)
        cv = conv(wv_ref, cwv_ref, cbv_ref, None, None, tv_ref)
    else:
        ca = conv(wa_ref, cwa_ref, cba_ref, p1a_ref, p2a_ref, ta_ref)
        cv = conv(wv_ref, cwv_ref, cbv_ref, p1v_ref, p2v_ref, tv_ref)
    act = (jax.nn.silu(ca) * cv).astype(BF16)
    o_ref[...] += _dot(act, wd_ref[...])


def _ffn(x, g, w_up, conv_w, conv_b, w_down, *, layer, seq_len, hist=None, tm=512, tf=1024):
    M, D = x.shape
    F = w_down.shape[1]
    halo = hist is None
    n_seq = M // seq_len
    tm = _pick(seq_len, tm, FFN_HALO) if halo else M
    tf = _pick(F, tf, LANES)
    nf = F // tf
    xs = pl.BlockSpec((tm, D), lambda i, f: (i, 0))
    a_col = lambda i, f: (layer, 0, f)
    v_col = lambda i, f: (layer, 0, nf + f)
    wspecs = [pl.BlockSpec((1, D), lambda i, f: (0, 0)),
              pl.BlockSpec((None, D, tf), a_col), pl.BlockSpec((None, D, tf), v_col),
              pl.BlockSpec((None, CONV_W, tf), a_col), pl.BlockSpec((None, CONV_W, tf), v_col),
              pl.BlockSpec((None, 1, tf), a_col), pl.BlockSpec((None, 1, tf), v_col),
              pl.BlockSpec((None, tf, D), lambda i, f: (layer, f, 0))]
    wargs = [g, w_up, w_up, conv_w, conv_w, conv_b, conv_b, w_down]
    scratch = [pltpu.VMEM((tm, D), BF16)]
    seq_tiles = max(seq_len // tm, 1)
    if halo:
        per = tm // FFN_HALO
        extra = [pl.BlockSpec((FFN_HALO, D), lambda i, f: (jnp.maximum(i * per - 1, 0), 0))]
        eargs = [x]
        scratch.append(pltpu.VMEM((FFN_HALO, D), BF16))
        tspec = pl.BlockSpec((FFN_TAIL, tf), lambda i, f: (i // seq_tiles, f))
        tshape = jax.ShapeDtypeStruct((n_seq * FFN_TAIL, F), F32)
    else:
        ha = pl.BlockSpec((tm, tf), lambda i, f: (i, f))
        hv = pl.BlockSpec((tm, tf), lambda i, f: (i, nf + f))
        extra = [ha, hv, ha, hv]
        eargs = [hist[0], hist[0], hist[1], hist[1]]
        scratch.append(pltpu.VMEM((tf // LANES, tm, LANES), F32))
        tspec = pl.BlockSpec((CONV_W - 1, n_seq, tf), lambda i, f: (0, 0, f))
        tshape = jax.ShapeDtypeStruct((CONV_W - 1, n_seq, F), F32)
    kern = functools.partial(_ffn_kernel, seq_tiles=seq_tiles, seq_len=seq_len, halo=halo)
    y, ta, tv = pl.pallas_call(
        kern,
        grid=(M // tm, nf),
        in_specs=[xs] + extra + wspecs,
        out_specs=[pl.BlockSpec((tm, D), lambda i, f: (i, 0)), tspec, tspec],
        out_shape=[jax.ShapeDtypeStruct((M, D), F32), tshape, tshape],
        scratch_shapes=scratch,
        compiler_params=_cparams("arbitrary", "arbitrary"),
        name="conv_ffn" if halo else "conv_ffn_hist",
    )(x, *eargs, *wargs)
    if halo:
        tail = lambda t: t.reshape(n_seq, FFN_TAIL, F)[:, FFN_TAIL - (CONV_W - 1):]
    else:
        tail = lambda t: jnp.swapaxes(t, 0, 1)
    return y, jnp.concatenate([tail(ta), tail(tv)], axis=-1)


def _conv_ffn(xp, xs, g, w_up, conv_w, conv_b, w_down, hist_s, layer):
    B, S, D = xp.shape
    Bs, T, _ = xs.shape
    F2 = w_up.shape[2]
    g = g[None]
    yp, hp = _ffn(xp.reshape(B * S, D), g, w_up, conv_w, conv_b, w_down, layer=layer, seq_len=S)
    hs = hist_s.astype(F32)
    t = jnp.arange(T)[None, :, None]
    p1 = jnp.where(t == 0, hs[:, -1:], 0.0).reshape(Bs * T, F2)
    p2 = jnp.where(t == 0, hs[:, -2:-1], jnp.where(t == 1, hs[:, -1:], 0.0)).reshape(Bs * T, F2)
    ys, hs_new = _ffn(xs.reshape(Bs * T, D), g, w_up, conv_w, conv_b, w_down, layer=layer,
                      seq_len=T, hist=(p1, p2))
    return yp.reshape(B, S, D), ys.reshape(Bs, T, D), hp, hs_new


def kernel(x_prompt, x_sample, cache_a_k, cache_a_v, cache_a_kidx, cache_c_k, cache_c_v, state_b_re, state_b_im, state_ffn_conv, norm_mix, norm_ffn, a_w_in, a_w_out, a_q_norm, a_k_norm, b_a_re, b_a_im, b_log_dt, b_b_re, b_b_im, b_c_re, b_c_im, b_d, b_w_glu, c_w_in, c_w_out, c_q_norm, c_k_norm, c_rel_bias, ffn_w_up, ffn_conv_w, ffn_conv_b, ffn_w_down):
    xp, xs = x_prompt, x_sample
    depth = norm_mix.shape[0]
    outs_a, outs_b, outs_c, f_p, f_s = [], [], [], [], []
    ffn_w_up, ffn_w_down = ffn_w_up.astype(BF16), ffn_w_down.astype(BF16)
    ffn_conv_b = ffn_conv_b[:, None]
    for i in range(depth):
        li, kind = i // 3, i % 3
        if kind == 0:
            xp, xs, o = _mixer_a(xp, xs, norm_mix[i], a_w_in[li], a_w_out[li], a_q_norm[li],
                                 a_k_norm[li], (cache_a_k, cache_a_v, cache_a_kidx), li)
            outs_a.append(o)
        elif kind == 1:
            ops = _s5_operators(b_a_re[li], b_a_im[li], b_log_dt[li], b_b_re[li], b_b_im[li],
                                b_c_re[li], b_c_im[li])
            xp, xs, o = _mixer_b(xp, xs, norm_mix[i], ops, b_d[li], b_w_glu[li],
                                 state_b_re[li], state_b_im[li])
            outs_b.append(o)
        else:
            xp, xs, o = _mixer_c(xp, xs, norm_mix[i], c_w_in[li], c_w_out[li], c_q_norm[li],
                                 c_k_norm[li], c_rel_bias[li], cache_c_k[li], cache_c_v[li])
            outs_c.append(o)
        xp, xs, hp, hs = _conv_ffn(xp, xs, norm_ffn[i], ffn_w_up, ffn_conv_w, ffn_conv_b,
                                   ffn_w_down, state_ffn_conv[i], i)
        f_p.append(hp)
        f_s.append(hs)
    stack = lambda outs, j: jnp.stack([o[j] for o in outs])
    return (xp, xs,
            *[stack(outs_a, j) for j in range(6)],
            *[stack(outs_c, j) for j in range(4)],
            *[stack(outs_b, j) for j in range(4)],
            jnp.stack(f_p), jnp.stack(f_s))
```

```python
import functools
import math

import jax
import jax.numpy as jnp
from jax import lax
from jax.experimental import pallas as pl
from jax.experimental.pallas import tpu as pltpu

F32 = jnp.float32
BF16 = jnp.bfloat16
I32 = jnp.int32

EPS = 1e-6
NEG_INF = -1e30
ROW_MAX_INIT = -1e29
THR_BITS_PER_CHECK = 4
ROPE_THETA = 10000.0
CHUNK = 64
HEAD_DIM = 128
LANES = 128

A_HEADS = 16
A_KV_HEADS = 4
A_GROUP = A_HEADS // A_KV_HEADS
IDX_HEADS = 8
IDX_DIM = 64
TOPK_MAX = 256
A_Q = A_HEADS * HEAD_DIM
A_KV = A_KV_HEADS * HEAD_DIM
A_QI = IDX_HEADS * IDX_DIM
IDX_W_SCALE = (IDX_HEADS * IDX_DIM) ** -0.5
A_OFF_K = A_Q
A_OFF_V = A_OFF_K + A_KV
A_OFF_QI = A_OFF_V + A_KV
A_OFF_KI = A_OFF_QI + A_QI
A_OFF_WI = A_OFF_KI + LANES
A_IN_PAD = A_OFF_WI + LANES

S5_GROUP = 16
S5_STATE = 64
S5_CHUNK = 16
S5_LB_GROUPS = LANES // S5_GROUP

C_HEADS = 16
C_BAND_CHUNKS = 8
C_REACH = C_BAND_CHUNKS * CHUNK
REL_CLIP = 128
CONV_W = 3

VMEM_LIMIT = 56 * 1024 * 1024


def _cparams(*sem):
    return pltpu.CompilerParams(dimension_semantics=sem, vmem_limit_bytes=VMEM_LIMIT)


def _pick(n, target, mult):
    best = None
    for t in range(mult, min(n, target) + 1, mult):
        if n % t == 0:
            best = t
    return best if best is not None else n


def _dot(a, b):
    return jnp.dot(a, b, preferred_element_type=F32)


def _dot_nt(a, b):
    return lax.dot_general(a, b, (((1,), (1,)), ((), ())), preferred_element_type=F32)


def _rms(x, g):
    return x * lax.rsqrt(jnp.mean(x * x, axis=-1, keepdims=True) + EPS) * g


def _rms_matmul_kernel(x_ref, g_ref, w_ref, o_ref, xn_ref):
    @pl.when(pl.program_id(1) == 0)
    def _():
        xn_ref[...] = _rms(x_ref[...], g_ref[...]).astype(BF16)

    o_ref[...] = _dot(xn_ref[...], w_ref[...]).astype(o_ref.dtype)


def _rms_matmul(x, g, w, *, tm=1024, tn=1024, out_dtype=F32):
    M, D = x.shape
    N = w.shape[1]
    tm = _pick(M, tm, 16)
    tn = _pick(N, tn, LANES)
    return pl.pallas_call(
        _rms_matmul_kernel,
        grid=(M // tm, N // tn),
        in_specs=[pl.BlockSpec((tm, D), lambda i, j: (i, 0)),
                  pl.BlockSpec((1, D), lambda i, j: (0, 0)),
                  pl.BlockSpec((D, tn), lambda i, j: (0, j))],
        out_specs=pl.BlockSpec((tm, tn), lambda i, j: (i, j)),
        out_shape=jax.ShapeDtypeStruct((M, N), out_dtype),
        scratch_shapes=[pltpu.VMEM((tm, D), BF16)],
        compiler_params=_cparams("parallel", "arbitrary"),
        name="rms_matmul",
    )(x, g, w)


def _rmsnorm_kernel(x_ref, g_ref, o_ref):
    o_ref[...] = _rms(x_ref[...], g_ref[...])


def _rmsnorm(x, g, *, tm=512):
    M, D = x.shape
    tm = _pick(M, tm, 8)
    return pl.pallas_call(
        _rmsnorm_kernel,
        grid=(M // tm,),
        in_specs=[pl.BlockSpec((tm, D), lambda i: (i, 0)),
                  pl.BlockSpec((1, D), lambda i: (0, 0))],
        out_specs=pl.BlockSpec((tm, D), lambda i: (i, 0)),
        out_shape=jax.ShapeDtypeStruct((M, D), F32),
        compiler_params=_cparams("parallel"),
        name="rmsnorm",
    )(x, g)


def _matmul_res_kernel(x_ref, w_ref, r_ref, o_ref):
    o_ref[...] = r_ref[...] + _dot(x_ref[...], w_ref[...])


def _matmul_res(x, w, res, *, tm=512, tn=2048):
    M, K = x.shape
    N = w.shape[1]
    tm = _pick(M, tm, 16)
    tn = _pick(N, tn, LANES)
    return pl.pallas_call(
        _matmul_res_kernel,
        grid=(M // tm, N // tn),
        in_specs=[pl.BlockSpec((tm, K), lambda i, j: (i, 0)),
                  pl.BlockSpec((K, tn), lambda i, j: (0, j)),
                  pl.BlockSpec((tm, tn), lambda i, j: (i, j))],
        out_specs=pl.BlockSpec((tm, tn), lambda i, j: (i, j)),
        out_shape=jax.ShapeDtypeStruct((M, N), F32),
        compiler_params=_cparams("parallel", "arbitrary"),
        name="matmul_res",
    )(x, w, res)


def _rope_tables(pos, dim):
    half = dim // 2
    inv = jnp.power(ROPE_THETA, -jnp.arange(half, dtype=F32) / half)
    ang = pos.astype(F32)[:, None] * inv[None, :]
    cos, sin = jnp.cos(ang), jnp.sin(ang)
    reps = LANES // dim
    return (jnp.tile(jnp.concatenate([cos, cos], axis=-1), (1, reps)),
            jnp.tile(jnp.concatenate([-sin, sin], axis=-1), (1, reps)))


def _a_post_kernel(h_ref, c128_ref, s128_ref, c64_ref, s64_ref, qg_ref, kg_ref,
                   q_ref, kf_ref, vf_ref, kb_ref, vb_ref, qi_ref, kif_ref, kib_ref, wi_ref):
    c128, s128 = c128_ref[...], s128_ref[...]
    c64, s64 = c64_ref[...], s64_ref[...]
    tm = h_ref.shape[0]
    lane = lax.broadcasted_iota(I32, (tm, LANES), 1)

    def rope128(x):
        return x * c128 + pltpu.roll(x, HEAD_DIM // 2, axis=1) * s128

    def rope64(x):
        lo = (lane % IDX_DIM) < (IDX_DIM // 2)
        partner = jnp.where(lo, pltpu.roll(x, LANES - IDX_DIM // 2, axis=1),
                            pltpu.roll(x, IDX_DIM // 2, axis=1))
        return x * c64 + partner * s64

    for h in range(A_HEADS):
        x = h_ref[:, h * LANES:(h + 1) * LANES]
        q_ref[:, h * LANES:(h + 1) * LANES] = (
            rope128(_rms(x, qg_ref[...])) * (HEAD_DIM ** -0.5)).astype(BF16)
    for h in range(A_KV_HEADS):
        x = h_ref[:, A_OFF_K + h * LANES:A_OFF_K + (h + 1) * LANES]
        k = rope128(_rms(x, kg_ref[...]))
        kf_ref[:, h, :] = k
        kb_ref[:, h * LANES:(h + 1) * LANES] = k.astype(BF16)
    ones = jnp.ones((tm, LANES), BF16)
    for h in range(A_KV_HEADS):
        v = h_ref[:, A_OFF_V + h * LANES:A_OFF_V + (h + 1) * LANES]
        vf_ref[:, h, :] = v
        vb_ref[:, 2 * h * LANES:(2 * h + 1) * LANES] = v.astype(BF16)
        vb_ref[:, (2 * h + 1) * LANES:(2 * h + 2) * LANES] = ones
    for p in range(A_QI // LANES):
        x = rope64(h_ref[:, A_OFF_QI + p * LANES:A_OFF_QI + (p + 1) * LANES])
        qi_ref[:, (2 * p) * LANES:(2 * p + 1) * LANES] = (
            jnp.where(lane < IDX_DIM, x, 0.0).astype(BF16))
        qi_ref[:, (2 * p + 1) * LANES:(2 * p + 2) * LANES] = (
            jnp.where(lane >= IDX_DIM, x, 0.0).astype(BF16))
    ki = rope64(h_ref[:, A_OFF_KI:A_OFF_KI + LANES])
    kif_ref[...] = ki
    kib_ref[...] = (ki + pltpu.roll(ki, IDX_DIM, axis=1)).astype(BF16)
    wi_ref[...] = h_ref[:, A_OFF_WI:A_OFF_WI + LANES] * IDX_W_SCALE


def _a_post(h, tabs, qg, kg, *, tm=512):
    M = h.shape[0]
    P = tabs[0].shape[0]
    tm = _pick(math.gcd(M, P), tm, 16)
    nt = P // tm
    row = lambda i: (i, 0)
    tab = lambda i: (i % nt, 0)
    one = lambda i: (0, 0)
    widths = [(A_KV, BF16), (2 * A_KV, BF16),
              (IDX_HEADS * LANES, BF16), (LANES, F32), (LANES, BF16), (LANES, F32)]
    heads = (A_KV_HEADS, HEAD_DIM)
    return pl.pallas_call(
        _a_post_kernel,
        grid=(M // tm,),
        in_specs=[pl.BlockSpec((tm, A_IN_PAD), row)]
        + [pl.BlockSpec((tm, LANES), tab)] * 4
        + [pl.BlockSpec((1, LANES), one)] * 2,
        out_specs=[pl.BlockSpec((tm, A_Q), row)]
        + [pl.BlockSpec((tm,) + heads, lambda i: (i, 0, 0))] * 2
        + [pl.BlockSpec((tm, w), row) for w, _ in widths],
        out_shape=[jax.ShapeDtypeStruct((M, A_Q), BF16)]
        + [jax.ShapeDtypeStruct((M,) + heads, F32)] * 2
        + [jax.ShapeDtypeStruct((M, w), dt) for w, dt in widths],
        compiler_params=_cparams("parallel"),
        name="a_post",
    )(h, *tabs, qg, kg)


def _dsa_kernel(*refs, tq, tk, ksel, causal, l_valid, n_chunks, idx_bits):
    if causal:
        q_ref, qi_ref, wi_ref, k_ref, v_ref, ki_ref, o_ref = refs[:7]
    else:
        q_ref, qi_ref, wi_ref, k_ref, v_ref, ki_ref, ck_ref, cv_ref, cki_ref, o_ref = refs[:10]
    key_scr, bias_scr, qst_scr, m_scr, acc_scr = refs[-5:]
    t0 = pl.program_id(1) * tq
    nch = (t0 + tq + tk - 1) // tk if causal else n_chunks
    nslab = tk // LANES
    row = lax.broadcasted_iota(I32, (tq, tk), 0)
    col = lax.broadcasted_iota(I32, (tq, tk), 1)
    lane = lax.broadcasted_iota(I32, (tq, LANES), 1)
    limit = ((t0 + row) // CHUNK + 1) * CHUNK if causal else l_valid

    def chunk_off(c):
        return c * tk if isinstance(c, int) else pl.multiple_of(c * tk, tk)

    def cached(c):
        return not causal and c < n_chunks - 1

    def new_off(c):
        return chunk_off(c) if causal else 0

    def load_ki(c):
        if cached(c):
            x = cki_ref[0, c * tk:(c + 1) * tk, :]
            return jnp.concatenate([x, x], axis=1).astype(BF16)
        return ki_ref[0, pl.ds(new_off(c), tk), :]

    def load_k(c, j):
        if cached(c):
            return ck_ref[0, c * tk:(c + 1) * tk, j, :].astype(BF16)
        return k_ref[0, pl.ds(new_off(c), tk), j * LANES:(j + 1) * LANES]

    def load_v(c, j):
        if cached(c):
            v = cv_ref[0, c * tk:(c + 1) * tk, j, :].astype(BF16)
            return jnp.concatenate([v, jnp.ones((tk, LANES), BF16)], axis=1)
        return v_ref[0, pl.ds(new_off(c), tk), 2 * j * LANES:2 * (j + 1) * LANES]

    def chunk_loop(body):
        if causal:
            lax.fori_loop(0, nch, lambda c, carry: (body(c), carry)[1], 0)
        else:
            for c in range(n_chunks):
                body(c)

    wi = wi_ref[0]

    def score_body(c):
        off = chunk_off(c)
        kic = load_ki(c)
        shs = [_dot_nt(qi_ref[0, :, h * LANES:(h + 1) * LANES], kic) for h in range(IDX_HEADS)]
        s = jnp.zeros((tq, tk), F32)
        for h in range(IDX_HEADS):
            s = s + wi[:, h:h + 1] * jnp.maximum(shs[h], 0.0)
        s = jnp.where(s == 0.0, 0.0, s)
        s = jnp.where(off + col < limit, s, NEG_INF)
        bits = pltpu.bitcast(s, I32)
        key_scr[:, pl.ds(off, tk)] = bits ^ ((bits >> 31) & 0x7FFFFFFF)

    chunk_loop(score_body)

    def count(pred):
        def body(c, acc):
            off = chunk_off(c)
            blk = key_scr[:, pl.ds(off, tk)]
            for s in range(nslab):
                sl = slice(s * LANES, (s + 1) * LANES)
                acc = acc + pred(blk[:, sl], lane + (off + s * LANES))
            return acc

        acc = lax.fori_loop(0, nch, body, jnp.zeros((tq, LANES), F32))
        return jnp.broadcast_to(jnp.sum(acc, axis=-1, keepdims=True), (tq, LANES))

    def thr_cond(st):
        i, _, n_ge = st
        return jnp.logical_and(i < 32, jnp.max(n_ge) > ksel)

    def thr_body(st):
        i, thr, n_ge = st
        for b in range(THR_BITS_PER_CHECK):
            cand = thr + lax.shift_left(jnp.int32(1), 31 - i - b)
            cnt = count(lambda key, idx: jnp.where(key >= cand, 1.0, 0.0))
            take = cnt >= ksel
            thr, n_ge = jnp.where(take, cand, thr), jnp.where(take, cnt, n_ge)
        return i + THR_BITS_PER_CHECK, thr, n_ge

    n_keys = (nch * tk).astype(F32) if causal else float(nch * tk)
    _, thr, _ = lax.while_loop(
        thr_cond, thr_body,
        (jnp.int32(0), jnp.full((tq, LANES), -2 ** 31, I32), jnp.full((tq, LANES), n_keys, F32)))

    need = ksel - count(lambda key, idx: jnp.where(key > thr, 1.0, 0.0))
    n_eq = count(lambda key, idx: jnp.where(key == thr, 1.0, 0.0))

    def cut_search():
        def body(i, cut):
            cand = cut + lax.shift_left(jnp.int32(1), idx_bits - 1 - i)
            below = count(lambda key, idx: jnp.where(key == thr, jnp.where(idx < cand, 1.0, 0.0), 0.0))
            return jnp.where(below <= need, cand, cut)

        return lax.fori_loop(0, idx_bits, body, jnp.zeros((tq, LANES), I32))

    cut = lax.cond(jnp.max(n_eq - need) > 0.0, cut_search,
                   lambda: jnp.full((tq, LANES), 2 ** idx_bits, I32))

    thr_t = jnp.tile(thr, (1, nslab))
    cut_t = jnp.tile(cut, (1, nslab))

    def bias_body(c, carry):
        off = chunk_off(c)
        key = key_scr[:, pl.ds(off, tk)]
        idx = off + col
        tie = jnp.where(key == thr_t, jnp.where(idx < cut_t, 0.0, NEG_INF), NEG_INF)
        bias = jnp.where(key > thr_t, 0.0, tie)
        bias_scr[:, pl.ds(off, tk)] = jnp.where(idx < limit, bias, NEG_INF)
        return carry

    lax.fori_loop(0, nch, bias_body, 0)

    for j in range(A_KV_HEADS):
        for g in range(A_GROUP):
            h = j * A_GROUP + g
            qst_scr[j, g * tq:(g + 1) * tq, :] = q_ref[0, :, h * LANES:(h + 1) * LANES]
    m_scr[...] = jnp.full(m_scr.shape, ROW_MAX_INIT, F32)
    acc_scr[...] = jnp.zeros(acc_scr.shape, F32)
    gr = A_GROUP * tq

    def attn_body(c):
        off = chunk_off(c)
        bias = bias_scr[:, pl.ds(off, tk)]
        logits = [_dot_nt(qst_scr[j], load_k(c, j)) for j in range(A_KV_HEADS)]
        for j in range(A_KV_HEADS):
            rows = slice(j * gr, (j + 1) * gr)
            m_old = m_scr[rows, :]
            alphas, ps = [], []
            for g in range(A_GROUP):
                sl = slice(g * tq, (g + 1) * tq)
                sg = logits[j][sl, :] + bias
                m_new = jnp.maximum(m_old[sl, :], jnp.max(sg, axis=-1, keepdims=True))
                alphas.append(jnp.exp(m_old[sl, :] - m_new))
                ps.append(jnp.exp((sg - jnp.tile(m_new, (1, nslab))).astype(BF16)))
                m_scr[j * gr + g * tq:j * gr + (g + 1) * tq, :] = m_new
            alpha = jnp.tile(jnp.concatenate(alphas, axis=0), (1, 2))
            pv = _dot(jnp.concatenate(ps, axis=0), load_v(c, j))
            acc_scr[rows, :] = alpha * acc_scr[rows, :] + pv

    chunk_loop(attn_body)
    for h in range(A_HEADS):
        rows = slice(h * tq, (h + 1) * tq)
        o_ref[0, :, h * LANES:(h + 1) * LANES] = (
            acc_scr[rows, :LANES] / acc_scr[rows, LANES:]).astype(BF16)


def _dsa(q, qi, wi, k, v, ki, cache=None, *, tq, tk, ksel, causal, l_valid):
    B, T, _ = q.shape
    Lk = k.shape[1]
    blk_q = lambda w: pl.BlockSpec((1, tq, w), lambda b, i: (b, i, 0))
    blk_k = lambda n, w: pl.BlockSpec((1, n, w), lambda b, i: (b, 0, 0))
    specs, args = [], []
    L = Lk
    if cache is not None:
        ck, cv, cki, li = cache
        past = ck.shape[2]
        L = past + Lk
        blk_c = pl.BlockSpec((None, 1, past, A_KV_HEADS, HEAD_DIM),
                             lambda b, i: (li, b, 0, 0, 0))
        specs = [blk_c, blk_c,
                 pl.BlockSpec((None, 1, past, IDX_DIM), lambda b, i: (li, b, 0, 0))]
        args = [ck, cv, cki]
    kern = functools.partial(
        _dsa_kernel, tq=tq, tk=tk, ksel=ksel, causal=causal, l_valid=l_valid,
        n_chunks=-(-l_valid // tk), idx_bits=max(1, (L - 1).bit_length()) + 1)
    return pl.pallas_call(
        kern,
        grid=(B, T // tq),
        in_specs=[blk_q(A_Q), blk_q(IDX_HEADS * LANES), blk_q(LANES),
                  blk_k(Lk, A_KV), blk_k(Lk, 2 * A_KV), blk_k(Lk, LANES)] + specs,
        out_specs=blk_q(A_Q),
        out_shape=jax.ShapeDtypeStruct((B, T, A_Q), BF16),
        scratch_shapes=[pltpu.VMEM((tq, L), I32), pltpu.VMEM((tq, L), F32),
                        pltpu.VMEM((A_KV_HEADS, A_GROUP * tq, LANES), BF16),
                        pltpu.VMEM((A_HEADS * tq, LANES), F32),
                        pltpu.VMEM((A_HEADS * tq, 2 * LANES), F32)],
        compiler_params=_cparams("parallel", "arbitrary"),
        name="dsa_causal" if causal else "dsa_cached",
    )(q, qi, wi, k, v, ki, *args)


def _pack_a_w_in(w):
    D = w.shape[0]
    o_ki = A_Q + 2 * A_KV + A_QI
    pad = lambda n: jnp.zeros((D, n), w.dtype)
    return jnp.concatenate(
        [w[:, :o_ki], w[:, o_ki:o_ki + IDX_DIM], pad(LANES - IDX_DIM),
         w[:, o_ki + IDX_DIM:], pad(LANES - IDX_HEADS)], axis=1).astype(BF16)


def _mixer_a(xp, xs, g, w_in, w_out, qg, kg, caches, li):
    B, S, D = xp.shape
    Bs, T, _ = xs.shape
    past = caches[0].shape[2]
    w_in = _pack_a_w_in(w_in)
    w_out = w_out.astype(BF16)
    g, qg, kg = g[None], qg[None], kg[None]

    def project(x, pos):
        b, t, _ = x.shape
        h = _rms_matmul(x.reshape(b * t, D), g, w_in, tn=A_IN_PAD // 5)
        tabs = _rope_tables(pos, HEAD_DIM) + _rope_tables(pos, IDX_DIM)
        q, kf, vf, *rest = _a_post(h, tabs, qg, kg)
        heads = (b, t, A_KV_HEADS, HEAD_DIM)
        return [q.reshape(b, t, -1), kf.reshape(heads), vf.reshape(heads)] + [
            o.reshape(b, t, -1) for o in rest]

    q, k_p, v_p, kb, vb, qi, kif, kib, wi = project(xp, jnp.arange(S))
    tk = _pick(S, 512, LANES)
    o = _dsa(q, qi, wi, kb, vb, kib, tq=min(128, S), tk=tk, ksel=min(TOPK_MAX, S // 4),
             causal=True, l_valid=S)
    yp = _matmul_res(o.reshape(B * S, A_Q), w_out, xp.reshape(B * S, D)).reshape(B, S, D)
    ki_p = kif[..., :IDX_DIM]

    pos_s = jnp.tile(past + jnp.arange(T), Bs)
    q, k_s, v_s, kb, vb, qi, kif, kib, wi = project(xs, pos_s)
    L = past + T
    tk = _pick(past, 512, LANES)
    padk = lambda a: jnp.pad(a, ((0, 0), (0, tk - T), (0, 0)))
    o = _dsa(q, qi, wi, padk(kb), padk(vb), padk(kib), caches + (li,),
             tq=T, tk=tk, ksel=min(TOPK_MAX, L // 4), causal=False, l_valid=L)
    ys = _matmul_res(o.reshape(Bs * T, A_Q), w_out, xs.reshape(Bs * T, D)).reshape(Bs, T, D)
    return yp, ys, (k_p, v_p, ki_p, k_s, v_s, kif[..., :IDX_DIM])


def _c_post_kernel(h_ref, qg_ref, kg_ref, q_ref, kb_ref, kf_ref, vf_ref):
    W = C_HEADS * HEAD_DIM
    for h in range(C_HEADS):
        sl = slice(h * LANES, (h + 1) * LANES)
        q = _rms(h_ref[:, sl].astype(F32), qg_ref[...])
        q_ref[:, sl] = (q * (HEAD_DIM ** -0.5)).astype(BF16)
        k = _rms(h_ref[:, W + h * LANES:W + (h + 1) * LANES].astype(F32), kg_ref[...])
        kf_ref[:, sl] = k
        kb_ref[:, sl] = k.astype(BF16)
    vf_ref[...] = h_ref[:, 2 * W:3 * W].astype(F32)


def _c_post(h, qg, kg, *, t, keep, tm=512):
    M = h.shape[0]
    W = C_HEADS * HEAD_DIM
    tm = _pick(math.gcd(t, keep), tm, 16) if keep < t else _pick(M, tm, 16)
    row = lambda i: (i, 0)
    if keep < t:
        tps, kps = t // tm, keep // tm
        tail = lambda i: ((i // tps) * kps + jnp.maximum(i % tps - (tps - kps), 0), 0)
    else:
        tail = row
    n_tail = M // t * keep
    return pl.pallas_call(
        _c_post_kernel,
        grid=(M // tm,),
        in_specs=[pl.BlockSpec((tm, 3 * W), row),
                  pl.BlockSpec((1, LANES), lambda i: (0, 0)),
                  pl.BlockSpec((1, LANES), lambda i: (0, 0))],
        out_specs=[pl.BlockSpec((tm, W), row)] * 2 + [pl.BlockSpec((tm, W), tail)] * 2,
        out_shape=[jax.ShapeDtypeStruct((M, W), BF16)] * 2
        + [jax.ShapeDtypeStruct((n_tail, W), F32)] * 2,
        compiler_params=_cparams("arbitrary"),
        name="c_post",
    )(h, qg, kg)


def _band_kernel(q_ref, k_ref, v_ref, b_ref, o_ref, *, sub, nsub, win, first_key):
    t0 = pl.program_id(2) * (sub * nsub)
    starts = [pl.multiple_of(t0 + s * sub, sub) for s in range(nsub)]
    lgs = [_dot_nt(q_ref[0, s * sub:(s + 1) * sub, :], k_ref[0, pl.ds(starts[s], win), :])
           + b_ref[0] for s in range(nsub)]
    kk = lax.broadcasted_iota(I32, (sub, win), 1)
    lgs = [jnp.where(kk >= first_key - starts[s], lgs[s], NEG_INF) for s in range(nsub)]
    ms = [jnp.max(lg, axis=-1, keepdims=True) for lg in lgs]
    ps = [jnp.exp(lg - m) for lg, m in zip(lgs, ms)]
    ls = [jnp.sum(p, axis=-1, keepdims=True) for p in ps]
    os_ = [_dot(ps[s].astype(BF16), v_ref[0, pl.ds(starts[s], win), :]) for s in range(nsub)]
    for s in range(nsub):
        o_ref[0, s * sub:(s + 1) * sub, :] = (os_[s] / ls[s]).astype(BF16)


def _band(q, k, v, bias, *, sub, nsub, first_key):
    B, T, W = q.shape
    Lk = k.shape[1]
    win = bias.shape[2]
    tq = sub * nsub
    kern = functools.partial(_band_kernel, sub=sub, nsub=nsub, win=win, first_key=first_key)
    return pl.pallas_call(
        kern,
        grid=(B, C_HEADS, T // tq),
        in_specs=[pl.BlockSpec((1, tq, LANES), lambda b, h, i: (b, i, h)),
                  pl.BlockSpec((1, Lk, LANES), lambda b, h, i: (b, 0, h)),
                  pl.BlockSpec((1, Lk, LANES), lambda b, h, i: (b, 0, h)),
                  pl.BlockSpec((1, sub, win), lambda b, h, i: (h, 0, 0))],
        out_specs=pl.BlockSpec((1, tq, LANES), lambda b, h, i: (b, i, h)),
        out_shape=jax.ShapeDtypeStruct((B, T, W), BF16),
        compiler_params=_cparams("parallel", "parallel", "arbitrary"),
        name="band_attn",
    )(q, k, v, bias)


def _band_cached_kernel(q_ref, k_ref, v_ref, ck_ref, cv_ref, b_ref, o_ref):
    T = q_ref.shape[1]
    cp = ck_ref.shape[1]
    win = b_ref.shape[2]
    pad = jnp.zeros((win - cp - T, LANES), BF16)

    def window(c_ref, n_ref, h):
        new = n_ref[0, :, h * LANES:(h + 1) * LANES]
        return jnp.concatenate([c_ref[0, :, h, :].astype(BF16), new, pad], axis=0)

    heads = range(C_HEADS)
    lgs = [_dot_nt(q_ref[0, :, h * LANES:(h + 1) * LANES], window(ck_ref, k_ref, h)) + b_ref[h]
           for h in heads]
    ms = [jnp.max(lg, axis=-1, keepdims=True) for lg in lgs]
    ps = [jnp.exp(lg - m) for lg, m in zip(lgs, ms)]
    ls = [jnp.sum(p, axis=-1, keepdims=True) for p in ps]
    for h in heads:
        o = _dot(ps[h].astype(BF16), window(cv_ref, v_ref, h)) / ls[h]
        o_ref[0, :, h * LANES:(h + 1) * LANES] = o.astype(BF16)


def _band_cached(q, k, v, ck, cv, bias):
    B, T, W = q.shape
    cp = ck.shape[1]
    row = pl.BlockSpec((1, T, W), lambda b: (b, 0, 0))
    cache = pl.BlockSpec((1, cp, C_HEADS, HEAD_DIM), lambda b: (b, 0, 0, 0))
    return pl.pallas_call(
        _band_cached_kernel,
        grid=(B,),
        in_specs=[row, row, row, cache, cache,
                  pl.BlockSpec(bias.shape, lambda b: (0, 0, 0))],
        out_specs=row,
        out_shape=jax.ShapeDtypeStruct((B, T, W), BF16),
        compiler_params=_cparams("parallel"),
        name="band_cached",
    )(q, k, v, ck, cv, bias)


def _mixer_c(xp, xs, g, w_in, w_out, qg, kg, rel, ck, cv):
    B, S, D = xp.shape
    Bs, T, _ = xs.shape
    W = C_HEADS * HEAD_DIM
    cp = ck.shape[1]
    w_in = w_in.astype(BF16)
    w_out = w_out.astype(BF16)
    g, qg, kg = g[None], qg[None], kg[None]
    relf = rel.astype(F32)

    def project(x, keep):
        b, t, _ = x.shape
        h = _rms_matmul(x.reshape(b * t, D), g, w_in, out_dtype=BF16)
        q, kb, kf, vf = _c_post(h, qg, kg, t=t, keep=keep)
        vb = h.reshape(b, t, 3 * W)[:, :, 2 * W:]
        tail = lambda a: a.reshape(b, keep, C_HEADS, HEAD_DIM)
        return q.reshape(b, t, W), tail(kf), kb.reshape(b, t, W), vb, tail(vf)

    def rel_bias(rows, win, shift):
        n = rows + win - 1
        j = jnp.arange(n)
        vec = relf[:, jnp.clip(shift + rows - 1 - j, -REL_CLIP, REL_CLIP) + REL_CLIP]
        flat = jnp.tile(jnp.pad(vec, ((0, 0), (0, 1))), (1, rows))
        toep = flat[:, :rows * n].reshape(-1, rows, n)
        return toep[:, :, rows - 1:rows - 1 + win]

    sub = min(2 * CHUNK, S)
    win = C_REACH + sub
    r = jnp.arange(sub)[:, None]
    kk = jnp.arange(win)[None, :]
    lo = (r // CHUNK) * CHUNK
    bias = jnp.where((kk >= lo) & (kk < lo + C_REACH + CHUNK), rel_bias(sub, win, C_REACH),
                     NEG_INF)
    q, k_p, kb, vb, v_p = project(xp, min(C_REACH, S))
    padk = lambda a: jnp.pad(a, ((0, 0), (C_REACH, 0), (0, 0)))
    o = _band(q, padk(kb), padk(vb), bias, sub=sub, nsub=_pick(S // sub, 8, 1),
              first_key=C_REACH)
    yp = _matmul_res(o.reshape(B * S, W), w_out, xp.reshape(B * S, D)).reshape(B, S, D)

    L = cp + T
    win = -(-L // LANES) * LANES
    bias = jnp.where(jnp.arange(win)[None, :] < L, rel_bias(T, win, cp), NEG_INF)
    q, kf, kb, vb, vf = project(xs, T)
    o = _band_cached(q, kb, vb, ck, cv, bias)
    ys = _matmul_res(o.reshape(Bs * T, W), w_out, xs.reshape(Bs * T, D)).reshape(Bs, T, D)
    return yp, ys, (k_p, v_p, kf, vf)


def _s5_ops_kernel(ar_ref, ai_ref, ldt_ref, br_ref, bi_ref, cr_ref, ci_ref, cr4_ref, ci4_ref,
                   lag_ref, xr_ref, xi_ref, or_ref, oi_ref, lnr_ref, lni_ref, sr_ref, si_ref):
    n, P = xr_ref.shape[:2]
    dt = jnp.exp(ldt_ref[...])
    ar, ai = ar_ref[...], ai_ref[...]
    mag = jnp.exp(ar * dt)
    lr, li = mag * jnp.cos(ai * dt), mag * jnp.sin(ai * dt)
    den = ar * ar + ai * ai
    nr = lr - 1.0
    fr = (nr * ar + li * ai) / den
    fi = (li * ar - nr * ai) / den
    br, bi = br_ref[...], bi_ref[...]
    bbr = fr * br - fi * bi
    bbi = fr * bi + fi * br
    cr, ci = cr_ref[...], ci_ref[...]
    pr, pi = jnp.ones_like(lr), jnp.zeros_like(lr)
    for tau in range(n):
        sr_ref[...] = pr * bbr - pi * bbi
        si_ref[...] = pr * bbi + pi * bbr
        xr_ref[tau] = sr_ref[...].astype(BF16)
        xi_ref[tau] = si_ref[...].astype(BF16)

        def body(p, acc):
            return acc + cr4_ref[p] * sr_ref[p][None] - ci4_ref[p] * si_ref[p][None]

        lag_ref[tau] = lax.fori_loop(0, P, body, jnp.zeros(lag_ref.shape[1:], F32))
        pr, pi = pr * lr - pi * li, pr * li + pi * lr
        or_ref[tau] = (cr * pr - ci * pi).astype(BF16)
        oi_ref[tau] = (-(cr * pi + ci * pr)).astype(BF16)
    lnr_ref[...] = pr
    lni_ref[...] = pi


def _s5_operators(a_re, a_im, log_dt, b_re, b_im, c_re, c_im):
    G, P = a_re.shape
    C = S5_GROUP
    n = S5_CHUNK
    nlb = G // S5_LB_GROUPS
    f = lambda a: a.astype(F32)
    a_t = lambda a: f(a).T.reshape(P, 1, G)
    b_t = lambda a: jnp.transpose(f(a), (1, 2, 0))
    c_t = lambda a: jnp.transpose(f(a), (2, 1, 0))
    crt, cit = c_t(c_re), c_t(c_im)
    sds = lambda *shape: jax.ShapeDtypeStruct(shape, F32)
    lag, xr, xi, o_re, o_im, lnr, lni = pl.pallas_call(
        _s5_ops_kernel,
        out_shape=[sds(n, C, C, G)] + [jax.ShapeDtypeStruct((n, P, C, G), BF16)] * 4
        + [sds(P, 1, G)] * 2,
        scratch_shapes=[pltpu.VMEM((P, C, G), F32)] * 2,
        compiler_params=pltpu.CompilerParams(vmem_limit_bytes=VMEM_LIMIT),
        name="s5_ops",
    )(a_t(a_re), a_t(a_im), f(log_dt).reshape(1, 1, G), b_t(b_re), b_t(b_im), crt, cit,
      crt.reshape(P, C, 1, G), cit.reshape(P, C, 1, G))

    split = lambda a: a.reshape(a.shape[:-1] + (nlb, S5_LB_GROUPS))
    taps = jnp.transpose(split(lag), (3, 0, 4, 2, 1)).reshape(nlb, n, LANES, C)
    taps = jnp.pad(taps.astype(BF16), ((0, 0), (0, 0), (0, 0), (0, LANES - C)))

    def inc(x):
        w = jnp.transpose(split(x[::-1]), (3, 4, 0, 2, 1))
        return jnp.concatenate([w, w], axis=-1)

    def out(x):
        return jnp.transpose(split(x), (3, 4, 1, 0, 2)).reshape(nlb, S5_LB_GROUPS, P, n * C)

    lam_n = tuple(a.reshape(P, G).T.reshape(1, 1, G * P) for a in (lnr, lni))
    return taps, inc(xr), inc(xi), out(o_re), out(o_im), lam_n


def _chunk_rows(u_ref, tc):
    return jnp.concatenate(
        [u_ref[pl.ds(t, tc, stride=S5_CHUNK), :] for t in range(S5_CHUNK)], axis=1)


def _s5_inc_kernel(u_ref, xr_ref, xi_ref, sr_ref, si_ref, wr_ref, wi_ref):
    n, C = S5_CHUNK, S5_GROUP
    P = wr_ref.shape[1] // S5_LB_GROUPS

    @pl.when(pl.program_id(1) == 0)
    def _():
        for x_ref, w_ref in ((xr_ref, wr_ref), (xi_ref, wi_ref)):
            w_ref[...] = jnp.zeros(w_ref.shape, BF16)
            for g in range(S5_LB_GROUPS):
                half = (g % 2) * P
                for t in range(n):
                    r0 = t * LANES + g * C
                    w_ref[r0:r0 + C, g * P:(g + 1) * P] = x_ref[0, g, t, :, half:half + P]

    u = _chunk_rows(u_ref, sr_ref.shape[0]).astype(BF16)
    sr_ref[...] = _dot(u, wr_ref[...])
    si_ref[...] = _dot(u, wi_ref[...])


def _s5_inc(u, x_re, x_im, *, tc):
    Nc = u.shape[0] // S5_CHUNK
    nlb = x_re.shape[0]
    K = S5_CHUNK * LANES
    Wn = S5_LB_GROUPS * (x_re.shape[-1] // 2)
    xspec = pl.BlockSpec((1,) + x_re.shape[1:], lambda lb, i: (lb, 0, 0, 0, 0))
    ospec = pl.BlockSpec((tc, Wn), lambda lb, i: (i, lb))
    return pl.pallas_call(
        _s5_inc_kernel,
        grid=(nlb, Nc // tc),
        in_specs=[pl.BlockSpec((tc * S5_CHUNK, LANES), lambda lb, i: (i, lb)), xspec, xspec],
        out_specs=[ospec, ospec],
        out_shape=[jax.ShapeDtypeStruct((Nc, nlb * Wn), F32)] * 2,
        scratch_shapes=[pltpu.VMEM((K, Wn), BF16)] * 2,
        compiler_params=_cparams("parallel", "arbitrary"),
        name="s5_inc",
    )(u, x_re, x_im)


def _s5_scan_kernel(sr_ref, si_ref, lr_ref, li_ref, h0r_ref, h0i_ref,
                    pr_ref, pi_ref, fr_ref, fi_ref):
    lr, li = lr_ref[...], li_ref[...]
    nck = sr_ref.shape[1]

    def body(k, carry):
        hr, hi = carry
        pr_ref[:, pl.ds(k, 1), :] = hr
        pi_ref[:, pl.ds(k, 1), :] = hi
        sr = sr_ref[:, pl.ds(k, 1), :]
        si = si_ref[:, pl.ds(k, 1), :]
        return lr * hr - li * hi + sr, lr * hi + li * hr + si

    hr, hi = lax.fori_loop(0, nck, body, (h0r_ref[...], h0i_ref[...]))
    fr_ref[...] = hr
    fi_ref[...] = hi


def _s5_scan(sr, si, lam, h0r, h0i, *, cb=1024):
    B, nck, GP = sr.shape
    cb = _pick(GP, cb, LANES)
    full = pl.BlockSpec((B, nck, cb), lambda j: (0, 0, j))
    vec = pl.BlockSpec((1, 1, cb), lambda j: (0, 0, j))
    st = pl.BlockSpec((B, 1, cb), lambda j: (0, 0, j))
    return pl.pallas_call(
        _s5_scan_kernel,
        grid=(GP // cb,),
        in_specs=[full, full, vec, vec, st, st],
        out_specs=[full, full, st, st],
        out_shape=[jax.ShapeDtypeStruct((B, nck, GP), F32)] * 2
        + [jax.ShapeDtypeStruct((B, 1, GP), F32)] * 2,
        compiler_params=_cparams("parallel"),
        name="s5_scan",
    )(sr, si, lam[0], lam[1], h0r, h0i)


def _s5_out_kernel(u_ref, taps_ref, hr_ref, hi_ref, ar_ref, ai_ref, d_ref, o_ref,
                   toep_ref, wr_ref, wi_ref):
    n, C = S5_CHUNK, S5_GROUP
    tc = hr_ref.shape[0]
    P = ar_ref.shape[2]

    @pl.when(pl.program_id(1) == 0)
    def _():
        row = lax.broadcasted_iota(I32, (LANES, LANES), 0)
        lane = lax.broadcasted_iota(I32, (LANES, LANES), 1)
        spread = jnp.where(lane % C == row, 1.0, 0.0).astype(BF16)
        same_group = row // C == lane // C
        zero = jnp.zeros((LANES, LANES), BF16)
        for tau in range(n):
            tap = jnp.where(same_group, _dot(taps_ref[0, tau], spread), 0.0).astype(BF16)
            for t in range(n - tau):
                v = t + tau
                toep_ref[t * LANES:(t + 1) * LANES, v * LANES:(v + 1) * LANES] = tap
            for t in range(tau):
                toep_ref[tau * LANES:(tau + 1) * LANES, t * LANES:(t + 1) * LANES] = zero
        r2 = lax.broadcasted_iota(I32, (n * C, n * LANES), 0)
        c2 = lax.broadcasted_iota(I32, (n * C, n * LANES), 1)
        for g in range(S5_LB_GROUPS):
            place = jnp.where(c2 == (r2 // C) * LANES + g * C + r2 % C, 1.0, 0.0).astype(BF16)
            wr_ref[g * P:(g + 1) * P, :] = _dot(ar_ref[0, g], place).astype(BF16)
            wi_ref[g * P:(g + 1) * P, :] = _dot(ai_ref[0, g], place).astype(BF16)

    u = _chunk_rows(u_ref, tc)
    y = _dot(u.astype(BF16), toep_ref[...])
    y = y + _dot(hr_ref[...].astype(BF16), wr_ref[...]) + _dot(hi_ref[...].astype(BF16), wi_ref[...])
    y = jax.nn.gelu(y + jnp.tile(d_ref[...], (1, n)) * u)
    for t in range(n):
        o_ref[pl.ds(t, tc, stride=n), :] = y[:, t * LANES:(t + 1) * LANES]


def _s5_out(u, taps, hr, hi, a_re, a_im, d, *, tc):
    M, D = u.shape
    n = S5_CHUNK
    nlb, ng, P, _ = a_re.shape
    Wn, K = ng * P, n * LANES
    hspec = pl.BlockSpec((tc, Wn), lambda lb, i: (i, lb))
    uspec = pl.BlockSpec((tc * n, LANES), lambda lb, i: (i, lb))
    aspec = pl.BlockSpec((1,) + a_re.shape[1:], lambda lb, i: (lb, 0, 0, 0))
    return pl.pallas_call(
        _s5_out_kernel,
        grid=(nlb, M // (tc * n)),
        in_specs=[uspec,
                  pl.BlockSpec((1, n, LANES, LANES), lambda lb, i: (lb, 0, 0, 0)), hspec, hspec,
                  aspec, aspec,
                  pl.BlockSpec((1, LANES), lambda lb, i: (0, lb))],
        out_specs=uspec,
        out_shape=jax.ShapeDtypeStruct((M, D), F32),
        scratch_shapes=[pltpu.VMEM((K, K), BF16), pltpu.VMEM((Wn, K), BF16),
                        pltpu.VMEM((Wn, K), BF16)],
        compiler_params=_cparams("parallel", "arbitrary"),
        name="s5_out",
    )(u, taps, hr, hi, a_re, a_im, d)


def _glu_res_kernel(x_ref, wa_ref, wb_ref, r_ref, o_ref, xc_ref):
    @pl.when(pl.program_id(1) == 0)
    def _():
        xc_ref[...] = x_ref[...].astype(BF16)

    x = xc_ref[...]
    a = _dot(x, wa_ref[...])
    b = _dot(x, wb_ref[...])
    o_ref[...] = r_ref[...] + a * jax.nn.sigmoid(b)


def _glu_res(x, w, res, *, tm=1024, tn=512):
    M, D = x.shape
    N = w.shape[1] // 2
    tm = _pick(M, tm, 16)
    tn = _pick(N, tn, LANES)
    nj = N // tn
    return pl.pallas_call(
        _glu_res_kernel,
        grid=(M // tm, nj),
        in_specs=[pl.BlockSpec((tm, D), lambda i, j: (i, 0)),
                  pl.BlockSpec((D, tn), lambda i, j: (0, j)),
                  pl.BlockSpec((D, tn), lambda i, j: (0, nj + j)),
                  pl.BlockSpec((tm, tn), lambda i, j: (i, j))],
        out_specs=pl.BlockSpec((tm, tn), lambda i, j: (i, j)),
        out_shape=jax.ShapeDtypeStruct((M, N), F32),
        scratch_shapes=[pltpu.VMEM((tm, D), BF16)],
        compiler_params=_cparams("parallel", "arbitrary"),
        name="glu_res",
    )(x, w, w, res)


def _mixer_b(xp, xs, g, ops, d, w_glu, h0r, h0i):
    taps, inc_re, inc_im, out_re, out_im, lam = ops
    n = S5_CHUNK
    GP = lam[0].shape[-1]
    g, d = g[None], d.astype(F32)[None]
    w_glu = w_glu.astype(BF16)

    def stream(x, hr0, hi0):
        B, T, D = x.shape
        nck = T // n
        Nc = B * nck
        x2 = x.reshape(B * T, D)
        u = _rmsnorm(x2, g)
        tc = _pick(Nc, 256, 8)
        sr, si = _s5_inc(u, inc_re, inc_im, tc=tc)
        pr, pi, fr, fi = _s5_scan(sr.reshape(B, nck, GP), si.reshape(B, nck, GP), lam, hr0, hi0)
        gy = _s5_out(u, taps, pr.reshape(Nc, GP), pi.reshape(Nc, GP), out_re, out_im, d, tc=tc)
        y = _glu_res(gy, w_glu, x2)
        shp = (B, GP // S5_STATE, S5_STATE)
        return y.reshape(B, T, D), fr.reshape(shp), fi.reshape(shp)

    B = xp.shape[0]
    zero = jnp.zeros((B, 1, GP), F32)
    yp, rp, ip = stream(xp, zero, zero)
    Bs = xs.shape[0]
    ys, rs, is_ = stream(xs, h0r.astype(F32).reshape(Bs, 1, GP), h0i.astype(F32).reshape(Bs, 1, GP))
    return yp, ys, (rp, ip, rs, is_)


FFN_HALO = 16
FFN_TAIL = 8


def _ffn_kernel(*refs, seq_tiles, seq_len, halo):
    if halo:
        (x_ref, xh_ref, g_ref, wa_ref, wv_ref, cwa_ref, cwv_ref, cba_ref, cbv_ref, wd_ref,
         o_ref, ta_ref, tv_ref, xn_ref, xhn_ref) = refs
    else:
        (x_ref, p1a_ref, p1v_ref, p2a_ref, p2v_ref, g_ref, wa_ref, wv_ref, cwa_ref, cwv_ref,
         cba_ref, cbv_ref, wd_ref, o_ref, ta_ref, tv_ref, xn_ref, h_ref) = refs
    i, f = pl.program_id(0), pl.program_id(1)
    tm = x_ref.shape[0]

    @pl.when(f == 0)
    def _():
        x = x_ref[...]
        xn_ref[...] = _rms(x, g_ref[...]).astype(BF16)
        o_ref[...] = x
        if halo:
            xhn_ref[...] = _rms(xh_ref[...], g_ref[...]).astype(BF16)

    xn = xn_ref[...]
    tf = wa_ref.shape[1]
    r = lax.broadcasted_iota(I32, (tm, tf), 0)

    def conv(w_ref, cw_ref, cb_ref, p1_ref, p2_ref, t_ref):
        h = _dot(xn, w_ref[...])
        h1 = pltpu.roll(h, 1, axis=0)
        h2 = pltpu.roll(h, 2, axis=0)
        if halo:
            t_ref[...] = h[tm - FFN_TAIL:, :]
            hh = _dot(xhn_ref[...], w_ref[...])
            hh = hh * jnp.where(i % seq_tiles == 0, 0.0, 1.0)
            last, prev = hh[-1:, :], hh[-2:-1, :]
            h1 = jnp.where(r == 0, last, h1)
            h2 = jnp.where(r == 0, prev, jnp.where(r == 1, last, h2))
        else:
            for s in range(tf // LANES):
                sl = slice(s * LANES, (s + 1) * LANES)
                h_ref[s] = h[:, sl]
                for k in range(CONV_W - 1):
                    rows = pl.ds(seq_len - (CONV_W - 1) + k, tm // seq_len, stride=seq_len)
                    t_ref[k, :, sl] = h_ref.at[s][rows, :]
            t = r % seq_len
            h1 = jnp.where(t >= 1, h1, p1_ref[...])
            h2 = jnp.where(t >= 2, h2, p2_ref[...])
        cw = cw_ref[...]
        return cb_ref[...] + cw[0:1] * h2 + cw[1:2] * h1 + cw[2:3] * h

    if halo:
        ca = conv(wa_ref, cwa_ref, cba_ref, None, None, ta_ref)
        cv = conv(wv_ref, cwv_ref, cbv_ref, None, None, tv_ref)
    else:
        ca = conv(wa_ref, cwa_ref, cba_ref, p1a_ref, p2a_ref, ta_ref)
        cv = conv(wv_ref, cwv_ref, cbv_ref, p1v_ref, p2v_ref, tv_ref)
    act = (jax.nn.silu(ca) * cv).astype(BF16)
    o_ref[...] += _dot(act, wd_ref[...])


def _ffn(x, g, w_up, conv_w, conv_b, w_down, *, layer, seq_len, hist=None, tm=512, tf=1024):
    M, D = x.shape
    F = w_down.shape[1]
    halo = hist is None
    n_seq = M // seq_len
    tm = _pick(seq_len, tm, FFN_HALO) if halo else M
    tf = _pick(F, tf, LANES)
    nf = F // tf
    xs = pl.BlockSpec((tm, D), lambda i, f: (i, 0))
    a_col = lambda i, f: (layer, 0, f)
    v_col = lambda i, f: (layer, 0, nf + f)
    wspecs = [pl.BlockSpec((1, D), lambda i, f: (0, 0)),
              pl.BlockSpec((None, D, tf), a_col), pl.BlockSpec((None, D, tf), v_col),
              pl.BlockSpec((None, CONV_W, tf), a_col), pl.BlockSpec((None, CONV_W, tf), v_col),
              pl.BlockSpec((None, 1, tf), a_col), pl.BlockSpec((None, 1, tf), v_col),
              pl.BlockSpec((None, tf, D), lambda i, f: (layer, f, 0))]
    wargs = [g, w_up, w_up, conv_w, conv_w, conv_b, conv_b, w_down]
    scratch = [pltpu.VMEM((tm, D), BF16)]
    seq_tiles = max(seq_len // tm, 1)
    if halo:
        per = tm // FFN_HALO
        extra = [pl.BlockSpec((FFN_HALO, D), lambda i, f: (jnp.maximum(i * per - 1, 0), 0))]
        eargs = [x]
        scratch.append(pltpu.VMEM((FFN_HALO, D), BF16))
        tspec = pl.BlockSpec((FFN_TAIL, tf), lambda i, f: (i // seq_tiles, f))
        tshape = jax.ShapeDtypeStruct((n_seq * FFN_TAIL, F), F32)
    else:
        ha = pl.BlockSpec((tm, tf), lambda i, f: (i, f))
        hv = pl.BlockSpec((tm, tf), lambda i, f: (i, nf + f))
        extra = [ha, hv, ha, hv]
        eargs = [hist[0], hist[0], hist[1], hist[1]]
        scratch.append(pltpu.VMEM((tf // LANES, tm, LANES), F32))
        tspec = pl.BlockSpec((CONV_W - 1, n_seq, tf), lambda i, f: (0, 0, f))
        tshape = jax.ShapeDtypeStruct((CONV_W - 1, n_seq, F), F32)
    kern = functools.partial(_ffn_kernel, seq_tiles=seq_tiles, seq_len=seq_len, halo=halo)
    y, ta, tv = pl.pallas_call(
        kern,
        grid=(M // tm, nf),
        in_specs=[xs] + extra + wspecs,
        out_specs=[pl.BlockSpec((tm, D), lambda i, f: (i, 0)), tspec, tspec],
        out_shape=[jax.ShapeDtypeStruct((M, D), F32), tshape, tshape],
        scratch_shapes=scratch,
        compiler_params=_cparams("arbitrary", "arbitrary"),
        name="conv_ffn" if halo else "conv_ffn_hist",
    )(x, *eargs, *wargs)
    if halo:
        tail = lambda t: t.reshape(n_seq, FFN_TAIL, F)[:, FFN_TAIL - (CONV_W - 1):]
    else:
        tail = lambda t: jnp.swapaxes(t, 0, 1)
    return y, jnp.concatenate([tail(ta), tail(tv)], axis=-1)


def _conv_ffn(xp, xs, g, w_up, conv_w, conv_b, w_down, hist_s, layer):
    B, S, D = xp.shape
    Bs, T, _ = xs.shape
    F2 = w_up.shape[2]
    g = g[None]
    yp, hp = _ffn(xp.reshape(B * S, D), g, w_up, conv_w, conv_b, w_down, layer=layer, seq_len=S)
    hs = hist_s.astype(F32)
    t = jnp.arange(T)[None, :, None]
    p1 = jnp.where(t == 0, hs[:, -1:], 0.0).reshape(Bs * T, F2)
    p2 = jnp.where(t == 0, hs[:, -2:-1], jnp.where(t == 1, hs[:, -1:], 0.0)).reshape(Bs * T, F2)
    ys, hs_new = _ffn(xs.reshape(Bs * T, D), g, w_up, conv_w, conv_b, w_down, layer=layer,
                      seq_len=T, hist=(p1, p2))
    return yp.reshape(B, S, D), ys.reshape(Bs, T, D), hp, hs_new


def kernel(x_prompt, x_sample, cache_a_k, cache_a_v, cache_a_kidx, cache_c_k, cache_c_v, state_b_re, state_b_im, state_ffn_conv, norm_mix, norm_ffn, a_w_in, a_w_out, a_q_norm, a_k_norm, b_a_re, b_a_im, b_log_dt, b_b_re, b_b_im, b_c_re, b_c_im, b_d, b_w_glu, c_w_in, c_w_out, c_q_norm, c_k_norm, c_rel_bias, ffn_w_up, ffn_conv_w, ffn_conv_b, ffn_w_down):
    xp, xs = x_prompt, x_sample
    depth = norm_mix.shape[0]
    outs_a, outs_b, outs_c, f_p, f_s = [], [], [], [], []
    ffn_w_up, ffn_w_down = ffn_w_up.astype(BF16), ffn_w_down.astype(BF16)
    ffn_conv_b = ffn_conv_b[:, None]
    for i in range(depth):
        li, kind = i // 3, i % 3
        if kind == 0:
            xp, xs, o = _mixer_a(xp, xs, norm_mix[i], a_w_in[li], a_w_out[li], a_q_norm[li],
                                 a_k_norm[li], (cache_a_k, cache_a_v, cache_a_kidx), li)
            outs_a.append(o)
        elif kind == 1:
            ops = _s5_operators(b_a_re[li], b_a_im[li], b_log_dt[li], b_b_re[li], b_b_im[li],
                                b_c_re[li], b_c_im[li])
            xp, xs, o = _mixer_b(xp, xs, norm_mix[i], ops, b_d[li], b_w_glu[li],
                                 state_b_re[li], state_b_im[li])
            outs_b.append(o)
        else:
            xp, xs, o = _mixer_c(xp, xs, norm_mix[i], c_w_in[li], c_w_out[li], c_q_norm[li],
                                 c_k_norm[li], c_rel_bias[li], cache_c_k[li], cache_c_v[li])
            outs_c.append(o)
        xp, xs, hp, hs = _conv_ffn(xp, xs, norm_ffn[i], ffn_w_up, ffn_conv_w, ffn_conv_b,
                                   ffn_w_down, state_ffn_conv[i], i)
        f_p.append(hp)
        f_s.append(hs)
    stack = lambda outs, j: jnp.stack([o[j] for o in outs])
    return (xp, xs,
            *[stack(outs_a, j) for j in range(6)],
            *[stack(outs_c, j) for j in range(4)],
            *[stack(outs_b, j) for j in range(4)],
            jnp.stack(f_p), jnp.stack(f_s))
```

```python
import functools
import math

import jax
import jax.numpy as jnp
from jax import lax
from jax.experimental import pallas as pl
from jax.experimental.pallas import tpu as pltpu

F32 = jnp.float32
BF16 = jnp.bfloat16
I32 = jnp.int32

EPS = 1e-6
NEG_INF = -1e30
ROW_MAX_INIT = -1e29
THR_BITS_PER_CHECK = 4
ROPE_THETA = 10000.0
CHUNK = 64
HEAD_DIM = 128
LANES = 128

A_HEADS = 16
A_KV_HEADS = 4
A_GROUP = A_HEADS // A_KV_HEADS
IDX_HEADS = 8
IDX_DIM = 64
TOPK_MAX = 256
A_Q = A_HEADS * HEAD_DIM
A_KV = A_KV_HEADS * HEAD_DIM
A_QI = IDX_HEADS * IDX_DIM
IDX_W_SCALE = (IDX_HEADS * IDX_DIM) ** -0.5
A_OFF_K = A_Q
A_OFF_V = A_OFF_K + A_KV
A_OFF_QI = A_OFF_V + A_KV
A_OFF_KI = A_OFF_QI + A_QI
A_OFF_WI = A_OFF_KI + LANES
A_IN_PAD = A_OFF_WI + LANES

S5_GROUP = 16
S5_STATE = 64
S5_CHUNK = 16
S5_LB_GROUPS = LANES // S5_GROUP

C_HEADS = 16
C_BAND_CHUNKS = 8
C_REACH = C_BAND_CHUNKS * CHUNK
REL_CLIP = 128
CONV_W = 3

VMEM_LIMIT = 56 * 1024 * 1024


def _cparams(*sem):
    return pltpu.CompilerParams(dimension_semantics=sem, vmem_limit_bytes=VMEM_LIMIT)


def _pick(n, target, mult):
    best = None
    for t in range(mult, min(n, target) + 1, mult):
        if n % t == 0:
            best = t
    return best if best is not None else n


def _dot(a, b):
    return jnp.dot(a, b, preferred_element_type=F32)


def _dot_nt(a, b):
    return lax.dot_general(a, b, (((1,), (1,)), ((), ())), preferred_element_type=F32)


def _rms(x, g):
    return x * lax.rsqrt(jnp.mean(x * x, axis=-1, keepdims=True) + EPS) * g


def _rms_matmul_kernel(x_ref, g_ref, w_ref, o_ref, xn_ref):
    @pl.when(pl.program_id(1) == 0)
    def _():
        xn_ref[...] = _rms(x_ref[...], g_ref[...]).astype(BF16)

    o_ref[...] = _dot(xn_ref[...], w_ref[...]).astype(o_ref.dtype)


def _rms_matmul(x, g, w, *, tm=1024, tn=1024, out_dtype=F32):
    M, D = x.shape
    N = w.shape[1]
    tm = _pick(M, tm, 16)
    tn = _pick(N, tn, LANES)
    return pl.pallas_call(
        _rms_matmul_kernel,
        grid=(M // tm, N // tn),
        in_specs=[pl.BlockSpec((tm, D), lambda i, j: (i, 0)),
                  pl.BlockSpec((1, D), lambda i, j: (0, 0)),
                  pl.BlockSpec((D, tn), lambda i, j: (0, j))],
        out_specs=pl.BlockSpec((tm, tn), lambda i, j: (i, j)),
        out_shape=jax.ShapeDtypeStruct((M, N), out_dtype),
        scratch_shapes=[pltpu.VMEM((tm, D), BF16)],
        compiler_params=_cparams("parallel", "arbitrary"),
        name="rms_matmul",
    )(x, g, w)


def _rmsnorm_kernel(x_ref, g_ref, o_ref):
    o_ref[...] = _rms(x_ref[...], g_ref[...])


def _rmsnorm(x, g, *, tm=512):
    M, D = x.shape
    tm = _pick(M, tm, 8)
    return pl.pallas_call(
        _rmsnorm_kernel,
        grid=(M // tm,),
        in_specs=[pl.BlockSpec((tm, D), lambda i: (i, 0)),
                  pl.BlockSpec((1, D), lambda i: (0, 0))],
        out_specs=pl.BlockSpec((tm, D), lambda i: (i, 0)),
        out_shape=jax.ShapeDtypeStruct((M, D), F32),
        compiler_params=_cparams("parallel"),
        name="rmsnorm",
    )(x, g)


def _matmul_res_kernel(x_ref, w_ref, r_ref, o_ref):
    o_ref[...] = r_ref[...] + _dot(x_ref[...], w_ref[...])


def _matmul_res(x, w, res, *, tm=512, tn=2048):
    M, K = x.shape
    N = w.shape[1]
    tm = _pick(M, tm, 16)
    tn = _pick(N, tn, LANES)
    return pl.pallas_call(
        _matmul_res_kernel,
        grid=(M // tm, N // tn),
        in_specs=[pl.BlockSpec((tm, K), lambda i, j: (i, 0)),
                  pl.BlockSpec((K, tn), lambda i, j: (0, j)),
                  pl.BlockSpec((tm, tn), lambda i, j: (i, j))],
        out_specs=pl.BlockSpec((tm, tn), lambda i, j: (i, j)),
        out_shape=jax.ShapeDtypeStruct((M, N), F32),
        compiler_params=_cparams("parallel", "arbitrary"),
        name="matmul_res",
    )(x, w, res)


def _rope_tables(pos, dim):
    half = dim // 2
    inv = jnp.power(ROPE_THETA, -jnp.arange(half, dtype=F32) / half)
    ang = pos.astype(F32)[:, None] * inv[None, :]
    cos, sin = jnp.cos(ang), jnp.sin(ang)
    reps = LANES // dim
    return (jnp.tile(jnp.concatenate([cos, cos], axis=-1), (1, reps)),
            jnp.tile(jnp.concatenate([-sin, sin], axis=-1), (1, reps)))


def _a_post_kernel(h_ref, c128_ref, s128_ref, c64_ref, s64_ref, qg_ref, kg_ref,
                   q_ref, kf_ref, vf_ref, kb_ref, vb_ref, qi_ref, kif_ref, kib_ref, wi_ref):
    c128, s128 = c128_ref[...], s128_ref[...]
    c64, s64 = c64_ref[...], s64_ref[...]
    tm = h_ref.shape[0]
    lane = lax.broadcasted_iota(I32, (tm, LANES), 1)

    def rope128(x):
        return x * c128 + pltpu.roll(x, HEAD_DIM // 2, axis=1) * s128

    def rope64(x):
        lo = (lane % IDX_DIM) < (IDX_DIM // 2)
        partner = jnp.where(lo, pltpu.roll(x, LANES - IDX_DIM // 2, axis=1),
                            pltpu.roll(x, IDX_DIM // 2, axis=1))
        return x * c64 + partner * s64

    for h in range(A_HEADS):
        x = h_ref[:, h * LANES:(h + 1) * LANES]
        q_ref[:, h * LANES:(h + 1) * LANES] = (
            rope128(_rms(x, qg_ref[...])) * (HEAD_DIM ** -0.5)).astype(BF16)
    for h in range(A_KV_HEADS):
        x = h_ref[:, A_OFF_K + h * LANES:A_OFF_K + (h + 1) * LANES]
        k = rope128(_rms(x, kg_ref[...]))
        kf_ref[:, h, :] = k
        kb_ref[:, h * LANES:(h + 1) * LANES] = k.astype(BF16)
    ones = jnp.ones((tm, LANES), BF16)
    for h in range(A_KV_HEADS):
        v = h_ref[:, A_OFF_V + h * LANES:A_OFF_V + (h + 1) * LANES]
        vf_ref[:, h, :] = v
        vb_ref[:, 2 * h * LANES:(2 * h + 1) * LANES] = v.astype(BF16)
        vb_ref[:, (2 * h + 1) * LANES:(2 * h + 2) * LANES] = ones
    for p in range(A_QI // LANES):
        x = rope64(h_ref[:, A_OFF_QI + p * LANES:A_OFF_QI + (p + 1) * LANES])
        qi_ref[:, (2 * p) * LANES:(2 * p + 1) * LANES] = (
            jnp.where(lane < IDX_DIM, x, 0.0).astype(BF16))
        qi_ref[:, (2 * p + 1) * LANES:(2 * p + 2) * LANES] = (
            jnp.where(lane >= IDX_DIM, x, 0.0).astype(BF16))
    ki = rope64(h_ref[:, A_OFF_KI:A_OFF_KI + LANES])
    kif_ref[...] = ki
    kib_ref[...] = (ki + pltpu.roll(ki, IDX_DIM, axis=1)).astype(BF16)
    wi_ref[...] = h_ref[:, A_OFF_WI:A_OFF_WI + LANES] * IDX_W_SCALE


def _a_post(h, tabs, qg, kg, *, tm=512):
    M = h.shape[0]
    P = tabs[0].shape[0]
    tm = _pick(math.gcd(M, P), tm, 16)
    nt = P // tm
    row = lambda i: (i, 0)
    tab = lambda i: (i % nt, 0)
    one = lambda i: (0, 0)
    widths = [(A_KV, BF16), (2 * A_KV, BF16),
              (IDX_HEADS * LANES, BF16), (LANES, F32), (LANES, BF16), (LANES, F32)]
    heads = (A_KV_HEADS, HEAD_DIM)
    return pl.pallas_call(
        _a_post_kernel,
        grid=(M // tm,),
        in_specs=[pl.BlockSpec((tm, A_IN_PAD), row)]
        + [pl.BlockSpec((tm, LANES), tab)] * 4
        + [pl.BlockSpec((1, LANES), one)] * 2,
        out_specs=[pl.BlockSpec((tm, A_Q), row)]
        + [pl.BlockSpec((tm,) + heads, lambda i: (i, 0, 0))] * 2
        + [pl.BlockSpec((tm, w), row) for w, _ in widths],
        out_shape=[jax.ShapeDtypeStruct((M, A_Q), BF16)]
        + [jax.ShapeDtypeStruct((M,) + heads, F32)] * 2
        + [jax.ShapeDtypeStruct((M, w), dt) for w, dt in widths],
        compiler_params=_cparams("parallel"),
        name="a_post",
    )(h, *tabs, qg, kg)


def _dsa_kernel(*refs, tq, tk, ksel, causal, l_valid, n_chunks, idx_bits):
    if causal:
        q_ref, qi_ref, wi_ref, k_ref, v_ref, ki_ref, o_ref = refs[:7]
    else:
        q_ref, qi_ref, wi_ref, k_ref, v_ref, ki_ref, ck_ref, cv_ref, cki_ref, o_ref = refs[:10]
    key_scr, bias_scr, qst_scr, m_scr, acc_scr = refs[-5:]
    t0 = pl.program_id(1) * tq
    nch = (t0 + tq + tk - 1) // tk if causal else n_chunks
    nslab = tk // LANES
    row = lax.broadcasted_iota(I32, (tq, tk), 0)
    col = lax.broadcasted_iota(I32, (tq, tk), 1)
    lane = lax.broadcasted_iota(I32, (tq, LANES), 1)
    limit = ((t0 + row) // CHUNK + 1) * CHUNK if causal else l_valid

    def chunk_off(c):
        return c * tk if isinstance(c, int) else pl.multiple_of(c * tk, tk)

    def cached(c):
        return not causal and c < n_chunks - 1

    def new_off(c):
        return chunk_off(c) if causal else 0

    def load_ki(c):
        if cached(c):
            x = cki_ref[0, c * tk:(c + 1) * tk, :]
            return jnp.concatenate([x, x], axis=1).astype(BF16)
        return ki_ref[0, pl.ds(new_off(c), tk), :]

    def load_k(c, j):
        if cached(c):
            return ck_ref[0, c * tk:(c + 1) * tk, j, :].astype(BF16)
        return k_ref[0, pl.ds(new_off(c), tk), j * LANES:(j + 1) * LANES]

    def load_v(c, j):
        if cached(c):
            v = cv_ref[0, c * tk:(c + 1) * tk, j, :].astype(BF16)
            return jnp.concatenate([v, jnp.ones((tk, LANES), BF16)], axis=1)
        return v_ref[0, pl.ds(new_off(c), tk), 2 * j * LANES:2 * (j + 1) * LANES]

    def chunk_loop(body):
        if causal:
            lax.fori_loop(0, nch, lambda c, carry: (body(c), carry)[1], 0)
        else:
            for c in range(n_chunks):
                body(c)

    wi = wi_ref[0]

    def score_body(c):
        off = chunk_off(c)
        kic = load_ki(c)
        shs = [_dot_nt(qi_ref[0, :, h * LANES:(h + 1) * LANES], kic) for h in range(IDX_HEADS)]
        s = jnp.zeros((tq, tk), F32)
        for h in range(IDX_HEADS):
            s = s + wi[:, h:h + 1] * jnp.maximum(shs[h], 0.0)
        s = jnp.where(s == 0.0, 0.0, s)
        s = jnp.where(off + col < limit, s, NEG_INF)
        bits = pltpu.bitcast(s, I32)
        key_scr[:, pl.ds(off, tk)] = bits ^ ((bits >> 31) & 0x7FFFFFFF)

    chunk_loop(score_body)

    def count(pred):
        def body(c, acc):
            off = chunk_off(c)
            blk = key_scr[:, pl.ds(off, tk)]
            for s in range(nslab):
                sl = slice(s * LANES, (s + 1) * LANES)
                acc = acc + pred(blk[:, sl], lane + (off + s * LANES))
            return acc

        acc = lax.fori_loop(0, nch, body, jnp.zeros((tq, LANES), F32))
        return jnp.broadcast_to(jnp.sum(acc, axis=-1, keepdims=True), (tq, LANES))

    def thr_cond(st):
        i, _, n_ge = st
        return jnp.logical_and(i < 32, jnp.max(n_ge) > ksel)

    def thr_body(st):
        i, thr, n_ge = st
        for b in range(THR_BITS_PER_CHECK):
            cand = thr + lax.shift_left(jnp.int32(1), 31 - i - b)
            cnt = count(lambda key, idx: jnp.where(key >= cand, 1.0, 0.0))
            take = cnt >= ksel
            thr, n_ge = jnp.where(take, cand, thr), jnp.where(take, cnt, n_ge)
        return i + THR_BITS_PER_CHECK, thr, n_ge

    n_keys = (nch * tk).astype(F32) if causal else float(nch * tk)
    _, thr, n_ge = lax.while_loop(
        thr_cond, thr_body,
        (jnp.int32(0), jnp.full((tq, LANES), -2 ** 31, I32), jnp.full((tq, LANES), n_keys, F32)))

    n_gt = count(lambda key, idx: jnp.where(key > thr, 1.0, 0.0))
    need = ksel - n_gt
    n_eq = n_ge - n_gt

    def cut_search():
        def body(i, cut):
            cand = cut + lax.shift_left(jnp.int32(1), idx_bits - 1 - i)
            below = count(lambda key, idx: jnp.where(key == thr, jnp.where(idx < cand, 1.0, 0.0), 0.0))
            return jnp.where(below <= need, cand, cut)

        return lax.fori_loop(0, idx_bits, body, jnp.zeros((tq, LANES), I32))

    cut = lax.cond(jnp.max(n_eq - need) > 0.0, cut_search,
                   lambda: jnp.full((tq, LANES), 2 ** idx_bits, I32))

    thr_t = jnp.tile(thr, (1, nslab))
    cut_t = jnp.tile(cut, (1, nslab))

    def bias_body(c, carry):
        off = chunk_off(c)
        key = key_scr[:, pl.ds(off, tk)]
        idx = off + col
        tie = jnp.where(key == thr_t, jnp.where(idx < cut_t, 0.0, NEG_INF), NEG_INF)
        bias = jnp.where(key > thr_t, 0.0, tie)
        bias_scr[:, pl.ds(off, tk)] = jnp.where(idx < limit, bias, NEG_INF)
        return carry

    lax.fori_loop(0, nch, bias_body, 0)

    for j in range(A_KV_HEADS):
        for g in range(A_GROUP):
            h = j * A_GROUP + g
            qst_scr[j, g * tq:(g + 1) * tq, :] = q_ref[0, :, h * LANES:(h + 1) * LANES]
    m_scr[...] = jnp.full(m_scr.shape, ROW_MAX_INIT, F32)
    acc_scr[...] = jnp.zeros(acc_scr.shape, F32)
    gr = A_GROUP * tq

    def attn_body(c):
        off = chunk_off(c)
        bias = bias_scr[:, pl.ds(off, tk)]
        logits = [_dot_nt(qst_scr[j], load_k(c, j)) for j in range(A_KV_HEADS)]
        for j in range(A_KV_HEADS):
            rows = slice(j * gr, (j + 1) * gr)
            m_old = m_scr[rows, :]
            alphas, ps = [], []
            for g in range(A_GROUP):
                sl = slice(g * tq, (g + 1) * tq)
                sg = logits[j][sl, :] + bias
                m_new = jnp.maximum(m_old[sl, :], jnp.max(sg, axis=-1, keepdims=True))
                alphas.append(jnp.exp(m_old[sl, :] - m_new))
                ps.append(jnp.exp((sg - jnp.tile(m_new, (1, nslab))).astype(BF16)))
                m_scr[j * gr + g * tq:j * gr + (g + 1) * tq, :] = m_new
            alpha = jnp.tile(jnp.concatenate(alphas, axis=0), (1, 2))
            pv = _dot(jnp.concatenate(ps, axis=0), load_v(c, j))
            acc_scr[rows, :] = alpha * acc_scr[rows, :] + pv

    chunk_loop(attn_body)
    for h in range(A_HEADS):
        rows = slice(h * tq, (h + 1) * tq)
        o_ref[0, :, h * LANES:(h + 1) * LANES] = (
            acc_scr[rows, :LANES] / acc_scr[rows, LANES:]).astype(BF16)


def _dsa(q, qi, wi, k, v, ki, cache=None, *, tq, tk, ksel, causal, l_valid):
    B, T, _ = q.shape
    Lk = k.shape[1]
    blk_q = lambda w: pl.BlockSpec((1, tq, w), lambda b, i: (b, i, 0))
    blk_k = lambda n, w: pl.BlockSpec((1, n, w), lambda b, i: (b, 0, 0))
    specs, args = [], []
    L = Lk
    if cache is not None:
        ck, cv, cki, li = cache
        past = ck.shape[2]
        L = past + Lk
        blk_c = pl.BlockSpec((None, 1, past, A_KV_HEADS, HEAD_DIM),
                             lambda b, i: (li, b, 0, 0, 0))
        specs = [blk_c, blk_c,
                 pl.BlockSpec((None, 1, past, IDX_DIM), lambda b, i: (li, b, 0, 0))]
        args = [ck, cv, cki]
    kern = functools.partial(
        _dsa_kernel, tq=tq, tk=tk, ksel=ksel, causal=causal, l_valid=l_valid,
        n_chunks=-(-l_valid // tk), idx_bits=max(1, (L - 1).bit_length()) + 1)
    return pl.pallas_call(
        kern,
        grid=(B, T // tq),
        in_specs=[blk_q(A_Q), blk_q(IDX_HEADS * LANES), blk_q(LANES),
                  blk_k(Lk, A_KV), blk_k(Lk, 2 * A_KV), blk_k(Lk, LANES)] + specs,
        out_specs=blk_q(A_Q),
        out_shape=jax.ShapeDtypeStruct((B, T, A_Q), BF16),
        scratch_shapes=[pltpu.VMEM((tq, L), I32), pltpu.VMEM((tq, L), F32),
                        pltpu.VMEM((A_KV_HEADS, A_GROUP * tq, LANES), BF16),
                        pltpu.VMEM((A_HEADS * tq, LANES), F32),
                        pltpu.VMEM((A_HEADS * tq, 2 * LANES), F32)],
        compiler_params=_cparams("parallel", "arbitrary"),
        name="dsa_causal" if causal else "dsa_cached",
    )(q, qi, wi, k, v, ki, *args)


def _pack_a_w_in(w):
    D = w.shape[0]
    o_ki = A_Q + 2 * A_KV + A_QI
    pad = lambda n: jnp.zeros((D, n), w.dtype)
    return jnp.concatenate(
        [w[:, :o_ki], w[:, o_ki:o_ki + IDX_DIM], pad(LANES - IDX_DIM),
         w[:, o_ki + IDX_DIM:], pad(LANES - IDX_HEADS)], axis=1).astype(BF16)


def _mixer_a(xp, xs, g, w_in, w_out, qg, kg, caches, li):
    B, S, D = xp.shape
    Bs, T, _ = xs.shape
    past = caches[0].shape[2]
    w_in = _pack_a_w_in(w_in)
    w_out = w_out.astype(BF16)
    g, qg, kg = g[None], qg[None], kg[None]

    def project(x, pos):
        b, t, _ = x.shape
        h = _rms_matmul(x.reshape(b * t, D), g, w_in, tn=A_IN_PAD // 5)
        tabs = _rope_tables(pos, HEAD_DIM) + _rope_tables(pos, IDX_DIM)
        q, kf, vf, *rest = _a_post(h, tabs, qg, kg)
        heads = (b, t, A_KV_HEADS, HEAD_DIM)
        return [q.reshape(b, t, -1), kf.reshape(heads), vf.reshape(heads)] + [
            o.reshape(b, t, -1) for o in rest]

    q, k_p, v_p, kb, vb, qi, kif, kib, wi = project(xp, jnp.arange(S))
    tk = _pick(S, 512, LANES)
    o = _dsa(q, qi, wi, kb, vb, kib, tq=min(128, S), tk=tk, ksel=min(TOPK_MAX, S // 4),
             causal=True, l_valid=S)
    yp = _matmul_res(o.reshape(B * S, A_Q), w_out, xp.reshape(B * S, D)).reshape(B, S, D)
    ki_p = kif[..., :IDX_DIM]

    pos_s = jnp.tile(past + jnp.arange(T), Bs)
    q, k_s, v_s, kb, vb, qi, kif, kib, wi = project(xs, pos_s)
    L = past + T
    tk = _pick(past, 512, LANES)
    padk = lambda a: jnp.pad(a, ((0, 0), (0, tk - T), (0, 0)))
    o = _dsa(q, qi, wi, padk(kb), padk(vb), padk(kib), caches + (li,),
             tq=T, tk=tk, ksel=min(TOPK_MAX, L // 4), causal=False, l_valid=L)
    ys = _matmul_res(o.reshape(Bs * T, A_Q), w_out, xs.reshape(Bs * T, D)).reshape(Bs, T, D)
    return yp, ys, (k_p, v_p, ki_p, k_s, v_s, kif[..., :IDX_DIM])


def _c_post_kernel(h_ref, qg_ref, kg_ref, q_ref, kb_ref, kf_ref, vf_ref):
    W = C_HEADS * HEAD_DIM
    for h in range(C_HEADS):
        sl = slice(h * LANES, (h + 1) * LANES)
        q = _rms(h_ref[:, sl].astype(F32), qg_ref[...])
        q_ref[:, sl] = (q * (HEAD_DIM ** -0.5)).astype(BF16)
        k = _rms(h_ref[:, W + h * LANES:W + (h + 1) * LANES].astype(F32), kg_ref[...])
        kf_ref[:, sl] = k
        kb_ref[:, sl] = k.astype(BF16)
    vf_ref[...] = h_ref[:, 2 * W:3 * W].astype(F32)


def _c_post(h, qg, kg, *, t, keep, tm=512):
    M = h.shape[0]
    W = C_HEADS * HEAD_DIM
    tm = _pick(math.gcd(t, keep), tm, 16) if keep < t else _pick(M, tm, 16)
    row = lambda i: (i, 0)
    if keep < t:
        tps, kps = t // tm, keep // tm
        tail = lambda i: ((i // tps) * kps + jnp.maximum(i % tps - (tps - kps), 0), 0)
    else:
        tail = row
    n_tail = M // t * keep
    return pl.pallas_call(
        _c_post_kernel,
        grid=(M // tm,),
        in_specs=[pl.BlockSpec((tm, 3 * W), row),
                  pl.BlockSpec((1, LANES), lambda i: (0, 0)),
                  pl.BlockSpec((1, LANES), lambda i: (0, 0))],
        out_specs=[pl.BlockSpec((tm, W), row)] * 2 + [pl.BlockSpec((tm, W), tail)] * 2,
        out_shape=[jax.ShapeDtypeStruct((M, W), BF16)] * 2
        + [jax.ShapeDtypeStruct((n_tail, W), F32)] * 2,
        compiler_params=_cparams("arbitrary"),
        name="c_post",
    )(h, qg, kg)


def _band_kernel(q_ref, k_ref, v_ref, b_ref, o_ref, *, sub, nsub, win, first_key):
    t0 = pl.program_id(2) * (sub * nsub)
    starts = [pl.multiple_of(t0 + s * sub, sub) for s in range(nsub)]
    lgs = [_dot_nt(q_ref[0, s * sub:(s + 1) * sub, :], k_ref[0, pl.ds(starts[s], win), :])
           + b_ref[0] for s in range(nsub)]
    kk = lax.broadcasted_iota(I32, (sub, win), 1)
    lgs = [jnp.where(kk >= first_key - starts[s], lgs[s], NEG_INF) for s in range(nsub)]
    ms = [jnp.max(lg, axis=-1, keepdims=True) for lg in lgs]
    ps = [jnp.exp(lg - m) for lg, m in zip(lgs, ms)]
    ls = [jnp.sum(p, axis=-1, keepdims=True) for p in ps]
    os_ = [_dot(ps[s].astype(BF16), v_ref[0, pl.ds(starts[s], win), :]) for s in range(nsub)]
    for s in range(nsub):
        o_ref[0, s * sub:(s + 1) * sub, :] = (os_[s] / ls[s]).astype(BF16)


def _band(q, k, v, bias, *, sub, nsub, first_key):
    B, T, W = q.shape
    Lk = k.shape[1]
    win = bias.shape[2]
    tq = sub * nsub
    kern = functools.partial(_band_kernel, sub=sub, nsub=nsub, win=win, first_key=first_key)
    return pl.pallas_call(
        kern,
        grid=(B, C_HEADS, T // tq),
        in_specs=[pl.BlockSpec((1, tq, LANES), lambda b, h, i: (b, i, h)),
                  pl.BlockSpec((1, Lk, LANES), lambda b, h, i: (b, 0, h)),
                  pl.BlockSpec((1, Lk, LANES), lambda b, h, i: (b, 0, h)),
                  pl.BlockSpec((1, sub, win), lambda b, h, i: (h, 0, 0))],
        out_specs=pl.BlockSpec((1, tq, LANES), lambda b, h, i: (b, i, h)),
        out_shape=jax.ShapeDtypeStruct((B, T, W), BF16),
        compiler_params=_cparams("parallel", "parallel", "arbitrary"),
        name="band_attn",
    )(q, k, v, bias)


def _band_cached_kernel(q_ref, k_ref, v_ref, ck_ref, cv_ref, b_ref, o_ref):
    T = q_ref.shape[1]
    cp = ck_ref.shape[1]
    win = b_ref.shape[2]
    pad = jnp.zeros((win - cp - T, LANES), BF16)

    def window(c_ref, n_ref, h):
        new = n_ref[0, :, h * LANES:(h + 1) * LANES]
        return jnp.concatenate([c_ref[0, :, h, :].astype(BF16), new, pad], axis=0)

    heads = range(C_HEADS)
    lgs = [_dot_nt(q_ref[0, :, h * LANES:(h + 1) * LANES], window(ck_ref, k_ref, h)) + b_ref[h]
           for h in heads]
    ms = [jnp.max(lg, axis=-1, keepdims=True) for lg in lgs]
    ps = [jnp.exp(lg - m) for lg, m in zip(lgs, ms)]
    ls = [jnp.sum(p, axis=-1, keepdims=True) for p in ps]
    for h in heads:
        o = _dot(ps[h].astype(BF16), window(cv_ref, v_ref, h)) / ls[h]
        o_ref[0, :, h * LANES:(h + 1) * LANES] = o.astype(BF16)


def _band_cached(q, k, v, ck, cv, bias):
    B, T, W = q.shape
    cp = ck.shape[1]
    row = pl.BlockSpec((1, T, W), lambda b: (b, 0, 0))
    cache = pl.BlockSpec((1, cp, C_HEADS, HEAD_DIM), lambda b: (b, 0, 0, 0))
    return pl.pallas_call(
        _band_cached_kernel,
        grid=(B,),
        in_specs=[row, row, row, cache, cache,
                  pl.BlockSpec(bias.shape, lambda b: (0, 0, 0))],
        out_specs=row,
        out_shape=jax.ShapeDtypeStruct((B, T, W), BF16),
        compiler_params=_cparams("parallel"),
        name="band_cached",
    )(q, k, v, ck, cv, bias)


def _mixer_c(xp, xs, g, w_in, w_out, qg, kg, rel, ck, cv):
    B, S, D = xp.shape
    Bs, T, _ = xs.shape
    W = C_HEADS * HEAD_DIM
    cp = ck.shape[1]
    w_in = w_in.astype(BF16)
    w_out = w_out.astype(BF16)
    g, qg, kg = g[None], qg[None], kg[None]
    relf = rel.astype(F32)

    def project(x, keep):
        b, t, _ = x.shape
        h = _rms_matmul(x.reshape(b * t, D), g, w_in, out_dtype=BF16)
        q, kb, kf, vf = _c_post(h, qg, kg, t=t, keep=keep)
        vb = h.reshape(b, t, 3 * W)[:, :, 2 * W:]
        tail = lambda a: a.reshape(b, keep, C_HEADS, HEAD_DIM)
        return q.reshape(b, t, W), tail(kf), kb.reshape(b, t, W), vb, tail(vf)

    def rel_bias(rows, win, shift):
        n = rows + win - 1
        j = jnp.arange(n)
        vec = relf[:, jnp.clip(shift + rows - 1 - j, -REL_CLIP, REL_CLIP) + REL_CLIP]
        flat = jnp.tile(jnp.pad(vec, ((0, 0), (0, 1))), (1, rows))
        toep = flat[:, :rows * n].reshape(-1, rows, n)
        return toep[:, :, rows - 1:rows - 1 + win]

    sub = min(2 * CHUNK, S)
    win = C_REACH + sub
    r = jnp.arange(sub)[:, None]
    kk = jnp.arange(win)[None, :]
    lo = (r // CHUNK) * CHUNK
    bias = jnp.where((kk >= lo) & (kk < lo + C_REACH + CHUNK), rel_bias(sub, win, C_REACH),
                     NEG_INF)
    q, k_p, kb, vb, v_p = project(xp, min(C_REACH, S))
    padk = lambda a: jnp.pad(a, ((0, 0), (C_REACH, 0), (0, 0)))
    o = _band(q, padk(kb), padk(vb), bias, sub=sub, nsub=_pick(S // sub, 8, 1),
              first_key=C_REACH)
    yp = _matmul_res(o.reshape(B * S, W), w_out, xp.reshape(B * S, D)).reshape(B, S, D)

    L = cp + T
    win = -(-L // LANES) * LANES
    bias = jnp.where(jnp.arange(win)[None, :] < L, rel_bias(T, win, cp), NEG_INF)
    q, kf, kb, vb, vf = project(xs, T)
    o = _band_cached(q, kb, vb, ck, cv, bias)
    ys = _matmul_res(o.reshape(Bs * T, W), w_out, xs.reshape(Bs * T, D)).reshape(Bs, T, D)
    return yp, ys, (k_p, v_p, kf, vf)


def _s5_ops_kernel(ar_ref, ai_ref, ldt_ref, br_ref, bi_ref, cr_ref, ci_ref, cr4_ref, ci4_ref,
                   lag_ref, xr_ref, xi_ref, or_ref, oi_ref, lnr_ref, lni_ref, sr_ref, si_ref):
    n, P = xr_ref.shape[:2]
    dt = jnp.exp(ldt_ref[...])
    ar, ai = ar_ref[...], ai_ref[...]
    mag = jnp.exp(ar * dt)
    lr, li = mag * jnp.cos(ai * dt), mag * jnp.sin(ai * dt)
    den = ar * ar + ai * ai
    nr = lr - 1.0
    fr = (nr * ar + li * ai) / den
    fi = (li * ar - nr * ai) / den
    br, bi = br_ref[...], bi_ref[...]
    bbr = fr * br - fi * bi
    bbi = fr * bi + fi * br
    cr, ci = cr_ref[...], ci_ref[...]
    pr, pi = jnp.ones_like(lr), jnp.zeros_like(lr)
    for tau in range(n):
        sr_ref[...] = pr * bbr - pi * bbi
        si_ref[...] = pr * bbi + pi * bbr
        xr_ref[tau] = sr_ref[...].astype(BF16)
        xi_ref[tau] = si_ref[...].astype(BF16)

        def body(p, acc):
            return acc + cr4_ref[p] * sr_ref[p][None] - ci4_ref[p] * si_ref[p][None]

        lag_ref[tau] = lax.fori_loop(0, P, body, jnp.zeros(lag_ref.shape[1:], F32))
        pr, pi = pr * lr - pi * li, pr * li + pi * lr
        or_ref[tau] = (cr * pr - ci * pi).astype(BF16)
        oi_ref[tau] = (-(cr * pi + ci * pr)).astype(BF16)
    lnr_ref[...] = pr
    lni_ref[...] = pi


def _s5_operators(a_re, a_im, log_dt, b_re, b_im, c_re, c_im):
    G, P = a_re.shape
    C = S5_GROUP
    n = S5_CHUNK
    nlb = G // S5_LB_GROUPS
    f = lambda a: a.astype(F32)
    a_t = lambda a: f(a).T.reshape(P, 1, G)
    b_t = lambda a: jnp.transpose(f(a), (1, 2, 0))
    c_t = lambda a: jnp.transpose(f(a), (2, 1, 0))
    crt, cit = c_t(c_re), c_t(c_im)
    sds = lambda *shape: jax.ShapeDtypeStruct(shape, F32)
    lag, xr, xi, o_re, o_im, lnr, lni = pl.pallas_call(
        _s5_ops_kernel,
        out_shape=[sds(n, C, C, G)] + [jax.ShapeDtypeStruct((n, P, C, G), BF16)] * 4
        + [sds(P, 1, G)] * 2,
        scratch_shapes=[pltpu.VMEM((P, C, G), F32)] * 2,
        compiler_params=pltpu.CompilerParams(vmem_limit_bytes=VMEM_LIMIT),
        name="s5_ops",
    )(a_t(a_re), a_t(a_im), f(log_dt).reshape(1, 1, G), b_t(b_re), b_t(b_im), crt, cit,
      crt.reshape(P, C, 1, G), cit.reshape(P, C, 1, G))

    split = lambda a: a.reshape(a.shape[:-1] + (nlb, S5_LB_GROUPS))
    taps = jnp.transpose(split(lag), (3, 0, 4, 2, 1)).reshape(nlb, n, LANES, C)
    taps = jnp.pad(taps.astype(BF16), ((0, 0), (0, 0), (0, 0), (0, LANES - C)))

    def inc(x):
        w = jnp.transpose(split(x[::-1]), (3, 4, 0, 2, 1))
        return jnp.concatenate([w, w], axis=-1)

    def out(x):
        return jnp.transpose(split(x), (3, 4, 1, 0, 2)).reshape(nlb, S5_LB_GROUPS, P, n * C)

    lam_n = tuple(a.reshape(P, G).T.reshape(1, 1, G * P) for a in (lnr, lni))
    return taps, inc(xr), inc(xi), out(o_re), out(o_im), lam_n


def _chunk_rows(u_ref, tc):
    return jnp.concatenate(
        [u_ref[pl.ds(t, tc, stride=S5_CHUNK), :] for t in range(S5_CHUNK)], axis=1)


def _s5_inc_kernel(u_ref, xr_ref, xi_ref, sr_ref, si_ref, wr_ref, wi_ref):
    n, C = S5_CHUNK, S5_GROUP
    P = wr_ref.shape[1] // S5_LB_GROUPS

    @pl.when(pl.program_id(1) == 0)
    def _():
        for x_ref, w_ref in ((xr_ref, wr_ref), (xi_ref, wi_ref)):
            w_ref[...] = jnp.zeros(w_ref.shape, BF16)
            for g in range(S5_LB_GROUPS):
                half = (g % 2) * P
                for t in range(n):
                    r0 = t * LANES + g * C
                    w_ref[r0:r0 + C, g * P:(g + 1) * P] = x_ref[0, g, t, :, half:half + P]

    u = _chunk_rows(u_ref, sr_ref.shape[0]).astype(BF16)
    sr_ref[...] = _dot(u, wr_ref[...])
    si_ref[...] = _dot(u, wi_ref[...])


def _s5_inc(u, x_re, x_im, *, tc):
    Nc = u.shape[0] // S5_CHUNK
    nlb = x_re.shape[0]
    K = S5_CHUNK * LANES
    Wn = S5_LB_GROUPS * (x_re.shape[-1] // 2)
    xspec = pl.BlockSpec((1,) + x_re.shape[1:], lambda lb, i: (lb, 0, 0, 0, 0))
    ospec = pl.BlockSpec((tc, Wn), lambda lb, i: (i, lb))
    return pl.pallas_call(
        _s5_inc_kernel,
        grid=(nlb, Nc // tc),
        in_specs=[pl.BlockSpec((tc * S5_CHUNK, LANES), lambda lb, i: (i, lb)), xspec, xspec],
        out_specs=[ospec, ospec],
        out_shape=[jax.ShapeDtypeStruct((Nc, nlb * Wn), F32)] * 2,
        scratch_shapes=[pltpu.VMEM((K, Wn), BF16)] * 2,
        compiler_params=_cparams("parallel", "arbitrary"),
        name="s5_inc",
    )(u, x_re, x_im)


def _s5_scan_kernel(sr_ref, si_ref, lr_ref, li_ref, h0r_ref, h0i_ref,
                    pr_ref, pi_ref, fr_ref, fi_ref):
    lr, li = lr_ref[...], li_ref[...]
    nck = sr_ref.shape[1]

    def body(k, carry):
        hr, hi = carry
        pr_ref[:, pl.ds(k, 1), :] = hr
        pi_ref[:, pl.ds(k, 1), :] = hi
        sr = sr_ref[:, pl.ds(k, 1), :]
        si = si_ref[:, pl.ds(k, 1), :]
        return lr * hr - li * hi + sr, lr * hi + li * hr + si

    hr, hi = lax.fori_loop(0, nck, body, (h0r_ref[...], h0i_ref[...]))
    fr_ref[...] = hr
    fi_ref[...] = hi


def _s5_scan(sr, si, lam, h0r, h0i, *, cb=1024):
    B, nck, GP = sr.shape
    cb = _pick(GP, cb, LANES)
    full = pl.BlockSpec((B, nck, cb), lambda j: (0, 0, j))
    vec = pl.BlockSpec((1, 1, cb), lambda j: (0, 0, j))
    st = pl.BlockSpec((B, 1, cb), lambda j: (0, 0, j))
    return pl.pallas_call(
        _s5_scan_kernel,
        grid=(GP // cb,),
        in_specs=[full, full, vec, vec, st, st],
        out_specs=[full, full, st, st],
        out_shape=[jax.ShapeDtypeStruct((B, nck, GP), F32)] * 2
        + [jax.ShapeDtypeStruct((B, 1, GP), F32)] * 2,
        compiler_params=_cparams("parallel"),
        name="s5_scan",
    )(sr, si, lam[0], lam[1], h0r, h0i)


def _s5_out_kernel(u_ref, taps_ref, hr_ref, hi_ref, ar_ref, ai_ref, d_ref, o_ref,
                   toep_ref, wr_ref, wi_ref):
    n, C = S5_CHUNK, S5_GROUP
    tc = hr_ref.shape[0]
    P = ar_ref.shape[2]

    @pl.when(pl.program_id(1) == 0)
    def _():
        row = lax.broadcasted_iota(I32, (LANES, LANES), 0)
        lane = lax.broadcasted_iota(I32, (LANES, LANES), 1)
        spread = jnp.where(lane % C == row, 1.0, 0.0).astype(BF16)
        same_group = row // C == lane // C
        zero = jnp.zeros((LANES, LANES), BF16)
        for tau in range(n):
            tap = jnp.where(same_group, _dot(taps_ref[0, tau], spread), 0.0).astype(BF16)
            for t in range(n - tau):
                v = t + tau
                toep_ref[t * LANES:(t + 1) * LANES, v * LANES:(v + 1) * LANES] = tap
            for t in range(tau):
                toep_ref[tau * LANES:(tau + 1) * LANES, t * LANES:(t + 1) * LANES] = zero
        r2 = lax.broadcasted_iota(I32, (n * C, n * LANES), 0)
        c2 = lax.broadcasted_iota(I32, (n * C, n * LANES), 1)
        for g in range(S5_LB_GROUPS):
            place = jnp.where(c2 == (r2 // C) * LANES + g * C + r2 % C, 1.0, 0.0).astype(BF16)
            wr_ref[g * P:(g + 1) * P, :] = _dot(ar_ref[0, g], place).astype(BF16)
            wi_ref[g * P:(g + 1) * P, :] = _dot(ai_ref[0, g], place).astype(BF16)

    u = _chunk_rows(u_ref, tc)
    y = _dot(u.astype(BF16), toep_ref[...])
    y = y + _dot(hr_ref[...].astype(BF16), wr_ref[...]) + _dot(hi_ref[...].astype(BF16), wi_ref[...])
    y = jax.nn.gelu(y + jnp.tile(d_ref[...], (1, n)) * u)
    for t in range(n):
        o_ref[pl.ds(t, tc, stride=n), :] = y[:, t * LANES:(t + 1) * LANES]


def _s5_out(u, taps, hr, hi, a_re, a_im, d, *, tc):
    M, D = u.shape
    n = S5_CHUNK
    nlb, ng, P, _ = a_re.shape
    Wn, K = ng * P, n * LANES
    hspec = pl.BlockSpec((tc, Wn), lambda lb, i: (i, lb))
    uspec = pl.BlockSpec((tc * n, LANES), lambda lb, i: (i, lb))
    aspec = pl.BlockSpec((1,) + a_re.shape[1:], lambda lb, i: (lb, 0, 0, 0))
    return pl.pallas_call(
        _s5_out_kernel,
        grid=(nlb, M // (tc * n)),
        in_specs=[uspec,
                  pl.BlockSpec((1, n, LANES, LANES), lambda lb, i: (lb, 0, 0, 0)), hspec, hspec,
                  aspec, aspec,
                  pl.BlockSpec((1, LANES), lambda lb, i: (0, lb))],
        out_specs=uspec,
        out_shape=jax.ShapeDtypeStruct((M, D), F32),
        scratch_shapes=[pltpu.VMEM((K, K), BF16), pltpu.VMEM((Wn, K), BF16),
                        pltpu.VMEM((Wn, K), BF16)],
        compiler_params=_cparams("parallel", "arbitrary"),
        name="s5_out",
    )(u, taps, hr, hi, a_re, a_im, d)


def _glu_res_kernel(x_ref, wa_ref, wb_ref, r_ref, o_ref, xc_ref):
    @pl.when(pl.program_id(1) == 0)
    def _():
        xc_ref[...] = x_ref[...].astype(BF16)

    x = xc_ref[...]
    a = _dot(x, wa_ref[...])
    b = _dot(x, wb_ref[...])
    o_ref[...] = r_ref[...] + a * jax.nn.sigmoid(b)


def _glu_res(x, w, res, *, tm=1024, tn=512):
    M, D = x.shape
    N = w.shape[1] // 2
    tm = _pick(M, tm, 16)
    tn = _pick(N, tn, LANES)
    nj = N // tn
    return pl.pallas_call(
        _glu_res_kernel,
        grid=(M // tm, nj),
        in_specs=[pl.BlockSpec((tm, D), lambda i, j: (i, 0)),
                  pl.BlockSpec((D, tn), lambda i, j: (0, j)),
                  pl.BlockSpec((D, tn), lambda i, j: (0, nj + j)),
                  pl.BlockSpec((tm, tn), lambda i, j: (i, j))],
        out_specs=pl.BlockSpec((tm, tn), lambda i, j: (i, j)),
        out_shape=jax.ShapeDtypeStruct((M, N), F32),
        scratch_shapes=[pltpu.VMEM((tm, D), BF16)],
        compiler_params=_cparams("parallel", "arbitrary"),
        name="glu_res",
    )(x, w, w, res)


def _mixer_b(xp, xs, g, ops, d, w_glu, h0r, h0i):
    taps, inc_re, inc_im, out_re, out_im, lam = ops
    n = S5_CHUNK
    GP = lam[0].shape[-1]
    g, d = g[None], d.astype(F32)[None]
    w_glu = w_glu.astype(BF16)

    def stream(x, hr0, hi0):
        B, T, D = x.shape
        nck = T // n
        Nc = B * nck
        x2 = x.reshape(B * T, D)
        u = _rmsnorm(x2, g)
        tc = _pick(Nc, 256, 8)
        sr, si = _s5_inc(u, inc_re, inc_im, tc=tc)
        pr, pi, fr, fi = _s5_scan(sr.reshape(B, nck, GP), si.reshape(B, nck, GP), lam, hr0, hi0)
        gy = _s5_out(u, taps, pr.reshape(Nc, GP), pi.reshape(Nc, GP), out_re, out_im, d, tc=tc)
        y = _glu_res(gy, w_glu, x2)
        shp = (B, GP // S5_STATE, S5_STATE)
        return y.reshape(B, T, D), fr.reshape(shp), fi.reshape(shp)

    B = xp.shape[0]
    zero = jnp.zeros((B, 1, GP), F32)
    yp, rp, ip = stream(xp, zero, zero)
    Bs = xs.shape[0]
    ys, rs, is_ = stream(xs, h0r.astype(F32).reshape(Bs, 1, GP), h0i.astype(F32).reshape(Bs, 1, GP))
    return yp, ys, (rp, ip, rs, is_)


FFN_HALO = 16
FFN_TAIL = 8


def _ffn_kernel(*refs, seq_tiles, seq_len, halo):
    if halo:
        (x_ref, xh_ref, g_ref, wa_ref, wv_ref, cwa_ref, cwv_ref, cba_ref, cbv_ref, wd_ref,
         o_ref, ta_ref, tv_ref, xn_ref, xhn_ref) = refs
    else:
        (x_ref, p1a_ref, p1v_ref, p2a_ref, p2v_ref, g_ref, wa_ref, wv_ref, cwa_ref, cwv_ref,
         cba_ref, cbv_ref, wd_ref, o_ref, ta_ref, tv_ref, xn_ref, h_ref) = refs
    i, f = pl.program_id(0), pl.program_id(1)
    tm = x_ref.shape[0]

    @pl.when(f == 0)
    def _():
        x = x_ref[...]
        xn_ref[...] = _rms(x, g_ref[...]).astype(BF16)
        o_ref[...] = x
        if halo:
            xhn_ref[...] = _rms(xh_ref[...], g_ref[...]).astype(BF16)

    xn = xn_ref[...]
    tf = wa_ref.shape[1]
    r = lax.broadcasted_iota(I32, (tm, tf), 0)

    def conv(w_ref, cw_ref, cb_ref, p1_ref, p2_ref, t_ref):
        h = _dot(xn, w_ref[...])
        h1 = pltpu.roll(h, 1, axis=0)
        h2 = pltpu.roll(h, 2, axis=0)
        if halo:
            t_ref[...] = h[tm - FFN_TAIL:, :]
            hh = _dot(xhn_ref[...], w_ref[...])
            hh = hh * jnp.where(i % seq_tiles == 0, 0.0, 1.0)
            last, prev = hh[-1:, :], hh[-2:-1, :]
            h1 = jnp.where(r == 0, last, h1)
            h2 = jnp.where(r == 0, prev, jnp.where(r == 1, last, h2))
        else:
            for s in range(tf // LANES):
                sl = slice(s * LANES, (s + 1) * LANES)
                h_ref[s] = h[:, sl]
                for k in range(CONV_W - 1):
                    rows = pl.ds(seq_len - (CONV_W - 1) + k, tm // seq_len, stride=seq_len)
                    t_ref[k, :, sl] = h_ref.at[s][rows, :]
            t = r % seq_len
            h1 = jnp.where(t >= 1, h1, p1_ref[...])
            h2 = jnp.where(t >= 2, h2, p2_ref[...])
        cw = cw_ref[...]
        return cb_ref[...] + cw[0:1] * h2 + cw[1:2] * h1 + cw[2:3] * h

    if halo:
        ca = conv(wa_ref, cwa_ref, cba_ref, None, None, ta_ref)
        cv = conv(wv_ref, cwv_ref, cbv_ref, None, None, tv_ref)
    else:
        ca = conv(wa_ref, cwa_ref, cba_ref, p1a_ref, p2a_ref, ta_ref)
        cv = conv(wv_ref, cwv_ref, cbv_ref, p1v_ref, p2v_ref, tv_ref)
    act = (jax.nn.silu(ca) * cv).astype(BF16)
    o_ref[...] += _dot(act, wd_ref[...])


def _ffn(x, g, w_up, conv_w, conv_b, w_down, *, layer, seq_len, hist=None, tm=512, tf=1024):
    M, D = x.shape
    F = w_down.shape[1]
    halo = hist is None
    n_seq = M // seq_len
    tm = _pick(seq_len, tm, FFN_HALO) if halo else M
    tf = _pick(F, tf, LANES)
    nf = F // tf
    xs = pl.BlockSpec((tm, D), lambda i, f: (i, 0))
    a_col = lambda i, f: (layer, 0, f)
    v_col = lambda i, f: (layer, 0, nf + f)
    wspecs = [pl.BlockSpec((1, D), lambda i, f: (0, 0)),
              pl.BlockSpec((None, D, tf), a_col), pl.BlockSpec((None, D, tf), v_col),
              pl.BlockSpec((None, CONV_W, tf), a_col), pl.BlockSpec((None, CONV_W, tf), v_col),
              pl.BlockSpec((None, 1, tf), a_col), pl.BlockSpec((None, 1, tf), v_col),
              pl.BlockSpec((None, tf, D), lambda i, f: (layer, f, 0))]
    wargs = [g, w_up, w_up, conv_w, conv_w, conv_b, conv_b, w_down]
    scratch = [pltpu.VMEM((tm, D), BF16)]
    seq_tiles = max(seq_len // tm, 1)
    if halo:
        per = tm // FFN_HALO
        extra = [pl.BlockSpec((FFN_HALO, D), lambda i, f: (jnp.maximum(i * per - 1, 0), 0))]
        eargs = [x]
        scratch.append(pltpu.VMEM((FFN_HALO, D), BF16))
        tspec = pl.BlockSpec((FFN_TAIL, tf), lambda i, f: (i // seq_tiles, f))
        tshape = jax.ShapeDtypeStruct((n_seq * FFN_TAIL, F), F32)
    else:
        ha = pl.BlockSpec((tm, tf), lambda i, f: (i, f))
        hv = pl.BlockSpec((tm, tf), lambda i, f: (i, nf + f))
        extra = [ha, hv, ha, hv]
        eargs = [hist[0], hist[0], hist[1], hist[1]]
        scratch.append(pltpu.VMEM((tf // LANES, tm, LANES), F32))
        tspec = pl.BlockSpec((CONV_W - 1, n_seq, tf), lambda i, f: (0, 0, f))
        tshape = jax.ShapeDtypeStruct((CONV_W - 1, n_seq, F), F32)
    kern = functools.partial(_ffn_kernel, seq_tiles=seq_tiles, seq_len=seq_len, halo=halo)
    y, ta, tv = pl.pallas_call(
        kern,
        grid=(M // tm, nf),
        in_specs=[xs] + extra + wspecs,
        out_specs=[pl.BlockSpec((tm, D), lambda i, f: (i, 0)), tspec, tspec],
        out_shape=[jax.ShapeDtypeStruct((M, D), F32), tshape, tshape],
        scratch_shapes=scratch,
        compiler_params=_cparams("arbitrary", "arbitrary"),
        name="conv_ffn" if halo else "conv_ffn_hist",
    )(x, *eargs, *wargs)
    if halo:
        tail = lambda t: t.reshape(n_seq, FFN_TAIL, F)[:, FFN_TAIL - (CONV_W - 1):]
    else:
        tail = lambda t: jnp.swapaxes(t, 0, 1)
    return y, jnp.concatenate([tail(ta), tail(tv)], axis=-1)


def _conv_ffn(xp, xs, g, w_up, conv_w, conv_b, w_down, hist_s, layer):
    B, S, D = xp.shape
    Bs, T, _ = xs.shape
    F2 = w_up.shape[2]
    g = g[None]
    yp, hp = _ffn(xp.reshape(B * S, D), g, w_up, conv_w, conv_b, w_down, layer=layer, seq_len=S)
    hs = hist_s.astype(F32)
    t = jnp.arange(T)[None, :, None]
    p1 = jnp.where(t == 0, hs[:, -1:], 0.0).reshape(Bs * T, F2)
    p2 = jnp.where(t == 0, hs[:, -2:-1], jnp.where(t == 1, hs[:, -1:], 0.0)).reshape(Bs * T, F2)
    ys, hs_new = _ffn(xs.reshape(Bs * T, D), g, w_up, conv_w, conv_b, w_down, layer=layer,
                      seq_len=T, hist=(p1, p2))
    return yp.reshape(B, S, D), ys.reshape(Bs, T, D), hp, hs_new


def kernel(x_prompt, x_sample, cache_a_k, cache_a_v, cache_a_kidx, cache_c_k, cache_c_v, state_b_re, state_b_im, state_ffn_conv, norm_mix, norm_ffn, a_w_in, a_w_out, a_q_norm, a_k_norm, b_a_re, b_a_im, b_log_dt, b_b_re, b_b_im, b_c_re, b_c_im, b_d, b_w_glu, c_w_in, c_w_out, c_q_norm, c_k_norm, c_rel_bias, ffn_w_up, ffn_conv_w, ffn_conv_b, ffn_w_down):
    xp, xs = x_prompt, x_sample
    depth = norm_mix.shape[0]
    outs_a, outs_b, outs_c, f_p, f_s = [], [], [], [], []
    ffn_w_up, ffn_w_down = ffn_w_up.astype(BF16), ffn_w_down.astype(BF16)
    ffn_conv_b = ffn_conv_b[:, None]
    for i in range(depth):
        li, kind = i // 3, i % 3
        if kind == 0:
            xp, xs, o = _mixer_a(xp, xs, norm_mix[i], a_w_in[li], a_w_out[li], a_q_norm[li],
                                 a_k_norm[li], (cache_a_k, cache_a_v, cache_a_kidx), li)
            outs_a.append(o)
        elif kind == 1:
            ops = _s5_operators(b_a_re[li], b_a_im[li], b_log_dt[li], b_b_re[li], b_b_im[li],
                                b_c_re[li], b_c_im[li])
            xp, xs, o = _mixer_b(xp, xs, norm_mix[i], ops, b_d[li], b_w_glu[li],
                                 state_b_re[li], state_b_im[li])
            outs_b.append(o)
        else:
            xp, xs, o = _mixer_c(xp, xs, norm_mix[i], c_w_in[li], c_w_out[li], c_q_norm[li],
                                 c_k_norm[li], c_rel_bias[li], cache_c_k[li], cache_c_v[li])
            outs_c.append(o)
        xp, xs, hp, hs = _conv_ffn(xp, xs, norm_ffn[i], ffn_w_up, ffn_conv_w, ffn_conv_b,
                                   ffn_w_down, state_ffn_conv[i], i)
        f_p.append(hp)
        f_s.append(hs)
    stack = lambda outs, j: jnp.stack([o[j] for o in outs])
    return (xp, xs,
            *[stack(outs_a, j) for j in range(6)],
            *[stack(outs_c, j) for j in range(4)],
            *[stack(outs_b, j) for j in range(4)],
            jnp.stack(f_p), jnp.stack(f_s))
```

```python
import functools
import math

import jax
import jax.numpy as jnp
from jax import lax
from jax.experimental import pallas as pl
from jax.experimental.pallas import tpu as pltpu

F32 = jnp.float32
BF16 = jnp.bfloat16
I32 = jnp.int32

EPS = 1e-6
NEG_INF = -1e30
ROW_MAX_INIT = -1e29
THR_BITS_PER_CHECK = 4
ROPE_THETA = 10000.0
CHUNK = 64
HEAD_DIM = 128
LANES = 128

A_HEADS = 16
A_KV_HEADS = 4
A_GROUP = A_HEADS // A_KV_HEADS
IDX_HEADS = 8
IDX_DIM = 64
TOPK_MAX = 256
A_Q = A_HEADS * HEAD_DIM
A_KV = A_KV_HEADS * HEAD_DIM
A_QI = IDX_HEADS * IDX_DIM
IDX_W_SCALE = (IDX_HEADS * IDX_DIM) ** -0.5
A_OFF_K = A_Q
A_OFF_V = A_OFF_K + A_KV
A_OFF_QI = A_OFF_V + A_KV
A_OFF_KI = A_OFF_QI + A_QI
A_OFF_WI = A_OFF_KI + LANES
A_IN_PAD = A_OFF_WI + LANES

S5_GROUP = 16
S5_STATE = 64
S5_CHUNK = 16
S5_LB_GROUPS = LANES // S5_GROUP

C_HEADS = 16
C_BAND_CHUNKS = 8
C_REACH = C_BAND_CHUNKS * CHUNK
REL_CLIP = 128
CONV_W = 3

VMEM_LIMIT = 56 * 1024 * 1024


def _cparams(*sem):
    return pltpu.CompilerParams(dimension_semantics=sem, vmem_limit_bytes=VMEM_LIMIT)


def _pick(n, target, mult):
    best = None
    for t in range(mult, min(n, target) + 1, mult):
        if n % t == 0:
            best = t
    return best if best is not None else n


def _dot(a, b):
    return jnp.dot(a, b, preferred_element_type=F32)


def _dot_nt(a, b):
    return lax.dot_general(a, b, (((1,), (1,)), ((), ())), preferred_element_type=F32)


def _rms(x, g):
    return x * lax.rsqrt(jnp.mean(x * x, axis=-1, keepdims=True) + EPS) * g


def _rms_matmul_kernel(x_ref, g_ref, w_ref, o_ref, xn_ref):
    @pl.when(pl.program_id(1) == 0)
    def _():
        xn_ref[...] = _rms(x_ref[...], g_ref[...]).astype(BF16)

    o_ref[...] = _dot(xn_ref[...], w_ref[...]).astype(o_ref.dtype)


def _rms_matmul(x, g, w, *, tm=1024, tn=1024, out_dtype=F32):
    M, D = x.shape
    N = w.shape[1]
    tm = _pick(M, tm, 16)
    tn = _pick(N, tn, LANES)
    return pl.pallas_call(
        _rms_matmul_kernel,
        grid=(M // tm, N // tn),
        in_specs=[pl.BlockSpec((tm, D), lambda i, j: (i, 0)),
                  pl.BlockSpec((1, D), lambda i, j: (0, 0)),
                  pl.BlockSpec((D, tn), lambda i, j: (0, j))],
        out_specs=pl.BlockSpec((tm, tn), lambda i, j: (i, j)),
        out_shape=jax.ShapeDtypeStruct((M, N), out_dtype),
        scratch_shapes=[pltpu.VMEM((tm, D), BF16)],
        compiler_params=_cparams("parallel", "arbitrary"),
        name="rms_matmul",
    )(x, g, w)


def _rmsnorm_kernel(x_ref, g_ref, o_ref):
    o_ref[...] = _rms(x_ref[...], g_ref[...])


def _rmsnorm(x, g, *, tm=512):
    M, D = x.shape
    tm = _pick(M, tm, 8)
    return pl.pallas_call(
        _rmsnorm_kernel,
        grid=(M // tm,),
        in_specs=[pl.BlockSpec((tm, D), lambda i: (i, 0)),
                  pl.BlockSpec((1, D), lambda i: (0, 0))],
        out_specs=pl.BlockSpec((tm, D), lambda i: (i, 0)),
        out_shape=jax.ShapeDtypeStruct((M, D), F32),
        compiler_params=_cparams("parallel"),
        name="rmsnorm",
    )(x, g)


def _matmul_res_kernel(x_ref, w_ref, r_ref, o_ref):
    o_ref[...] = r_ref[...] + _dot(x_ref[...], w_ref[...])


def _matmul_res(x, w, res, *, tm=512, tn=2048):
    M, K = x.shape
    N = w.shape[1]
    tm = _pick(M, tm, 16)
    tn = _pick(N, tn, LANES)
    return pl.pallas_call(
        _matmul_res_kernel,
        grid=(M // tm, N // tn),
        in_specs=[pl.BlockSpec((tm, K), lambda i, j: (i, 0)),
                  pl.BlockSpec((K, tn), lambda i, j: (0, j)),
                  pl.BlockSpec((tm, tn), lambda i, j: (i, j))],
        out_specs=pl.BlockSpec((tm, tn), lambda i, j: (i, j)),
        out_shape=jax.ShapeDtypeStruct((M, N), F32),
        compiler_params=_cparams("parallel", "arbitrary"),
        name="matmul_res",
    )(x, w, res)


def _rope_tables(pos, dim):
    half = dim // 2
    inv = jnp.power(ROPE_THETA, -jnp.arange(half, dtype=F32) / half)
    ang = pos.astype(F32)[:, None] * inv[None, :]
    cos, sin = jnp.cos(ang), jnp.sin(ang)
    reps = LANES // dim
    return (jnp.tile(jnp.concatenate([cos, cos], axis=-1), (1, reps)),
            jnp.tile(jnp.concatenate([-sin, sin], axis=-1), (1, reps)))


def _a_post_kernel(h_ref, c128_ref, s128_ref, c64_ref, s64_ref, qg_ref, kg_ref,
                   q_ref, kf_ref, vf_ref, kb_ref, vb_ref, qi_ref, kif_ref, kib_ref, wi_ref):
    c128, s128 = c128_ref[...], s128_ref[...]
    c64, s64 = c64_ref[...], s64_ref[...]
    tm = h_ref.shape[0]
    lane = lax.broadcasted_iota(I32, (tm, LANES), 1)

    def rope128(x):
        return x * c128 + pltpu.roll(x, HEAD_DIM // 2, axis=1) * s128

    def rope64(x):
        lo = (lane % IDX_DIM) < (IDX_DIM // 2)
        partner = jnp.where(lo, pltpu.roll(x, LANES - IDX_DIM // 2, axis=1),
                            pltpu.roll(x, IDX_DIM // 2, axis=1))
        return x * c64 + partner * s64

    for h in range(A_HEADS):
        x = h_ref[:, h * LANES:(h + 1) * LANES]
        q_ref[:, h * LANES:(h + 1) * LANES] = (
            rope128(_rms(x, qg_ref[...])) * (HEAD_DIM ** -0.5)).astype(BF16)
    for h in range(A_KV_HEADS):
        x = h_ref[:, A_OFF_K + h * LANES:A_OFF_K + (h + 1) * LANES]
        k = rope128(_rms(x, kg_ref[...]))
        kf_ref[:, h, :] = k
        kb_ref[:, h * LANES:(h + 1) * LANES] = k.astype(BF16)
    ones = jnp.ones((tm, LANES), BF16)
    for h in range(A_KV_HEADS):
        v = h_ref[:, A_OFF_V + h * LANES:A_OFF_V + (h + 1) * LANES]
        vf_ref[:, h, :] = v
        vb_ref[:, 2 * h * LANES:(2 * h + 1) * LANES] = v.astype(BF16)
        vb_ref[:, (2 * h + 1) * LANES:(2 * h + 2) * LANES] = ones
    for p in range(A_QI // LANES):
        x = rope64(h_ref[:, A_OFF_QI + p * LANES:A_OFF_QI + (p + 1) * LANES])
        qi_ref[:, (2 * p) * LANES:(2 * p + 1) * LANES] = (
            jnp.where(lane < IDX_DIM, x, 0.0).astype(BF16))
        qi_ref[:, (2 * p + 1) * LANES:(2 * p + 2) * LANES] = (
            jnp.where(lane >= IDX_DIM, x, 0.0).astype(BF16))
    ki = rope64(h_ref[:, A_OFF_KI:A_OFF_KI + LANES])
    kif_ref[...] = ki
    kib_ref[...] = (ki + pltpu.roll(ki, IDX_DIM, axis=1)).astype(BF16)
    wi_ref[...] = h_ref[:, A_OFF_WI:A_OFF_WI + LANES] * IDX_W_SCALE


def _a_post(h, tabs, qg, kg, *, tm=512):
    M = h.shape[0]
    P = tabs[0].shape[0]
    tm = _pick(math.gcd(M, P), tm, 16)
    nt = P // tm
    row = lambda i: (i, 0)
    tab = lambda i: (i % nt, 0)
    one = lambda i: (0, 0)
    widths = [(A_KV, BF16), (2 * A_KV, BF16),
              (IDX_HEADS * LANES, BF16), (LANES, F32), (LANES, BF16), (LANES, F32)]
    heads = (A_KV_HEADS, HEAD_DIM)
    return pl.pallas_call(
        _a_post_kernel,
        grid=(M // tm,),
        in_specs=[pl.BlockSpec((tm, A_IN_PAD), row)]
        + [pl.BlockSpec((tm, LANES), tab)] * 4
        + [pl.BlockSpec((1, LANES), one)] * 2,
        out_specs=[pl.BlockSpec((tm, A_Q), row)]
        + [pl.BlockSpec((tm,) + heads, lambda i: (i, 0, 0))] * 2
        + [pl.BlockSpec((tm, w), row) for w, _ in widths],
        out_shape=[jax.ShapeDtypeStruct((M, A_Q), BF16)]
        + [jax.ShapeDtypeStruct((M,) + heads, F32)] * 2
        + [jax.ShapeDtypeStruct((M, w), dt) for w, dt in widths],
        compiler_params=_cparams("parallel"),
        name="a_post",
    )(h, *tabs, qg, kg)


def _dsa_kernel(*refs, tq, tk, ksel, causal, l_valid, n_chunks, idx_bits):
    if causal:
        q_ref, qi_ref, wi_ref, k_ref, v_ref, ki_ref, o_ref = refs[:7]
    else:
        q_ref, qi_ref, wi_ref, k_ref, v_ref, ki_ref, ck_ref, cv_ref, cki_ref, o_ref = refs[:10]
    key_scr, bias_scr, qst_scr, m_scr, acc_scr = refs[-5:]
    t0 = pl.program_id(1) * tq
    nch = (t0 + tq + tk - 1) // tk if causal else n_chunks
    nslab = tk // LANES
    row = lax.broadcasted_iota(I32, (tq, tk), 0)
    col = lax.broadcasted_iota(I32, (tq, tk), 1)
    lane = lax.broadcasted_iota(I32, (tq, LANES), 1)
    limit = ((t0 + row) // CHUNK + 1) * CHUNK if causal else l_valid

    def chunk_off(c):
        return c * tk if isinstance(c, int) else pl.multiple_of(c * tk, tk)

    def cached(c):
        return not causal and c < n_chunks - 1

    def new_off(c):
        return chunk_off(c) if causal else 0

    def load_ki(c):
        if cached(c):
            x = cki_ref[0, c * tk:(c + 1) * tk, :]
            return jnp.concatenate([x, x], axis=1).astype(BF16)
        return ki_ref[0, pl.ds(new_off(c), tk), :]

    def load_k(c, j):
        if cached(c):
            return ck_ref[0, c * tk:(c + 1) * tk, j, :].astype(BF16)
        return k_ref[0, pl.ds(new_off(c), tk), j * LANES:(j + 1) * LANES]

    def load_v(c, j):
        if cached(c):
            v = cv_ref[0, c * tk:(c + 1) * tk, j, :].astype(BF16)
            return jnp.concatenate([v, jnp.ones((tk, LANES), BF16)], axis=1)
        return v_ref[0, pl.ds(new_off(c), tk), 2 * j * LANES:2 * (j + 1) * LANES]

    def chunk_loop(body):
        if causal:
            lax.fori_loop(0, nch, lambda c, carry: (body(c), carry)[1], 0)
        else:
            for c in range(n_chunks):
                body(c)

    wi = wi_ref[0]

    def score_body(c):
        off = chunk_off(c)
        kic = load_ki(c)
        shs = [_dot_nt(qi_ref[0, :, h * LANES:(h + 1) * LANES], kic) for h in range(IDX_HEADS)]
        s = jnp.zeros((tq, tk), F32)
        for h in range(IDX_HEADS):
            s = s + wi[:, h:h + 1] * jnp.maximum(shs[h], 0.0)
        s = jnp.where(s == 0.0, 0.0, s)
        s = jnp.where(off + col < limit, s, NEG_INF)
        bits = pltpu.bitcast(s, I32)
        key_scr[:, pl.ds(off, tk)] = bits ^ ((bits >> 31) & 0x7FFFFFFF)

    chunk_loop(score_body)

    def count(pred):
        def body(c, acc):
            off = chunk_off(c)
            blk = key_scr[:, pl.ds(off, tk)]
            for s in range(nslab):
                sl = slice(s * LANES, (s + 1) * LANES)
                acc = acc + pred(blk[:, sl], lane + (off + s * LANES))
            return acc

        acc = lax.fori_loop(0, nch, body, jnp.zeros((tq, LANES), F32))
        return jnp.broadcast_to(jnp.sum(acc, axis=-1, keepdims=True), (tq, LANES))

    def thr_cond(st):
        i, _, n_ge = st
        return jnp.logical_and(i < 32, jnp.max(n_ge) > ksel)

    def thr_body(st):
        i, thr, n_ge = st
        for b in range(THR_BITS_PER_CHECK):
            cand = thr + lax.shift_left(jnp.int32(1), 31 - i - b)
            cnt = count(lambda key, idx: jnp.where(key >= cand, 1.0, 0.0))
            take = cnt >= ksel
            thr, n_ge = jnp.where(take, cand, thr), jnp.where(take, cnt, n_ge)
        return i + THR_BITS_PER_CHECK, thr, n_ge

    n_keys = (nch * tk).astype(F32) if causal else float(nch * tk)
    _, thr, n_ge = lax.while_loop(
        thr_cond, thr_body,
        (jnp.int32(0), jnp.full((tq, LANES), -2 ** 31, I32), jnp.full((tq, LANES), n_keys, F32)))

    n_gt = count(lambda key, idx: jnp.where(key > thr, 1.0, 0.0))
    need = ksel - n_gt
    n_eq = n_ge - n_gt

    def cut_search():
        def body(i, cut):
            cand = cut + lax.shift_left(jnp.int32(1), idx_bits - 1 - i)
            below = count(lambda key, idx: jnp.where(key == thr, jnp.where(idx < cand, 1.0, 0.0), 0.0))
            return jnp.where(below <= need, cand, cut)

        return lax.fori_loop(0, idx_bits, body, jnp.zeros((tq, LANES), I32))

    cut = lax.cond(jnp.max(n_eq - need) > 0.0, cut_search,
                   lambda: jnp.full((tq, LANES), 2 ** idx_bits, I32))

    thr_t = jnp.tile(thr, (1, nslab))
    cut_t = jnp.tile(cut, (1, nslab))

    def bias_body(c, carry):
        off = chunk_off(c)
        key = key_scr[:, pl.ds(off, tk)]
        idx = off + col
        tie = jnp.where(key == thr_t, jnp.where(idx < cut_t, 0.0, NEG_INF), NEG_INF)
        bias = jnp.where(key > thr_t, 0.0, tie)
        bias_scr[:, pl.ds(off, tk)] = jnp.where(idx < limit, bias, NEG_INF)
        return carry

    lax.fori_loop(0, nch, bias_body, 0)

    for j in range(A_KV_HEADS):
        for g in range(A_GROUP):
            h = j * A_GROUP + g
            qst_scr[j, g * tq:(g + 1) * tq, :] = q_ref[0, :, h * LANES:(h + 1) * LANES]
    m_scr[...] = jnp.full(m_scr.shape, ROW_MAX_INIT, F32)
    acc_scr[...] = jnp.zeros(acc_scr.shape, F32)
    gr = A_GROUP * tq

    def attn_body(c):
        off = chunk_off(c)
        bias = bias_scr[:, pl.ds(off, tk)]
        logits = [_dot_nt(qst_scr[j], load_k(c, j)) for j in range(A_KV_HEADS)]
        for j in range(A_KV_HEADS):
            rows = slice(j * gr, (j + 1) * gr)
            m_old = m_scr[rows, :]
            alphas, ps = [], []
            for g in range(A_GROUP):
                sl = slice(g * tq, (g + 1) * tq)
                sg = logits[j][sl, :] + bias
                m_new = jnp.maximum(m_old[sl, :], jnp.max(sg, axis=-1, keepdims=True))
                alphas.append(jnp.exp(m_old[sl, :] - m_new))
                ps.append(jnp.exp((sg - jnp.tile(m_new, (1, nslab))).astype(BF16)))
                m_scr[j * gr + g * tq:j * gr + (g + 1) * tq, :] = m_new
            alpha = jnp.tile(jnp.concatenate(alphas, axis=0), (1, 2))
            pv = _dot(jnp.concatenate(ps, axis=0), load_v(c, j))
            acc_scr[rows, :] = alpha * acc_scr[rows, :] + pv

    chunk_loop(attn_body)
    for h in range(A_HEADS):
        rows = slice(h * tq, (h + 1) * tq)
        o_ref[0, :, h * LANES:(h + 1) * LANES] = (
            acc_scr[rows, :LANES] / acc_scr[rows, LANES:]).astype(BF16)


def _dsa(q, qi, wi, k, v, ki, cache=None, *, tq, tk, ksel, causal, l_valid):
    B, T, _ = q.shape
    Lk = k.shape[1]
    blk_q = lambda w: pl.BlockSpec((1, tq, w), lambda b, i: (b, i, 0))
    blk_k = lambda n, w: pl.BlockSpec((1, n, w), lambda b, i: (b, 0, 0))
    specs, args = [], []
    L = Lk
    if cache is not None:
        ck, cv, cki, li = cache
        past = ck.shape[2]
        L = past + Lk
        blk_c = pl.BlockSpec((None, 1, past, A_KV_HEADS, HEAD_DIM),
                             lambda b, i: (li, b, 0, 0, 0))
        specs = [blk_c, blk_c,
                 pl.BlockSpec((None, 1, past, IDX_DIM), lambda b, i: (li, b, 0, 0))]
        args = [ck, cv, cki]
    kern = functools.partial(
        _dsa_kernel, tq=tq, tk=tk, ksel=ksel, causal=causal, l_valid=l_valid,
        n_chunks=-(-l_valid // tk), idx_bits=max(1, (L - 1).bit_length()) + 1)
    return pl.pallas_call(
        kern,
        grid=(B, T // tq),
        in_specs=[blk_q(A_Q), blk_q(IDX_HEADS * LANES), blk_q(LANES),
                  blk_k(Lk, A_KV), blk_k(Lk, 2 * A_KV), blk_k(Lk, LANES)] + specs,
        out_specs=blk_q(A_Q),
        out_shape=jax.ShapeDtypeStruct((B, T, A_Q), BF16),
        scratch_shapes=[pltpu.VMEM((tq, L), I32), pltpu.VMEM((tq, L), F32),
                        pltpu.VMEM((A_KV_HEADS, A_GROUP * tq, LANES), BF16),
                        pltpu.VMEM((A_HEADS * tq, LANES), F32),
                        pltpu.VMEM((A_HEADS * tq, 2 * LANES), F32)],
        compiler_params=_cparams("parallel", "arbitrary"),
        name="dsa_causal" if causal else "dsa_cached",
    )(q, qi, wi, k, v, ki, *args)


def _pack_a_w_in(w):
    D = w.shape[0]
    o_ki = A_Q + 2 * A_KV + A_QI
    pad = lambda n: jnp.zeros((D, n), w.dtype)
    return jnp.concatenate(
        [w[:, :o_ki], w[:, o_ki:o_ki + IDX_DIM], pad(LANES - IDX_DIM),
         w[:, o_ki + IDX_DIM:], pad(LANES - IDX_HEADS)], axis=1).astype(BF16)


def _mixer_a(xp, xs, g, w_in, w_out, qg, kg, caches, li):
    B, S, D = xp.shape
    Bs, T, _ = xs.shape
    past = caches[0].shape[2]
    w_in = _pack_a_w_in(w_in)
    w_out = w_out.astype(BF16)
    g, qg, kg = g[None], qg[None], kg[None]

    def project(x, pos):
        b, t, _ = x.shape
        h = _rms_matmul(x.reshape(b * t, D), g, w_in, tn=A_IN_PAD // 5)
        tabs = _rope_tables(pos, HEAD_DIM) + _rope_tables(pos, IDX_DIM)
        q, kf, vf, *rest = _a_post(h, tabs, qg, kg)
        heads = (b, t, A_KV_HEADS, HEAD_DIM)
        return [q.reshape(b, t, -1), kf.reshape(heads), vf.reshape(heads)] + [
            o.reshape(b, t, -1) for o in rest]

    q, k_p, v_p, kb, vb, qi, kif, kib, wi = project(xp, jnp.arange(S))
    tk = _pick(S, 512, LANES)
    o = _dsa(q, qi, wi, kb, vb, kib, tq=min(128, S), tk=tk, ksel=min(TOPK_MAX, S // 4),
             causal=True, l_valid=S)
    yp = _matmul_res(o.reshape(B * S, A_Q), w_out, xp.reshape(B * S, D)).reshape(B, S, D)
    ki_p = kif[..., :IDX_DIM]

    pos_s = jnp.tile(past + jnp.arange(T), Bs)
    q, k_s, v_s, kb, vb, qi, kif, kib, wi = project(xs, pos_s)
    L = past + T
    tk = _pick(past, 512, LANES)
    padk = lambda a: jnp.pad(a, ((0, 0), (0, tk - T), (0, 0)))
    o = _dsa(q, qi, wi, padk(kb), padk(vb), padk(kib), caches + (li,),
             tq=T, tk=tk, ksel=min(TOPK_MAX, L // 4), causal=False, l_valid=L)
    ys = _matmul_res(o.reshape(Bs * T, A_Q), w_out, xs.reshape(Bs * T, D)).reshape(Bs, T, D)
    return yp, ys, (k_p, v_p, ki_p, k_s, v_s, kif[..., :IDX_DIM])


def _c_post_kernel(h_ref, qg_ref, kg_ref, q_ref, kb_ref, kf_ref, vf_ref):
    W = C_HEADS * HEAD_DIM
    for h in range(C_HEADS):
        sl = slice(h * LANES, (h + 1) * LANES)
        q = _rms(h_ref[:, sl].astype(F32), qg_ref[...])
        q_ref[:, sl] = (q * (HEAD_DIM ** -0.5)).astype(BF16)
        k = _rms(h_ref[:, W + h * LANES:W + (h + 1) * LANES].astype(F32), kg_ref[...])
        kf_ref[:, sl] = k
        kb_ref[:, sl] = k.astype(BF16)
    vf_ref[...] = h_ref[:, 2 * W:3 * W].astype(F32)


def _c_post(h, qg, kg, *, t, keep, tm=512):
    M = h.shape[0]
    W = C_HEADS * HEAD_DIM
    tm = _pick(math.gcd(t, keep), tm, 16) if keep < t else _pick(M, tm, 16)
    row = lambda i: (i, 0)
    if keep < t:
        tps, kps = t // tm, keep // tm
        tail = lambda i: ((i // tps) * kps + jnp.maximum(i % tps - (tps - kps), 0), 0)
    else:
        tail = row
    n_tail = M // t * keep
    return pl.pallas_call(
        _c_post_kernel,
        grid=(M // tm,),
        in_specs=[pl.BlockSpec((tm, 3 * W), row),
                  pl.BlockSpec((1, LANES), lambda i: (0, 0)),
                  pl.BlockSpec((1, LANES), lambda i: (0, 0))],
        out_specs=[pl.BlockSpec((tm, W), row)] * 2 + [pl.BlockSpec((tm, W), tail)] * 2,
        out_shape=[jax.ShapeDtypeStruct((M, W), BF16)] * 2
        + [jax.ShapeDtypeStruct((n_tail, W), F32)] * 2,
        compiler_params=_cparams("arbitrary"),
        name="c_post",
    )(h, qg, kg)


def _band_kernel(q_ref, k_ref, v_ref, b_ref, o_ref, *, sub, nsub, win, first_key):
    t0 = pl.program_id(2) * (sub * nsub)
    starts = [pl.multiple_of(t0 + s * sub, sub) for s in range(nsub)]
    lgs = [_dot_nt(q_ref[0, s * sub:(s + 1) * sub, :], k_ref[0, pl.ds(starts[s], win), :])
           + b_ref[0] for s in range(nsub)]
    kk = lax.broadcasted_iota(I32, (sub, win), 1)
    lgs = [jnp.where(kk >= first_key - starts[s], lgs[s], NEG_INF) for s in range(nsub)]
    ms = [jnp.max(lg, axis=-1, keepdims=True) for lg in lgs]
    ps = [jnp.exp(lg - m) for lg, m in zip(lgs, ms)]
    ls = [jnp.sum(p, axis=-1, keepdims=True) for p in ps]
    os_ = [_dot(ps[s].astype(BF16), v_ref[0, pl.ds(starts[s], win), :]) for s in range(nsub)]
    for s in range(nsub):
        o_ref[0, s * sub:(s + 1) * sub, :] = (os_[s] / ls[s]).astype(BF16)


def _band(q, k, v, bias, *, sub, nsub, first_key):
    B, T, W = q.shape
    Lk = k.shape[1]
    win = bias.shape[2]
    tq = sub * nsub
    kern = functools.partial(_band_kernel, sub=sub, nsub=nsub, win=win, first_key=first_key)
    return pl.pallas_call(
        kern,
        grid=(B, C_HEADS, T // tq),
        in_specs=[pl.BlockSpec((1, tq, LANES), lambda b, h, i: (b, i, h)),
                  pl.BlockSpec((1, Lk, LANES), lambda b, h, i: (b, 0, h)),
                  pl.BlockSpec((1, Lk, LANES), lambda b, h, i: (b, 0, h)),
                  pl.BlockSpec((1, sub, win), lambda b, h, i: (h, 0, 0))],
        out_specs=pl.BlockSpec((1, tq, LANES), lambda b, h, i: (b, i, h)),
        out_shape=jax.ShapeDtypeStruct((B, T, W), BF16),
        compiler_params=_cparams("parallel", "parallel", "arbitrary"),
        name="band_attn",
    )(q, k, v, bias)


def _band_cached_kernel(q_ref, k_ref, v_ref, ck_ref, cv_ref, b_ref, o_ref):
    T = q_ref.shape[1]
    cp = ck_ref.shape[1]
    win = b_ref.shape[2]
    pad = jnp.zeros((win - cp - T, LANES), BF16)

    def window(c_ref, n_ref, h):
        new = n_ref[0, :, h * LANES:(h + 1) * LANES]
        return jnp.concatenate([c_ref[0, :, h, :].astype(BF16), new, pad], axis=0)

    heads = range(C_HEADS)
    lgs = [_dot_nt(q_ref[0, :, h * LANES:(h + 1) * LANES], window(ck_ref, k_ref, h)) + b_ref[h]
           for h in heads]
    ms = [jnp.max(lg, axis=-1, keepdims=True) for lg in lgs]
    ps = [jnp.exp(lg - m) for lg, m in zip(lgs, ms)]
    ls = [jnp.sum(p, axis=-1, keepdims=True) for p in ps]
    for h in heads:
        o = _dot(ps[h].astype(BF16), window(cv_ref, v_ref, h)) / ls[h]
        o_ref[0, :, h * LANES:(h + 1) * LANES] = o.astype(BF16)


def _band_cached(q, k, v, ck, cv, bias):
    B, T, W = q.shape
    cp = ck.shape[1]
    row = pl.BlockSpec((1, T, W), lambda b: (b, 0, 0))
    cache = pl.BlockSpec((1, cp, C_HEADS, HEAD_DIM), lambda b: (b, 0, 0, 0))
    return pl.pallas_call(
        _band_cached_kernel,
        grid=(B,),
        in_specs=[row, row, row, cache, cache,
                  pl.BlockSpec(bias.shape, lambda b: (0, 0, 0))],
        out_specs=row,
        out_shape=jax.ShapeDtypeStruct((B, T, W), BF16),
        compiler_params=_cparams("parallel"),
        name="band_cached",
    )(q, k, v, ck, cv, bias)


def _mixer_c(xp, xs, g, w_in, w_out, qg, kg, rel, ck, cv):
    B, S, D = xp.shape
    Bs, T, _ = xs.shape
    W = C_HEADS * HEAD_DIM
    cp = ck.shape[1]
    w_in = w_in.astype(BF16)
    w_out = w_out.astype(BF16)
    g, qg, kg = g[None], qg[None], kg[None]
    relf = rel.astype(F32)

    def project(x, keep):
        b, t, _ = x.shape
        h = _rms_matmul(x.reshape(b * t, D), g, w_in, out_dtype=BF16)
        q, kb, kf, vf = _c_post(h, qg, kg, t=t, keep=keep)
        vb = h.reshape(b, t, 3 * W)[:, :, 2 * W:]
        tail = lambda a: a.reshape(b, keep, C_HEADS, HEAD_DIM)
        return q.reshape(b, t, W), tail(kf), kb.reshape(b, t, W), vb, tail(vf)

    def rel_bias(rows, win, shift):
        n = rows + win - 1
        j = jnp.arange(n)
        vec = relf[:, jnp.clip(shift + rows - 1 - j, -REL_CLIP, REL_CLIP) + REL_CLIP]
        flat = jnp.tile(jnp.pad(vec, ((0, 0), (0, 1))), (1, rows))
        toep = flat[:, :rows * n].reshape(-1, rows, n)
        return toep[:, :, rows - 1:rows - 1 + win]

    sub = min(2 * CHUNK, S)
    win = C_REACH + sub
    r = jnp.arange(sub)[:, None]
    kk = jnp.arange(win)[None, :]
    lo = (r // CHUNK) * CHUNK
    bias = jnp.where((kk >= lo) & (kk < lo + C_REACH + CHUNK), rel_bias(sub, win, C_REACH),
                     NEG_INF)
    q, k_p, kb, vb, v_p = project(xp, min(C_REACH, S))
    padk = lambda a: jnp.pad(a, ((0, 0), (C_REACH, 0), (0, 0)))
    o = _band(q, padk(kb), padk(vb), bias, sub=sub, nsub=_pick(S // sub, 16, 1),
              first_key=C_REACH)
    yp = _matmul_res(o.reshape(B * S, W), w_out, xp.reshape(B * S, D)).reshape(B, S, D)

    L = cp + T
    win = -(-L // LANES) * LANES
    bias = jnp.where(jnp.arange(win)[None, :] < L, rel_bias(T, win, cp), NEG_INF)
    q, kf, kb, vb, vf = project(xs, T)
    o = _band_cached(q, kb, vb, ck, cv, bias)
    ys = _matmul_res(o.reshape(Bs * T, W), w_out, xs.reshape(Bs * T, D)).reshape(Bs, T, D)
    return yp, ys, (k_p, v_p, kf, vf)


def _s5_ops_kernel(ar_ref, ai_ref, ldt_ref, br_ref, bi_ref, cr_ref, ci_ref, cr4_ref, ci4_ref,
                   lag_ref, xr_ref, xi_ref, or_ref, oi_ref, lnr_ref, lni_ref, sr_ref, si_ref):
    n, P = xr_ref.shape[:2]
    dt = jnp.exp(ldt_ref[...])
    ar, ai = ar_ref[...], ai_ref[...]
    mag = jnp.exp(ar * dt)
    lr, li = mag * jnp.cos(ai * dt), mag * jnp.sin(ai * dt)
    den = ar * ar + ai * ai
    nr = lr - 1.0
    fr = (nr * ar + li * ai) / den
    fi = (li * ar - nr * ai) / den
    br, bi = br_ref[...], bi_ref[...]
    bbr = fr * br - fi * bi
    bbi = fr * bi + fi * br
    cr, ci = cr_ref[...], ci_ref[...]
    pr, pi = jnp.ones_like(lr), jnp.zeros_like(lr)
    for tau in range(n):
        sr_ref[...] = pr * bbr - pi * bbi
        si_ref[...] = pr * bbi + pi * bbr
        xr_ref[tau] = sr_ref[...].astype(BF16)
        xi_ref[tau] = si_ref[...].astype(BF16)

        def body(p, acc):
            return acc + cr4_ref[p] * sr_ref[p][None] - ci4_ref[p] * si_ref[p][None]

        lag_ref[tau] = lax.fori_loop(0, P, body, jnp.zeros(lag_ref.shape[1:], F32))
        pr, pi = pr * lr - pi * li, pr * li + pi * lr
        or_ref[tau] = (cr * pr - ci * pi).astype(BF16)
        oi_ref[tau] = (-(cr * pi + ci * pr)).astype(BF16)
    lnr_ref[...] = pr
    lni_ref[...] = pi


def _s5_operators(a_re, a_im, log_dt, b_re, b_im, c_re, c_im):
    G, P = a_re.shape
    C = S5_GROUP
    n = S5_CHUNK
    nlb = G // S5_LB_GROUPS
    f = lambda a: a.astype(F32)
    a_t = lambda a: f(a).T.reshape(P, 1, G)
    b_t = lambda a: jnp.transpose(f(a), (1, 2, 0))
    c_t = lambda a: jnp.transpose(f(a), (2, 1, 0))
    crt, cit = c_t(c_re), c_t(c_im)
    sds = lambda *shape: jax.ShapeDtypeStruct(shape, F32)
    lag, xr, xi, o_re, o_im, lnr, lni = pl.pallas_call(
        _s5_ops_kernel,
        out_shape=[sds(n, C, C, G)] + [jax.ShapeDtypeStruct((n, P, C, G), BF16)] * 4
        + [sds(P, 1, G)] * 2,
        scratch_shapes=[pltpu.VMEM((P, C, G), F32)] * 2,
        compiler_params=pltpu.CompilerParams(vmem_limit_bytes=VMEM_LIMIT),
        name="s5_ops",
    )(a_t(a_re), a_t(a_im), f(log_dt).reshape(1, 1, G), b_t(b_re), b_t(b_im), crt, cit,
      crt.reshape(P, C, 1, G), cit.reshape(P, C, 1, G))

    split = lambda a: a.reshape(a.shape[:-1] + (nlb, S5_LB_GROUPS))
    taps = jnp.transpose(split(lag), (3, 0, 4, 2, 1)).reshape(nlb, n, LANES, C)
    taps = jnp.pad(taps.astype(BF16), ((0, 0), (0, 0), (0, 0), (0, LANES - C)))

    def inc(x):
        w = jnp.transpose(split(x[::-1]), (3, 4, 0, 2, 1))
        return jnp.concatenate([w, w], axis=-1)

    def out(x):
        return jnp.transpose(split(x), (3, 4, 1, 0, 2)).reshape(nlb, S5_LB_GROUPS, P, n * C)

    lam_n = tuple(a.reshape(P, G).T.reshape(1, 1, G * P) for a in (lnr, lni))
    return taps, inc(xr), inc(xi), out(o_re), out(o_im), lam_n


def _chunk_rows(u_ref, tc):
    return jnp.concatenate(
        [u_ref[pl.ds(t, tc, stride=S5_CHUNK), :] for t in range(S5_CHUNK)], axis=1)


def _s5_inc_kernel(u_ref, xr_ref, xi_ref, sr_ref, si_ref, wr_ref, wi_ref):
    n, C = S5_CHUNK, S5_GROUP
    P = wr_ref.shape[1] // S5_LB_GROUPS

    @pl.when(pl.program_id(1) == 0)
    def _():
        for x_ref, w_ref in ((xr_ref, wr_ref), (xi_ref, wi_ref)):
            w_ref[...] = jnp.zeros(w_ref.shape, BF16)
            for g in range(S5_LB_GROUPS):
                half = (g % 2) * P
                for t in range(n):
                    r0 = t * LANES + g * C
                    w_ref[r0:r0 + C, g * P:(g + 1) * P] = x_ref[0, g, t, :, half:half + P]

    u = _chunk_rows(u_ref, sr_ref.shape[0]).astype(BF16)
    sr_ref[...] = _dot(u, wr_ref[...])
    si_ref[...] = _dot(u, wi_ref[...])


def _s5_inc(u, x_re, x_im, *, tc):
    Nc = u.shape[0] // S5_CHUNK
    nlb = x_re.shape[0]
    K = S5_CHUNK * LANES
    Wn = S5_LB_GROUPS * (x_re.shape[-1] // 2)
    xspec = pl.BlockSpec((1,) + x_re.shape[1:], lambda lb, i: (lb, 0, 0, 0, 0))
    ospec = pl.BlockSpec((tc, Wn), lambda lb, i: (i, lb))
    return pl.pallas_call(
        _s5_inc_kernel,
        grid=(nlb, Nc // tc),
        in_specs=[pl.BlockSpec((tc * S5_CHUNK, LANES), lambda lb, i: (i, lb)), xspec, xspec],
        out_specs=[ospec, ospec],
        out_shape=[jax.ShapeDtypeStruct((Nc, nlb * Wn), F32)] * 2,
        scratch_shapes=[pltpu.VMEM((K, Wn), BF16)] * 2,
        compiler_params=_cparams("parallel", "arbitrary"),
        name="s5_inc",
    )(u, x_re, x_im)


def _s5_scan_kernel(sr_ref, si_ref, lr_ref, li_ref, h0r_ref, h0i_ref,
                    pr_ref, pi_ref, fr_ref, fi_ref):
    lr, li = lr_ref[...], li_ref[...]
    nck = sr_ref.shape[1]

    def body(k, carry):
        hr, hi = carry
        pr_ref[:, pl.ds(k, 1), :] = hr
        pi_ref[:, pl.ds(k, 1), :] = hi
        sr = sr_ref[:, pl.ds(k, 1), :]
        si = si_ref[:, pl.ds(k, 1), :]
        return lr * hr - li * hi + sr, lr * hi + li * hr + si

    hr, hi = lax.fori_loop(0, nck, body, (h0r_ref[...], h0i_ref[...]))
    fr_ref[...] = hr
    fi_ref[...] = hi


def _s5_scan(sr, si, lam, h0r, h0i, *, cb=1024):
    B, nck, GP = sr.shape
    cb = _pick(GP, cb, LANES)
    full = pl.BlockSpec((B, nck, cb), lambda j: (0, 0, j))
    vec = pl.BlockSpec((1, 1, cb), lambda j: (0, 0, j))
    st = pl.BlockSpec((B, 1, cb), lambda j: (0, 0, j))
    return pl.pallas_call(
        _s5_scan_kernel,
        grid=(GP // cb,),
        in_specs=[full, full, vec, vec, st, st],
        out_specs=[full, full, st, st],
        out_shape=[jax.ShapeDtypeStruct((B, nck, GP), F32)] * 2
        + [jax.ShapeDtypeStruct((B, 1, GP), F32)] * 2,
        compiler_params=_cparams("parallel"),
        name="s5_scan",
    )(sr, si, lam[0], lam[1], h0r, h0i)


def _s5_out_kernel(u_ref, taps_ref, hr_ref, hi_ref, ar_ref, ai_ref, d_ref, o_ref,
                   toep_ref, wr_ref, wi_ref):
    n, C = S5_CHUNK, S5_GROUP
    tc = hr_ref.shape[0]
    P = ar_ref.shape[2]

    @pl.when(pl.program_id(1) == 0)
    def _():
        row = lax.broadcasted_iota(I32, (LANES, LANES), 0)
        lane = lax.broadcasted_iota(I32, (LANES, LANES), 1)
        spread = jnp.where(lane % C == row, 1.0, 0.0).astype(BF16)
        same_group = row // C == lane // C
        zero = jnp.zeros((LANES, LANES), BF16)
        for tau in range(n):
            tap = jnp.where(same_group, _dot(taps_ref[0, tau], spread), 0.0).astype(BF16)
            for t in range(n - tau):
                v = t + tau
                toep_ref[t * LANES:(t + 1) * LANES, v * LANES:(v + 1) * LANES] = tap
            for t in range(tau):
                toep_ref[tau * LANES:(tau + 1) * LANES, t * LANES:(t + 1) * LANES] = zero
        r2 = lax.broadcasted_iota(I32, (n * C, n * LANES), 0)
        c2 = lax.broadcasted_iota(I32, (n * C, n * LANES), 1)
        for g in range(S5_LB_GROUPS):
            place = jnp.where(c2 == (r2 // C) * LANES + g * C + r2 % C, 1.0, 0.0).astype(BF16)
            wr_ref[g * P:(g + 1) * P, :] = _dot(ar_ref[0, g], place).astype(BF16)
            wi_ref[g * P:(g + 1) * P, :] = _dot(ai_ref[0, g], place).astype(BF16)

    u = _chunk_rows(u_ref, tc)
    y = _dot(u.astype(BF16), toep_ref[...])
    y = y + _dot(hr_ref[...].astype(BF16), wr_ref[...]) + _dot(hi_ref[...].astype(BF16), wi_ref[...])
    y = jax.nn.gelu(y + jnp.tile(d_ref[...], (1, n)) * u)
    for t in range(n):
        o_ref[pl.ds(t, tc, stride=n), :] = y[:, t * LANES:(t + 1) * LANES]


def _s5_out(u, taps, hr, hi, a_re, a_im, d, *, tc):
    M, D = u.shape
    n = S5_CHUNK
    nlb, ng, P, _ = a_re.shape
    Wn, K = ng * P, n * LANES
    hspec = pl.BlockSpec((tc, Wn), lambda lb, i: (i, lb))
    uspec = pl.BlockSpec((tc * n, LANES), lambda lb, i: (i, lb))
    aspec = pl.BlockSpec((1,) + a_re.shape[1:], lambda lb, i: (lb, 0, 0, 0))
    return pl.pallas_call(
        _s5_out_kernel,
        grid=(nlb, M // (tc * n)),
        in_specs=[uspec,
                  pl.BlockSpec((1, n, LANES, LANES), lambda lb, i: (lb, 0, 0, 0)), hspec, hspec,
                  aspec, aspec,
                  pl.BlockSpec((1, LANES), lambda lb, i: (0, lb))],
        out_specs=uspec,
        out_shape=jax.ShapeDtypeStruct((M, D), F32),
        scratch_shapes=[pltpu.VMEM((K, K), BF16), pltpu.VMEM((Wn, K), BF16),
                        pltpu.VMEM((Wn, K), BF16)],
        compiler_params=_cparams("parallel", "arbitrary"),
        name="s5_out",
    )(u, taps, hr, hi, a_re, a_im, d)


def _glu_res_kernel(x_ref, wa_ref, wb_ref, r_ref, o_ref, xc_ref):
    @pl.when(pl.program_id(1) == 0)
    def _():
        xc_ref[...] = x_ref[...].astype(BF16)

    x = xc_ref[...]
    a = _dot(x, wa_ref[...])
    b = _dot(x, wb_ref[...])
    o_ref[...] = r_ref[...] + a * jax.nn.sigmoid(b)


def _glu_res(x, w, res, *, tm=1024, tn=512):
    M, D = x.shape
    N = w.shape[1] // 2
    tm = _pick(M, tm, 16)
    tn = _pick(N, tn, LANES)
    nj = N // tn
    return pl.pallas_call(
        _glu_res_kernel,
        grid=(M // tm, nj),
        in_specs=[pl.BlockSpec((tm, D), lambda i, j: (i, 0)),
                  pl.BlockSpec((D, tn), lambda i, j: (0, j)),
                  pl.BlockSpec((D, tn), lambda i, j: (0, nj + j)),
                  pl.BlockSpec((tm, tn), lambda i, j: (i, j))],
        out_specs=pl.BlockSpec((tm, tn), lambda i, j: (i, j)),
        out_shape=jax.ShapeDtypeStruct((M, N), F32),
        scratch_shapes=[pltpu.VMEM((tm, D), BF16)],
        compiler_params=_cparams("parallel", "arbitrary"),
        name="glu_res",
    )(x, w, w, res)


def _mixer_b(xp, xs, g, ops, d, w_glu, h0r, h0i):
    taps, inc_re, inc_im, out_re, out_im, lam = ops
    n = S5_CHUNK
    GP = lam[0].shape[-1]
    g, d = g[None], d.astype(F32)[None]
    w_glu = w_glu.astype(BF16)

    def stream(x, hr0, hi0):
        B, T, D = x.shape
        nck = T // n
        Nc = B * nck
        x2 = x.reshape(B * T, D)
        u = _rmsnorm(x2, g)
        tc = _pick(Nc, 256, 8)
        sr, si = _s5_inc(u, inc_re, inc_im, tc=tc)
        pr, pi, fr, fi = _s5_scan(sr.reshape(B, nck, GP), si.reshape(B, nck, GP), lam, hr0, hi0)
        gy = _s5_out(u, taps, pr.reshape(Nc, GP), pi.reshape(Nc, GP), out_re, out_im, d, tc=tc)
        y = _glu_res(gy, w_glu, x2)
        shp = (B, GP // S5_STATE, S5_STATE)
        return y.reshape(B, T, D), fr.reshape(shp), fi.reshape(shp)

    B = xp.shape[0]
    zero = jnp.zeros((B, 1, GP), F32)
    yp, rp, ip = stream(xp, zero, zero)
    Bs = xs.shape[0]
    ys, rs, is_ = stream(xs, h0r.astype(F32).reshape(Bs, 1, GP), h0i.astype(F32).reshape(Bs, 1, GP))
    return yp, ys, (rp, ip, rs, is_)


FFN_HALO = 16
FFN_TAIL = 8


def _ffn_kernel(*refs, seq_tiles, seq_len, halo):
    if halo:
        (x_ref, xh_ref, g_ref, wa_ref, wv_ref, cwa_ref, cwv_ref, cba_ref, cbv_ref, wd_ref,
         o_ref, ta_ref, tv_ref, xn_ref, xhn_ref) = refs
    else:
        (x_ref, p1a_ref, p1v_ref, p2a_ref, p2v_ref, g_ref, wa_ref, wv_ref, cwa_ref, cwv_ref,
         cba_ref, cbv_ref, wd_ref, o_ref, ta_ref, tv_ref, xn_ref, h_ref) = refs
    i, f = pl.program_id(0), pl.program_id(1)
    tm = x_ref.shape[0]

    @pl.when(f == 0)
    def _():
        x = x_ref[...]
        xn_ref[...] = _rms(x, g_ref[...]).astype(BF16)
        o_ref[...] = x
        if halo:
            xhn_ref[...] = _rms(xh_ref[...], g_ref[...]).astype(BF16)

    xn = xn_ref[...]
    tf = wa_ref.shape[1]
    r = lax.broadcasted_iota(I32, (tm, tf), 0)

    def conv(w_ref, cw_ref, cb_ref, p1_ref, p2_ref, t_ref):
        h = _dot(xn, w_ref[...])
        h1 = pltpu.roll(h, 1, axis=0)
        h2 = pltpu.roll(h, 2, axis=0)
        if halo:
            t_ref[...] = h[tm - FFN_TAIL:, :]
            hh = _dot(xhn_ref[...], w_ref[...])
            hh = hh * jnp.where(i % seq_tiles == 0, 0.0, 1.0)
            last, prev = hh[-1:, :], hh[-2:-1, :]
            h1 = jnp.where(r == 0, last, h1)
            h2 = jnp.where(r == 0, prev, jnp.where(r == 1, last, h2))
        else:
            for s in range(tf // LANES):
                sl = slice(s * LANES, (s + 1) * LANES)
                h_ref[s] = h[:, sl]
                for k in range(CONV_W - 1):
                    rows = pl.ds(seq_len - (CONV_W - 1) + k, tm // seq_len, stride=seq_len)
                    t_ref[k, :, sl] = h_ref.at[s][rows, :]
            t = r % seq_len
            h1 = jnp.where(t >= 1, h1, p1_ref[...])
            h2 = jnp.where(t >= 2, h2, p2_ref[...])
        cw = cw_ref[...]
        return cb_ref[...] + cw[0:1] * h2 + cw[1:2] * h1 + cw[2:3] * h

    if halo:
        ca = conv(wa_ref, cwa_ref, cba_ref, None, None, ta_ref)
        cv = conv(wv_ref, cwv_ref, cbv_ref, None, None, tv_ref)
    else:
        ca = conv(wa_ref, cwa_ref, cba_ref, p1a_ref, p2a_ref, ta_ref)
        cv = conv(wv_ref, cwv_ref, cbv_ref, p1v_ref, p2v_ref, tv_ref)
    act = (jax.nn.silu(ca) * cv).astype(BF16)
    o_ref[...] += _dot(act, wd_ref[...])


def _ffn(x, g, w_up, conv_w, conv_b, w_down, *, layer, seq_len, hist=None, tm=512, tf=1024):
    M, D = x.shape
    F = w_down.shape[1]
    halo = hist is None
    n_seq = M // seq_len
    tm = _pick(seq_len, tm, FFN_HALO) if halo else M
    tf = _pick(F, tf, LANES)
    nf = F // tf
    xs = pl.BlockSpec((tm, D), lambda i, f: (i, 0))
    a_col = lambda i, f: (layer, 0, f)
    v_col = lambda i, f: (layer, 0, nf + f)
    wspecs = [pl.BlockSpec((1, D), lambda i, f: (0, 0)),
              pl.BlockSpec((None, D, tf), a_col), pl.BlockSpec((None, D, tf), v_col),
              pl.BlockSpec((None, CONV_W, tf), a_col), pl.BlockSpec((None, CONV_W, tf), v_col),
              pl.BlockSpec((None, 1, tf), a_col), pl.BlockSpec((None, 1, tf), v_col),
              pl.BlockSpec((None, tf, D), lambda i, f: (layer, f, 0))]
    wargs = [g, w_up, w_up, conv_w, conv_w, conv_b, conv_b, w_down]
    scratch = [pltpu.VMEM((tm, D), BF16)]
    seq_tiles = max(seq_len // tm, 1)
    if halo:
        per = tm // FFN_HALO
        extra = [pl.BlockSpec((FFN_HALO, D), lambda i, f: (jnp.maximum(i * per - 1, 0), 0))]
        eargs = [x]
        scratch.append(pltpu.VMEM((FFN_HALO, D), BF16))
        tspec = pl.BlockSpec((FFN_TAIL, tf), lambda i, f: (i // seq_tiles, f))
        tshape = jax.ShapeDtypeStruct((n_seq * FFN_TAIL, F), F32)
    else:
        ha = pl.BlockSpec((tm, tf), lambda i, f: (i, f))
        hv = pl.BlockSpec((tm, tf), lambda i, f: (i, nf + f))
        extra = [ha, hv, ha, hv]
        eargs = [hist[0], hist[0], hist[1], hist[1]]
        scratch.append(pltpu.VMEM((tf // LANES, tm, LANES), F32))
        tspec = pl.BlockSpec((CONV_W - 1, n_seq, tf), lambda i, f: (0, 0, f))
        tshape = jax.ShapeDtypeStruct((CONV_W - 1, n_seq, F), F32)
    kern = functools.partial(_ffn_kernel, seq_tiles=seq_tiles, seq_len=seq_len, halo=halo)
    y, ta, tv = pl.pallas_call(
        kern,
        grid=(M // tm, nf),
        in_specs=[xs] + extra + wspecs,
        out_specs=[pl.BlockSpec((tm, D), lambda i, f: (i, 0)), tspec, tspec],
        out_shape=[jax.ShapeDtypeStruct((M, D), F32), tshape, tshape],
        scratch_shapes=scratch,
        compiler_params=_cparams("arbitrary", "arbitrary"),
        name="conv_ffn" if halo else "conv_ffn_hist",
    )(x, *eargs, *wargs)
    if halo:
        tail = lambda t: t.reshape(n_seq, FFN_TAIL, F)[:, FFN_TAIL - (CONV_W - 1):]
    else:
        tail = lambda t: jnp.swapaxes(t, 0, 1)
    return y, jnp.concatenate([tail(ta), tail(tv)], axis=-1)


def _conv_ffn(xp, xs, g, w_up, conv_w, conv_b, w_down, hist_s, layer):
    B, S, D = xp.shape
    Bs, T, _ = xs.shape
    F2 = w_up.shape[2]
    g = g[None]
    yp, hp = _ffn(xp.reshape(B * S, D), g, w_up, conv_w, conv_b, w_down, layer=layer, seq_len=S)
    hs = hist_s.astype(F32)
    t = jnp.arange(T)[None, :, None]
    p1 = jnp.where(t == 0, hs[:, -1:], 0.0).reshape(Bs * T, F2)
    p2 = jnp.where(t == 0, hs[:, -2:-1], jnp.where(t == 1, hs[:, -1:], 0.0)).reshape(Bs * T, F2)
    ys, hs_new = _ffn(xs.reshape(Bs * T, D), g, w_up, conv_w, conv_b, w_down, layer=layer,
                      seq_len=T, hist=(p1, p2))
    return yp.reshape(B, S, D), ys.reshape(Bs, T, D), hp, hs_new


def kernel(x_prompt, x_sample, cache_a_k, cache_a_v, cache_a_kidx, cache_c_k, cache_c_v, state_b_re, state_b_im, state_ffn_conv, norm_mix, norm_ffn, a_w_in, a_w_out, a_q_norm, a_k_norm, b_a_re, b_a_im, b_log_dt, b_b_re, b_b_im, b_c_re, b_c_im, b_d, b_w_glu, c_w_in, c_w_out, c_q_norm, c_k_norm, c_rel_bias, ffn_w_up, ffn_conv_w, ffn_conv_b, ffn_w_down):
    xp, xs = x_prompt, x_sample
    depth = norm_mix.shape[0]
    outs_a, outs_b, outs_c, f_p, f_s = [], [], [], [], []
    ffn_w_up, ffn_w_down = ffn_w_up.astype(BF16), ffn_w_down.astype(BF16)
    ffn_conv_b = ffn_conv_b[:, None]
    for i in range(depth):
        li, kind = i // 3, i % 3
        if kind == 0:
            xp, xs, o = _mixer_a(xp, xs, norm_mix[i], a_w_in[li], a_w_out[li], a_q_norm[li],
                                 a_k_norm[li], (cache_a_k, cache_a_v, cache_a_kidx), li)
            outs_a.append(o)
        elif kind == 1:
            ops = _s5_operators(b_a_re[li], b_a_im[li], b_log_dt[li], b_b_re[li], b_b_im[li],
                                b_c_re[li], b_c_im[li])
            xp, xs, o = _mixer_b(xp, xs, norm_mix[i], ops, b_d[li], b_w_glu[li],
                                 state_b_re[li], state_b_im[li])
            outs_b.append(o)
        else:
            xp, xs, o = _mixer_c(xp, xs, norm_mix[i], c_w_in[li], c_w_out[li], c_q_norm[li],
                                 c_k_norm[li], c_rel_bias[li], cache_c_k[li], cache_c_v[li])
            outs_c.append(o)
        xp, xs, hp, hs = _conv_ffn(xp, xs, norm_ffn[i], ffn_w_up, ffn_conv_w, ffn_conv_b,
                                   ffn_w_down, state_ffn_conv[i], i)
        f_p.append(hp)
        f_s.append(hs)
    stack = lambda outs, j: jnp.stack([o[j] for o in outs])
    return (xp, xs,
            *[stack(outs_a, j) for j in range(6)],
            *[stack(outs_c, j) for j in range(4)],
            *[stack(outs_b, j) for j in range(4)],
            jnp.stack(f_p), jnp.stack(f_s))
```

```python
import functools
import math

import jax
import jax.numpy as jnp
from jax import lax
from jax.experimental import pallas as pl
from jax.experimental.pallas import tpu as pltpu

F32 = jnp.float32
BF16 = jnp.bfloat16
I32 = jnp.int32

EPS = 1e-6
NEG_INF = -1e30
ROW_MAX_INIT = -1e29
THR_BITS_PER_CHECK = 4
ROPE_THETA = 10000.0
CHUNK = 64
HEAD_DIM = 128
LANES = 128

A_HEADS = 16
A_KV_HEADS = 4
A_GROUP = A_HEADS // A_KV_HEADS
IDX_HEADS = 8
IDX_DIM = 64
TOPK_MAX = 256
A_Q = A_HEADS * HEAD_DIM
A_KV = A_KV_HEADS * HEAD_DIM
A_QI = IDX_HEADS * IDX_DIM
IDX_W_SCALE = (IDX_HEADS * IDX_DIM) ** -0.5
A_OFF_K = A_Q
A_OFF_V = A_OFF_K + A_KV
A_OFF_QI = A_OFF_V + A_KV
A_OFF_KI = A_OFF_QI + A_QI
A_OFF_WI = A_OFF_KI + LANES
A_IN_PAD = A_OFF_WI + LANES

S5_GROUP = 16
S5_STATE = 64
S5_CHUNK = 16
S5_LB_GROUPS = LANES // S5_GROUP

C_HEADS = 16
C_BAND_CHUNKS = 8
C_REACH = C_BAND_CHUNKS * CHUNK
REL_CLIP = 128
CONV_W = 3

VMEM_LIMIT = 56 * 1024 * 1024


def _cparams(*sem):
    return pltpu.CompilerParams(dimension_semantics=sem, vmem_limit_bytes=VMEM_LIMIT)


def _pick(n, target, mult):
    best = None
    for t in range(mult, min(n, target) + 1, mult):
        if n % t == 0:
            best = t
    return best if best is not None else n


def _dot(a, b):
    return jnp.dot(a, b, preferred_element_type=F32)


def _dot_nt(a, b):
    return lax.dot_general(a, b, (((1,), (1,)), ((), ())), preferred_element_type=F32)


def _rms(x, g):
    return x * lax.rsqrt(jnp.mean(x * x, axis=-1, keepdims=True) + EPS) * g


def _rms_matmul_kernel(x_ref, g_ref, w_ref, o_ref, xn_ref):
    @pl.when(pl.program_id(1) == 0)
    def _():
        xn_ref[...] = _rms(x_ref[...], g_ref[...]).astype(BF16)

    o_ref[...] = _dot(xn_ref[...], w_ref[...]).astype(o_ref.dtype)


def _rms_matmul(x, g, w, *, tm=1024, tn=1024, out_dtype=F32):
    M, D = x.shape
    N = w.shape[1]
    tm = _pick(M, tm, 16)
    tn = _pick(N, tn, LANES)
    return pl.pallas_call(
        _rms_matmul_kernel,
        grid=(M // tm, N // tn),
        in_specs=[pl.BlockSpec((tm, D), lambda i, j: (i, 0)),
                  pl.BlockSpec((1, D), lambda i, j: (0, 0)),
                  pl.BlockSpec((D, tn), lambda i, j: (0, j))],
        out_specs=pl.BlockSpec((tm, tn), lambda i, j: (i, j)),
        out_shape=jax.ShapeDtypeStruct((M, N), out_dtype),
        scratch_shapes=[pltpu.VMEM((tm, D), BF16)],
        compiler_params=_cparams("parallel", "arbitrary"),
        name="rms_matmul",
    )(x, g, w)


def _rmsnorm_kernel(x_ref, g_ref, o_ref):
    o_ref[...] = _rms(x_ref[...], g_ref[...])


def _rmsnorm(x, g, *, tm=512):
    M, D = x.shape
    tm = _pick(M, tm, 8)
    return pl.pallas_call(
        _rmsnorm_kernel,
        grid=(M // tm,),
        in_specs=[pl.BlockSpec((tm, D), lambda i: (i, 0)),
                  pl.BlockSpec((1, D), lambda i: (0, 0))],
        out_specs=pl.BlockSpec((tm, D), lambda i: (i, 0)),
        out_shape=jax.ShapeDtypeStruct((M, D), F32),
        compiler_params=_cparams("parallel"),
        name="rmsnorm",
    )(x, g)


def _matmul_res_kernel(x_ref, w_ref, r_ref, o_ref):
    o_ref[...] = r_ref[...] + _dot(x_ref[...], w_ref[...])


def _matmul_res(x, w, res, *, tm=512, tn=2048):
    M, K = x.shape
    N = w.shape[1]
    tm = _pick(M, tm, 16)
    tn = _pick(N, tn, LANES)
    return pl.pallas_call(
        _matmul_res_kernel,
        grid=(M // tm, N // tn),
        in_specs=[pl.BlockSpec((tm, K), lambda i, j: (i, 0)),
                  pl.BlockSpec((K, tn), lambda i, j: (0, j)),
                  pl.BlockSpec((tm, tn), lambda i, j: (i, j))],
        out_specs=pl.BlockSpec((tm, tn), lambda i, j: (i, j)),
        out_shape=jax.ShapeDtypeStruct((M, N), F32),
        compiler_params=_cparams("parallel", "arbitrary"),
        name="matmul_res",
    )(x, w, res)


def _rope_tables(pos, dim):
    half = dim // 2
    inv = jnp.power(ROPE_THETA, -jnp.arange(half, dtype=F32) / half)
    ang = pos.astype(F32)[:, None] * inv[None, :]
    cos, sin = jnp.cos(ang), jnp.sin(ang)
    reps = LANES // dim
    return (jnp.tile(jnp.concatenate([cos, cos], axis=-1), (1, reps)),
            jnp.tile(jnp.concatenate([-sin, sin], axis=-1), (1, reps)))


def _a_post_kernel(h_ref, c128_ref, s128_ref, c64_ref, s64_ref, qg_ref, kg_ref,
                   q_ref, kf_ref, vf_ref, kb_ref, vb_ref, qi_ref, kif_ref, kib_ref, wi_ref):
    c128, s128 = c128_ref[...], s128_ref[...]
    c64, s64 = c64_ref[...], s64_ref[...]
    tm = h_ref.shape[0]
    lane = lax.broadcasted_iota(I32, (tm, LANES), 1)

    def rope128(x):
        return x * c128 + pltpu.roll(x, HEAD_DIM // 2, axis=1) * s128

    def rope64(x):
        lo = (lane % IDX_DIM) < (IDX_DIM // 2)
        partner = jnp.where(lo, pltpu.roll(x, LANES - IDX_DIM // 2, axis=1),
                            pltpu.roll(x, IDX_DIM // 2, axis=1))
        return x * c64 + partner * s64

    for h in range(A_HEADS):
        x = h_ref[:, h * LANES:(h + 1) * LANES]
        q_ref[:, h * LANES:(h + 1) * LANES] = (
            rope128(_rms(x, qg_ref[...])) * (HEAD_DIM ** -0.5)).astype(BF16)
    for h in range(A_KV_HEADS):
        x = h_ref[:, A_OFF_K + h * LANES:A_OFF_K + (h + 1) * LANES]
        k = rope128(_rms(x, kg_ref[...]))
        kf_ref[:, h, :] = k
        kb_ref[:, h * LANES:(h + 1) * LANES] = k.astype(BF16)
    ones = jnp.ones((tm, LANES), BF16)
    for h in range(A_KV_HEADS):
        v = h_ref[:, A_OFF_V + h * LANES:A_OFF_V + (h + 1) * LANES]
        vf_ref[:, h, :] = v
        vb_ref[:, 2 * h * LANES:(2 * h + 1) * LANES] = v.astype(BF16)
        vb_ref[:, (2 * h + 1) * LANES:(2 * h + 2) * LANES] = ones
    for p in range(A_QI // LANES):
        x = rope64(h_ref[:, A_OFF_QI + p * LANES:A_OFF_QI + (p + 1) * LANES])
        qi_ref[:, (2 * p) * LANES:(2 * p + 1) * LANES] = (
            jnp.where(lane < IDX_DIM, x, 0.0).astype(BF16))
        qi_ref[:, (2 * p + 1) * LANES:(2 * p + 2) * LANES] = (
            jnp.where(lane >= IDX_DIM, x, 0.0).astype(BF16))
    ki = rope64(h_ref[:, A_OFF_KI:A_OFF_KI + LANES])
    kif_ref[...] = ki
    kib_ref[...] = (ki + pltpu.roll(ki, IDX_DIM, axis=1)).astype(BF16)
    wi_ref[...] = h_ref[:, A_OFF_WI:A_OFF_WI + LANES] * IDX_W_SCALE


def _a_post(h, tabs, qg, kg, *, tm=512):
    M = h.shape[0]
    P = tabs[0].shape[0]
    tm = _pick(math.gcd(M, P), tm, 16)
    nt = P // tm
    row = lambda i: (i, 0)
    tab = lambda i: (i % nt, 0)
    one = lambda i: (0, 0)
    widths = [(A_KV, BF16), (2 * A_KV, BF16),
              (IDX_HEADS * LANES, BF16), (LANES, F32), (LANES, BF16), (LANES, F32)]
    heads = (A_KV_HEADS, HEAD_DIM)
    return pl.pallas_call(
        _a_post_kernel,
        grid=(M // tm,),
        in_specs=[pl.BlockSpec((tm, A_IN_PAD), row)]
        + [pl.BlockSpec((tm, LANES), tab)] * 4
        + [pl.BlockSpec((1, LANES), one)] * 2,
        out_specs=[pl.BlockSpec((tm, A_Q), row)]
        + [pl.BlockSpec((tm,) + heads, lambda i: (i, 0, 0))] * 2
        + [pl.BlockSpec((tm, w), row) for w, _ in widths],
        out_shape=[jax.ShapeDtypeStruct((M, A_Q), BF16)]
        + [jax.ShapeDtypeStruct((M,) + heads, F32)] * 2
        + [jax.ShapeDtypeStruct((M, w), dt) for w, dt in widths],
        compiler_params=_cparams("parallel"),
        name="a_post",
    )(h, *tabs, qg, kg)


def _dsa_kernel(*refs, tq, tk, ksel, causal, l_valid, n_chunks, idx_bits):
    if causal:
        q_ref, qi_ref, wi_ref, k_ref, v_ref, ki_ref, o_ref = refs[:7]
    else:
        q_ref, qi_ref, wi_ref, k_ref, v_ref, ki_ref, ck_ref, cv_ref, cki_ref, o_ref = refs[:10]
    key_scr, bias_scr, qst_scr, m_scr, acc_scr = refs[-5:]
    t0 = pl.program_id(1) * tq
    nch = (t0 + tq + tk - 1) // tk if causal else n_chunks
    nslab = tk // LANES
    row = lax.broadcasted_iota(I32, (tq, tk), 0)
    col = lax.broadcasted_iota(I32, (tq, tk), 1)
    lane = lax.broadcasted_iota(I32, (tq, LANES), 1)
    limit = ((t0 + row) // CHUNK + 1) * CHUNK if causal else l_valid

    def chunk_off(c):
        return c * tk if isinstance(c, int) else pl.multiple_of(c * tk, tk)

    def cached(c):
        return not causal and c < n_chunks - 1

    def new_off(c):
        return chunk_off(c) if causal else 0

    def load_ki(c):
        if cached(c):
            x = cki_ref[0, c * tk:(c + 1) * tk, :]
            return jnp.concatenate([x, x], axis=1).astype(BF16)
        return ki_ref[0, pl.ds(new_off(c), tk), :]

    def load_k(c, j):
        if cached(c):
            return ck_ref[0, c * tk:(c + 1) * tk, j, :].astype(BF16)
        return k_ref[0, pl.ds(new_off(c), tk), j * LANES:(j + 1) * LANES]

    def load_v(c, j):
        if cached(c):
            v = cv_ref[0, c * tk:(c + 1) * tk, j, :].astype(BF16)
            return jnp.concatenate([v, jnp.ones((tk, LANES), BF16)], axis=1)
        return v_ref[0, pl.ds(new_off(c), tk), 2 * j * LANES:2 * (j + 1) * LANES]

    def chunk_loop(body):
        if causal:
            lax.fori_loop(0, nch, lambda c, carry: (body(c), carry)[1], 0)
        else:
            for c in range(n_chunks):
                body(c)

    wi = wi_ref[0]

    def score_body(c):
        off = chunk_off(c)
        kic = load_ki(c)
        shs = [_dot_nt(qi_ref[0, :, h * LANES:(h + 1) * LANES], kic) for h in range(IDX_HEADS)]
        s = jnp.zeros((tq, tk), F32)
        for h in range(IDX_HEADS):
            s = s + wi[:, h:h + 1] * jnp.maximum(shs[h], 0.0)
        s = jnp.where(s == 0.0, 0.0, s)
        s = jnp.where(off + col < limit, s, NEG_INF)
        bits = pltpu.bitcast(s, I32)
        key_scr[:, pl.ds(off, tk)] = bits ^ ((bits >> 31) & 0x7FFFFFFF)

    chunk_loop(score_body)

    def count(pred):
        def body(c, acc):
            off = chunk_off(c)
            blk = key_scr[:, pl.ds(off, tk)]
            for s in range(nslab):
                sl = slice(s * LANES, (s + 1) * LANES)
                acc = acc + pred(blk[:, sl], lane + (off + s * LANES))
            return acc

        acc = lax.fori_loop(0, nch, body, jnp.zeros((tq, LANES), F32))
        return jnp.broadcast_to(jnp.sum(acc, axis=-1, keepdims=True), (tq, LANES))

    def thr_cond(st):
        i, _, n_ge = st
        return jnp.logical_and(i < 32, jnp.max(n_ge) > ksel)

    def thr_body(st):
        i, thr, n_ge = st
        for b in range(THR_BITS_PER_CHECK):
            cand = thr + lax.shift_left(jnp.int32(1), 31 - i - b)
            cnt = count(lambda key, idx: jnp.where(key >= cand, 1.0, 0.0))
            take = cnt >= ksel
            thr, n_ge = jnp.where(take, cand, thr), jnp.where(take, cnt, n_ge)
        return i + THR_BITS_PER_CHECK, thr, n_ge

    n_keys = (nch * tk).astype(F32) if causal else float(nch * tk)
    _, thr, n_ge = lax.while_loop(
        thr_cond, thr_body,
        (jnp.int32(0), jnp.full((tq, LANES), -2 ** 31, I32), jnp.full((tq, LANES), n_keys, F32)))

    n_gt = count(lambda key, idx: jnp.where(key > thr, 1.0, 0.0))
    need = ksel - n_gt
    n_eq = n_ge - n_gt

    def cut_search():
        def body(i, cut):
            cand = cut + lax.shift_left(jnp.int32(1), idx_bits - 1 - i)
            below = count(lambda key, idx: jnp.where(key == thr, jnp.where(idx < cand, 1.0, 0.0), 0.0))
            return jnp.where(below <= need, cand, cut)

        return lax.fori_loop(0, idx_bits, body, jnp.zeros((tq, LANES), I32))

    cut = lax.cond(jnp.max(n_eq - need) > 0.0, cut_search,
                   lambda: jnp.full((tq, LANES), 2 ** idx_bits, I32))

    thr_t = jnp.tile(thr, (1, nslab))
    cut_t = jnp.tile(cut, (1, nslab))

    def bias_body(c, carry):
        off = chunk_off(c)
        key = key_scr[:, pl.ds(off, tk)]
        idx = off + col
        tie = jnp.where(key == thr_t, jnp.where(idx < cut_t, 0.0, NEG_INF), NEG_INF)
        bias = jnp.where(key > thr_t, 0.0, tie)
        bias_scr[:, pl.ds(off, tk)] = jnp.where(idx < limit, bias, NEG_INF)
        return carry

    lax.fori_loop(0, nch, bias_body, 0)

    for j in range(A_KV_HEADS):
        for g in range(A_GROUP):
            h = j * A_GROUP + g
            qst_scr[j, g * tq:(g + 1) * tq, :] = q_ref[0, :, h * LANES:(h + 1) * LANES]
    m_scr[...] = jnp.full(m_scr.shape, ROW_MAX_INIT, F32)
    acc_scr[...] = jnp.zeros(acc_scr.shape, F32)
    gr = A_GROUP * tq

    def attn_body(c):
        off = chunk_off(c)
        bias = bias_scr[:, pl.ds(off, tk)]
        logits = [_dot_nt(qst_scr[j], load_k(c, j)) for j in range(A_KV_HEADS)]
        for j in range(A_KV_HEADS):
            rows = slice(j * gr, (j + 1) * gr)
            m_old = m_scr[rows, :]
            alphas, ps = [], []
            for g in range(A_GROUP):
                sl = slice(g * tq, (g + 1) * tq)
                sg = logits[j][sl, :] + bias
                m_new = jnp.maximum(m_old[sl, :], jnp.max(sg, axis=-1, keepdims=True))
                alphas.append(jnp.exp(m_old[sl, :] - m_new))
                ps.append(jnp.exp((sg - jnp.tile(m_new, (1, nslab))).astype(BF16)))
                m_scr[j * gr + g * tq:j * gr + (g + 1) * tq, :] = m_new
            alpha = jnp.tile(jnp.concatenate(alphas, axis=0), (1, 2))
            pv = _dot(jnp.concatenate(ps, axis=0), load_v(c, j))
            acc_scr[rows, :] = alpha * acc_scr[rows, :] + pv

    chunk_loop(attn_body)
    for h in range(A_HEADS):
        rows = slice(h * tq, (h + 1) * tq)
        o_ref[0, :, h * LANES:(h + 1) * LANES] = (
            acc_scr[rows, :LANES] / acc_scr[rows, LANES:]).astype(BF16)


def _dsa(q, qi, wi, k, v, ki, cache=None, *, tq, tk, ksel, causal, l_valid):
    B, T, _ = q.shape
    Lk = k.shape[1]
    blk_q = lambda w: pl.BlockSpec((1, tq, w), lambda b, i: (b, i, 0))
    blk_k = lambda n, w: pl.BlockSpec((1, n, w), lambda b, i: (b, 0, 0))
    specs, args = [], []
    L = Lk
    if cache is not None:
        ck, cv, cki, li = cache
        past = ck.shape[2]
        L = past + Lk
        blk_c = pl.BlockSpec((None, 1, past, A_KV_HEADS, HEAD_DIM),
                             lambda b, i: (li, b, 0, 0, 0))
        specs = [blk_c, blk_c,
                 pl.BlockSpec((None, 1, past, IDX_DIM), lambda b, i: (li, b, 0, 0))]
        args = [ck, cv, cki]
    kern = functools.partial(
        _dsa_kernel, tq=tq, tk=tk, ksel=ksel, causal=causal, l_valid=l_valid,
        n_chunks=-(-l_valid // tk), idx_bits=max(1, (L - 1).bit_length()) + 1)
    return pl.pallas_call(
        kern,
        grid=(B, T // tq),
        in_specs=[blk_q(A_Q), blk_q(IDX_HEADS * LANES), blk_q(LANES),
                  blk_k(Lk, A_KV), blk_k(Lk, 2 * A_KV), blk_k(Lk, LANES)] + specs,
        out_specs=blk_q(A_Q),
        out_shape=jax.ShapeDtypeStruct((B, T, A_Q), BF16),
        scratch_shapes=[pltpu.VMEM((tq, L), I32), pltpu.VMEM((tq, L), F32),
                        pltpu.VMEM((A_KV_HEADS, A_GROUP * tq, LANES), BF16),
                        pltpu.VMEM((A_HEADS * tq, LANES), F32),
                        pltpu.VMEM((A_HEADS * tq, 2 * LANES), F32)],
        compiler_params=_cparams("parallel", "arbitrary"),
        name="dsa_causal" if causal else "dsa_cached",
    )(q, qi, wi, k, v, ki, *args)


def _pack_a_w_in(w):
    D = w.shape[0]
    o_ki = A_Q + 2 * A_KV + A_QI
    pad = lambda n: jnp.zeros((D, n), w.dtype)
    return jnp.concatenate(
        [w[:, :o_ki], w[:, o_ki:o_ki + IDX_DIM], pad(LANES - IDX_DIM),
         w[:, o_ki + IDX_DIM:], pad(LANES - IDX_HEADS)], axis=1).astype(BF16)


def _mixer_a(xp, xs, g, w_in, w_out, qg, kg, caches, li):
    B, S, D = xp.shape
    Bs, T, _ = xs.shape
    past = caches[0].shape[2]
    w_in = _pack_a_w_in(w_in)
    w_out = w_out.astype(BF16)
    g, qg, kg = g[None], qg[None], kg[None]

    def project(x, pos):
        b, t, _ = x.shape
        h = _rms_matmul(x.reshape(b * t, D), g, w_in, tn=A_IN_PAD // 5)
        tabs = _rope_tables(pos, HEAD_DIM) + _rope_tables(pos, IDX_DIM)
        q, kf, vf, *rest = _a_post(h, tabs, qg, kg)
        heads = (b, t, A_KV_HEADS, HEAD_DIM)
        return [q.reshape(b, t, -1), kf.reshape(heads), vf.reshape(heads)] + [
            o.reshape(b, t, -1) for o in rest]

    q, k_p, v_p, kb, vb, qi, kif, kib, wi = project(xp, jnp.arange(S))
    tk = _pick(S, 512, LANES)
    o = _dsa(q, qi, wi, kb, vb, kib, tq=min(128, S), tk=tk, ksel=min(TOPK_MAX, S // 4),
             causal=True, l_valid=S)
    yp = _matmul_res(o.reshape(B * S, A_Q), w_out, xp.reshape(B * S, D)).reshape(B, S, D)
    ki_p = kif[..., :IDX_DIM]

    pos_s = jnp.tile(past + jnp.arange(T), Bs)
    q, k_s, v_s, kb, vb, qi, kif, kib, wi = project(xs, pos_s)
    L = past + T
    tk = _pick(past, 512, LANES)
    padk = lambda a: jnp.pad(a, ((0, 0), (0, tk - T), (0, 0)))
    o = _dsa(q, qi, wi, padk(kb), padk(vb), padk(kib), caches + (li,),
             tq=T, tk=tk, ksel=min(TOPK_MAX, L // 4), causal=False, l_valid=L)
    ys = _matmul_res(o.reshape(Bs * T, A_Q), w_out, xs.reshape(Bs * T, D)).reshape(Bs, T, D)
    return yp, ys, (k_p, v_p, ki_p, k_s, v_s, kif[..., :IDX_DIM])


def _c_post_kernel(h_ref, qg_ref, kg_ref, q_ref, kb_ref, kf_ref, vf_ref):
    W = C_HEADS * HEAD_DIM
    for h in range(C_HEADS):
        sl = slice(h * LANES, (h + 1) * LANES)
        q = _rms(h_ref[:, sl].astype(F32), qg_ref[...])
        q_ref[:, sl] = (q * (HEAD_DIM ** -0.5)).astype(BF16)
        k = _rms(h_ref[:, W + h * LANES:W + (h + 1) * LANES].astype(F32), kg_ref[...])
        kf_ref[:, sl] = k
        kb_ref[:, sl] = k.astype(BF16)
    vf_ref[...] = h_ref[:, 2 * W:3 * W].astype(F32)


def _c_post(h, qg, kg, *, t, keep, tm=512):
    M = h.shape[0]
    W = C_HEADS * HEAD_DIM
    tm = _pick(math.gcd(t, keep), tm, 16) if keep < t else _pick(M, tm, 16)
    row = lambda i: (i, 0)
    if keep < t:
        tps, kps = t // tm, keep // tm
        tail = lambda i: ((i // tps) * kps + jnp.maximum(i % tps - (tps - kps), 0), 0)
    else:
        tail = row
    n_tail = M // t * keep
    return pl.pallas_call(
        _c_post_kernel,
        grid=(M // tm,),
        in_specs=[pl.BlockSpec((tm, 3 * W), row),
                  pl.BlockSpec((1, LANES), lambda i: (0, 0)),
                  pl.BlockSpec((1, LANES), lambda i: (0, 0))],
        out_specs=[pl.BlockSpec((tm, W), row)] * 2 + [pl.BlockSpec((tm, W), tail)] * 2,
        out_shape=[jax.ShapeDtypeStruct((M, W), BF16)] * 2
        + [jax.ShapeDtypeStruct((n_tail, W), F32)] * 2,
        compiler_params=_cparams("arbitrary"),
        name="c_post",
    )(h, qg, kg)


def _band_kernel(q_ref, k_ref, v_ref, b_ref, o_ref, *, sub, nsub, win, first_key):
    t0 = pl.program_id(2) * (sub * nsub)
    starts = [pl.multiple_of(t0 + s * sub, sub) for s in range(nsub)]
    lgs = [_dot_nt(q_ref[0, s * sub:(s + 1) * sub, :], k_ref[0, pl.ds(starts[s], win), :])
           + b_ref[0] for s in range(nsub)]
    kk = lax.broadcasted_iota(I32, (sub, win), 1)
    lgs = [jnp.where(kk >= first_key - starts[s], lgs[s], NEG_INF) for s in range(nsub)]
    ms = [jnp.max(lg, axis=-1, keepdims=True) for lg in lgs]
    ps = [jnp.exp(lg - m) for lg, m in zip(lgs, ms)]
    ls = [jnp.sum(p, axis=-1, keepdims=True) for p in ps]
    os_ = [_dot(ps[s].astype(BF16), v_ref[0, pl.ds(starts[s], win), :]) for s in range(nsub)]
    for s in range(nsub):
        o_ref[0, s * sub:(s + 1) * sub, :] = (os_[s] / ls[s]).astype(BF16)


def _band(q, k, v, bias, *, sub, nsub, first_key):
    B, T, W = q.shape
    Lk = k.shape[1]
    win = bias.shape[2]
    tq = sub * nsub
    kern = functools.partial(_band_kernel, sub=sub, nsub=nsub, win=win, first_key=first_key)
    return pl.pallas_call(
        kern,
        grid=(B, C_HEADS, T // tq),
        in_specs=[pl.BlockSpec((1, tq, LANES), lambda b, h, i: (b, i, h)),
                  pl.BlockSpec((1, Lk, LANES), lambda b, h, i: (b, 0, h)),
                  pl.BlockSpec((1, Lk, LANES), lambda b, h, i: (b, 0, h)),
                  pl.BlockSpec((1, sub, win), lambda b, h, i: (h, 0, 0))],
        out_specs=pl.BlockSpec((1, tq, LANES), lambda b, h, i: (b, i, h)),
        out_shape=jax.ShapeDtypeStruct((B, T, W), BF16),
        compiler_params=_cparams("parallel", "parallel", "arbitrary"),
        name="band_attn",
    )(q, k, v, bias)


def _band_cached_kernel(q_ref, k_ref, v_ref, ck_ref, cv_ref, b_ref, o_ref):
    T = q_ref.shape[1]
    cp = ck_ref.shape[1]
    win = b_ref.shape[2]
    pad = jnp.zeros((win - cp - T, LANES), BF16)

    def window(c_ref, n_ref, h):
        new = n_ref[0, :, h * LANES:(h + 1) * LANES]
        return jnp.concatenate([c_ref[0, :, h, :].astype(BF16), new, pad], axis=0)

    heads = range(C_HEADS)
    lgs = [_dot_nt(q_ref[0, :, h * LANES:(h + 1) * LANES], window(ck_ref, k_ref, h)) + b_ref[h]
           for h in heads]
    ms = [jnp.max(lg, axis=-1, keepdims=True) for lg in lgs]
    ps = [jnp.exp(lg - m) for lg, m in zip(lgs, ms)]
    ls = [jnp.sum(p, axis=-1, keepdims=True) for p in ps]
    for h in heads:
        o = _dot(ps[h].astype(BF16), window(cv_ref, v_ref, h)) / ls[h]
        o_ref[0, :, h * LANES:(h + 1) * LANES] = o.astype(BF16)


def _band_cached(q, k, v, ck, cv, bias):
    B, T, W = q.shape
    cp = ck.shape[1]
    row = pl.BlockSpec((1, T, W), lambda b: (b, 0, 0))
    cache = pl.BlockSpec((1, cp, C_HEADS, HEAD_DIM), lambda b: (b, 0, 0, 0))
    return pl.pallas_call(
        _band_cached_kernel,
        grid=(B,),
        in_specs=[row, row, row, cache, cache,
                  pl.BlockSpec(bias.shape, lambda b: (0, 0, 0))],
        out_specs=row,
        out_shape=jax.ShapeDtypeStruct((B, T, W), BF16),
        compiler_params=_cparams("parallel"),
        name="band_cached",
    )(q, k, v, ck, cv, bias)


def _mixer_c(xp, xs, g, w_in, w_out, qg, kg, rel, ck, cv):
    B, S, D = xp.shape
    Bs, T, _ = xs.shape
    W = C_HEADS * HEAD_DIM
    cp = ck.shape[1]
    w_in = w_in.astype(BF16)
    w_out = w_out.astype(BF16)
    g, qg, kg = g[None], qg[None], kg[None]
    relf = rel.astype(F32)

    def project(x, keep):
        b, t, _ = x.shape
        h = _rms_matmul(x.reshape(b * t, D), g, w_in, out_dtype=BF16)
        q, kb, kf, vf = _c_post(h, qg, kg, t=t, keep=keep)
        vb = h.reshape(b, t, 3 * W)[:, :, 2 * W:]
        tail = lambda a: a.reshape(b, keep, C_HEADS, HEAD_DIM)
        return q.reshape(b, t, W), tail(kf), kb.reshape(b, t, W), vb, tail(vf)

    def rel_bias(rows, win, shift):
        n = rows + win - 1
        j = jnp.arange(n)
        vec = relf[:, jnp.clip(shift + rows - 1 - j, -REL_CLIP, REL_CLIP) + REL_CLIP]
        flat = jnp.tile(jnp.pad(vec, ((0, 0), (0, 1))), (1, rows))
        toep = flat[:, :rows * n].reshape(-1, rows, n)
        return toep[:, :, rows - 1:rows - 1 + win]

    sub = min(2 * CHUNK, S)
    win = C_REACH + sub
    r = jnp.arange(sub)[:, None]
    kk = jnp.arange(win)[None, :]
    lo = (r // CHUNK) * CHUNK
    bias = jnp.where((kk >= lo) & (kk < lo + C_REACH + CHUNK), rel_bias(sub, win, C_REACH),
                     NEG_INF)
    q, k_p, kb, vb, v_p = project(xp, min(C_REACH, S))
    padk = lambda a: jnp.pad(a, ((0, 0), (C_REACH, 0), (0, 0)))
    o = _band(q, padk(kb), padk(vb), bias, sub=sub, nsub=_pick(S // sub, 32, 1),
              first_key=C_REACH)
    yp = _matmul_res(o.reshape(B * S, W), w_out, xp.reshape(B * S, D)).reshape(B, S, D)

    L = cp + T
    win = -(-L // LANES) * LANES
    bias = jnp.where(jnp.arange(win)[None, :] < L, rel_bias(T, win, cp), NEG_INF)
    q, kf, kb, vb, vf = project(xs, T)
    o = _band_cached(q, kb, vb, ck, cv, bias)
    ys = _matmul_res(o.reshape(Bs * T, W), w_out, xs.reshape(Bs * T, D)).reshape(Bs, T, D)
    return yp, ys, (k_p, v_p, kf, vf)


def _s5_ops_kernel(ar_ref, ai_ref, ldt_ref, br_ref, bi_ref, cr_ref, ci_ref, cr4_ref, ci4_ref,
                   lag_ref, xr_ref, xi_ref, or_ref, oi_ref, lnr_ref, lni_ref, sr_ref, si_ref):
    n, P = xr_ref.shape[:2]
    dt = jnp.exp(ldt_ref[...])
    ar, ai = ar_ref[...], ai_ref[...]
    mag = jnp.exp(ar * dt)
    lr, li = mag * jnp.cos(ai * dt), mag * jnp.sin(ai * dt)
    den = ar * ar + ai * ai
    nr = lr - 1.0
    fr = (nr * ar + li * ai) / den
    fi = (li * ar - nr * ai) / den
    br, bi = br_ref[...], bi_ref[...]
    bbr = fr * br - fi * bi
    bbi = fr * bi + fi * br
    cr, ci = cr_ref[...], ci_ref[...]
    pr, pi = jnp.ones_like(lr), jnp.zeros_like(lr)
    for tau in range(n):
        sr_ref[...] = pr * bbr - pi * bbi
        si_ref[...] = pr * bbi + pi * bbr
        xr_ref[tau] = sr_ref[...].astype(BF16)
        xi_ref[tau] = si_ref[...].astype(BF16)

        def body(p, acc):
            return acc + cr4_ref[p] * sr_ref[p][None] - ci4_ref[p] * si_ref[p][None]

        lag_ref[tau] = lax.fori_loop(0, P, body, jnp.zeros(lag_ref.shape[1:], F32))
        pr, pi = pr * lr - pi * li, pr * li + pi * lr
        or_ref[tau] = (cr * pr - ci * pi).astype(BF16)
        oi_ref[tau] = (-(cr * pi + ci * pr)).astype(BF16)
    lnr_ref[...] = pr
    lni_ref[...] = pi


def _s5_operators(a_re, a_im, log_dt, b_re, b_im, c_re, c_im):
    G, P = a_re.shape
    C = S5_GROUP
    n = S5_CHUNK
    nlb = G // S5_LB_GROUPS
    f = lambda a: a.astype(F32)
    a_t = lambda a: f(a).T.reshape(P, 1, G)
    b_t = lambda a: jnp.transpose(f(a), (1, 2, 0))
    c_t = lambda a: jnp.transpose(f(a), (2, 1, 0))
    crt, cit = c_t(c_re), c_t(c_im)
    sds = lambda *shape: jax.ShapeDtypeStruct(shape, F32)
    lag, xr, xi, o_re, o_im, lnr, lni = pl.pallas_call(
        _s5_ops_kernel,
        out_shape=[sds(n, C, C, G)] + [jax.ShapeDtypeStruct((n, P, C, G), BF16)] * 4
        + [sds(P, 1, G)] * 2,
        scratch_shapes=[pltpu.VMEM((P, C, G), F32)] * 2,
        compiler_params=pltpu.CompilerParams(vmem_limit_bytes=VMEM_LIMIT),
        name="s5_ops",
    )(a_t(a_re), a_t(a_im), f(log_dt).reshape(1, 1, G), b_t(b_re), b_t(b_im), crt, cit,
      crt.reshape(P, C, 1, G), cit.reshape(P, C, 1, G))

    split = lambda a: a.reshape(a.shape[:-1] + (nlb, S5_LB_GROUPS))
    taps = jnp.transpose(split(lag), (3, 0, 4, 2, 1)).reshape(nlb, n, LANES, C)
    taps = jnp.pad(taps.astype(BF16), ((0, 0), (0, 0), (0, 0), (0, LANES - C)))

    def inc(x):
        w = jnp.transpose(split(x[::-1]), (3, 4, 0, 2, 1))
        return jnp.concatenate([w, w], axis=-1)

    def out(x):
        return jnp.transpose(split(x), (3, 4, 1, 0, 2)).reshape(nlb, S5_LB_GROUPS, P, n * C)

    lam_n = tuple(a.reshape(P, G).T.reshape(1, 1, G * P) for a in (lnr, lni))
    return taps, inc(xr), inc(xi), out(o_re), out(o_im), lam_n


def _chunk_rows(u_ref, tc):
    return jnp.concatenate(
        [u_ref[pl.ds(t, tc, stride=S5_CHUNK), :] for t in range(S5_CHUNK)], axis=1)


def _s5_inc_kernel(u_ref, xr_ref, xi_ref, sr_ref, si_ref, wr_ref, wi_ref):
    n, C = S5_CHUNK, S5_GROUP
    P = wr_ref.shape[1] // S5_LB_GROUPS

    @pl.when(pl.program_id(1) == 0)
    def _():
        for x_ref, w_ref in ((xr_ref, wr_ref), (xi_ref, wi_ref)):
            w_ref[...] = jnp.zeros(w_ref.shape, BF16)
            for g in range(S5_LB_GROUPS):
                half = (g % 2) * P
                for t in range(n):
                    r0 = t * LANES + g * C
                    w_ref[r0:r0 + C, g * P:(g + 1) * P] = x_ref[0, g, t, :, half:half + P]

    u = _chunk_rows(u_ref, sr_ref.shape[0]).astype(BF16)
    sr_ref[...] = _dot(u, wr_ref[...])
    si_ref[...] = _dot(u, wi_ref[...])


def _s5_inc(u, x_re, x_im, *, tc):
    Nc = u.shape[0] // S5_CHUNK
    nlb = x_re.shape[0]
    K = S5_CHUNK * LANES
    Wn = S5_LB_GROUPS * (x_re.shape[-1] // 2)
    xspec = pl.BlockSpec((1,) + x_re.shape[1:], lambda lb, i: (lb, 0, 0, 0, 0))
    ospec = pl.BlockSpec((tc, Wn), lambda lb, i: (i, lb))
    return pl.pallas_call(
        _s5_inc_kernel,
        grid=(nlb, Nc // tc),
        in_specs=[pl.BlockSpec((tc * S5_CHUNK, LANES), lambda lb, i: (i, lb)), xspec, xspec],
        out_specs=[ospec, ospec],
        out_shape=[jax.ShapeDtypeStruct((Nc, nlb * Wn), F32)] * 2,
        scratch_shapes=[pltpu.VMEM((K, Wn), BF16)] * 2,
        compiler_params=_cparams("parallel", "arbitrary"),
        name="s5_inc",
    )(u, x_re, x_im)


def _s5_scan_kernel(sr_ref, si_ref, lr_ref, li_ref, h0r_ref, h0i_ref,
                    pr_ref, pi_ref, fr_ref, fi_ref):
    lr, li = lr_ref[...], li_ref[...]
    nck = sr_ref.shape[1]

    def body(k, carry):
        hr, hi = carry
        pr_ref[:, pl.ds(k, 1), :] = hr
        pi_ref[:, pl.ds(k, 1), :] = hi
        sr = sr_ref[:, pl.ds(k, 1), :]
        si = si_ref[:, pl.ds(k, 1), :]
        return lr * hr - li * hi + sr, lr * hi + li * hr + si

    hr, hi = lax.fori_loop(0, nck, body, (h0r_ref[...], h0i_ref[...]))
    fr_ref[...] = hr
    fi_ref[...] = hi


def _s5_scan(sr, si, lam, h0r, h0i, *, cb=1024):
    B, nck, GP = sr.shape
    cb = _pick(GP, cb, LANES)
    full = pl.BlockSpec((B, nck, cb), lambda j: (0, 0, j))
    vec = pl.BlockSpec((1, 1, cb), lambda j: (0, 0, j))
    st = pl.BlockSpec((B, 1, cb), lambda j: (0, 0, j))
    return pl.pallas_call(
        _s5_scan_kernel,
        grid=(GP // cb,),
        in_specs=[full, full, vec, vec, st, st],
        out_specs=[full, full, st, st],
        out_shape=[jax.ShapeDtypeStruct((B, nck, GP), F32)] * 2
        + [jax.ShapeDtypeStruct((B, 1, GP), F32)] * 2,
        compiler_params=_cparams("parallel"),
        name="s5_scan",
    )(sr, si, lam[0], lam[1], h0r, h0i)


def _s5_out_kernel(u_ref, taps_ref, hr_ref, hi_ref, ar_ref, ai_ref, d_ref, o_ref,
                   toep_ref, wr_ref, wi_ref):
    n, C = S5_CHUNK, S5_GROUP
    tc = hr_ref.shape[0]
    P = ar_ref.shape[2]

    @pl.when(pl.program_id(1) == 0)
    def _():
        row = lax.broadcasted_iota(I32, (LANES, LANES), 0)
        lane = lax.broadcasted_iota(I32, (LANES, LANES), 1)
        spread = jnp.where(lane % C == row, 1.0, 0.0).astype(BF16)
        same_group = row // C == lane // C
        zero = jnp.zeros((LANES, LANES), BF16)
        for tau in range(n):
            tap = jnp.where(same_group, _dot(taps_ref[0, tau], spread), 0.0).astype(BF16)
            for t in range(n - tau):
                v = t + tau
                toep_ref[t * LANES:(t + 1) * LANES, v * LANES:(v + 1) * LANES] = tap
            for t in range(tau):
                toep_ref[tau * LANES:(tau + 1) * LANES, t * LANES:(t + 1) * LANES] = zero
        r2 = lax.broadcasted_iota(I32, (n * C, n * LANES), 0)
        c2 = lax.broadcasted_iota(I32, (n * C, n * LANES), 1)
        for g in range(S5_LB_GROUPS):
            place = jnp.where(c2 == (r2 // C) * LANES + g * C + r2 % C, 1.0, 0.0).astype(BF16)
            wr_ref[g * P:(g + 1) * P, :] = _dot(ar_ref[0, g], place).astype(BF16)
            wi_ref[g * P:(g + 1) * P, :] = _dot(ai_ref[0, g], place).astype(BF16)

    u = _chunk_rows(u_ref, tc)
    y = _dot(u.astype(BF16), toep_ref[...])
    y = y + _dot(hr_ref[...].astype(BF16), wr_ref[...]) + _dot(hi_ref[...].astype(BF16), wi_ref[...])
    y = jax.nn.gelu(y + jnp.tile(d_ref[...], (1, n)) * u)
    for t in range(n):
        o_ref[pl.ds(t, tc, stride=n), :] = y[:, t * LANES:(t + 1) * LANES]


def _s5_out(u, taps, hr, hi, a_re, a_im, d, *, tc):
    M, D = u.shape
    n = S5_CHUNK
    nlb, ng, P, _ = a_re.shape
    Wn, K = ng * P, n * LANES
    hspec = pl.BlockSpec((tc, Wn), lambda lb, i: (i, lb))
    uspec = pl.BlockSpec((tc * n, LANES), lambda lb, i: (i, lb))
    aspec = pl.BlockSpec((1,) + a_re.shape[1:], lambda lb, i: (lb, 0, 0, 0))
    return pl.pallas_call(
        _s5_out_kernel,
        grid=(nlb, M // (tc * n)),
        in_specs=[uspec,
                  pl.BlockSpec((1, n, LANES, LANES), lambda lb, i: (lb, 0, 0, 0)), hspec, hspec,
                  aspec, aspec,
                  pl.BlockSpec((1, LANES), lambda lb, i: (0, lb))],
        out_specs=uspec,
        out_shape=jax.ShapeDtypeStruct((M, D), F32),
        scratch_shapes=[pltpu.VMEM((K, K), BF16), pltpu.VMEM((Wn, K), BF16),
                        pltpu.VMEM((Wn, K), BF16)],
        compiler_params=_cparams("parallel", "arbitrary"),
        name="s5_out",
    )(u, taps, hr, hi, a_re, a_im, d)


def _glu_res_kernel(x_ref, wa_ref, wb_ref, r_ref, o_ref, xc_ref):
    @pl.when(pl.program_id(1) == 0)
    def _():
        xc_ref[...] = x_ref[...].astype(BF16)

    x = xc_ref[...]
    a = _dot(x, wa_ref[...])
    b = _dot(x, wb_ref[...])
    o_ref[...] = r_ref[...] + a * jax.nn.sigmoid(b)


def _glu_res(x, w, res, *, tm=1024, tn=512):
    M, D = x.shape
    N = w.shape[1] // 2
    tm = _pick(M, tm, 16)
    tn = _pick(N, tn, LANES)
    nj = N // tn
    return pl.pallas_call(
        _glu_res_kernel,
        grid=(M // tm, nj),
        in_specs=[pl.BlockSpec((tm, D), lambda i, j: (i, 0)),
                  pl.BlockSpec((D, tn), lambda i, j: (0, j)),
                  pl.BlockSpec((D, tn), lambda i, j: (0, nj + j)),
                  pl.BlockSpec((tm, tn), lambda i, j: (i, j))],
        out_specs=pl.BlockSpec((tm, tn), lambda i, j: (i, j)),
        out_shape=jax.ShapeDtypeStruct((M, N), F32),
        scratch_shapes=[pltpu.VMEM((tm, D), BF16)],
        compiler_params=_cparams("parallel", "arbitrary"),
        name="glu_res",
    )(x, w, w, res)


def _mixer_b(xp, xs, g, ops, d, w_glu, h0r, h0i):
    taps, inc_re, inc_im, out_re, out_im, lam = ops
    n = S5_CHUNK
    GP = lam[0].shape[-1]
    g, d = g[None], d.astype(F32)[None]
    w_glu = w_glu.astype(BF16)

    def stream(x, hr0, hi0):
        B, T, D = x.shape
        nck = T // n
        Nc = B * nck
        x2 = x.reshape(B * T, D)
        u = _rmsnorm(x2, g)
        tc = _pick(Nc, 256, 8)
        sr, si = _s5_inc(u, inc_re, inc_im, tc=tc)
        pr, pi, fr, fi = _s5_scan(sr.reshape(B, nck, GP), si.reshape(B, nck, GP), lam, hr0, hi0)
        gy = _s5_out(u, taps, pr.reshape(Nc, GP), pi.reshape(Nc, GP), out_re, out_im, d, tc=tc)
        y = _glu_res(gy, w_glu, x2)
        shp = (B, GP // S5_STATE, S5_STATE)
        return y.reshape(B, T, D), fr.reshape(shp), fi.reshape(shp)

    B = xp.shape[0]
    zero = jnp.zeros((B, 1, GP), F32)
    yp, rp, ip = stream(xp, zero, zero)
    Bs = xs.shape[0]
    ys, rs, is_ = stream(xs, h0r.astype(F32).reshape(Bs, 1, GP), h0i.astype(F32).reshape(Bs, 1, GP))
    return yp, ys, (rp, ip, rs, is_)


FFN_HALO = 16
FFN_TAIL = 8


def _ffn_kernel(*refs, seq_tiles, seq_len, halo):
    if halo:
        (x_ref, xh_ref, g_ref, wa_ref, wv_ref, cwa_ref, cwv_ref, cba_ref, cbv_ref, wd_ref,
         o_ref, ta_ref, tv_ref, xn_ref, xhn_ref) = refs
    else:
        (x_ref, p1a_ref, p1v_ref, p2a_ref, p2v_ref, g_ref, wa_ref, wv_ref, cwa_ref, cwv_ref,
         cba_ref, cbv_ref, wd_ref, o_ref, ta_ref, tv_ref, xn_ref, h_ref) = refs
    i, f = pl.program_id(0), pl.program_id(1)
    tm = x_ref.shape[0]

    @pl.when(f == 0)
    def _():
        x = x_ref[...]
        xn_ref[...] = _rms(x, g_ref[...]).astype(BF16)
        o_ref[...] = x
        if halo:
            xhn_ref[...] = _rms(xh_ref[...], g_ref[...]).astype(BF16)

    xn = xn_ref[...]
    tf = wa_ref.shape[1]
    r = lax.broadcasted_iota(I32, (tm, tf), 0)

    def conv(w_ref, cw_ref, cb_ref, p1_ref, p2_ref, t_ref):
        h = _dot(xn, w_ref[...])
        h1 = pltpu.roll(h, 1, axis=0)
        h2 = pltpu.roll(h, 2, axis=0)
        if halo:
            t_ref[...] = h[tm - FFN_TAIL:, :]
            hh = _dot(xhn_ref[...], w_ref[...])
            hh = hh * jnp.where(i % seq_tiles == 0, 0.0, 1.0)
            last, prev = hh[-1:, :], hh[-2:-1, :]
            h1 = jnp.where(r == 0, last, h1)
            h2 = jnp.where(r == 0, prev, jnp.where(r == 1, last, h2))
        else:
            for s in range(tf // LANES):
                sl = slice(s * LANES, (s + 1) * LANES)
                h_ref[s] = h[:, sl]
                for k in range(CONV_W - 1):
                    rows = pl.ds(seq_len - (CONV_W - 1) + k, tm // seq_len, stride=seq_len)
                    t_ref[k, :, sl] = h_ref.at[s][rows, :]
            t = r % seq_len
            h1 = jnp.where(t >= 1, h1, p1_ref[...])
            h2 = jnp.where(t >= 2, h2, p2_ref[...])
        cw = cw_ref[...]
        return cb_ref[...] + cw[0:1] * h2 + cw[1:2] * h1 + cw[2:3] * h

    if halo:
        ca = conv(wa_ref, cwa_ref, cba_ref, None, None, ta_ref)
        cv = conv(wv_ref, cwv_ref, cbv_ref, None, None, tv_ref)
    else:
        ca = conv(wa_ref, cwa_ref, cba_ref, p1a_ref, p2a_ref, ta_ref)
        cv = conv(wv_ref, cwv_ref, cbv_ref, p1v_ref, p2v_ref, tv_ref)
    act = (jax.nn.silu(ca) * cv).astype(BF16)
    o_ref[...] += _dot(act, wd_ref[...])


def _ffn(x, g, w_up, conv_w, conv_b, w_down, *, layer, seq_len, hist=None, tm=512, tf=1024):
    M, D = x.shape
    F = w_down.shape[1]
    halo = hist is None
    n_seq = M // seq_len
    tm = _pick(seq_len, tm, FFN_HALO) if halo else M
    tf = _pick(F, tf, LANES)
    nf = F // tf
    xs = pl.BlockSpec((tm, D), lambda i, f: (i, 0))
    a_col = lambda i, f: (layer, 0, f)
    v_col = lambda i, f: (layer, 0, nf + f)
    wspecs = [pl.BlockSpec((1, D), lambda i, f: (0, 0)),
              pl.BlockSpec((None, D, tf), a_col), pl.BlockSpec((None, D, tf), v_col),
              pl.BlockSpec((None, CONV_W, tf), a_col), pl.BlockSpec((None, CONV_W, tf), v_col),
              pl.BlockSpec((None, 1, tf), a_col), pl.BlockSpec((None, 1, tf), v_col),
              pl.BlockSpec((None, tf, D), lambda i, f: (layer, f, 0))]
    wargs = [g, w_up, w_up, conv_w, conv_w, conv_b, conv_b, w_down]
    scratch = [pltpu.VMEM((tm, D), BF16)]
    seq_tiles = max(seq_len // tm, 1)
    if halo:
        per = tm // FFN_HALO
        extra = [pl.BlockSpec((FFN_HALO, D), lambda i, f: (jnp.maximum(i * per - 1, 0), 0))]
        eargs = [x]
        scratch.append(pltpu.VMEM((FFN_HALO, D), BF16))
        tspec = pl.BlockSpec((FFN_TAIL, tf), lambda i, f: (i // seq_tiles, f))
        tshape = jax.ShapeDtypeStruct((n_seq * FFN_TAIL, F), F32)
    else:
        ha = pl.BlockSpec((tm, tf), lambda i, f: (i, f))
        hv = pl.BlockSpec((tm, tf), lambda i, f: (i, nf + f))
        extra = [ha, hv, ha, hv]
        eargs = [hist[0], hist[0], hist[1], hist[1]]
        scratch.append(pltpu.VMEM((tf // LANES, tm, LANES), F32))
        tspec = pl.BlockSpec((CONV_W - 1, n_seq, tf), lambda i, f: (0, 0, f))
        tshape = jax.ShapeDtypeStruct((CONV_W - 1, n_seq, F), F32)
    kern = functools.partial(_ffn_kernel, seq_tiles=seq_tiles, seq_len=seq_len, halo=halo)
    y, ta, tv = pl.pallas_call(
        kern,
        grid=(M // tm, nf),
        in_specs=[xs] + extra + wspecs,
        out_specs=[pl.BlockSpec((tm, D), lambda i, f: (i, 0)), tspec, tspec],
        out_shape=[jax.ShapeDtypeStruct((M, D), F32), tshape, tshape],
        scratch_shapes=scratch,
        compiler_params=_cparams("arbitrary", "arbitrary"),
        name="conv_ffn" if halo else "conv_ffn_hist",
    )(x, *eargs, *wargs)
    if halo:
        tail = lambda t: t.reshape(n_seq, FFN_TAIL, F)[:, FFN_TAIL - (CONV_W - 1):]
    else:
        tail = lambda t: jnp.swapaxes(t, 0, 1)
    return y, jnp.concatenate([tail(ta), tail(tv)], axis=-1)


def _conv_ffn(xp, xs, g, w_up, conv_w, conv_b, w_down, hist_s, layer):
    B, S, D = xp.shape
    Bs, T, _ = xs.shape
    F2 = w_up.shape[2]
    g = g[None]
    yp, hp = _ffn(xp.reshape(B * S, D), g, w_up, conv_w, conv_b, w_down, layer=layer, seq_len=S)
    hs = hist_s.astype(F32)
    t = jnp.arange(T)[None, :, None]
    p1 = jnp.where(t == 0, hs[:, -1:], 0.0).reshape(Bs * T, F2)
    p2 = jnp.where(t == 0, hs[:, -2:-1], jnp.where(t == 1, hs[:, -1:], 0.0)).reshape(Bs * T, F2)
    ys, hs_new = _ffn(xs.reshape(Bs * T, D), g, w_up, conv_w, conv_b, w_down, layer=layer,
                      seq_len=T, hist=(p1, p2))
    return yp.reshape(B, S, D), ys.reshape(Bs, T, D), hp, hs_new


def kernel(x_prompt, x_sample, cache_a_k, cache_a_v, cache_a_kidx, cache_c_k, cache_c_v, state_b_re, state_b_im, state_ffn_conv, norm_mix, norm_ffn, a_w_in, a_w_out, a_q_norm, a_k_norm, b_a_re, b_a_im, b_log_dt, b_b_re, b_b_im, b_c_re, b_c_im, b_d, b_w_glu, c_w_in, c_w_out, c_q_norm, c_k_norm, c_rel_bias, ffn_w_up, ffn_conv_w, ffn_conv_b, ffn_w_down):
    xp, xs = x_prompt, x_sample
    depth = norm_mix.shape[0]
    outs_a, outs_b, outs_c, f_p, f_s = [], [], [], [], []
    ffn_w_up, ffn_w_down = ffn_w_up.astype(BF16), ffn_w_down.astype(BF16)
    ffn_conv_b = ffn_conv_b[:, None]
    for i in range(depth):
        li, kind = i // 3, i % 3
        if kind == 0:
            xp, xs, o = _mixer_a(xp, xs, norm_mix[i], a_w_in[li], a_w_out[li], a_q_norm[li],
                                 a_k_norm[li], (cache_a_k, cache_a_v, cache_a_kidx), li)
            outs_a.append(o)
        elif kind == 1:
            ops = _s5_operators(b_a_re[li], b_a_im[li], b_log_dt[li], b_b_re[li], b_b_im[li],
                                b_c_re[li], b_c_im[li])
            xp, xs, o = _mixer_b(xp, xs, norm_mix[i], ops, b_d[li], b_w_glu[li],
                                 state_b_re[li], state_b_im[li])
            outs_b.append(o)
        else:
            xp, xs, o = _mixer_c(xp, xs, norm_mix[i], c_w_in[li], c_w_out[li], c_q_norm[li],
                                 c_k_norm[li], c_rel_bias[li], cache_c_k[li], cache_c_v[li])
            outs_c.append(o)
        xp, xs, hp, hs = _conv_ffn(xp, xs, norm_ffn[i], ffn_w_up, ffn_conv_w, ffn_conv_b,
                                   ffn_w_down, state_ffn_conv[i], i)
        f_p.append(hp)
        f_s.append(hs)
    stack = lambda outs, j: jnp.stack([o[j] for o in outs])
    return (xp, xs,
            *[stack(outs_a, j) for j in range(6)],
            *[stack(outs_c, j) for j in range(4)],
            *[stack(outs_b, j) for j in range(4)],
            jnp.stack(f_p), jnp.stack(f_s))
```
